```python
import math
import jax, jax.numpy as jnp
from jax import lax
import numpy as np

D_MODEL = 1024
BATCH = 16
SEQ = 4096
DEPTH = 1

CHUNK = 64
MIX_WIDTH = D_MODEL
HGRN_WIDTH = MIX_WIDTH // 2
GMLP_WIDTH = MIX_WIDTH - HGRN_WIDTH
HGRN_HEADS = 4
HGRN_DK = HGRN_WIDTH // HGRN_HEADS
HGRN_DV = HGRN_WIDTH // HGRN_HEADS
GMLP_BLOCK = 128
GMLP_GROUPS = 4
GMLP_GROUP_DIM = GMLP_WIDTH // GMLP_GROUPS
IN_COLS = 4 * HGRN_WIDTH + 2 * GMLP_WIDTH
N_EXPERTS = 32
TOP_K = 4
D_FF = D_MODEL
SWIGLU_LIMIT = 7.0
SWIGLU_ALPHA = 1.702
ALPHA = (2.0 * DEPTH) ** 0.25
BETA = (8.0 * DEPTH) ** -0.25
EPS = 1e-5

kernel_name = "hybrid_hgrn2_gmlp_moe_deepnorm"


def layer_norm(x, g, b):
    xf = x.astype(jnp.float32)
    mu = jnp.mean(xf, axis=-1, keepdims=True)
    var = jnp.mean(jnp.square(xf - mu), axis=-1, keepdims=True)
    return ((xf - mu) * lax.rsqrt(var + EPS) * g.astype(jnp.float32) + b.astype(jnp.float32)).astype(x.dtype)


def rms_norm(x, g):
    xf = x.astype(jnp.float32)
    return xf * lax.rsqrt(jnp.mean(jnp.square(xf), axis=-1, keepdims=True) + EPS) * g.astype(jnp.float32)


def hgrn2_mix(q, f_logit, i_val, gate, lb, norm_g):
    B, L, _ = q.shape
    n_chunks = L // CHUNK
    fl = f_logit.astype(jnp.float32)
    lbf = lb.astype(jnp.float32)
    logf = jnp.log(lbf + (1.0 - lbf) * jax.nn.sigmoid(fl))
    k = (1.0 - lbf) * jax.nn.sigmoid(-fl)

    def heads(t, d):
        return t.astype(jnp.float32).reshape(B, n_chunks, CHUNK, HGRN_HEADS, d).transpose(1, 0, 3, 2, 4)

    qc = heads(q, HGRN_DK) * (HGRN_DK ** -0.5)
    kc = heads(k, HGRN_DK)
    vc = heads(i_val, HGRN_DV)
    Gc = jnp.cumsum(heads(logf, HGRN_DK), axis=3)
    causal = jnp.tril(jnp.ones((CHUNK, CHUNK), dtype=bool))

    def step(S, inp):
        qb, kb, vb, Gb = inp
        diff = Gb[:, :, :, None, :] - Gb[:, :, None, :, :]
        decay = jnp.exp(jnp.where(causal[None, None, :, :, None], diff, -jnp.inf))
        A = jnp.einsum('bhik,bhjk,bhijk->bhij', qb, kb, decay)
        o = jnp.einsum('bhij,bhjv->bhiv', A, vb) + jnp.einsum('bhik,bhkv->bhiv', qb * jnp.exp(Gb), S)
        G_last = Gb[:, :, -1:, :]
        S_new = jnp.exp(G_last[:, :, 0, :])[..., None] * S + jnp.einsum(
            'bhjk,bhjv->bhkv', kb * jnp.exp(G_last - Gb), vb)
        return S_new, o

    S0 = jnp.zeros((B, HGRN_HEADS, HGRN_DK, HGRN_DV), jnp.float32)
    _, oc = lax.scan(step, S0, (qc, kc, vc, Gc))
    o = oc.transpose(1, 0, 3, 2, 4).reshape(B, L, HGRN_HEADS * HGRN_DV)
    o = rms_norm(o, norm_g) * jax.nn.silu(gate.astype(jnp.float32))
    return o.astype(q.dtype)


def gmlp_mix(u, v, ln_g, ln_b, ws, bs):
    B, L, _ = v.shape
    n_win = L // GMLP_BLOCK
    u = jax.nn.gelu(u, approximate=False)
    v = layer_norm(jax.nn.gelu(v, approximate=False), ln_g, ln_b)
    vb = v.reshape(B, n_win, GMLP_BLOCK, GMLP_GROUPS, GMLP_GROUP_DIM)
    chunk_id = jnp.arange(GMLP_BLOCK) // CHUNK
    mask = chunk_id[None, :] <= chunk_id[:, None]
    wm = jnp.where(mask[None], ws, jnp.zeros_like(ws))
    s = jnp.einsum('gqp,bwpgc->bwqgc', wm, vb) + bs.T[None, None, :, :, None]
    return u * s.reshape(B, L, GMLP_WIDTH)


def moe_ffn(h, router_w, router_b, w1, b1, w2, b2):
    B, L, Dm = h.shape
    T = B * L
    t = h.reshape(T, Dm)
    logits = (t @ router_w + router_b).astype(jnp.float32)
    vals, idx = lax.top_k(logits, TOP_K)
    gates = jax.nn.softmax(vals, axis=-1)
    flat_e = idx.reshape(-1)
    order = jnp.argsort(flat_e)
    sorted_e = flat_e[order]
    tok = order // TOP_K
    xs = t[tok]
    sizes = jnp.bincount(flat_e, length=N_EXPERTS).astype(jnp.int32)
    hmid = lax.ragged_dot(xs, w1, sizes) + b1[sorted_e]
    x_glu = jnp.minimum(hmid[:, ::2], SWIGLU_LIMIT)
    x_lin = jnp.clip(hmid[:, 1::2], -SWIGLU_LIMIT, SWIGLU_LIMIT)
    act = x_glu * jax.nn.sigmoid(SWIGLU_ALPHA * x_glu) * (x_lin + 1.0)
    y = lax.ragged_dot(act, w2, sizes) + b2[sorted_e]
    y = y * gates.reshape(-1)[order][:, None].astype(y.dtype)
    out = jax.ops.segment_sum(y, tok, num_segments=T)
    return out.reshape(B, L, Dm).astype(h.dtype)


def setup_inputs(seed: int = 0) -> dict:
    key = jax.random.key(seed)
    ks = jax.random.split(key, 20)
    nrm = jax.random.normal
    f32 = jnp.float32
    return {
        "x": nrm(ks[0], (BATCH, SEQ, D_MODEL), f32),
        "w_in": nrm(ks[1], (DEPTH, D_MODEL, IN_COLS), f32) * D_MODEL ** -0.5,
        "lb_logits": nrm(ks[2], (DEPTH + 1, HGRN_WIDTH), f32) * 0.5,
        "hgrn_norm_g": 1.0 + 0.02 * nrm(ks[3], (DEPTH, HGRN_WIDTH), f32),
        "gmlp_ln_g": 1.0 + 0.02 * nrm(ks[4], (DEPTH, GMLP_WIDTH), f32),
        "gmlp_ln_b": 0.02 * nrm(ks[5], (DEPTH, GMLP_WIDTH), f32),
        "gmlp_ws": nrm(ks[6], (DEPTH, GMLP_GROUPS, GMLP_BLOCK, GMLP_BLOCK), f32) * GMLP_BLOCK ** -0.5,
        "gmlp_bs": 1.0 + 0.02 * nrm(ks[7], (DEPTH, GMLP_GROUPS, GMLP_BLOCK), f32),
        "w_out": nrm(ks[8], (DEPTH, MIX_WIDTH, D_MODEL), f32) * (MIX_WIDTH ** -0.5 * BETA),
        "ln1_g": 1.0 + 0.02 * nrm(ks[9], (DEPTH, D_MODEL), f32),
        "ln1_b": 0.02 * nrm(ks[10], (DEPTH, D_MODEL), f32),
        "router_w": nrm(ks[11], (DEPTH, D_MODEL, N_EXPERTS), f32) * D_MODEL ** -0.5,
        "router_b": 0.01 * nrm(ks[12], (DEPTH, N_EXPERTS), f32),
        "exp_w1": nrm(ks[13], (DEPTH, N_EXPERTS, D_MODEL, 2 * D_FF), f32) * D_MODEL ** -0.5,
        "exp_b1": 0.01 * nrm(ks[14], (DEPTH, N_EXPERTS, 2 * D_FF), f32),
        "exp_w2": nrm(ks[15], (DEPTH, N_EXPERTS, D_FF, D_MODEL), f32) * (D_FF ** -0.5 * BETA),
        "exp_b2": 0.01 * nrm(ks[16], (DEPTH, N_EXPERTS, D_MODEL), f32),
        "ln2_g": 1.0 + 0.02 * nrm(ks[17], (DEPTH, D_MODEL), f32),
        "ln2_b": 0.02 * nrm(ks[18], (DEPTH, D_MODEL), f32),
    }


def reference(x, w_in, lb_logits, hgrn_norm_g, gmlp_ln_g, gmlp_ln_b, gmlp_ws, gmlp_bs, w_out,
              ln1_g, ln1_b, router_w, router_b, exp_w1, exp_b1, exp_w2, exp_b2, ln2_g, ln2_b):
    lb_all = jnp.cumsum(jax.nn.softmax(lb_logits.astype(jnp.float32), axis=0), axis=0)[:DEPTH]
    h = x
    for l in range(DEPTH):
        p = h @ w_in[l]
        o1 = HGRN_WIDTH
        q = p[..., 0:o1]
        f_logit = p[..., o1:2 * o1]
        i_val = p[..., 2 * o1:3 * o1]
        gate = p[..., 3 * o1:4 * o1]
        u = p[..., 4 * o1:4 * o1 + GMLP_WIDTH]
        v = p[..., 4 * o1 + GMLP_WIDTH:]
        y_rec = hgrn2_mix(q, f_logit, i_val, gate, lb_all[l], hgrn_norm_g[l])
        y_sg = gmlp_mix(u, v, gmlp_ln_g[l], gmlp_ln_b[l], gmlp_ws[l], gmlp_bs[l])
        mix = jnp.concatenate([y_rec, y_sg], axis=-1) @ w_out[l]
        h = layer_norm(ALPHA * h + mix, ln1_g[l], ln1_b[l])
        ffn = moe_ffn(h, router_w[l], router_b[l], exp_w1[l], exp_b1[l], exp_w2[l], exp_b2[l])
        h = layer_norm(ALPHA * h + ffn, ln2_g[l], ln2_b[l])
    return h
```

```python
import functools
import math

import jax
import jax.numpy as jnp
from jax import lax
from jax.experimental import pallas as pl
from jax.experimental.pallas import tpu as pltpu
from jax.experimental.pallas import tpu_sc as plsc

D_MODEL = 1024
CHUNK = 64
SUB = 16
N_SUB = CHUNK // SUB
HGRN_WIDTH = 512
HGRN_HEADS = 4
HEAD_DIM = 128
GMLP_WIDTH = 512
GMLP_BLOCK = 128
GMLP_GROUPS = 4
IN_COLS = 3072
N_EXPERTS = 32
TOP_K = 4
SWIGLU_LIMIT = 7.0
SWIGLU_ALPHA = 1.702
ALPHA = 2.0 ** 0.25
EPS = 1e-5
HALF = D_MODEL // 2

TB = 512
TM = 256
TC3 = 512
SC_WINDOW = 128
VMEM_LIMIT = 56 * 1024 * 1024

_NT = (((1,), (1,)), ((), ()))


def _sigmoid(x):
    z = jnp.exp(-jnp.abs(x))
    r = 1.0 / (1.0 + z)
    return jnp.where(x >= 0, r, z * r)


def _gelu(x):
    return 0.5 * x * (1.0 + lax.erf(x * (1.0 / math.sqrt(2.0))))


def _layer_norm(x, g, b):
    mu = jnp.mean(x, axis=-1, keepdims=True)
    xc = x - mu
    var = jnp.mean(xc * xc, axis=-1, keepdims=True)
    return xc * lax.rsqrt(var + EPS) * g + b


def _pack_rows(h):
    a = h[:, :HALF].astype(jnp.bfloat16).astype(jnp.float32)
    b = h[:, HALF:].astype(jnp.bfloat16).astype(jnp.float32)
    au = lax.bitcast_convert_type(a, jnp.uint32) >> 16
    bu = lax.bitcast_convert_type(b, jnp.uint32) & jnp.uint32(0xFFFF0000)
    return lax.bitcast_convert_type(au | bu, jnp.int32)


def _unpack_rows(w):
    u = lax.bitcast_convert_type(w, jnp.uint32)
    a = lax.bitcast_convert_type(u << 16, jnp.float32)
    b = lax.bitcast_convert_type(u & jnp.uint32(0xFFFF0000), jnp.float32)
    return a, b


def _mixer_kernel(x_ref, win_ref, lb_ref, hg_ref, lng_ref, lnb_ref, wm_ref, bst_ref, wout_ref,
                  l1g_ref, l1b_ref, rwh_ref, rwl_ref, rb_ref, tri_ref, upp_ref,
                  h1f_ref, h1p_ref, idx_ref, gate_ref, rank_ref, cnt_ref,
                  p_scr, lf_scr, kk_scr, g_scr, kc_scr, o_scr, st_scr, carry_scr):
    b = pl.program_id(0)
    t = pl.program_id(1)
    f32 = jnp.float32
    bf16 = jnp.bfloat16

    @pl.when(t == 0)
    def _():
        st_scr[...] = jnp.zeros_like(st_scr)

    @pl.when((b == 0) & (t == 0))
    def _():
        carry_scr[...] = jnp.zeros_like(carry_scr)

    x = x_ref[...]
    p_scr[...] = jnp.dot(x.astype(bf16), win_ref[...], preferred_element_type=f32)

    fl = p_scr[:, HGRN_WIDTH:2 * HGRN_WIDTH]
    lb = lb_ref[...]
    z = jnp.exp(-jnp.abs(fl))
    r = 1.0 / (1.0 + z)
    zr = z * r
    pos = fl >= 0
    lf_scr[...] = jnp.log(lb + (1.0 - lb) * jnp.where(pos, r, zr))
    kk_scr[...] = (1.0 - lb) * jnp.where(pos, zr, r)

    tri = tri_ref[...]
    row64 = lax.broadcasted_iota(jnp.int32, (CHUNK, CHUNK), 0)
    col64 = lax.broadcasted_iota(jnp.int32, (CHUNK, CHUNK), 1)
    lane_sub = lax.broadcasted_iota(jnp.int32, (SUB, CHUNK), 1)
    scale = HEAD_DIM ** -0.5

    def chunk_body(c, carry):
        r0 = pl.multiple_of(c * CHUNK, CHUNK)
        rows = pl.ds(r0, CHUNK)
        lf = lf_scr[rows, :]
        hi = lf.astype(bf16)
        lo = (lf - hi.astype(f32)).astype(bf16)
        gg = jnp.dot(tri, jnp.concatenate([hi, lo], axis=1), preferred_element_type=f32)
        g_all = gg[:, :HGRN_WIDTH] + gg[:, HGRN_WIDTH:]
        g_scr[...] = g_all
        kc_scr[...] = kk_scr[rows, :]
        for h in range(HGRN_HEADS):
            cs = slice(h * HEAD_DIM, (h + 1) * HEAD_DIM)
            q = p_scr[rows, h * HEAD_DIM:(h + 1) * HEAD_DIM]
            v = p_scr[rows, 2 * HGRN_WIDTH + h * HEAD_DIM:2 * HGRN_WIDTH + (h + 1) * HEAD_DIM]
            k = kc_scr[:, cs]
            gh = g_all[:, cs]
            st = st_scr[h]
            vb = v.astype(bf16)

            o_inter = lax.dot_general((q * jnp.exp(gh)).astype(bf16), st.astype(bf16), _NT,
                                      preferred_element_type=f32)

            qts, kts = [], []
            for a in range(1, N_SUB):
                lo_r, hi_r = a * SUB, (a + 1) * SUB
                ra = g_scr[lo_r - 1:lo_r, cs]
                qa = q[lo_r:hi_r] * jnp.exp(gh[lo_r:hi_r] - ra)
                pieces = [jnp.zeros((lo_r, HEAD_DIM), f32), qa]
                if hi_r < CHUNK:
                    pieces.append(jnp.zeros((CHUNK - hi_r, HEAD_DIM), f32))
                qts.append(jnp.concatenate(pieces, axis=0))
                ka = k[:lo_r] * jnp.exp(ra - gh[:lo_r])
                kts.append(jnp.concatenate([ka, jnp.zeros((CHUNK - lo_r, HEAD_DIM), f32)], axis=0))
            a_off = lax.dot_general(jnp.concatenate(qts, axis=1).astype(bf16),
                                    jnp.concatenate(kts, axis=1).astype(bf16), _NT,
                                    preferred_element_type=f32)

            diag_rows = []
            for a in range(N_SUB):
                lo_r = a * SUB
                gs = gh[lo_r:lo_r + SUB]
                qs = q[lo_r:lo_r + SUB]
                blk = jnp.zeros((SUB, CHUNK), f32)
                for jl in range(SUB):
                    j = lo_r + jl
                    gj = g_scr[j:j + 1, cs]
                    kj = kc_scr[j:j + 1, cs]
                    e = jnp.exp(jnp.minimum(gs - gj, 0.0))
                    col = jnp.sum(qs * (kj * e), axis=-1, keepdims=True)
                    blk = jnp.where(lane_sub == j, col, blk)
                diag_rows.append(blk)
            a_diag = jnp.concatenate(diag_rows, axis=0)
            a_mat = a_off + jnp.where(row64 >= col64, a_diag, 0.0)

            o = jnp.dot(a_mat.astype(bf16), vb, preferred_element_type=f32) + o_inter
            o_scr[rows, h * HEAD_DIM:(h + 1) * HEAD_DIM] = o * scale

            gl = g_scr[CHUNK - 1:CHUNK, cs]
            kd = k * jnp.exp(gl - gh)
            upd = jnp.dot(v.T.astype(bf16), kd.astype(bf16), preferred_element_type=f32)
            st_scr[h] = st * jnp.exp(gl) + upd
        return carry

    lax.fori_loop(0, TB // CHUNK, chunk_body, 0)

    o = o_scr[...]
    ms = jnp.mean(o * o, axis=-1, keepdims=True)
    gate = p_scr[:, 3 * HGRN_WIDTH:4 * HGRN_WIDTH]
    y_rec = o * lax.rsqrt(ms + EPS) * hg_ref[...] * (gate * _sigmoid(gate))

    u = _gelu(p_scr[:, 4 * HGRN_WIDTH:4 * HGRN_WIDTH + GMLP_WIDTH])
    vn = _layer_norm(_gelu(p_scr[:, 4 * HGRN_WIDTH + GMLP_WIDTH:]), lng_ref[...], lnb_ref[...])
    vnb = vn.astype(bf16)
    win_rows = []
    for w in range(TB // GMLP_BLOCK):
        cols = []
        for g in range(GMLP_GROUPS):
            vblk = vnb[w * GMLP_BLOCK:(w + 1) * GMLP_BLOCK, g * HEAD_DIM:(g + 1) * HEAD_DIM]
            s = jnp.dot(wm_ref[g], vblk, preferred_element_type=f32) + bst_ref[:, g:g + 1]
            cols.append(s)
        win_rows.append(jnp.concatenate(cols, axis=1))
    y_sg = u * jnp.concatenate(win_rows, axis=0)

    mix = jnp.dot(jnp.concatenate([y_rec, y_sg], axis=1).astype(bf16), wout_ref[...],
                  preferred_element_type=f32)
    h1 = _layer_norm(ALPHA * x + mix, l1g_ref[...], l1b_ref[...])
    h1f_ref[...] = h1
    h1p_ref[...] = _pack_rows(h1)

    hh = h1.astype(bf16)
    hl = (h1 - hh.astype(f32)).astype(bf16)
    rwh = rwh_ref[...]
    logits = (lax.dot_general(rwh, hh, _NT, preferred_element_type=f32)
              + lax.dot_general(rwh, hl, _NT, preferred_element_type=f32)
              + lax.dot_general(rwl_ref[...], hh, _NT, preferred_element_type=f32)
              + rb_ref[...])

    e_iota = lax.broadcasted_iota(jnp.int32, (N_EXPERTS, TB), 0)
    work = logits
    vals, idxs, hots = [], [], []
    for _ in range(TOP_K):
        m = jnp.max(work, axis=0, keepdims=True)
        ik = jnp.min(jnp.where(work == m, e_iota, N_EXPERTS), axis=0, keepdims=True)
        hot = e_iota == ik
        work = jnp.where(hot, -jnp.inf, work)
        vals.append(m)
        idxs.append(ik)
        hots.append(hot)
    exps = [jnp.exp(vk - vals[0]) for vk in vals]
    denom = exps[0] + exps[1] + exps[2] + exps[3]
    gate_ref[0] = jnp.concatenate([ek / denom for ek in exps], axis=0)
    idx_ref[0] = jnp.concatenate(idxs, axis=0)

    hot_any = jnp.where(hots[0] | hots[1] | hots[2] | hots[3], 1.0, 0.0)
    prefix = jnp.dot(hot_any.astype(bf16), upp_ref[...], preferred_element_type=f32)
    base = prefix + carry_scr[:, 0:1]
    ranks = [jnp.sum(jnp.where(hk, base, 0.0), axis=0, keepdims=True) for hk in hots]
    rank_ref[0] = jnp.concatenate(ranks, axis=0).astype(jnp.int32)
    new_carry = carry_scr[...] + jnp.sum(hot_any, axis=1, keepdims=True)
    carry_scr[...] = new_carry
    cnt_ref[...] = new_carry


def _mixer(xt, win, lb, hg, lng, lnb, wm, bst, wout, l1g, l1b, rwh, rwl, rb, tri, upp, batch):
    t_total = xt.shape[0]
    nt = t_total // batch // TB
    nblk = batch * nt
    const2 = lambda b, t: (0, 0)
    const3 = lambda b, t: (0, 0, 0)
    row_blk = lambda b, t: (b * nt + t, 0)
    meta_blk = lambda b, t: (b * nt + t, 0, 0)
    in_specs = [
        pl.BlockSpec((TB, D_MODEL), row_blk),
        pl.BlockSpec((D_MODEL, IN_COLS), const2),
        pl.BlockSpec((1, HGRN_WIDTH), const2),
        pl.BlockSpec((1, HGRN_WIDTH), const2),
        pl.BlockSpec((1, GMLP_WIDTH), const2),
        pl.BlockSpec((1, GMLP_WIDTH), const2),
        pl.BlockSpec((GMLP_GROUPS, GMLP_BLOCK, GMLP_BLOCK), const3),
        pl.BlockSpec((GMLP_BLOCK, GMLP_GROUPS), const2),
        pl.BlockSpec((D_MODEL, D_MODEL), const2),
        pl.BlockSpec((1, D_MODEL), const2),
        pl.BlockSpec((1, D_MODEL), const2),
        pl.BlockSpec((N_EXPERTS, D_MODEL), const2),
        pl.BlockSpec((N_EXPERTS, D_MODEL), const2),
        pl.BlockSpec((N_EXPERTS, 1), const2),
        pl.BlockSpec((CHUNK, CHUNK), const2),
        pl.BlockSpec((TB, TB), const2),
    ]
    out_shape = [
        jax.ShapeDtypeStruct((t_total, D_MODEL), jnp.float32),
        jax.ShapeDtypeStruct((t_total, HALF), jnp.int32),
        jax.ShapeDtypeStruct((nblk, TOP_K, TB), jnp.int32),
        jax.ShapeDtypeStruct((nblk, TOP_K, TB), jnp.float32),
        jax.ShapeDtypeStruct((nblk, TOP_K, TB), jnp.int32),
        jax.ShapeDtypeStruct((N_EXPERTS, 128), jnp.float32),
    ]
    out_specs = [
        pl.BlockSpec((TB, D_MODEL), row_blk),
        pl.BlockSpec((TB, HALF), row_blk),
        pl.BlockSpec((1, TOP_K, TB), meta_blk),
        pl.BlockSpec((1, TOP_K, TB), meta_blk),
        pl.BlockSpec((1, TOP_K, TB), meta_blk),
        pl.BlockSpec((N_EXPERTS, 128), const2),
    ]
    scratch = [
        pltpu.VMEM((TB, IN_COLS), jnp.float32),
        pltpu.VMEM((TB, HGRN_WIDTH), jnp.float32),
        pltpu.VMEM((TB, HGRN_WIDTH), jnp.float32),
        pltpu.VMEM((CHUNK, HGRN_WIDTH), jnp.float32),
        pltpu.VMEM((CHUNK, HGRN_WIDTH), jnp.float32),
        pltpu.VMEM((TB, HGRN_WIDTH), jnp.float32),
        pltpu.VMEM((HGRN_HEADS, HEAD_DIM, HEAD_DIM), jnp.float32),
        pltpu.VMEM((N_EXPERTS, 128), jnp.float32),
    ]
    return pl.pallas_call(
        _mixer_kernel,
        grid=(batch, nt),
        in_specs=in_specs,
        out_specs=out_specs,
        out_shape=out_shape,
        scratch_shapes=scratch,
        compiler_params=pltpu.CompilerParams(
            dimension_semantics=("arbitrary", "arbitrary"),
            vmem_limit_bytes=VMEM_LIMIT),
        name="mixer",
    )(xt, win, lb, hg, lng, lnb, wm, bst, wout, l1g, l1b, rwh, rwl, rb, tri, upp)


def _sc_workers():
    info = plsc.get_sparse_core_info()
    return info.num_cores, info.num_cores * info.num_subcores


def _sc_dispatch(rows, dest, n_out):
    t_total, dw = rows.shape
    nc, nw = _sc_workers()
    per_w = t_total // nw
    mesh = plsc.VectorSubcoreMesh(core_axis_name="c", subcore_axis_name="s")

    @functools.partial(
        pl.kernel,
        out_type=jax.ShapeDtypeStruct((n_out, dw), rows.dtype),
        mesh=mesh,
        scratch_types=[pltpu.VMEM((SC_WINDOW,), jnp.int32) for _ in range(TOP_K)]
        + [pltpu.VMEM((SC_WINDOW, dw), rows.dtype), pltpu.SemaphoreType.DMA],
        name="sc_dispatch",
    )
    def k(x_hbm, i_hbm, o_hbm, i0, i1, i2, i3, rows_v, sem):
        wid = lax.axis_index("s") * nc + lax.axis_index("c")
        idx_bufs = (i0, i1, i2, i3)

        @pl.loop(0, per_w // SC_WINDOW)
        def _(j):
            base = wid * per_w + j * SC_WINDOW
            for kk in range(TOP_K):
                pltpu.sync_copy(i_hbm.at[pl.ds(kk * t_total + base, SC_WINDOW)], idx_bufs[kk])
            pltpu.sync_copy(x_hbm.at[pl.ds(base, SC_WINDOW)], rows_v)
            copies = [pltpu.async_copy(rows_v, o_hbm.at[idx_bufs[kk]], sem) for kk in range(TOP_K)]
            for cp in copies:
                cp.wait()

    return k(rows, dest)


def _sc_gather(table, idx):
    n = idx.shape[0]
    dw = table.shape[1]
    nc, nw = _sc_workers()
    per_w = n // nw
    mesh = plsc.VectorSubcoreMesh(core_axis_name="c", subcore_axis_name="s")

    @functools.partial(
        pl.kernel,
        out_type=jax.ShapeDtypeStruct((n, dw), table.dtype),
        mesh=mesh,
        scratch_types=[pltpu.VMEM((SC_WINDOW,), jnp.int32),
                       pltpu.VMEM((SC_WINDOW, dw), table.dtype),
                       pltpu.SemaphoreType.DMA],
        name="sc_gather",
    )
    def k(t_hbm, i_hbm, o_hbm, idx_v, rows_v, sem):
        wid = lax.axis_index("s") * nc + lax.axis_index("c")

        @pl.loop(0, per_w // SC_WINDOW)
        def _(j):
            base = wid * per_w + j * SC_WINDOW
            pltpu.sync_copy(i_hbm.at[pl.ds(base, SC_WINDOW)], idx_v)
            pltpu.async_copy(t_hbm.at[idx_v], rows_v, sem).wait()
            pltpu.sync_copy(rows_v, o_hbm.at[pl.ds(base, SC_WINDOW)])

    return k(table, idx)


def _expert_kernel(te_ref, tv_ref, xs_ref, w1g_ref, w1l_ref, w2_ref, b1g_ref, b1l_ref, b2_ref, y_ref):
    i = pl.program_id(0)
    valid = tv_ref[i]
    f32 = jnp.float32
    bf16 = jnp.bfloat16

    @pl.when(valid == 0)
    def _():
        y_ref[...] = jnp.zeros_like(y_ref)

    @pl.when(valid > 0)
    def _():
        a, b = _unpack_rows(xs_ref[...])
        keep = lax.broadcasted_iota(jnp.int32, (TM, 1), 0) < valid
        x = jnp.where(keep, jnp.concatenate([a, b], axis=1), 0.0).astype(bf16)
        hg = jnp.dot(x, w1g_ref[0], preferred_element_type=f32) + b1g_ref[0]
        hl = jnp.dot(x, w1l_ref[0], preferred_element_type=f32) + b1l_ref[0]
        xg = jnp.minimum(hg, SWIGLU_LIMIT)
        xl = jnp.clip(hl, -SWIGLU_LIMIT, SWIGLU_LIMIT)
        act = xg * _sigmoid(SWIGLU_ALPHA * xg) * (xl + 1.0)
        y = jnp.dot(act.astype(bf16), w2_ref[0], preferred_element_type=f32) + b2_ref[0]
        y_ref[...] = _pack_rows(y)


def _experts(xs, tile_expert, tile_valid, w1g, w1l, w2, b1g, b1l, b2):
    n_slots = xs.shape[0]
    n_tiles = n_slots // TM
    d_ff = w1g.shape[2]
    wmap = lambda i, te, tv: (te[i], 0, 0)
    grid_spec = pltpu.PrefetchScalarGridSpec(
        num_scalar_prefetch=2,
        grid=(n_tiles,),
        in_specs=[
            pl.BlockSpec((TM, HALF), lambda i, te, tv: (i, 0)),
            pl.BlockSpec((1, D_MODEL, d_ff), wmap),
            pl.BlockSpec((1, D_MODEL, d_ff), wmap),
            pl.BlockSpec((1, d_ff, D_MODEL), wmap),
            pl.BlockSpec((1, 1, d_ff), wmap),
            pl.BlockSpec((1, 1, d_ff), wmap),
            pl.BlockSpec((1, 1, D_MODEL), wmap),
        ],
        out_specs=pl.BlockSpec((TM, HALF), lambda i, te, tv: (i, 0)),
    )
    return pl.pallas_call(
        _expert_kernel,
        grid_spec=grid_spec,
        out_shape=jax.ShapeDtypeStruct((n_slots, HALF), jnp.int32),
        compiler_params=pltpu.CompilerParams(
            dimension_semantics=("arbitrary",),
            vmem_limit_bytes=VMEM_LIMIT),
        name="experts",
    )(tile_expert, tile_valid, xs, w1g, w1l, w2, b1g, b1l, b2)


def _combine_kernel(h_ref, yk_ref, gate_ref, g_ref, b_ref, o_ref):
    gates = gate_ref[...]
    acc_a = None
    acc_b = None
    for k in range(TOP_K):
        a, b = _unpack_rows(yk_ref[:, k * HALF:(k + 1) * HALF])
        gk = gates[:, k:k + 1]
        acc_a = gk * a if acc_a is None else acc_a + gk * a
        acc_b = gk * b if acc_b is None else acc_b + gk * b
    ffn = jnp.concatenate([acc_a, acc_b], axis=1)
    o_ref[...] = _layer_norm(ALPHA * h_ref[...] + ffn, g_ref[...], b_ref[...])


def _combine(h1f, yk, gates, g2, b2):
    t_total = h1f.shape[0]
    row = lambda i: (i, 0)
    const = lambda i: (0, 0)
    return pl.pallas_call(
        _combine_kernel,
        grid=(t_total // TC3,),
        in_specs=[
            pl.BlockSpec((TC3, D_MODEL), row),
            pl.BlockSpec((TC3, TOP_K * HALF), row),
            pl.BlockSpec((TC3, TOP_K), row),
            pl.BlockSpec((1, D_MODEL), const),
            pl.BlockSpec((1, D_MODEL), const),
        ],
        out_specs=pl.BlockSpec((TC3, D_MODEL), row),
        out_shape=jax.ShapeDtypeStruct((t_total, D_MODEL), jnp.float32),
        compiler_params=pltpu.CompilerParams(
            dimension_semantics=("arbitrary",),
            vmem_limit_bytes=VMEM_LIMIT),
        name="combine",
    )(h1f, yk, gates, g2, b2)


def kernel(x, w_in, lb_logits, hgrn_norm_g, gmlp_ln_g, gmlp_ln_b, gmlp_ws, gmlp_bs, w_out, ln1_g, ln1_b, router_w, router_b, exp_w1, exp_b1, exp_w2, exp_b2, ln2_g, ln2_b):
    batch, seq, d = x.shape
    assert d == D_MODEL and seq % TB == 0 and w_in.shape[0] == 1
    t_total = batch * seq
    f32 = jnp.float32
    bf16 = jnp.bfloat16

    lb = jnp.cumsum(jax.nn.softmax(lb_logits.astype(f32), axis=0), axis=0)[0:1]
    chunk_id = jnp.arange(GMLP_BLOCK) // CHUNK
    wm = jnp.where((chunk_id[None, :] <= chunk_id[:, None])[None], gmlp_ws[0], 0.0).astype(bf16)
    rwt = router_w[0].T
    rwh = rwt.astype(bf16)
    rwl = (rwt - rwh.astype(f32)).astype(bf16)
    tri = (jnp.arange(CHUNK)[None, :] <= jnp.arange(CHUNK)[:, None]).astype(bf16)
    upp = (jnp.arange(TB)[:, None] < jnp.arange(TB)[None, :]).astype(bf16)

    h1f, h1p, idx, gates, rank, cnt = _mixer(
        x.reshape(t_total, d), w_in[0].astype(bf16), lb, hgrn_norm_g[0:1], gmlp_ln_g[0:1],
        gmlp_ln_b[0:1], wm, gmlp_bs[0].T, w_out[0].astype(bf16), ln1_g[0:1], ln1_b[0:1],
        rwh, rwl, router_b[0][:, None], tri, upp, batch)

    counts = cnt[:, 0].astype(jnp.int32)
    padded = ((counts + TM - 1) // TM) * TM
    ends = jnp.cumsum(padded)
    starts = ends - padded
    n_slots = t_total * TOP_K + N_EXPERTS * TM
    n_tiles = n_slots // TM
    experts = jnp.arange(N_EXPERTS, dtype=jnp.int32)
    start_of = jnp.sum(jnp.where(idx[..., None] == experts, starts, 0), axis=-1)
    dest = start_of + rank
    dest_kt = dest.transpose(1, 0, 2).reshape(TOP_K * t_total)
    dest_tk = dest.transpose(0, 2, 1).reshape(t_total * TOP_K)
    gates_tk = gates.transpose(0, 2, 1).reshape(t_total, TOP_K)

    tile_row = jnp.arange(n_tiles, dtype=jnp.int32) * TM
    tile_expert = jnp.minimum(
        jnp.sum((tile_row[:, None] >= ends[None, :]).astype(jnp.int32), axis=1), N_EXPERTS - 1)
    tile_valid = jnp.clip(starts[tile_expert] + counts[tile_expert] - tile_row, 0, TM)

    xs = _sc_dispatch(h1p, dest_kt, n_slots)

    w1 = exp_w1[0]
    y = _experts(xs, tile_expert, tile_valid.astype(jnp.int32),
                 w1[:, :, 0::2].astype(bf16), w1[:, :, 1::2].astype(bf16), exp_w2[0].astype(bf16),
                 exp_b1[0][:, None, 0::2], exp_b1[0][:, None, 1::2], exp_b2[0][:, None, :])

    yk = _sc_gather(y, dest_tk).reshape(t_total, TOP_K * HALF)
    out = _combine(h1f, yk, gates_tk, ln2_g[0:1], ln2_b[0:1])
    return out.reshape(batch, seq, d)
```

```python
import functools
import math

import jax
import jax.numpy as jnp
from jax import lax
from jax.experimental import pallas as pl
from jax.experimental.pallas import tpu as pltpu
from jax.experimental.pallas import tpu_sc as plsc

D_MODEL = 1024
CHUNK = 64
SUB = 16
N_SUB = CHUNK // SUB
HGRN_WIDTH = 512
HGRN_HEADS = 4
HEAD_DIM = 128
GMLP_WIDTH = 512
GMLP_BLOCK = 128
GMLP_GROUPS = 4
IN_COLS = 3072
N_EXPERTS = 32
TOP_K = 4
SWIGLU_LIMIT = 7.0
SWIGLU_ALPHA = 1.702
ALPHA = 2.0 ** 0.25
EPS = 1e-5
HALF = D_MODEL // 2

TB = 512
TM = 256
TC3 = 512
SC_WINDOW = 128
PERM_BLOCK = 256
VMEM_LIMIT = 56 * 1024 * 1024

_NT = (((1,), (1,)), ((), ()))


def _sigmoid(x):
    z = jnp.exp(-jnp.abs(x))
    r = 1.0 / (1.0 + z)
    return jnp.where(x >= 0, r, z * r)


def _gelu(x):
    return 0.5 * x * (1.0 + lax.erf(x * (1.0 / math.sqrt(2.0))))


def _layer_norm(x, g, b):
    mu = jnp.mean(x, axis=-1, keepdims=True)
    xc = x - mu
    var = jnp.mean(xc * xc, axis=-1, keepdims=True)
    return xc * lax.rsqrt(var + EPS) * g + b


def _pack_rows(h):
    a = h[:, :HALF].astype(jnp.bfloat16).astype(jnp.float32)
    b = h[:, HALF:].astype(jnp.bfloat16).astype(jnp.float32)
    au = lax.bitcast_convert_type(a, jnp.uint32) >> 16
    bu = lax.bitcast_convert_type(b, jnp.uint32) & jnp.uint32(0xFFFF0000)
    return lax.bitcast_convert_type(au | bu, jnp.int32)


def _unpack_rows(w):
    u = lax.bitcast_convert_type(w, jnp.uint32)
    a = lax.bitcast_convert_type(u << 16, jnp.float32)
    b = lax.bitcast_convert_type(u & jnp.uint32(0xFFFF0000), jnp.float32)
    return a, b


def _mixer_kernel(x_ref, win_ref, lb_ref, hg_ref, lng_ref, lnb_ref, wm_ref, bst_ref, wout_ref,
                  l1g_ref, l1b_ref, rwh_ref, rwl_ref, rb_ref, tri_ref, upp_ref,
                  h1f_ref, h1p_ref, idx_ref, gate_ref, rank_ref, cnt_ref,
                  p_scr, lf_scr, kk_scr, g_scr, kc_scr, o_scr, st_scr, carry_scr):
    b = pl.program_id(0)
    t = pl.program_id(1)
    f32 = jnp.float32
    bf16 = jnp.bfloat16

    @pl.when(t == 0)
    def _():
        st_scr[...] = jnp.zeros_like(st_scr)

    @pl.when((b == 0) & (t == 0))
    def _():
        carry_scr[...] = jnp.zeros_like(carry_scr)

    x = x_ref[...]
    p_scr[...] = jnp.dot(x.astype(bf16), win_ref[...], preferred_element_type=f32)

    fl = p_scr[:, HGRN_WIDTH:2 * HGRN_WIDTH]
    lb = lb_ref[...]
    z = jnp.exp(-jnp.abs(fl))
    r = 1.0 / (1.0 + z)
    zr = z * r
    pos = fl >= 0
    lf_scr[...] = jnp.log(lb + (1.0 - lb) * jnp.where(pos, r, zr))
    kk_scr[...] = (1.0 - lb) * jnp.where(pos, zr, r)

    tri = tri_ref[...]
    row64 = lax.broadcasted_iota(jnp.int32, (CHUNK, CHUNK), 0)
    col64 = lax.broadcasted_iota(jnp.int32, (CHUNK, CHUNK), 1)
    lane_sub = lax.broadcasted_iota(jnp.int32, (SUB, CHUNK), 1)
    scale = HEAD_DIM ** -0.5

    def chunk_body(c, carry):
        r0 = pl.multiple_of(c * CHUNK, CHUNK)
        rows = pl.ds(r0, CHUNK)
        lf = lf_scr[rows, :]
        hi = lf.astype(bf16)
        lo = (lf - hi.astype(f32)).astype(bf16)
        gg = jnp.dot(tri, jnp.concatenate([hi, lo], axis=1), preferred_element_type=f32)
        g_all = gg[:, :HGRN_WIDTH] + gg[:, HGRN_WIDTH:]
        g_scr[...] = g_all
        kc_scr[...] = kk_scr[rows, :]
        for h in range(HGRN_HEADS):
            cs = slice(h * HEAD_DIM, (h + 1) * HEAD_DIM)
            q = p_scr[rows, h * HEAD_DIM:(h + 1) * HEAD_DIM]
            v = p_scr[rows, 2 * HGRN_WIDTH + h * HEAD_DIM:2 * HGRN_WIDTH + (h + 1) * HEAD_DIM]
            k = kc_scr[:, cs]
            gh = g_all[:, cs]
            st = st_scr[h]
            vb = v.astype(bf16)

            o_inter = lax.dot_general((q * jnp.exp(gh)).astype(bf16), st.astype(bf16), _NT,
                                      preferred_element_type=f32)

            qts, kts = [], []
            for a in range(1, N_SUB):
                lo_r, hi_r = a * SUB, (a + 1) * SUB
                ra = g_scr[lo_r - 1:lo_r, cs]
                qa = q[lo_r:hi_r] * jnp.exp(gh[lo_r:hi_r] - ra)
                pieces = [jnp.zeros((lo_r, HEAD_DIM), f32), qa]
                if hi_r < CHUNK:
                    pieces.append(jnp.zeros((CHUNK - hi_r, HEAD_DIM), f32))
                qts.append(jnp.concatenate(pieces, axis=0))
                ka = k[:lo_r] * jnp.exp(ra - gh[:lo_r])
                kts.append(jnp.concatenate([ka, jnp.zeros((CHUNK - lo_r, HEAD_DIM), f32)], axis=0))
            a_off = lax.dot_general(jnp.concatenate(qts, axis=1).astype(bf16),
                                    jnp.concatenate(kts, axis=1).astype(bf16), _NT,
                                    preferred_element_type=f32)

            diag_rows = []
            for a in range(N_SUB):
                lo_r = a * SUB
                gs = gh[lo_r:lo_r + SUB]
                qs = q[lo_r:lo_r + SUB]
                blk = jnp.zeros((SUB, CHUNK), f32)
                for jl in range(SUB):
                    j = lo_r + jl
                    gj = g_scr[j:j + 1, cs]
                    kj = kc_scr[j:j + 1, cs]
                    e = jnp.exp(jnp.minimum(gs - gj, 0.0))
                    col = jnp.sum(qs * (kj * e), axis=-1, keepdims=True)
                    blk = jnp.where(lane_sub == j, col, blk)
                diag_rows.append(blk)
            a_diag = jnp.concatenate(diag_rows, axis=0)
            a_mat = a_off + jnp.where(row64 >= col64, a_diag, 0.0)

            o = jnp.dot(a_mat.astype(bf16), vb, preferred_element_type=f32) + o_inter
            o_scr[rows, h * HEAD_DIM:(h + 1) * HEAD_DIM] = o * scale

            gl = g_scr[CHUNK - 1:CHUNK, cs]
            kd = k * jnp.exp(gl - gh)
            upd = jnp.dot(v.T.astype(bf16), kd.astype(bf16), preferred_element_type=f32)
            st_scr[h] = st * jnp.exp(gl) + upd
        return carry

    lax.fori_loop(0, TB // CHUNK, chunk_body, 0)

    o = o_scr[...]
    ms = jnp.mean(o * o, axis=-1, keepdims=True)
    gate = p_scr[:, 3 * HGRN_WIDTH:4 * HGRN_WIDTH]
    y_rec = o * lax.rsqrt(ms + EPS) * hg_ref[...] * (gate * _sigmoid(gate))

    u = _gelu(p_scr[:, 4 * HGRN_WIDTH:4 * HGRN_WIDTH + GMLP_WIDTH])
    vn = _layer_norm(_gelu(p_scr[:, 4 * HGRN_WIDTH + GMLP_WIDTH:]), lng_ref[...], lnb_ref[...])
    vnb = vn.astype(bf16)
    win_rows = []
    for w in range(TB // GMLP_BLOCK):
        cols = []
        for g in range(GMLP_GROUPS):
            vblk = vnb[w * GMLP_BLOCK:(w + 1) * GMLP_BLOCK, g * HEAD_DIM:(g + 1) * HEAD_DIM]
            s = jnp.dot(wm_ref[g], vblk, preferred_element_type=f32) + bst_ref[:, g:g + 1]
            cols.append(s)
        win_rows.append(jnp.concatenate(cols, axis=1))
    y_sg = u * jnp.concatenate(win_rows, axis=0)

    mix = jnp.dot(jnp.concatenate([y_rec, y_sg], axis=1).astype(bf16), wout_ref[...],
                  preferred_element_type=f32)
    h1 = _layer_norm(ALPHA * x + mix, l1g_ref[...], l1b_ref[...])
    h1f_ref[...] = h1
    h1p_ref[...] = _pack_rows(h1)

    hh = h1.astype(bf16)
    hl = (h1 - hh.astype(f32)).astype(bf16)
    rwh = rwh_ref[...]
    logits = (lax.dot_general(rwh, hh, _NT, preferred_element_type=f32)
              + lax.dot_general(rwh, hl, _NT, preferred_element_type=f32)
              + lax.dot_general(rwl_ref[...], hh, _NT, preferred_element_type=f32)
              + rb_ref[...])

    e_iota = lax.broadcasted_iota(jnp.int32, (N_EXPERTS, TB), 0)
    work = logits
    vals, idxs, hots = [], [], []
    for _ in range(TOP_K):
        m = jnp.max(work, axis=0, keepdims=True)
        ik = jnp.min(jnp.where(work == m, e_iota, N_EXPERTS), axis=0, keepdims=True)
        hot = e_iota == ik
        work = jnp.where(hot, -jnp.inf, work)
        vals.append(m)
        idxs.append(ik)
        hots.append(hot)
    exps = [jnp.exp(vk - vals[0]) for vk in vals]
    denom = exps[0] + exps[1] + exps[2] + exps[3]
    gate_ref[0] = jnp.concatenate([ek / denom for ek in exps], axis=0)
    idx_ref[0] = jnp.concatenate(idxs, axis=0)

    hot_any = jnp.where(hots[0] | hots[1] | hots[2] | hots[3], 1.0, 0.0)
    prefix = jnp.dot(hot_any.astype(bf16), upp_ref[...], preferred_element_type=f32)
    base = prefix + carry_scr[:, 0:1]
    ranks = [jnp.sum(jnp.where(hk, base, 0.0), axis=0, keepdims=True) for hk in hots]
    rank_ref[0] = jnp.concatenate(ranks, axis=0).astype(jnp.int32)
    new_carry = carry_scr[...] + jnp.sum(hot_any, axis=1, keepdims=True)
    carry_scr[...] = new_carry
    cnt_ref[...] = new_carry


def _mixer(xt, win, lb, hg, lng, lnb, wm, bst, wout, l1g, l1b, rwh, rwl, rb, tri, upp, batch):
    t_total = xt.shape[0]
    nt = t_total // batch // TB
    nblk = batch * nt
    const2 = lambda b, t: (0, 0)
    const3 = lambda b, t: (0, 0, 0)
    row_blk = lambda b, t: (b * nt + t, 0)
    meta_blk = lambda b, t: (b * nt + t, 0, 0)
    in_specs = [
        pl.BlockSpec((TB, D_MODEL), row_blk),
        pl.BlockSpec((D_MODEL, IN_COLS), const2),
        pl.BlockSpec((1, HGRN_WIDTH), const2),
        pl.BlockSpec((1, HGRN_WIDTH), const2),
        pl.BlockSpec((1, GMLP_WIDTH), const2),
        pl.BlockSpec((1, GMLP_WIDTH), const2),
        pl.BlockSpec((GMLP_GROUPS, GMLP_BLOCK, GMLP_BLOCK), const3),
        pl.BlockSpec((GMLP_BLOCK, GMLP_GROUPS), const2),
        pl.BlockSpec((D_MODEL, D_MODEL), const2),
        pl.BlockSpec((1, D_MODEL), const2),
        pl.BlockSpec((1, D_MODEL), const2),
        pl.BlockSpec((N_EXPERTS, D_MODEL), const2),
        pl.BlockSpec((N_EXPERTS, D_MODEL), const2),
        pl.BlockSpec((N_EXPERTS, 1), const2),
        pl.BlockSpec((CHUNK, CHUNK), const2),
        pl.BlockSpec((TB, TB), const2),
    ]
    out_shape = [
        jax.ShapeDtypeStruct((t_total, D_MODEL), jnp.float32),
        jax.ShapeDtypeStruct((t_total, HALF), jnp.int32),
        jax.ShapeDtypeStruct((nblk, TOP_K, TB), jnp.int32),
        jax.ShapeDtypeStruct((nblk, TOP_K, TB), jnp.float32),
        jax.ShapeDtypeStruct((nblk, TOP_K, TB), jnp.int32),
        jax.ShapeDtypeStruct((N_EXPERTS, 128), jnp.float32),
    ]
    out_specs = [
        pl.BlockSpec((TB, D_MODEL), row_blk),
        pl.BlockSpec((TB, HALF), row_blk),
        pl.BlockSpec((1, TOP_K, TB), meta_blk),
        pl.BlockSpec((1, TOP_K, TB), meta_blk),
        pl.BlockSpec((1, TOP_K, TB), meta_blk),
        pl.BlockSpec((N_EXPERTS, 128), const2),
    ]
    scratch = [
        pltpu.VMEM((TB, IN_COLS), jnp.float32),
        pltpu.VMEM((TB, HGRN_WIDTH), jnp.float32),
        pltpu.VMEM((TB, HGRN_WIDTH), jnp.float32),
        pltpu.VMEM((CHUNK, HGRN_WIDTH), jnp.float32),
        pltpu.VMEM((CHUNK, HGRN_WIDTH), jnp.float32),
        pltpu.VMEM((TB, HGRN_WIDTH), jnp.float32),
        pltpu.VMEM((HGRN_HEADS, HEAD_DIM, HEAD_DIM), jnp.float32),
        pltpu.VMEM((N_EXPERTS, 128), jnp.float32),
    ]
    return pl.pallas_call(
        _mixer_kernel,
        grid=(batch, nt),
        in_specs=in_specs,
        out_specs=out_specs,
        out_shape=out_shape,
        scratch_shapes=scratch,
        compiler_params=pltpu.CompilerParams(
            dimension_semantics=("arbitrary", "arbitrary"),
            vmem_limit_bytes=VMEM_LIMIT),
        name="mixer",
    )(xt, win, lb, hg, lng, lnb, wm, bst, wout, l1g, l1b, rwh, rwl, rb, tri, upp)


def _sc_workers():
    info = plsc.get_sparse_core_info()
    return info.num_cores, info.num_cores * info.num_subcores


def _sc_dispatch(rows, dest, n_out):
    t_total, dw = rows.shape
    nc, nw = _sc_workers()
    per_w = t_total // nw
    mesh = plsc.VectorSubcoreMesh(core_axis_name="c", subcore_axis_name="s")

    @functools.partial(
        pl.kernel,
        out_type=jax.ShapeDtypeStruct((n_out, dw), rows.dtype),
        mesh=mesh,
        scratch_types=[pltpu.VMEM((SC_WINDOW,), jnp.int32) for _ in range(TOP_K)]
        + [pltpu.VMEM((SC_WINDOW, dw), rows.dtype), pltpu.SemaphoreType.DMA],
        name="sc_dispatch",
    )
    def k(x_hbm, i_hbm, o_hbm, i0, i1, i2, i3, rows_v, sem):
        wid = lax.axis_index("s") * nc + lax.axis_index("c")
        idx_bufs = (i0, i1, i2, i3)

        @pl.loop(0, per_w // SC_WINDOW)
        def _(j):
            base = wid * per_w + j * SC_WINDOW
            for kk in range(TOP_K):
                pltpu.sync_copy(i_hbm.at[pl.ds(kk * t_total + base, SC_WINDOW)], idx_bufs[kk])
            pltpu.sync_copy(x_hbm.at[pl.ds(base, SC_WINDOW)], rows_v)
            copies = [pltpu.async_copy(rows_v, o_hbm.at[idx_bufs[kk]], sem) for kk in range(TOP_K)]
            for cp in copies:
                cp.wait()

    return k(rows, dest)


def _sc_gather(table, idx):
    n = idx.shape[0]
    dw = table.shape[1]
    nc, nw = _sc_workers()
    per_w = n // nw
    mesh = plsc.VectorSubcoreMesh(core_axis_name="c", subcore_axis_name="s")

    @functools.partial(
        pl.kernel,
        out_type=jax.ShapeDtypeStruct((n, dw), table.dtype),
        mesh=mesh,
        scratch_types=[pltpu.VMEM((SC_WINDOW,), jnp.int32),
                       pltpu.VMEM((SC_WINDOW, dw), table.dtype),
                       pltpu.SemaphoreType.DMA],
        name="sc_gather",
    )
    def k(t_hbm, i_hbm, o_hbm, idx_v, rows_v, sem):
        wid = lax.axis_index("s") * nc + lax.axis_index("c")

        @pl.loop(0, per_w // SC_WINDOW)
        def _(j):
            base = wid * per_w + j * SC_WINDOW
            pltpu.sync_copy(i_hbm.at[pl.ds(base, SC_WINDOW)], idx_v)
            pltpu.async_copy(t_hbm.at[idx_v], rows_v, sem).wait()
            pltpu.sync_copy(rows_v, o_hbm.at[pl.ds(base, SC_WINDOW)])

    return k(table, idx)


def _expert_kernel(te_ref, tv_ref, xs_ref, w1_ref, w2_ref, b1g_ref, b1l_ref, b2_ref, perm_ref, y_ref,
                   w1g_scr, w1l_scr, w2_scr):
    i = pl.program_id(0)
    valid = tv_ref[i]
    f32 = jnp.float32
    bf16 = jnp.bfloat16
    expert_changed = (i == 0) | (te_ref[i] != te_ref[jnp.maximum(i - 1, 0)])

    @pl.when(valid == 0)
    def _():
        y_ref[...] = jnp.zeros_like(y_ref)

    @pl.when((valid > 0) & expert_changed)
    def _():
        w2_scr[...] = w2_ref[0].astype(bf16)
        perm = perm_ref[...]
        for c in range(2 * D_MODEL // PERM_BLOCK):
            blk = w1_ref[0, :, c * PERM_BLOCK:(c + 1) * PERM_BLOCK].astype(bf16)
            r = jnp.dot(blk, perm, preferred_element_type=f32).astype(bf16)
            half = PERM_BLOCK // 2
            w1g_scr[:, c * half:(c + 1) * half] = r[:, :half]
            w1l_scr[:, c * half:(c + 1) * half] = r[:, half:]

    @pl.when(valid > 0)
    def _():
        a, b = _unpack_rows(xs_ref[...])
        keep = lax.broadcasted_iota(jnp.int32, (TM, 1), 0) < valid
        x = jnp.where(keep, jnp.concatenate([a, b], axis=1), 0.0).astype(bf16)
        hg = jnp.dot(x, w1g_scr[...], preferred_element_type=f32) + b1g_ref[0]
        hl = jnp.dot(x, w1l_scr[...], preferred_element_type=f32) + b1l_ref[0]
        xg = jnp.minimum(hg, SWIGLU_LIMIT)
        xl = jnp.clip(hl, -SWIGLU_LIMIT, SWIGLU_LIMIT)
        act = xg * _sigmoid(SWIGLU_ALPHA * xg) * (xl + 1.0)
        y = jnp.dot(act.astype(bf16), w2_scr[...], preferred_element_type=f32) + b2_ref[0]
        y_ref[...] = _pack_rows(y)


def _experts(xs, tile_expert, tile_valid, w1, w2, b1g, b1l, b2, perm):
    n_slots = xs.shape[0]
    n_tiles = n_slots // TM
    d_ff = w2.shape[1]
    wmap = lambda i, te, tv: (te[i], 0, 0)
    grid_spec = pltpu.PrefetchScalarGridSpec(
        num_scalar_prefetch=2,
        grid=(n_tiles,),
        in_specs=[
            pl.BlockSpec((TM, HALF), lambda i, te, tv: (i, 0)),
            pl.BlockSpec((1, D_MODEL, 2 * d_ff), wmap),
            pl.BlockSpec((1, d_ff, D_MODEL), wmap),
            pl.BlockSpec((1, 1, d_ff), wmap),
            pl.BlockSpec((1, 1, d_ff), wmap),
            pl.BlockSpec((1, 1, D_MODEL), wmap),
            pl.BlockSpec((PERM_BLOCK, PERM_BLOCK), lambda i, te, tv: (0, 0)),
        ],
        out_specs=pl.BlockSpec((TM, HALF), lambda i, te, tv: (i, 0)),
        scratch_shapes=[
            pltpu.VMEM((D_MODEL, d_ff), jnp.bfloat16),
            pltpu.VMEM((D_MODEL, d_ff), jnp.bfloat16),
            pltpu.VMEM((d_ff, D_MODEL), jnp.bfloat16),
        ],
    )
    return pl.pallas_call(
        _expert_kernel,
        grid_spec=grid_spec,
        out_shape=jax.ShapeDtypeStruct((n_slots, HALF), jnp.int32),
        compiler_params=pltpu.CompilerParams(
            dimension_semantics=("arbitrary",),
            vmem_limit_bytes=VMEM_LIMIT),
        name="experts",
    )(tile_expert, tile_valid, xs, w1, w2, b1g, b1l, b2, perm)


def _combine_kernel(h_ref, yk_ref, gate_ref, g_ref, b_ref, o_ref):
    gates = gate_ref[...]
    acc_a = None
    acc_b = None
    for k in range(TOP_K):
        a, b = _unpack_rows(yk_ref[k])
        gk = gates[:, k:k + 1]
        acc_a = gk * a if acc_a is None else acc_a + gk * a
        acc_b = gk * b if acc_b is None else acc_b + gk * b
    ffn = jnp.concatenate([acc_a, acc_b], axis=1)
    o_ref[...] = _layer_norm(ALPHA * h_ref[...] + ffn, g_ref[...], b_ref[...])


def _combine(h1f, yk, gates, g2, b2):
    t_total = h1f.shape[0]
    row = lambda i: (i, 0)
    const = lambda i: (0, 0)
    return pl.pallas_call(
        _combine_kernel,
        grid=(t_total // TC3,),
        in_specs=[
            pl.BlockSpec((TC3, D_MODEL), row),
            pl.BlockSpec((TOP_K, TC3, HALF), lambda i: (0, i, 0)),
            pl.BlockSpec((TC3, TOP_K), row),
            pl.BlockSpec((1, D_MODEL), const),
            pl.BlockSpec((1, D_MODEL), const),
        ],
        out_specs=pl.BlockSpec((TC3, D_MODEL), row),
        out_shape=jax.ShapeDtypeStruct((t_total, D_MODEL), jnp.float32),
        compiler_params=pltpu.CompilerParams(
            dimension_semantics=("arbitrary",),
            vmem_limit_bytes=VMEM_LIMIT),
        name="combine",
    )(h1f, yk, gates, g2, b2)


def kernel(x, w_in, lb_logits, hgrn_norm_g, gmlp_ln_g, gmlp_ln_b, gmlp_ws, gmlp_bs, w_out, ln1_g, ln1_b, router_w, router_b, exp_w1, exp_b1, exp_w2, exp_b2, ln2_g, ln2_b):
    batch, seq, d = x.shape
    assert d == D_MODEL and seq % TB == 0 and w_in.shape[0] == 1
    t_total = batch * seq
    f32 = jnp.float32
    bf16 = jnp.bfloat16

    lb = jnp.cumsum(jax.nn.softmax(lb_logits.astype(f32), axis=0), axis=0)[0:1]
    chunk_id = jnp.arange(GMLP_BLOCK) // CHUNK
    wm = jnp.where((chunk_id[None, :] <= chunk_id[:, None])[None], gmlp_ws[0], 0.0).astype(bf16)
    rwt = router_w[0].T
    rwh = rwt.astype(bf16)
    rwl = (rwt - rwh.astype(f32)).astype(bf16)
    tri = (jnp.arange(CHUNK)[None, :] <= jnp.arange(CHUNK)[:, None]).astype(bf16)
    upp = (jnp.arange(TB)[:, None] < jnp.arange(TB)[None, :]).astype(bf16)

    h1f, h1p, idx, gates, rank, cnt = _mixer(
        x.reshape(t_total, d), w_in[0].astype(bf16), lb, hgrn_norm_g[0:1], gmlp_ln_g[0:1],
        gmlp_ln_b[0:1], wm, gmlp_bs[0].T, w_out[0].astype(bf16), ln1_g[0:1], ln1_b[0:1],
        rwh, rwl, router_b[0][:, None], tri, upp, batch)

    counts = cnt[:, 0].astype(jnp.int32)
    padded = ((counts + TM - 1) // TM) * TM
    ends = jnp.cumsum(padded)
    starts = ends - padded
    n_slots = t_total * TOP_K + N_EXPERTS * TM
    n_tiles = n_slots // TM
    experts = jnp.arange(N_EXPERTS, dtype=jnp.int32)
    start_of = jnp.sum(jnp.where(idx[..., None] == experts, starts, 0), axis=-1)
    dest = start_of + rank
    dest_kt = dest.transpose(1, 0, 2).reshape(TOP_K * t_total)
    gates_tk = gates.transpose(0, 2, 1).reshape(t_total, TOP_K)

    tile_row = jnp.arange(n_tiles, dtype=jnp.int32) * TM
    tile_expert = jnp.minimum(
        jnp.sum((tile_row[:, None] >= ends[None, :]).astype(jnp.int32), axis=1), N_EXPERTS - 1)
    tile_valid = jnp.clip(starts[tile_expert] + counts[tile_expert] - tile_row, 0, TM)

    xs = _sc_dispatch(h1p, dest_kt, n_slots)

    lane = jnp.arange(PERM_BLOCK)
    src = jnp.where(lane < PERM_BLOCK // 2, 2 * lane, 2 * (lane - PERM_BLOCK // 2) + 1)
    perm = (jnp.arange(PERM_BLOCK)[:, None] == src[None, :]).astype(bf16)
    y = _experts(xs, tile_expert, tile_valid.astype(jnp.int32), exp_w1[0], exp_w2[0],
                 exp_b1[0][:, None, 0::2], exp_b1[0][:, None, 1::2], exp_b2[0][:, None, :], perm)

    yk = _sc_gather(y, dest_kt).reshape(TOP_K, t_total, HALF)
    out = _combine(h1f, yk, gates_tk, ln2_g[0:1], ln2_b[0:1])
    return out.reshape(batch, seq, d)
```

```python
import functools
import math

import jax
import jax.numpy as jnp
from jax import lax
from jax.experimental import pallas as pl
from jax.experimental.pallas import tpu as pltpu
from jax.experimental.pallas import tpu_sc as plsc

D_MODEL = 1024
CHUNK = 64
SUB = 16
N_SUB = CHUNK // SUB
MAX_SUB_DECAY = 60.0
HGRN_WIDTH = 512
HGRN_HEADS = 4
HEAD_DIM = 128
GMLP_WIDTH = 512
GMLP_BLOCK = 128
GMLP_GROUPS = 4
IN_COLS = 3072
N_EXPERTS = 32
TOP_K = 4
SWIGLU_LIMIT = 7.0
SWIGLU_ALPHA = 1.702
ALPHA = 2.0 ** 0.25
EPS = 1e-5
HALF = D_MODEL // 2

TB = 512
TM = 512
TC3 = 512
SC_WINDOW = 128
PERM_BLOCK = 256
VMEM_LIMIT = 56 * 1024 * 1024

_NT = (((1,), (1,)), ((), ()))


def _sigmoid(x):
    z = jnp.exp(-jnp.abs(x))
    r = 1.0 / (1.0 + z)
    return jnp.where(x >= 0, r, z * r)


def _gelu(x):
    return 0.5 * x * (1.0 + lax.erf(x * (1.0 / math.sqrt(2.0))))


def _layer_norm(x, g, b):
    mu = jnp.mean(x, axis=-1, keepdims=True)
    xc = x - mu
    var = jnp.mean(xc * xc, axis=-1, keepdims=True)
    return xc * lax.rsqrt(var + EPS) * g + b


def _pack_rows(h):
    a = h[:, :HALF].astype(jnp.bfloat16).astype(jnp.float32)
    b = h[:, HALF:].astype(jnp.bfloat16).astype(jnp.float32)
    au = lax.bitcast_convert_type(a, jnp.uint32) >> 16
    bu = lax.bitcast_convert_type(b, jnp.uint32) & jnp.uint32(0xFFFF0000)
    return lax.bitcast_convert_type(au | bu, jnp.int32)


def _unpack_rows(w):
    u = lax.bitcast_convert_type(w, jnp.uint32)
    a = lax.bitcast_convert_type(u << 16, jnp.float32)
    b = lax.bitcast_convert_type(u & jnp.uint32(0xFFFF0000), jnp.float32)
    return a, b


def _mixer_kernel(x_ref, win_ref, lb_ref, hg_ref, lng_ref, lnb_ref, wm_ref, bst_ref, wout_ref,
                  l1g_ref, l1b_ref, rwh_ref, rwl_ref, rb_ref, tri_ref, upp_ref,
                  h1f_ref, h1p_ref, idx_ref, gate_ref, rank_ref, cnt_ref,
                  p_scr, lf_scr, kk_scr, g_scr, kc_scr, o_scr, st_scr, carry_scr,
                  amat_scr, qg_scr, kd_scr, upd_scr):
    b = pl.program_id(0)
    t = pl.program_id(1)
    f32 = jnp.float32
    bf16 = jnp.bfloat16

    @pl.when(t == 0)
    def _():
        st_scr[...] = jnp.zeros_like(st_scr)

    @pl.when((b == 0) & (t == 0))
    def _():
        carry_scr[...] = jnp.zeros_like(carry_scr)

    x = x_ref[...]
    p_scr[...] = jnp.dot(x.astype(bf16), win_ref[...], preferred_element_type=f32)

    fl = p_scr[:, HGRN_WIDTH:2 * HGRN_WIDTH]
    lb = lb_ref[...]
    z = jnp.exp(-jnp.abs(fl))
    r = 1.0 / (1.0 + z)
    zr = z * r
    pos = fl >= 0
    lf_scr[...] = jnp.log(lb + (1.0 - lb) * jnp.where(pos, r, zr))
    kk_scr[...] = (1.0 - lb) * jnp.where(pos, zr, r)

    tri = tri_ref[...]
    row64 = lax.broadcasted_iota(jnp.int32, (CHUNK, CHUNK), 0)
    col64 = lax.broadcasted_iota(jnp.int32, (CHUNK, CHUNK), 1)
    lane_sub = lax.broadcasted_iota(jnp.int32, (SUB, CHUNK), 1)
    scale = HEAD_DIM ** -0.5

    def pad_rows(piece, lo_r):
        parts = []
        if lo_r > 0:
            parts.append(jnp.zeros((lo_r, HEAD_DIM), f32))
        parts.append(piece)
        rest = CHUNK - lo_r - piece.shape[0]
        if rest > 0:
            parts.append(jnp.zeros((rest, HEAD_DIM), f32))
        return jnp.concatenate(parts, axis=0) if len(parts) > 1 else piece

    def intra_factorised(q, k, gh):
        qts, kts = [], []
        for a in range(N_SUB):
            lo_r, hi_r = a * SUB, (a + 1) * SUB
            if a == 0:
                qa = q[:hi_r] * jnp.exp(gh[:hi_r])
                ka = k[:hi_r] * jnp.exp(-gh[:hi_r])
            else:
                ra = gh[lo_r - 1:lo_r]
                qa = q[lo_r:hi_r] * jnp.exp(gh[lo_r:hi_r] - ra)
                ka = k[:hi_r] * jnp.exp(ra - gh[:hi_r])
            qts.append(pad_rows(qa, lo_r))
            kts.append(pad_rows(ka, 0))
        a_mat = lax.dot_general(jnp.concatenate(qts, axis=1).astype(bf16),
                                jnp.concatenate(kts, axis=1).astype(bf16), _NT,
                                preferred_element_type=f32)
        return jnp.where(row64 >= col64, a_mat, 0.0)

    def chunks_factorised():
        n_chunks = TB // CHUNK
        heads = [(c, h) for c in range(n_chunks) for h in range(HGRN_HEADS)]

        def cols(h, base=0):
            return slice(base + h * HEAD_DIM, base + (h + 1) * HEAD_DIM)

        for c in range(n_chunks):
            rows = slice(c * CHUNK, (c + 1) * CHUNK)
            lf = lf_scr[rows, :]
            hi = lf.astype(bf16)
            lo = (lf - hi.astype(f32)).astype(bf16)
            gg = jnp.dot(tri, jnp.concatenate([hi, lo], axis=1), preferred_element_type=f32)
            lf_scr[rows, :] = gg[:, :HGRN_WIDTH] + gg[:, HGRN_WIDTH:]
        for c, h in heads:
            rows = slice(c * CHUNK, (c + 1) * CHUNK)
            q = p_scr[rows, cols(h)]
            k = kk_scr[rows, cols(h)]
            gh = lf_scr[rows, cols(h)]
            amat_scr[c * HGRN_HEADS + h] = intra_factorised(q, k, gh).astype(bf16)
            qg_scr[rows, cols(h)] = (q * jnp.exp(gh)).astype(bf16)
            kd_scr[rows, cols(h)] = (k * jnp.exp(gh[CHUNK - 1:CHUNK] - gh)).astype(bf16)
        for c, h in heads:
            rows = slice(c * CHUNK, (c + 1) * CHUNK)
            v = p_scr[rows, cols(h, 2 * HGRN_WIDTH)]
            o_scr[rows, cols(h)] = jnp.dot(amat_scr[c * HGRN_HEADS + h], v.astype(bf16),
                                           preferred_element_type=f32)
            upd_scr[c * HGRN_HEADS + h] = jnp.dot(v.T.astype(bf16), kd_scr[rows, cols(h)],
                                                  preferred_element_type=f32)
        states = [st_scr[h] for h in range(HGRN_HEADS)]
        for c, h in heads:
            rows = slice(c * CHUNK, (c + 1) * CHUNK)
            st = states[h]
            o_inter = lax.dot_general(qg_scr[rows, cols(h)], st.astype(bf16), _NT,
                                      preferred_element_type=f32)
            o_scr[rows, cols(h)] = (o_scr[rows, cols(h)] + o_inter) * scale
            gl = lf_scr[(c + 1) * CHUNK - 1:(c + 1) * CHUNK, cols(h)]
            states[h] = st * jnp.exp(gl) + upd_scr[c * HGRN_HEADS + h]
        for h in range(HGRN_HEADS):
            st_scr[h] = states[h]

    def intra_exact_diagonal(q, k, gh, cs):
        qts, kts = [], []
        for a in range(1, N_SUB):
            lo_r, hi_r = a * SUB, (a + 1) * SUB
            ra = g_scr[lo_r - 1:lo_r, cs]
            qts.append(pad_rows(q[lo_r:hi_r] * jnp.exp(gh[lo_r:hi_r] - ra), lo_r))
            kts.append(pad_rows(k[:lo_r] * jnp.exp(ra - gh[:lo_r]), 0))
        a_off = lax.dot_general(jnp.concatenate(qts, axis=1).astype(bf16),
                                jnp.concatenate(kts, axis=1).astype(bf16), _NT,
                                preferred_element_type=f32)
        diag_rows = []
        for a in range(N_SUB):
            lo_r = a * SUB
            gs = gh[lo_r:lo_r + SUB]
            qs = q[lo_r:lo_r + SUB]
            blk = jnp.zeros((SUB, CHUNK), f32)
            for jl in range(SUB):
                j = lo_r + jl
                gj = g_scr[j:j + 1, cs]
                kj = kc_scr[j:j + 1, cs]
                e = jnp.exp(jnp.minimum(gs - gj, 0.0))
                col = jnp.sum(qs * (kj * e), axis=-1, keepdims=True)
                blk = jnp.where(lane_sub == j, col, blk)
            diag_rows.append(blk)
        a_diag = jnp.concatenate(diag_rows, axis=0)
        return a_off + jnp.where(row64 >= col64, a_diag, 0.0)

    def make_chunk_body(intra):
        def chunk_body(c, carry):
            r0 = pl.multiple_of(c * CHUNK, CHUNK)
            rows = pl.ds(r0, CHUNK)
            lf = lf_scr[rows, :]
            hi = lf.astype(bf16)
            lo = (lf - hi.astype(f32)).astype(bf16)
            gg = jnp.dot(tri, jnp.concatenate([hi, lo], axis=1), preferred_element_type=f32)
            g_all = gg[:, :HGRN_WIDTH] + gg[:, HGRN_WIDTH:]
            g_scr[...] = g_all
            kc_scr[...] = kk_scr[rows, :]
            for h in range(HGRN_HEADS):
                cs = slice(h * HEAD_DIM, (h + 1) * HEAD_DIM)
                q = p_scr[rows, h * HEAD_DIM:(h + 1) * HEAD_DIM]
                v = p_scr[rows, 2 * HGRN_WIDTH + h * HEAD_DIM:2 * HGRN_WIDTH + (h + 1) * HEAD_DIM]
                k = kc_scr[:, cs]
                gh = g_all[:, cs]
                st = st_scr[h]
                o_inter = lax.dot_general((q * jnp.exp(gh)).astype(bf16), st.astype(bf16), _NT,
                                          preferred_element_type=f32)
                a_mat = intra(q, k, gh, cs)
                o = jnp.dot(a_mat.astype(bf16), v.astype(bf16), preferred_element_type=f32) + o_inter
                o_scr[rows, h * HEAD_DIM:(h + 1) * HEAD_DIM] = o * scale

                gl = g_scr[CHUNK - 1:CHUNK, cs]
                kd = k * jnp.exp(gl - gh)
                upd = jnp.dot(v.T.astype(bf16), kd.astype(bf16), preferred_element_type=f32)
                st_scr[h] = st * jnp.exp(gl) + upd
            return carry
        return chunk_body

    lf_all = lf_scr[...]
    sub_decay = -jnp.sum(lf_all.reshape(TB // SUB, SUB, HGRN_WIDTH), axis=1)
    bounded = jnp.max(sub_decay) <= MAX_SUB_DECAY

    @pl.when(bounded)
    def _():
        chunks_factorised()

    @pl.when(jnp.logical_not(bounded))
    def _():
        lax.fori_loop(0, TB // CHUNK, make_chunk_body(intra_exact_diagonal), 0)

    o = o_scr[...]
    ms = jnp.mean(o * o, axis=-1, keepdims=True)
    gate = p_scr[:, 3 * HGRN_WIDTH:4 * HGRN_WIDTH]
    y_rec = o * lax.rsqrt(ms + EPS) * hg_ref[...] * (gate * _sigmoid(gate))

    u = _gelu(p_scr[:, 4 * HGRN_WIDTH:4 * HGRN_WIDTH + GMLP_WIDTH])
    vn = _layer_norm(_gelu(p_scr[:, 4 * HGRN_WIDTH + GMLP_WIDTH:]), lng_ref[...], lnb_ref[...])
    vnb = vn.astype(bf16)
    win_rows = []
    for w in range(TB // GMLP_BLOCK):
        cols = []
        for g in range(GMLP_GROUPS):
            vblk = vnb[w * GMLP_BLOCK:(w + 1) * GMLP_BLOCK, g * HEAD_DIM:(g + 1) * HEAD_DIM]
            s = jnp.dot(wm_ref[g], vblk, preferred_element_type=f32) + bst_ref[:, g:g + 1]
            cols.append(s)
        win_rows.append(jnp.concatenate(cols, axis=1))
    y_sg = u * jnp.concatenate(win_rows, axis=0)

    mix = jnp.dot(jnp.concatenate([y_rec, y_sg], axis=1).astype(bf16), wout_ref[...],
                  preferred_element_type=f32)
    h1 = _layer_norm(ALPHA * x + mix, l1g_ref[...], l1b_ref[...])
    h1f_ref[...] = h1
    h1p_ref[...] = _pack_rows(h1)

    hh = h1.astype(bf16)
    hl = (h1 - hh.astype(f32)).astype(bf16)
    rwh = rwh_ref[...]
    logits = (lax.dot_general(rwh, hh, _NT, preferred_element_type=f32)
              + lax.dot_general(rwh, hl, _NT, preferred_element_type=f32)
              + lax.dot_general(rwl_ref[...], hh, _NT, preferred_element_type=f32)
              + rb_ref[...])

    e_iota = lax.broadcasted_iota(jnp.int32, (N_EXPERTS, TB), 0)
    work = logits
    vals, idxs, hots = [], [], []
    for _ in range(TOP_K):
        m = jnp.max(work, axis=0, keepdims=True)
        ik = jnp.min(jnp.where(work == m, e_iota, N_EXPERTS), axis=0, keepdims=True)
        hot = e_iota == ik
        work = jnp.where(hot, -jnp.inf, work)
        vals.append(m)
        idxs.append(ik)
        hots.append(hot)
    exps = [jnp.exp(vk - vals[0]) for vk in vals]
    denom = exps[0] + exps[1] + exps[2] + exps[3]
    gate_ref[0] = jnp.concatenate([ek / denom for ek in exps], axis=0)
    idx_ref[0] = jnp.concatenate(idxs, axis=0)

    hot_any = jnp.where(hots[0] | hots[1] | hots[2] | hots[3], 1.0, 0.0)
    prefix = jnp.dot(hot_any.astype(bf16), upp_ref[...], preferred_element_type=f32)
    base = prefix + carry_scr[:, 0:1]
    ranks = [jnp.sum(jnp.where(hk, base, 0.0), axis=0, keepdims=True) for hk in hots]
    rank_ref[0] = jnp.concatenate(ranks, axis=0).astype(jnp.int32)
    new_carry = carry_scr[...] + jnp.sum(hot_any, axis=1, keepdims=True)
    carry_scr[...] = new_carry
    cnt_ref[...] = new_carry


def _mixer(xt, win, lb, hg, lng, lnb, wm, bst, wout, l1g, l1b, rwh, rwl, rb, tri, upp, batch):
    t_total = xt.shape[0]
    nt = t_total // batch // TB
    nblk = batch * nt
    const2 = lambda b, t: (0, 0)
    const3 = lambda b, t: (0, 0, 0)
    row_blk = lambda b, t: (b * nt + t, 0)
    meta_blk = lambda b, t: (b * nt + t, 0, 0)
    in_specs = [
        pl.BlockSpec((TB, D_MODEL), row_blk),
        pl.BlockSpec((D_MODEL, IN_COLS), const2),
        pl.BlockSpec((1, HGRN_WIDTH), const2),
        pl.BlockSpec((1, HGRN_WIDTH), const2),
        pl.BlockSpec((1, GMLP_WIDTH), const2),
        pl.BlockSpec((1, GMLP_WIDTH), const2),
        pl.BlockSpec((GMLP_GROUPS, GMLP_BLOCK, GMLP_BLOCK), const3),
        pl.BlockSpec((GMLP_BLOCK, GMLP_GROUPS), const2),
        pl.BlockSpec((D_MODEL, D_MODEL), const2),
        pl.BlockSpec((1, D_MODEL), const2),
        pl.BlockSpec((1, D_MODEL), const2),
        pl.BlockSpec((N_EXPERTS, D_MODEL), const2),
        pl.BlockSpec((N_EXPERTS, D_MODEL), const2),
        pl.BlockSpec((N_EXPERTS, 1), const2),
        pl.BlockSpec((CHUNK, CHUNK), const2),
        pl.BlockSpec((TB, TB), const2),
    ]
    out_shape = [
        jax.ShapeDtypeStruct((t_total, D_MODEL), jnp.float32),
        jax.ShapeDtypeStruct((t_total, HALF), jnp.int32),
        jax.ShapeDtypeStruct((nblk, TOP_K, TB), jnp.int32),
        jax.ShapeDtypeStruct((nblk, TOP_K, TB), jnp.float32),
        jax.ShapeDtypeStruct((nblk, TOP_K, TB), jnp.int32),
        jax.ShapeDtypeStruct((N_EXPERTS, 128), jnp.float32),
    ]
    out_specs = [
        pl.BlockSpec((TB, D_MODEL), row_blk),
        pl.BlockSpec((TB, HALF), row_blk),
        pl.BlockSpec((1, TOP_K, TB), meta_blk),
        pl.BlockSpec((1, TOP_K, TB), meta_blk),
        pl.BlockSpec((1, TOP_K, TB), meta_blk),
        pl.BlockSpec((N_EXPERTS, 128), const2),
    ]
    scratch = [
        pltpu.VMEM((TB, IN_COLS), jnp.float32),
        pltpu.VMEM((TB, HGRN_WIDTH), jnp.float32),
        pltpu.VMEM((TB, HGRN_WIDTH), jnp.float32),
        pltpu.VMEM((CHUNK, HGRN_WIDTH), jnp.float32),
        pltpu.VMEM((CHUNK, HGRN_WIDTH), jnp.float32),
        pltpu.VMEM((TB, HGRN_WIDTH), jnp.float32),
        pltpu.VMEM((HGRN_HEADS, HEAD_DIM, HEAD_DIM), jnp.float32),
        pltpu.VMEM((N_EXPERTS, 128), jnp.float32),
        pltpu.VMEM((TB // CHUNK * HGRN_HEADS, CHUNK, CHUNK), jnp.bfloat16),
        pltpu.VMEM((TB, HGRN_WIDTH), jnp.bfloat16),
        pltpu.VMEM((TB, HGRN_WIDTH), jnp.bfloat16),
        pltpu.VMEM((TB // CHUNK * HGRN_HEADS, HEAD_DIM, HEAD_DIM), jnp.float32),
    ]
    return pl.pallas_call(
        _mixer_kernel,
        grid=(batch, nt),
        in_specs=in_specs,
        out_specs=out_specs,
        out_shape=out_shape,
        scratch_shapes=scratch,
        compiler_params=pltpu.CompilerParams(
            dimension_semantics=("arbitrary", "arbitrary"),
            vmem_limit_bytes=VMEM_LIMIT),
        name="mixer",
    )(xt, win, lb, hg, lng, lnb, wm, bst, wout, l1g, l1b, rwh, rwl, rb, tri, upp)


def _sc_workers():
    info = plsc.get_sparse_core_info()
    return info.num_cores, info.num_cores * info.num_subcores


def _sc_dispatch(rows, dest, n_out):
    t_total, dw = rows.shape
    nc, nw = _sc_workers()
    per_w = t_total // nw
    mesh = plsc.VectorSubcoreMesh(core_axis_name="c", subcore_axis_name="s")

    @functools.partial(
        pl.kernel,
        out_type=jax.ShapeDtypeStruct((n_out, dw), rows.dtype),
        mesh=mesh,
        scratch_types=[pltpu.VMEM((SC_WINDOW,), jnp.int32) for _ in range(TOP_K)]
        + [pltpu.VMEM((SC_WINDOW, dw), rows.dtype), pltpu.SemaphoreType.DMA],
        name="sc_dispatch",
    )
    def k(x_hbm, i_hbm, o_hbm, i0, i1, i2, i3, rows_v, sem):
        wid = lax.axis_index("s") * nc + lax.axis_index("c")
        idx_bufs = (i0, i1, i2, i3)

        @pl.loop(0, per_w // SC_WINDOW)
        def _(j):
            base = wid * per_w + j * SC_WINDOW
            for kk in range(TOP_K):
                pltpu.sync_copy(i_hbm.at[pl.ds(kk * t_total + base, SC_WINDOW)], idx_bufs[kk])
            pltpu.sync_copy(x_hbm.at[pl.ds(base, SC_WINDOW)], rows_v)
            copies = [pltpu.async_copy(rows_v, o_hbm.at[idx_bufs[kk]], sem) for kk in range(TOP_K)]
            for cp in copies:
                cp.wait()

    return k(rows, dest)


def _sc_gather(table, idx):
    n = idx.shape[0]
    dw = table.shape[1]
    nc, nw = _sc_workers()
    per_w = n // nw
    mesh = plsc.VectorSubcoreMesh(core_axis_name="c", subcore_axis_name="s")

    @functools.partial(
        pl.kernel,
        out_type=jax.ShapeDtypeStruct((n, dw), table.dtype),
        mesh=mesh,
        scratch_types=[pltpu.VMEM((SC_WINDOW,), jnp.int32),
                       pltpu.VMEM((SC_WINDOW, dw), table.dtype),
                       pltpu.SemaphoreType.DMA],
        name="sc_gather",
    )
    def k(t_hbm, i_hbm, o_hbm, idx_v, rows_v, sem):
        wid = lax.axis_index("s") * nc + lax.axis_index("c")

        @pl.loop(0, per_w // SC_WINDOW)
        def _(j):
            base = wid * per_w + j * SC_WINDOW
            pltpu.sync_copy(i_hbm.at[pl.ds(base, SC_WINDOW)], idx_v)
            pltpu.async_copy(t_hbm.at[idx_v], rows_v, sem).wait()
            pltpu.sync_copy(rows_v, o_hbm.at[pl.ds(base, SC_WINDOW)])

    return k(table, idx)


def _expert_kernel(te_ref, tv_ref, xs_ref, w1_ref, w2_ref, b1g_ref, b1l_ref, b2_ref, perm_ref, y_ref,
                   w1g_scr, w1l_scr, w2_scr):
    i = pl.program_id(0)
    valid = tv_ref[i]
    f32 = jnp.float32
    bf16 = jnp.bfloat16
    expert_changed = (i == 0) | (te_ref[i] != te_ref[jnp.maximum(i - 1, 0)])

    @pl.when(valid == 0)
    def _():
        y_ref[...] = jnp.zeros_like(y_ref)

    @pl.when((valid > 0) & expert_changed)
    def _():
        w2_scr[...] = w2_ref[0].astype(bf16)
        perm = perm_ref[...]
        for c in range(2 * D_MODEL // PERM_BLOCK):
            blk = w1_ref[0, :, c * PERM_BLOCK:(c + 1) * PERM_BLOCK].astype(bf16)
            r = jnp.dot(blk, perm, preferred_element_type=f32).astype(bf16)
            half = PERM_BLOCK // 2
            w1g_scr[:, c * half:(c + 1) * half] = r[:, :half]
            w1l_scr[:, c * half:(c + 1) * half] = r[:, half:]

    @pl.when(valid > 0)
    def _():
        a, b = _unpack_rows(xs_ref[...])
        keep = lax.broadcasted_iota(jnp.int32, (TM, 1), 0) < valid
        x = jnp.where(keep, jnp.concatenate([a, b], axis=1), 0.0).astype(bf16)
        hg = jnp.dot(x, w1g_scr[...], preferred_element_type=f32) + b1g_ref[0]
        hl = jnp.dot(x, w1l_scr[...], preferred_element_type=f32) + b1l_ref[0]
        xg = jnp.minimum(hg, SWIGLU_LIMIT)
        xl = jnp.clip(hl, -SWIGLU_LIMIT, SWIGLU_LIMIT)
        act = xg * _sigmoid(SWIGLU_ALPHA * xg) * (xl + 1.0)
        y = jnp.dot(act.astype(bf16), w2_scr[...], preferred_element_type=f32) + b2_ref[0]
        y_ref[...] = _pack_rows(y)


def _experts(xs, tile_expert, tile_valid, w1, w2, b1g, b1l, b2, perm):
    n_slots = xs.shape[0]
    n_tiles = n_slots // TM
    d_ff = w2.shape[1]
    wmap = lambda i, te, tv: (te[i], 0, 0)
    grid_spec = pltpu.PrefetchScalarGridSpec(
        num_scalar_prefetch=2,
        grid=(n_tiles,),
        in_specs=[
            pl.BlockSpec((TM, HALF), lambda i, te, tv: (i, 0)),
            pl.BlockSpec((1, D_MODEL, 2 * d_ff), wmap),
            pl.BlockSpec((1, d_ff, D_MODEL), wmap),
            pl.BlockSpec((1, 1, d_ff), wmap),
            pl.BlockSpec((1, 1, d_ff), wmap),
            pl.BlockSpec((1, 1, D_MODEL), wmap),
            pl.BlockSpec((PERM_BLOCK, PERM_BLOCK), lambda i, te, tv: (0, 0)),
        ],
        out_specs=pl.BlockSpec((TM, HALF), lambda i, te, tv: (i, 0)),
        scratch_shapes=[
            pltpu.VMEM((D_MODEL, d_ff), jnp.bfloat16),
            pltpu.VMEM((D_MODEL, d_ff), jnp.bfloat16),
            pltpu.VMEM((d_ff, D_MODEL), jnp.bfloat16),
        ],
    )
    return pl.pallas_call(
        _expert_kernel,
        grid_spec=grid_spec,
        out_shape=jax.ShapeDtypeStruct((n_slots, HALF), jnp.int32),
        compiler_params=pltpu.CompilerParams(
            dimension_semantics=("arbitrary",),
            vmem_limit_bytes=VMEM_LIMIT),
        name="experts",
    )(tile_expert, tile_valid, xs, w1, w2, b1g, b1l, b2, perm)


def _combine_kernel(h_ref, yk_ref, gate_ref, g_ref, b_ref, o_ref):
    gates = gate_ref[...]
    acc_a = None
    acc_b = None
    for k in range(TOP_K):
        a, b = _unpack_rows(yk_ref[k])
        gk = gates[:, k:k + 1]
        acc_a = gk * a if acc_a is None else acc_a + gk * a
        acc_b = gk * b if acc_b is None else acc_b + gk * b
    ffn = jnp.concatenate([acc_a, acc_b], axis=1)
    o_ref[...] = _layer_norm(ALPHA * h_ref[...] + ffn, g_ref[...], b_ref[...])


def _combine(h1f, yk, gates, g2, b2):
    t_total = h1f.shape[0]
    row = lambda i: (i, 0)
    const = lambda i: (0, 0)
    return pl.pallas_call(
        _combine_kernel,
        grid=(t_total // TC3,),
        in_specs=[
            pl.BlockSpec((TC3, D_MODEL), row),
            pl.BlockSpec((TOP_K, TC3, HALF), lambda i: (0, i, 0)),
            pl.BlockSpec((TC3, TOP_K), row),
            pl.BlockSpec((1, D_MODEL), const),
            pl.BlockSpec((1, D_MODEL), const),
        ],
        out_specs=pl.BlockSpec((TC3, D_MODEL), row),
        out_shape=jax.ShapeDtypeStruct((t_total, D_MODEL), jnp.float32),
        compiler_params=pltpu.CompilerParams(
            dimension_semantics=("arbitrary",),
            vmem_limit_bytes=VMEM_LIMIT),
        name="combine",
    )(h1f, yk, gates, g2, b2)


def kernel(x, w_in, lb_logits, hgrn_norm_g, gmlp_ln_g, gmlp_ln_b, gmlp_ws, gmlp_bs, w_out, ln1_g, ln1_b, router_w, router_b, exp_w1, exp_b1, exp_w2, exp_b2, ln2_g, ln2_b):
    batch, seq, d = x.shape
    assert d == D_MODEL and seq % TB == 0 and w_in.shape[0] == 1
    t_total = batch * seq
    f32 = jnp.float32
    bf16 = jnp.bfloat16

    lb = jnp.cumsum(jax.nn.softmax(lb_logits.astype(f32), axis=0), axis=0)[0:1]
    chunk_id = jnp.arange(GMLP_BLOCK) // CHUNK
    wm = jnp.where((chunk_id[None, :] <= chunk_id[:, None])[None], gmlp_ws[0], 0.0).astype(bf16)
    rwt = router_w[0].T
    rwh = rwt.astype(bf16)
    rwl = (rwt - rwh.astype(f32)).astype(bf16)
    tri = (jnp.arange(CHUNK)[None, :] <= jnp.arange(CHUNK)[:, None]).astype(bf16)
    upp = (jnp.arange(TB)[:, None] < jnp.arange(TB)[None, :]).astype(bf16)

    h1f, h1p, idx, gates, rank, cnt = _mixer(
        x.reshape(t_total, d), w_in[0].astype(bf16), lb, hgrn_norm_g[0:1], gmlp_ln_g[0:1],
        gmlp_ln_b[0:1], wm, gmlp_bs[0].T, w_out[0].astype(bf16), ln1_g[0:1], ln1_b[0:1],
        rwh, rwl, router_b[0][:, None], tri, upp, batch)

    counts = cnt[:, 0].astype(jnp.int32)
    padded = ((counts + TM - 1) // TM) * TM
    ends = jnp.cumsum(padded)
    starts = ends - padded
    n_slots = t_total * TOP_K + N_EXPERTS * TM
    n_tiles = n_slots // TM
    experts = jnp.arange(N_EXPERTS, dtype=jnp.int32)
    start_of = jnp.sum(jnp.where(idx[..., None] == experts, starts, 0), axis=-1)
    dest = start_of + rank
    dest_kt = dest.transpose(1, 0, 2).reshape(TOP_K * t_total)
    gates_tk = gates.transpose(0, 2, 1).reshape(t_total, TOP_K)

    tile_row = jnp.arange(n_tiles, dtype=jnp.int32) * TM
    tile_expert = jnp.minimum(
        jnp.sum((tile_row[:, None] >= ends[None, :]).astype(jnp.int32), axis=1), N_EXPERTS - 1)
    tile_valid = jnp.clip(starts[tile_expert] + counts[tile_expert] - tile_row, 0, TM)

    xs = _sc_dispatch(h1p, dest_kt, n_slots)

    lane = jnp.arange(PERM_BLOCK)
    src = jnp.where(lane < PERM_BLOCK // 2, 2 * lane, 2 * (lane - PERM_BLOCK // 2) + 1)
    perm = (jnp.arange(PERM_BLOCK)[:, None] == src[None, :]).astype(bf16)
    y = _experts(xs, tile_expert, tile_valid.astype(jnp.int32), exp_w1[0], exp_w2[0],
                 exp_b1[0][:, None, 0::2], exp_b1[0][:, None, 1::2], exp_b2[0][:, None, :], perm)

    yk = _sc_gather(y, dest_kt).reshape(TOP_K, t_total, HALF)
    out = _combine(h1f, yk, gates_tk, ln2_g[0:1], ln2_b[0:1])
    return out.reshape(batch, seq, d)
```

```python
import functools
import math

import jax
import jax.numpy as jnp
from jax import lax
from jax.experimental import pallas as pl
from jax.experimental.pallas import tpu as pltpu
from jax.experimental.pallas import tpu_sc as plsc

D_MODEL = 1024
CHUNK = 64
SUB = 16
N_SUB = CHUNK // SUB
MAX_SUB_DECAY = 60.0
HGRN_WIDTH = 512
HGRN_HEADS = 4
HEAD_DIM = 128
GMLP_WIDTH = 512
GMLP_BLOCK = 128
GMLP_GROUPS = 4
IN_COLS = 3072
N_EXPERTS = 32
TOP_K = 4
SWIGLU_LIMIT = 7.0
SWIGLU_ALPHA = 1.702
ALPHA = 2.0 ** 0.25
EPS = 1e-5
HALF = D_MODEL // 2

TB = 512
TM = 512
TC3 = 512
N_PARTS = 2
SC_WINDOW = 128
PERM_BLOCK = 256
VMEM_LIMIT = 56 * 1024 * 1024

_NT = (((1,), (1,)), ((), ()))


def _sigmoid(x):
    z = jnp.exp(-jnp.abs(x))
    r = 1.0 / (1.0 + z)
    return jnp.where(x >= 0, r, z * r)


def _gelu(x):
    return 0.5 * x * (1.0 + lax.erf(x * (1.0 / math.sqrt(2.0))))


def _layer_norm(x, g, b):
    mu = jnp.mean(x, axis=-1, keepdims=True)
    xc = x - mu
    var = jnp.mean(xc * xc, axis=-1, keepdims=True)
    return xc * lax.rsqrt(var + EPS) * g + b


def _pack_rows(h):
    a = h[:, :HALF].astype(jnp.bfloat16).astype(jnp.float32)
    b = h[:, HALF:].astype(jnp.bfloat16).astype(jnp.float32)
    au = lax.bitcast_convert_type(a, jnp.uint32) >> 16
    bu = lax.bitcast_convert_type(b, jnp.uint32) & jnp.uint32(0xFFFF0000)
    return lax.bitcast_convert_type(au | bu, jnp.int32)


def _unpack_rows(w):
    u = lax.bitcast_convert_type(w, jnp.uint32)
    a = lax.bitcast_convert_type(u << 16, jnp.float32)
    b = lax.bitcast_convert_type(u & jnp.uint32(0xFFFF0000), jnp.float32)
    return a, b


def _mixer_kernel(x_ref, win_ref, lb_ref, hg_ref, lng_ref, lnb_ref, wm_ref, bst_ref, wout_ref,
                  l1g_ref, l1b_ref, rwh_ref, rwl_ref, rb_ref, tri_ref, upp_ref,
                  h1f_ref, h1p_ref, idx_ref, gate_ref, rank_ref, cnt_ref,
                  p_scr, lf_scr, kk_scr, g_scr, kc_scr, o_scr, st_scr, carry_scr,
                  amat_scr, qg_scr, kd_scr, upd_scr):
    b = pl.program_id(0)
    t = pl.program_id(1)
    f32 = jnp.float32
    bf16 = jnp.bfloat16

    @pl.when(t == 0)
    def _():
        st_scr[...] = jnp.zeros_like(st_scr)

    @pl.when((b == 0) & (t == 0))
    def _():
        carry_scr[...] = jnp.zeros_like(carry_scr)

    x = x_ref[...]
    p_scr[...] = jnp.dot(x.astype(bf16), win_ref[...], preferred_element_type=f32)

    fl = p_scr[:, HGRN_WIDTH:2 * HGRN_WIDTH]
    lb = lb_ref[...]
    z = jnp.exp(-jnp.abs(fl))
    r = 1.0 / (1.0 + z)
    zr = z * r
    pos = fl >= 0
    lf_scr[...] = jnp.log(lb + (1.0 - lb) * jnp.where(pos, r, zr))
    kk_scr[...] = (1.0 - lb) * jnp.where(pos, zr, r)

    tri = tri_ref[...]
    row64 = lax.broadcasted_iota(jnp.int32, (CHUNK, CHUNK), 0)
    col64 = lax.broadcasted_iota(jnp.int32, (CHUNK, CHUNK), 1)
    lane_sub = lax.broadcasted_iota(jnp.int32, (SUB, CHUNK), 1)
    scale = HEAD_DIM ** -0.5

    def pad_rows(piece, lo_r):
        parts = []
        if lo_r > 0:
            parts.append(jnp.zeros((lo_r, HEAD_DIM), f32))
        parts.append(piece)
        rest = CHUNK - lo_r - piece.shape[0]
        if rest > 0:
            parts.append(jnp.zeros((rest, HEAD_DIM), f32))
        return jnp.concatenate(parts, axis=0) if len(parts) > 1 else piece

    def intra_factorised(q, k, gh):
        qts, kts = [], []
        for a in range(N_SUB):
            lo_r, hi_r = a * SUB, (a + 1) * SUB
            if a == 0:
                qa = q[:hi_r] * jnp.exp(gh[:hi_r])
                ka = k[:hi_r] * jnp.exp(-gh[:hi_r])
            else:
                ra = gh[lo_r - 1:lo_r]
                qa = q[lo_r:hi_r] * jnp.exp(gh[lo_r:hi_r] - ra)
                ka = k[:hi_r] * jnp.exp(ra - gh[:hi_r])
            qts.append(pad_rows(qa, lo_r))
            kts.append(pad_rows(ka, 0))
        a_mat = lax.dot_general(jnp.concatenate(qts, axis=1).astype(bf16),
                                jnp.concatenate(kts, axis=1).astype(bf16), _NT,
                                preferred_element_type=f32)
        return jnp.where(row64 >= col64, a_mat, 0.0)

    def chunks_factorised():
        n_chunks = TB // CHUNK
        heads = [(c, h) for c in range(n_chunks) for h in range(HGRN_HEADS)]

        def cols(h, base=0):
            return slice(base + h * HEAD_DIM, base + (h + 1) * HEAD_DIM)

        for c in range(n_chunks):
            rows = slice(c * CHUNK, (c + 1) * CHUNK)
            lf = lf_scr[rows, :]
            hi = lf.astype(bf16)
            lo = (lf - hi.astype(f32)).astype(bf16)
            gg = jnp.dot(tri, jnp.concatenate([hi, lo], axis=1), preferred_element_type=f32)
            lf_scr[rows, :] = gg[:, :HGRN_WIDTH] + gg[:, HGRN_WIDTH:]
        for c, h in heads:
            rows = slice(c * CHUNK, (c + 1) * CHUNK)
            q = p_scr[rows, cols(h)]
            k = kk_scr[rows, cols(h)]
            gh = lf_scr[rows, cols(h)]
            amat_scr[c * HGRN_HEADS + h] = intra_factorised(q, k, gh).astype(bf16)
            qg_scr[rows, cols(h)] = (q * jnp.exp(gh)).astype(bf16)
            kd_scr[rows, cols(h)] = (k * jnp.exp(gh[CHUNK - 1:CHUNK] - gh)).astype(bf16)
        for c, h in heads:
            rows = slice(c * CHUNK, (c + 1) * CHUNK)
            v = p_scr[rows, cols(h, 2 * HGRN_WIDTH)]
            o_scr[rows, cols(h)] = jnp.dot(amat_scr[c * HGRN_HEADS + h], v.astype(bf16),
                                           preferred_element_type=f32)
            upd_scr[c * HGRN_HEADS + h] = jnp.dot(v.T.astype(bf16), kd_scr[rows, cols(h)],
                                                  preferred_element_type=f32)
        states = [st_scr[h] for h in range(HGRN_HEADS)]
        for c, h in heads:
            rows = slice(c * CHUNK, (c + 1) * CHUNK)
            st = states[h]
            o_inter = lax.dot_general(qg_scr[rows, cols(h)], st.astype(bf16), _NT,
                                      preferred_element_type=f32)
            o_scr[rows, cols(h)] = (o_scr[rows, cols(h)] + o_inter) * scale
            gl = lf_scr[(c + 1) * CHUNK - 1:(c + 1) * CHUNK, cols(h)]
            states[h] = st * jnp.exp(gl) + upd_scr[c * HGRN_HEADS + h]
        for h in range(HGRN_HEADS):
            st_scr[h] = states[h]

    def intra_exact_diagonal(q, k, gh, cs):
        qts, kts = [], []
        for a in range(1, N_SUB):
            lo_r, hi_r = a * SUB, (a + 1) * SUB
            ra = g_scr[lo_r - 1:lo_r, cs]
            qts.append(pad_rows(q[lo_r:hi_r] * jnp.exp(gh[lo_r:hi_r] - ra), lo_r))
            kts.append(pad_rows(k[:lo_r] * jnp.exp(ra - gh[:lo_r]), 0))
        a_off = lax.dot_general(jnp.concatenate(qts, axis=1).astype(bf16),
                                jnp.concatenate(kts, axis=1).astype(bf16), _NT,
                                preferred_element_type=f32)
        diag_rows = []
        for a in range(N_SUB):
            lo_r = a * SUB
            gs = gh[lo_r:lo_r + SUB]
            qs = q[lo_r:lo_r + SUB]
            blk = jnp.zeros((SUB, CHUNK), f32)
            for jl in range(SUB):
                j = lo_r + jl
                gj = g_scr[j:j + 1, cs]
                kj = kc_scr[j:j + 1, cs]
                e = jnp.exp(jnp.minimum(gs - gj, 0.0))
                col = jnp.sum(qs * (kj * e), axis=-1, keepdims=True)
                blk = jnp.where(lane_sub == j, col, blk)
            diag_rows.append(blk)
        a_diag = jnp.concatenate(diag_rows, axis=0)
        return a_off + jnp.where(row64 >= col64, a_diag, 0.0)

    def make_chunk_body(intra):
        def chunk_body(c, carry):
            r0 = pl.multiple_of(c * CHUNK, CHUNK)
            rows = pl.ds(r0, CHUNK)
            lf = lf_scr[rows, :]
            hi = lf.astype(bf16)
            lo = (lf - hi.astype(f32)).astype(bf16)
            gg = jnp.dot(tri, jnp.concatenate([hi, lo], axis=1), preferred_element_type=f32)
            g_all = gg[:, :HGRN_WIDTH] + gg[:, HGRN_WIDTH:]
            g_scr[...] = g_all
            kc_scr[...] = kk_scr[rows, :]
            for h in range(HGRN_HEADS):
                cs = slice(h * HEAD_DIM, (h + 1) * HEAD_DIM)
                q = p_scr[rows, h * HEAD_DIM:(h + 1) * HEAD_DIM]
                v = p_scr[rows, 2 * HGRN_WIDTH + h * HEAD_DIM:2 * HGRN_WIDTH + (h + 1) * HEAD_DIM]
                k = kc_scr[:, cs]
                gh = g_all[:, cs]
                st = st_scr[h]
                o_inter = lax.dot_general((q * jnp.exp(gh)).astype(bf16), st.astype(bf16), _NT,
                                          preferred_element_type=f32)
                a_mat = intra(q, k, gh, cs)
                o = jnp.dot(a_mat.astype(bf16), v.astype(bf16), preferred_element_type=f32) + o_inter
                o_scr[rows, h * HEAD_DIM:(h + 1) * HEAD_DIM] = o * scale

                gl = g_scr[CHUNK - 1:CHUNK, cs]
                kd = k * jnp.exp(gl - gh)
                upd = jnp.dot(v.T.astype(bf16), kd.astype(bf16), preferred_element_type=f32)
                st_scr[h] = st * jnp.exp(gl) + upd
            return carry
        return chunk_body

    lf_all = lf_scr[...]
    sub_decay = -jnp.sum(lf_all.reshape(TB // SUB, SUB, HGRN_WIDTH), axis=1)
    bounded = jnp.max(sub_decay) <= MAX_SUB_DECAY

    @pl.when(bounded)
    def _():
        chunks_factorised()

    @pl.when(jnp.logical_not(bounded))
    def _():
        lax.fori_loop(0, TB // CHUNK, make_chunk_body(intra_exact_diagonal), 0)

    o = o_scr[...]
    ms = jnp.mean(o * o, axis=-1, keepdims=True)
    gate = p_scr[:, 3 * HGRN_WIDTH:4 * HGRN_WIDTH]
    y_rec = o * lax.rsqrt(ms + EPS) * hg_ref[...] * (gate * _sigmoid(gate))

    u = _gelu(p_scr[:, 4 * HGRN_WIDTH:4 * HGRN_WIDTH + GMLP_WIDTH])
    vn = _layer_norm(_gelu(p_scr[:, 4 * HGRN_WIDTH + GMLP_WIDTH:]), lng_ref[...], lnb_ref[...])
    vnb = vn.astype(bf16)
    win_rows = []
    for w in range(TB // GMLP_BLOCK):
        cols = []
        for g in range(GMLP_GROUPS):
            vblk = vnb[w * GMLP_BLOCK:(w + 1) * GMLP_BLOCK, g * HEAD_DIM:(g + 1) * HEAD_DIM]
            s = jnp.dot(wm_ref[g], vblk, preferred_element_type=f32) + bst_ref[:, g:g + 1]
            cols.append(s)
        win_rows.append(jnp.concatenate(cols, axis=1))
    y_sg = u * jnp.concatenate(win_rows, axis=0)

    mix = jnp.dot(jnp.concatenate([y_rec, y_sg], axis=1).astype(bf16), wout_ref[...],
                  preferred_element_type=f32)
    h1 = _layer_norm(ALPHA * x + mix, l1g_ref[...], l1b_ref[...])
    h1f_ref[...] = h1
    h1p_ref[...] = _pack_rows(h1)

    hh = h1.astype(bf16)
    hl = (h1 - hh.astype(f32)).astype(bf16)
    rwh = rwh_ref[...]
    logits = (lax.dot_general(rwh, hh, _NT, preferred_element_type=f32)
              + lax.dot_general(rwh, hl, _NT, preferred_element_type=f32)
              + lax.dot_general(rwl_ref[...], hh, _NT, preferred_element_type=f32)
              + rb_ref[...])

    e_iota = lax.broadcasted_iota(jnp.int32, (N_EXPERTS, TB), 0)
    work = logits
    vals, idxs, hots = [], [], []
    for _ in range(TOP_K):
        m = jnp.max(work, axis=0, keepdims=True)
        ik = jnp.min(jnp.where(work == m, e_iota, N_EXPERTS), axis=0, keepdims=True)
        hot = e_iota == ik
        work = jnp.where(hot, -jnp.inf, work)
        vals.append(m)
        idxs.append(ik)
        hots.append(hot)
    exps = [jnp.exp(vk - vals[0]) for vk in vals]
    denom = exps[0] + exps[1] + exps[2] + exps[3]
    gate_ref[0] = jnp.concatenate([ek / denom for ek in exps], axis=0)
    idx_ref[0] = jnp.concatenate(idxs, axis=0)

    hot_any = jnp.where(hots[0] | hots[1] | hots[2] | hots[3], 1.0, 0.0)
    prefix = jnp.dot(hot_any.astype(bf16), upp_ref[...], preferred_element_type=f32)
    base = prefix + carry_scr[:, 0:1]
    ranks = [jnp.sum(jnp.where(hk, base, 0.0), axis=0, keepdims=True) for hk in hots]
    rank_ref[0] = jnp.concatenate(ranks, axis=0).astype(jnp.int32)
    new_carry = carry_scr[...] + jnp.sum(hot_any, axis=1, keepdims=True)
    carry_scr[...] = new_carry
    cnt_ref[...] = new_carry


def _mixer(xt, win, lb, hg, lng, lnb, wm, bst, wout, l1g, l1b, rwh, rwl, rb, tri, upp,
           seq, batch, batch0):
    nt = seq // TB
    t_total = batch * seq
    nblk = batch * nt
    const2 = lambda b, t: (0, 0)
    const3 = lambda b, t: (0, 0, 0)
    row_blk = lambda b, t: (b * nt + t, 0)
    meta_blk = lambda b, t: (b * nt + t, 0, 0)
    in_specs = [
        pl.BlockSpec((TB, D_MODEL), lambda b, t: ((b + batch0) * nt + t, 0)),
        pl.BlockSpec((D_MODEL, IN_COLS), const2),
        pl.BlockSpec((1, HGRN_WIDTH), const2),
        pl.BlockSpec((1, HGRN_WIDTH), const2),
        pl.BlockSpec((1, GMLP_WIDTH), const2),
        pl.BlockSpec((1, GMLP_WIDTH), const2),
        pl.BlockSpec((GMLP_GROUPS, GMLP_BLOCK, GMLP_BLOCK), const3),
        pl.BlockSpec((GMLP_BLOCK, GMLP_GROUPS), const2),
        pl.BlockSpec((D_MODEL, D_MODEL), const2),
        pl.BlockSpec((1, D_MODEL), const2),
        pl.BlockSpec((1, D_MODEL), const2),
        pl.BlockSpec((N_EXPERTS, D_MODEL), const2),
        pl.BlockSpec((N_EXPERTS, D_MODEL), const2),
        pl.BlockSpec((N_EXPERTS, 1), const2),
        pl.BlockSpec((CHUNK, CHUNK), const2),
        pl.BlockSpec((TB, TB), const2),
    ]
    out_shape = [
        jax.ShapeDtypeStruct((t_total, D_MODEL), jnp.float32),
        jax.ShapeDtypeStruct((t_total, HALF), jnp.int32),
        jax.ShapeDtypeStruct((nblk, TOP_K, TB), jnp.int32),
        jax.ShapeDtypeStruct((nblk, TOP_K, TB), jnp.float32),
        jax.ShapeDtypeStruct((nblk, TOP_K, TB), jnp.int32),
        jax.ShapeDtypeStruct((N_EXPERTS, 128), jnp.float32),
    ]
    out_specs = [
        pl.BlockSpec((TB, D_MODEL), row_blk),
        pl.BlockSpec((TB, HALF), row_blk),
        pl.BlockSpec((1, TOP_K, TB), meta_blk),
        pl.BlockSpec((1, TOP_K, TB), meta_blk),
        pl.BlockSpec((1, TOP_K, TB), meta_blk),
        pl.BlockSpec((N_EXPERTS, 128), const2),
    ]
    scratch = [
        pltpu.VMEM((TB, IN_COLS), jnp.float32),
        pltpu.VMEM((TB, HGRN_WIDTH), jnp.float32),
        pltpu.VMEM((TB, HGRN_WIDTH), jnp.float32),
        pltpu.VMEM((CHUNK, HGRN_WIDTH), jnp.float32),
        pltpu.VMEM((CHUNK, HGRN_WIDTH), jnp.float32),
        pltpu.VMEM((TB, HGRN_WIDTH), jnp.float32),
        pltpu.VMEM((HGRN_HEADS, HEAD_DIM, HEAD_DIM), jnp.float32),
        pltpu.VMEM((N_EXPERTS, 128), jnp.float32),
        pltpu.VMEM((TB // CHUNK * HGRN_HEADS, CHUNK, CHUNK), jnp.bfloat16),
        pltpu.VMEM((TB, HGRN_WIDTH), jnp.bfloat16),
        pltpu.VMEM((TB, HGRN_WIDTH), jnp.bfloat16),
        pltpu.VMEM((TB // CHUNK * HGRN_HEADS, HEAD_DIM, HEAD_DIM), jnp.float32),
    ]
    return pl.pallas_call(
        _mixer_kernel,
        grid=(batch, nt),
        in_specs=in_specs,
        out_specs=out_specs,
        out_shape=out_shape,
        scratch_shapes=scratch,
        compiler_params=pltpu.CompilerParams(
            dimension_semantics=("arbitrary", "arbitrary"),
            vmem_limit_bytes=VMEM_LIMIT),
        name="mixer",
    )(xt, win, lb, hg, lng, lnb, wm, bst, wout, l1g, l1b, rwh, rwl, rb, tri, upp)


def _sc_workers():
    info = plsc.get_sparse_core_info()
    return info.num_cores, info.num_cores * info.num_subcores


def _sc_dispatch(rows, dest, n_out):
    t_total, dw = rows.shape
    nc, nw = _sc_workers()
    per_w = t_total // nw
    mesh = plsc.VectorSubcoreMesh(core_axis_name="c", subcore_axis_name="s")

    @functools.partial(
        pl.kernel,
        out_type=jax.ShapeDtypeStruct((n_out, dw), rows.dtype),
        mesh=mesh,
        scratch_types=[pltpu.VMEM((SC_WINDOW,), jnp.int32) for _ in range(TOP_K)]
        + [pltpu.VMEM((SC_WINDOW, dw), rows.dtype), pltpu.SemaphoreType.DMA],
        name="sc_dispatch",
    )
    def k(x_hbm, i_hbm, o_hbm, i0, i1, i2, i3, rows_v, sem):
        wid = lax.axis_index("s") * nc + lax.axis_index("c")
        idx_bufs = (i0, i1, i2, i3)

        @pl.loop(0, per_w // SC_WINDOW)
        def _(j):
            base = wid * per_w + j * SC_WINDOW
            for kk in range(TOP_K):
                pltpu.sync_copy(i_hbm.at[pl.ds(kk * t_total + base, SC_WINDOW)], idx_bufs[kk])
            pltpu.sync_copy(x_hbm.at[pl.ds(base, SC_WINDOW)], rows_v)
            copies = [pltpu.async_copy(rows_v, o_hbm.at[idx_bufs[kk]], sem) for kk in range(TOP_K)]
            for cp in copies:
                cp.wait()

    return k(rows, dest)


def _sc_gather(table, idx):
    n = idx.shape[0]
    dw = table.shape[1]
    nc, nw = _sc_workers()
    per_w = n // nw
    mesh = plsc.VectorSubcoreMesh(core_axis_name="c", subcore_axis_name="s")

    @functools.partial(
        pl.kernel,
        out_type=jax.ShapeDtypeStruct((n, dw), table.dtype),
        mesh=mesh,
        scratch_types=[pltpu.VMEM((SC_WINDOW,), jnp.int32),
                       pltpu.VMEM((SC_WINDOW, dw), table.dtype),
                       pltpu.SemaphoreType.DMA],
        name="sc_gather",
    )
    def k(t_hbm, i_hbm, o_hbm, idx_v, rows_v, sem):
        wid = lax.axis_index("s") * nc + lax.axis_index("c")

        @pl.loop(0, per_w // SC_WINDOW)
        def _(j):
            base = wid * per_w + j * SC_WINDOW
            pltpu.sync_copy(i_hbm.at[pl.ds(base, SC_WINDOW)], idx_v)
            pltpu.async_copy(t_hbm.at[idx_v], rows_v, sem).wait()
            pltpu.sync_copy(rows_v, o_hbm.at[pl.ds(base, SC_WINDOW)])

    return k(table, idx)


def _expert_kernel(te_ref, tv_ref, xs_ref, w1_ref, w2_ref, b1g_ref, b1l_ref, b2_ref, perm_ref, y_ref,
                   w1g_scr, w1l_scr, w2_scr):
    i = pl.program_id(0)
    valid = tv_ref[i]
    f32 = jnp.float32
    bf16 = jnp.bfloat16
    expert_changed = (i == 0) | (te_ref[i] != te_ref[jnp.maximum(i - 1, 0)])

    @pl.when(valid == 0)
    def _():
        y_ref[...] = jnp.zeros_like(y_ref)

    @pl.when((valid > 0) & expert_changed)
    def _():
        w2_scr[...] = w2_ref[0].astype(bf16)
        perm = perm_ref[...]
        for c in range(2 * D_MODEL // PERM_BLOCK):
            blk = w1_ref[0, :, c * PERM_BLOCK:(c + 1) * PERM_BLOCK].astype(bf16)
            r = jnp.dot(blk, perm, preferred_element_type=f32).astype(bf16)
            half = PERM_BLOCK // 2
            w1g_scr[:, c * half:(c + 1) * half] = r[:, :half]
            w1l_scr[:, c * half:(c + 1) * half] = r[:, half:]

    @pl.when(valid > 0)
    def _():
        a, b = _unpack_rows(xs_ref[...])
        keep = lax.broadcasted_iota(jnp.int32, (TM, 1), 0) < valid
        x = jnp.where(keep, jnp.concatenate([a, b], axis=1), 0.0).astype(bf16)
        hg = jnp.dot(x, w1g_scr[...], preferred_element_type=f32) + b1g_ref[0]
        hl = jnp.dot(x, w1l_scr[...], preferred_element_type=f32) + b1l_ref[0]
        xg = jnp.minimum(hg, SWIGLU_LIMIT)
        xl = jnp.clip(hl, -SWIGLU_LIMIT, SWIGLU_LIMIT)
        act = xg * _sigmoid(SWIGLU_ALPHA * xg) * (xl + 1.0)
        y = jnp.dot(act.astype(bf16), w2_scr[...], preferred_element_type=f32) + b2_ref[0]
        y_ref[...] = _pack_rows(y)


def _experts(xs, tile_expert, tile_valid, w1, w2, b1g, b1l, b2, perm):
    n_slots = xs.shape[0]
    n_tiles = n_slots // TM
    d_ff = w2.shape[1]
    wmap = lambda i, te, tv: (te[i], 0, 0)
    grid_spec = pltpu.PrefetchScalarGridSpec(
        num_scalar_prefetch=2,
        grid=(n_tiles,),
        in_specs=[
            pl.BlockSpec((TM, HALF), lambda i, te, tv: (i, 0)),
            pl.BlockSpec((1, D_MODEL, 2 * d_ff), wmap),
            pl.BlockSpec((1, d_ff, D_MODEL), wmap),
            pl.BlockSpec((1, 1, d_ff), wmap),
            pl.BlockSpec((1, 1, d_ff), wmap),
            pl.BlockSpec((1, 1, D_MODEL), wmap),
            pl.BlockSpec((PERM_BLOCK, PERM_BLOCK), lambda i, te, tv: (0, 0)),
        ],
        out_specs=pl.BlockSpec((TM, HALF), lambda i, te, tv: (i, 0)),
        scratch_shapes=[
            pltpu.VMEM((D_MODEL, d_ff), jnp.bfloat16),
            pltpu.VMEM((D_MODEL, d_ff), jnp.bfloat16),
            pltpu.VMEM((d_ff, D_MODEL), jnp.bfloat16),
        ],
    )
    return pl.pallas_call(
        _expert_kernel,
        grid_spec=grid_spec,
        out_shape=jax.ShapeDtypeStruct((n_slots, HALF), jnp.int32),
        compiler_params=pltpu.CompilerParams(
            dimension_semantics=("arbitrary",),
            vmem_limit_bytes=VMEM_LIMIT),
        name="experts",
    )(tile_expert, tile_valid, xs, w1, w2, b1g, b1l, b2, perm)


def _combine_kernel(h_ref, yk_ref, gate_ref, g_ref, b_ref, *rest):
    o_ref = rest[-1]
    gates = gate_ref[...]
    acc_a = None
    acc_b = None
    for k in range(TOP_K):
        a, b = _unpack_rows(yk_ref[k])
        gk = gates[:, k:k + 1]
        acc_a = gk * a if acc_a is None else acc_a + gk * a
        acc_b = gk * b if acc_b is None else acc_b + gk * b
    ffn = jnp.concatenate([acc_a, acc_b], axis=1)
    o_ref[...] = _layer_norm(ALPHA * h_ref[...] + ffn, g_ref[...], b_ref[...])


def _combine(h1f, yk, gates, g2, b2, out_prev, part, t_all):
    t_part = h1f.shape[0]
    blk0 = part * (t_part // TC3)
    row = lambda i: (i, 0)
    const = lambda i: (0, 0)
    in_specs = [
        pl.BlockSpec((TC3, D_MODEL), row),
        pl.BlockSpec((TOP_K, TC3, HALF), lambda i: (0, i, 0)),
        pl.BlockSpec((TC3, TOP_K), row),
        pl.BlockSpec((1, D_MODEL), const),
        pl.BlockSpec((1, D_MODEL), const),
    ]
    args = [h1f, yk, gates, g2, b2]
    aliases = {}
    if out_prev is not None:
        in_specs.append(pl.BlockSpec(memory_space=pl.ANY))
        args.append(out_prev)
        aliases = {len(args) - 1: 0}
    return pl.pallas_call(
        _combine_kernel,
        grid=(t_part // TC3,),
        in_specs=in_specs,
        out_specs=pl.BlockSpec((TC3, D_MODEL), lambda i: (i + blk0, 0)),
        out_shape=jax.ShapeDtypeStruct((t_all, D_MODEL), jnp.float32),
        input_output_aliases=aliases,
        compiler_params=pltpu.CompilerParams(
            dimension_semantics=("arbitrary",),
            vmem_limit_bytes=VMEM_LIMIT),
        name="combine",
    )(*args)


def kernel(x, w_in, lb_logits, hgrn_norm_g, gmlp_ln_g, gmlp_ln_b, gmlp_ws, gmlp_bs, w_out, ln1_g, ln1_b, router_w, router_b, exp_w1, exp_b1, exp_w2, exp_b2, ln2_g, ln2_b):
    batch, seq, d = x.shape
    assert d == D_MODEL and seq % TB == 0 and w_in.shape[0] == 1
    t_total = batch * seq
    f32 = jnp.float32
    bf16 = jnp.bfloat16

    lb = jnp.cumsum(jax.nn.softmax(lb_logits.astype(f32), axis=0), axis=0)[0:1]
    chunk_id = jnp.arange(GMLP_BLOCK) // CHUNK
    wm = jnp.where((chunk_id[None, :] <= chunk_id[:, None])[None], gmlp_ws[0], 0.0).astype(bf16)
    rwt = router_w[0].T
    rwh = rwt.astype(bf16)
    rwl = (rwt - rwh.astype(f32)).astype(bf16)
    tri = (jnp.arange(CHUNK)[None, :] <= jnp.arange(CHUNK)[:, None]).astype(bf16)
    upp = (jnp.arange(TB)[:, None] < jnp.arange(TB)[None, :]).astype(bf16)

    lane = jnp.arange(PERM_BLOCK)
    src = jnp.where(lane < PERM_BLOCK // 2, 2 * lane, 2 * (lane - PERM_BLOCK // 2) + 1)
    perm = (jnp.arange(PERM_BLOCK)[:, None] == src[None, :]).astype(bf16)
    xt = x.reshape(t_total, d)
    win = w_in[0].astype(bf16)
    wout = w_out[0].astype(bf16)
    b1g, b1l, b2e = exp_b1[0][:, None, 0::2], exp_b1[0][:, None, 1::2], exp_b2[0][:, None, :]
    experts = jnp.arange(N_EXPERTS, dtype=jnp.int32)

    assert batch % N_PARTS == 0
    pb = batch // N_PARTS
    t_part = pb * seq
    n_slots = t_part * TOP_K + N_EXPERTS * TM
    n_tiles = n_slots // TM
    out = None
    for part in range(N_PARTS):
        h1f, h1p, idx, gates, rank, cnt = _mixer(
            xt, win, lb, hgrn_norm_g[0:1], gmlp_ln_g[0:1], gmlp_ln_b[0:1], wm, gmlp_bs[0].T, wout,
            ln1_g[0:1], ln1_b[0:1], rwh, rwl, router_b[0][:, None], tri, upp, seq, pb, part * pb)

        counts = cnt[:, 0].astype(jnp.int32)
        padded = ((counts + TM - 1) // TM) * TM
        ends = jnp.cumsum(padded)
        starts = ends - padded
        start_of = jnp.sum(jnp.where(idx[..., None] == experts, starts, 0), axis=-1)
        dest = start_of + rank
        dest_kt = dest.transpose(1, 0, 2).reshape(TOP_K * t_part)
        gates_tk = gates.transpose(0, 2, 1).reshape(t_part, TOP_K)
        tile_row = jnp.arange(n_tiles, dtype=jnp.int32) * TM
        tile_expert = jnp.minimum(
            jnp.sum((tile_row[:, None] >= ends[None, :]).astype(jnp.int32), axis=1), N_EXPERTS - 1)
        tile_valid = jnp.clip(starts[tile_expert] + counts[tile_expert] - tile_row, 0, TM)

        xs = _sc_dispatch(h1p, dest_kt, n_slots)
        y = _experts(xs, tile_expert, tile_valid.astype(jnp.int32), exp_w1[0], exp_w2[0],
                     b1g, b1l, b2e, perm)
        yk = _sc_gather(y, dest_kt).reshape(TOP_K, t_part, HALF)
        out = _combine(h1f, yk, gates_tk, ln2_g[0:1], ln2_b[0:1], out, part, t_total)
    return out.reshape(batch, seq, d)
```

```python
import functools
import math

import jax
import jax.numpy as jnp
from jax import lax
from jax.experimental import pallas as pl
from jax.experimental.pallas import tpu as pltpu
from jax.experimental.pallas import tpu_sc as plsc

D_MODEL = 1024
CHUNK = 64
SUB = 16
N_SUB = CHUNK // SUB
MAX_SUB_DECAY = 60.0
HGRN_WIDTH = 512
HGRN_HEADS = 4
HEAD_DIM = 128
GMLP_WIDTH = 512
GMLP_BLOCK = 128
GMLP_GROUPS = 4
IN_COLS = 3072
N_EXPERTS = 32
TOP_K = 4
SWIGLU_LIMIT = 7.0
SWIGLU_ALPHA = 1.702
ALPHA = 2.0 ** 0.25
EPS = 1e-5
HALF = D_MODEL // 2

TB = 512
TM = 512
TC3 = 512
N_PARTS = 2
SC_WINDOW = 128
PROJ_GROUP = 256
PERM_BLOCK = 256
VMEM_LIMIT = 56 * 1024 * 1024

_NT = (((1,), (1,)), ((), ()))


def _sigmoid(x):
    z = jnp.exp(-jnp.abs(x))
    r = 1.0 / (1.0 + z)
    return jnp.where(x >= 0, r, z * r)


def _gelu(x):
    return 0.5 * x * (1.0 + lax.erf(x * (1.0 / math.sqrt(2.0))))


def _layer_norm(x, g, b):
    mu = jnp.mean(x, axis=-1, keepdims=True)
    xc = x - mu
    var = jnp.mean(xc * xc, axis=-1, keepdims=True)
    return xc * lax.rsqrt(var + EPS) * g + b


def _pack_rows(h):
    a = h[:, :HALF].astype(jnp.bfloat16).astype(jnp.float32)
    b = h[:, HALF:].astype(jnp.bfloat16).astype(jnp.float32)
    au = lax.bitcast_convert_type(a, jnp.uint32) >> 16
    bu = lax.bitcast_convert_type(b, jnp.uint32) & jnp.uint32(0xFFFF0000)
    return lax.bitcast_convert_type(au | bu, jnp.int32)


def _unpack_rows(w):
    u = lax.bitcast_convert_type(w, jnp.uint32)
    a = lax.bitcast_convert_type(u << 16, jnp.float32)
    b = lax.bitcast_convert_type(u & jnp.uint32(0xFFFF0000), jnp.float32)
    return a, b


def _mixer_kernel(x_ref, xn_ref, win_ref, lb_ref, hg_ref, lng_ref, lnb_ref, wm_ref, bst_ref,
                  wout_ref, l1g_ref, l1b_ref, rwh_ref, rwl_ref, rb_ref, tri_ref, upp_ref,
                  h1f_ref, h1p_ref, idx_ref, gate_ref, rank_ref, cnt_ref,
                  p_scr, lf_scr, kk_scr, g_scr, kc_scr, o_scr, st_scr, carry_scr,
                  amat_scr, qg_scr, kd_scr, upd_scr, mix_scr):
    b = pl.program_id(0)
    t = pl.program_id(1)
    step = b * pl.num_programs(1) + t
    f32 = jnp.float32
    bf16 = jnp.bfloat16

    @pl.when(t == 0)
    def _():
        st_scr[...] = jnp.zeros_like(st_scr)

    @pl.when((b == 0) & (t == 0))
    def _():
        carry_scr[...] = jnp.zeros_like(carry_scr)

    def project(xb, group):
        cols = slice(group * PROJ_GROUP, (group + 1) * PROJ_GROUP)
        p_scr[:, cols] = jnp.dot(xb, win_ref[:, cols], preferred_element_type=f32)

    def hgrn_gates():
        fl = p_scr[:, HGRN_WIDTH:2 * HGRN_WIDTH]
        lb = lb_ref[...]
        z = jnp.exp(-jnp.abs(fl))
        r = 1.0 / (1.0 + z)
        zr = z * r
        pos = fl >= 0
        lf_scr[...] = jnp.log(lb + (1.0 - lb) * jnp.where(pos, r, zr))
        kk_scr[...] = (1.0 - lb) * jnp.where(pos, zr, r)

    @pl.when(step == 0)
    def _():
        xb0 = x_ref[...].astype(bf16)
        for group in range(IN_COLS // PROJ_GROUP):
            project(xb0, group)
        hgrn_gates()

    x = x_ref[...]

    tri = tri_ref[...]
    row64 = lax.broadcasted_iota(jnp.int32, (CHUNK, CHUNK), 0)
    col64 = lax.broadcasted_iota(jnp.int32, (CHUNK, CHUNK), 1)
    lane_sub = lax.broadcasted_iota(jnp.int32, (SUB, CHUNK), 1)
    scale = HEAD_DIM ** -0.5

    def pad_rows(piece, lo_r):
        parts = []
        if lo_r > 0:
            parts.append(jnp.zeros((lo_r, HEAD_DIM), f32))
        parts.append(piece)
        rest = CHUNK - lo_r - piece.shape[0]
        if rest > 0:
            parts.append(jnp.zeros((rest, HEAD_DIM), f32))
        return jnp.concatenate(parts, axis=0) if len(parts) > 1 else piece

    def intra_factorised(q, k, gh):
        qts, kts = [], []
        for a in range(N_SUB):
            lo_r, hi_r = a * SUB, (a + 1) * SUB
            if a == 0:
                qa = q[:hi_r] * jnp.exp(gh[:hi_r])
                ka = k[:hi_r] * jnp.exp(-gh[:hi_r])
            else:
                ra = gh[lo_r - 1:lo_r]
                qa = q[lo_r:hi_r] * jnp.exp(gh[lo_r:hi_r] - ra)
                ka = k[:hi_r] * jnp.exp(ra - gh[:hi_r])
            qts.append(pad_rows(qa, lo_r))
            kts.append(pad_rows(ka, 0))
        a_mat = lax.dot_general(jnp.concatenate(qts, axis=1).astype(bf16),
                                jnp.concatenate(kts, axis=1).astype(bf16), _NT,
                                preferred_element_type=f32)
        return jnp.where(row64 >= col64, a_mat, 0.0)

    def chunks_factorised():
        n_chunks = TB // CHUNK
        heads = [(c, h) for c in range(n_chunks) for h in range(HGRN_HEADS)]

        def cols(h, base=0):
            return slice(base + h * HEAD_DIM, base + (h + 1) * HEAD_DIM)

        for c in range(n_chunks):
            rows = slice(c * CHUNK, (c + 1) * CHUNK)
            lf = lf_scr[rows, :]
            hi = lf.astype(bf16)
            lo = (lf - hi.astype(f32)).astype(bf16)
            gg = jnp.dot(tri, jnp.concatenate([hi, lo], axis=1), preferred_element_type=f32)
            lf_scr[rows, :] = gg[:, :HGRN_WIDTH] + gg[:, HGRN_WIDTH:]
        for c, h in heads:
            rows = slice(c * CHUNK, (c + 1) * CHUNK)
            q = p_scr[rows, cols(h)]
            k = kk_scr[rows, cols(h)]
            gh = lf_scr[rows, cols(h)]
            amat_scr[c * HGRN_HEADS + h] = intra_factorised(q, k, gh).astype(bf16)
            qg_scr[rows, cols(h)] = (q * jnp.exp(gh)).astype(bf16)
            kd_scr[rows, cols(h)] = (k * jnp.exp(gh[CHUNK - 1:CHUNK] - gh)).astype(bf16)
        for c, h in heads:
            rows = slice(c * CHUNK, (c + 1) * CHUNK)
            v = p_scr[rows, cols(h, 2 * HGRN_WIDTH)]
            o_scr[rows, cols(h)] = jnp.dot(amat_scr[c * HGRN_HEADS + h], v.astype(bf16),
                                           preferred_element_type=f32)
            upd_scr[c * HGRN_HEADS + h] = jnp.dot(v.T.astype(bf16), kd_scr[rows, cols(h)],
                                                  preferred_element_type=f32)
        states = [st_scr[h] for h in range(HGRN_HEADS)]
        for c, h in heads:
            rows = slice(c * CHUNK, (c + 1) * CHUNK)
            st = states[h]
            o_inter = lax.dot_general(qg_scr[rows, cols(h)], st.astype(bf16), _NT,
                                      preferred_element_type=f32)
            o_scr[rows, cols(h)] = (o_scr[rows, cols(h)] + o_inter) * scale
            gl = lf_scr[(c + 1) * CHUNK - 1:(c + 1) * CHUNK, cols(h)]
            states[h] = st * jnp.exp(gl) + upd_scr[c * HGRN_HEADS + h]
        for h in range(HGRN_HEADS):
            st_scr[h] = states[h]

    def intra_exact_diagonal(q, k, gh, cs):
        qts, kts = [], []
        for a in range(1, N_SUB):
            lo_r, hi_r = a * SUB, (a + 1) * SUB
            ra = g_scr[lo_r - 1:lo_r, cs]
            qts.append(pad_rows(q[lo_r:hi_r] * jnp.exp(gh[lo_r:hi_r] - ra), lo_r))
            kts.append(pad_rows(k[:lo_r] * jnp.exp(ra - gh[:lo_r]), 0))
        a_off = lax.dot_general(jnp.concatenate(qts, axis=1).astype(bf16),
                                jnp.concatenate(kts, axis=1).astype(bf16), _NT,
                                preferred_element_type=f32)
        diag_rows = []
        for a in range(N_SUB):
            lo_r = a * SUB
            gs = gh[lo_r:lo_r + SUB]
            qs = q[lo_r:lo_r + SUB]
            blk = jnp.zeros((SUB, CHUNK), f32)
            for jl in range(SUB):
                j = lo_r + jl
                gj = g_scr[j:j + 1, cs]
                kj = kc_scr[j:j + 1, cs]
                e = jnp.exp(jnp.minimum(gs - gj, 0.0))
                col = jnp.sum(qs * (kj * e), axis=-1, keepdims=True)
                blk = jnp.where(lane_sub == j, col, blk)
            diag_rows.append(blk)
        a_diag = jnp.concatenate(diag_rows, axis=0)
        return a_off + jnp.where(row64 >= col64, a_diag, 0.0)

    def make_chunk_body(intra):
        def chunk_body(c, carry):
            r0 = pl.multiple_of(c * CHUNK, CHUNK)
            rows = pl.ds(r0, CHUNK)
            lf = lf_scr[rows, :]
            hi = lf.astype(bf16)
            lo = (lf - hi.astype(f32)).astype(bf16)
            gg = jnp.dot(tri, jnp.concatenate([hi, lo], axis=1), preferred_element_type=f32)
            g_all = gg[:, :HGRN_WIDTH] + gg[:, HGRN_WIDTH:]
            g_scr[...] = g_all
            kc_scr[...] = kk_scr[rows, :]
            for h in range(HGRN_HEADS):
                cs = slice(h * HEAD_DIM, (h + 1) * HEAD_DIM)
                q = p_scr[rows, h * HEAD_DIM:(h + 1) * HEAD_DIM]
                v = p_scr[rows, 2 * HGRN_WIDTH + h * HEAD_DIM:2 * HGRN_WIDTH + (h + 1) * HEAD_DIM]
                k = kc_scr[:, cs]
                gh = g_all[:, cs]
                st = st_scr[h]
                o_inter = lax.dot_general((q * jnp.exp(gh)).astype(bf16), st.astype(bf16), _NT,
                                          preferred_element_type=f32)
                a_mat = intra(q, k, gh, cs)
                o = jnp.dot(a_mat.astype(bf16), v.astype(bf16), preferred_element_type=f32) + o_inter
                o_scr[rows, h * HEAD_DIM:(h + 1) * HEAD_DIM] = o * scale

                gl = g_scr[CHUNK - 1:CHUNK, cs]
                kd = k * jnp.exp(gl - gh)
                upd = jnp.dot(v.T.astype(bf16), kd.astype(bf16), preferred_element_type=f32)
                st_scr[h] = st * jnp.exp(gl) + upd
            return carry
        return chunk_body

    lf_all = lf_scr[...]
    sub_decay = -jnp.sum(lf_all.reshape(TB // SUB, SUB, HGRN_WIDTH), axis=1)
    bounded = jnp.max(sub_decay) <= MAX_SUB_DECAY

    @pl.when(bounded)
    def _():
        chunks_factorised()

    @pl.when(jnp.logical_not(bounded))
    def _():
        lax.fori_loop(0, TB // CHUNK, make_chunk_body(intra_exact_diagonal), 0)

    xnb = xn_ref[...].astype(bf16)
    f_groups = list(range(HGRN_WIDTH // PROJ_GROUP, 2 * HGRN_WIDTH // PROJ_GROUP))
    pending = f_groups + [g for g in range(IN_COLS // PROJ_GROUP) if g not in f_groups]

    def project_next(count):
        for _ in range(count):
            project(xnb, pending.pop(0))

    project_next(3 * HGRN_WIDTH // PROJ_GROUP)
    for w in range(TB // GMLP_BLOCK):
        rows = slice(w * GMLP_BLOCK, (w + 1) * GMLP_BLOCK)
        o = o_scr[rows, :]
        ms = jnp.mean(o * o, axis=-1, keepdims=True)
        gate = p_scr[rows, 3 * HGRN_WIDTH:4 * HGRN_WIDTH]
        y_rec = o * lax.rsqrt(ms + EPS) * hg_ref[...] * (gate * _sigmoid(gate))
        u = _gelu(p_scr[rows, 4 * HGRN_WIDTH:4 * HGRN_WIDTH + GMLP_WIDTH])
        vn = _layer_norm(_gelu(p_scr[rows, 4 * HGRN_WIDTH + GMLP_WIDTH:]), lng_ref[...], lnb_ref[...])
        vnb = vn.astype(bf16)
        cols = []
        for g in range(GMLP_GROUPS):
            s = jnp.dot(wm_ref[g], vnb[:, g * HEAD_DIM:(g + 1) * HEAD_DIM],
                        preferred_element_type=f32) + bst_ref[:, g:g + 1]
            cols.append(s)
        y_sg = u * jnp.concatenate(cols, axis=1)
        mix_scr[rows, :] = jnp.concatenate([y_rec, y_sg], axis=1).astype(bf16)

    mix = jnp.dot(mix_scr[...], wout_ref[...], preferred_element_type=f32)
    hgrn_gates()
    h1 = _layer_norm(ALPHA * x + mix, l1g_ref[...], l1b_ref[...])
    h1f_ref[...] = h1
    h1p_ref[...] = _pack_rows(h1)
    project_next(2)

    hh = h1.astype(bf16)
    hl = (h1 - hh.astype(f32)).astype(bf16)
    rwh = rwh_ref[...]
    logits = (lax.dot_general(rwh, hh, _NT, preferred_element_type=f32)
              + lax.dot_general(rwh, hl, _NT, preferred_element_type=f32)
              + lax.dot_general(rwl_ref[...], hh, _NT, preferred_element_type=f32)
              + rb_ref[...])
    project_next(4)
    assert not pending

    e_iota = lax.broadcasted_iota(jnp.int32, (N_EXPERTS, TB), 0)
    work = logits
    vals, idxs, hots = [], [], []
    for _ in range(TOP_K):
        m = jnp.max(work, axis=0, keepdims=True)
        ik = jnp.min(jnp.where(work == m, e_iota, N_EXPERTS), axis=0, keepdims=True)
        hot = e_iota == ik
        work = jnp.where(hot, -jnp.inf, work)
        vals.append(m)
        idxs.append(ik)
        hots.append(hot)
    exps = [jnp.exp(vk - vals[0]) for vk in vals]
    denom = exps[0] + exps[1] + exps[2] + exps[3]
    gate_ref[0] = jnp.concatenate([ek / denom for ek in exps], axis=0)
    idx_ref[0] = jnp.concatenate(idxs, axis=0)

    hot_any = jnp.where(hots[0] | hots[1] | hots[2] | hots[3], 1.0, 0.0)
    prefix = jnp.dot(hot_any.astype(bf16), upp_ref[...], preferred_element_type=f32)
    base = prefix + carry_scr[:, 0:1]
    ranks = [jnp.sum(jnp.where(hk, base, 0.0), axis=0, keepdims=True) for hk in hots]
    rank_ref[0] = jnp.concatenate(ranks, axis=0).astype(jnp.int32)
    new_carry = carry_scr[...] + jnp.sum(hot_any, axis=1, keepdims=True)
    carry_scr[...] = new_carry
    cnt_ref[...] = new_carry


def _mixer(xt, win, lb, hg, lng, lnb, wm, bst, wout, l1g, l1b, rwh, rwl, rb, tri, upp,
           seq, batch, batch0):
    nt = seq // TB
    t_total = batch * seq
    nblk = batch * nt
    const2 = lambda b, t: (0, 0)
    const3 = lambda b, t: (0, 0, 0)
    row_blk = lambda b, t: (b * nt + t, 0)
    meta_blk = lambda b, t: (b * nt + t, 0, 0)
    once = dict(pipeline_mode=pl.Buffered(1))
    in_specs = [
        pl.BlockSpec((TB, D_MODEL), lambda b, t: ((b + batch0) * nt + t, 0)),
        pl.BlockSpec((TB, D_MODEL),
                     lambda b, t: (batch0 * nt + jnp.minimum(b * nt + t + 1, nblk - 1), 0)),
        pl.BlockSpec((D_MODEL, IN_COLS), const2, **once),
        pl.BlockSpec((1, HGRN_WIDTH), const2),
        pl.BlockSpec((1, HGRN_WIDTH), const2),
        pl.BlockSpec((1, GMLP_WIDTH), const2),
        pl.BlockSpec((1, GMLP_WIDTH), const2),
        pl.BlockSpec((GMLP_GROUPS, GMLP_BLOCK, GMLP_BLOCK), const3),
        pl.BlockSpec((GMLP_BLOCK, GMLP_GROUPS), const2),
        pl.BlockSpec((D_MODEL, D_MODEL), const2, **once),
        pl.BlockSpec((1, D_MODEL), const2),
        pl.BlockSpec((1, D_MODEL), const2),
        pl.BlockSpec((N_EXPERTS, D_MODEL), const2),
        pl.BlockSpec((N_EXPERTS, D_MODEL), const2),
        pl.BlockSpec((N_EXPERTS, 1), const2),
        pl.BlockSpec((CHUNK, CHUNK), const2),
        pl.BlockSpec((TB, TB), const2, **once),
    ]
    out_shape = [
        jax.ShapeDtypeStruct((t_total, D_MODEL), jnp.float32),
        jax.ShapeDtypeStruct((t_total, HALF), jnp.int32),
        jax.ShapeDtypeStruct((nblk, TOP_K, TB), jnp.int32),
        jax.ShapeDtypeStruct((nblk, TOP_K, TB), jnp.float32),
        jax.ShapeDtypeStruct((nblk, TOP_K, TB), jnp.int32),
        jax.ShapeDtypeStruct((N_EXPERTS, 128), jnp.float32),
    ]
    out_specs = [
        pl.BlockSpec((TB, D_MODEL), row_blk),
        pl.BlockSpec((TB, HALF), row_blk),
        pl.BlockSpec((1, TOP_K, TB), meta_blk),
        pl.BlockSpec((1, TOP_K, TB), meta_blk),
        pl.BlockSpec((1, TOP_K, TB), meta_blk),
        pl.BlockSpec((N_EXPERTS, 128), const2),
    ]
    scratch = [
        pltpu.VMEM((TB, IN_COLS), jnp.float32),
        pltpu.VMEM((TB, HGRN_WIDTH), jnp.float32),
        pltpu.VMEM((TB, HGRN_WIDTH), jnp.float32),
        pltpu.VMEM((CHUNK, HGRN_WIDTH), jnp.float32),
        pltpu.VMEM((CHUNK, HGRN_WIDTH), jnp.float32),
        pltpu.VMEM((TB, HGRN_WIDTH), jnp.float32),
        pltpu.VMEM((HGRN_HEADS, HEAD_DIM, HEAD_DIM), jnp.float32),
        pltpu.VMEM((N_EXPERTS, 128), jnp.float32),
        pltpu.VMEM((TB // CHUNK * HGRN_HEADS, CHUNK, CHUNK), jnp.bfloat16),
        pltpu.VMEM((TB, HGRN_WIDTH), jnp.bfloat16),
        pltpu.VMEM((TB, HGRN_WIDTH), jnp.bfloat16),
        pltpu.VMEM((TB // CHUNK * HGRN_HEADS, HEAD_DIM, HEAD_DIM), jnp.float32),
        pltpu.VMEM((TB, D_MODEL), jnp.bfloat16),
    ]
    return pl.pallas_call(
        _mixer_kernel,
        grid=(batch, nt),
        in_specs=in_specs,
        out_specs=out_specs,
        out_shape=out_shape,
        scratch_shapes=scratch,
        compiler_params=pltpu.CompilerParams(
            dimension_semantics=("arbitrary", "arbitrary"),
            vmem_limit_bytes=VMEM_LIMIT),
        name="mixer",
    )(xt, xt, win, lb, hg, lng, lnb, wm, bst, wout, l1g, l1b, rwh, rwl, rb, tri, upp)


def _sc_workers():
    info = plsc.get_sparse_core_info()
    return info.num_cores, info.num_cores * info.num_subcores


def _sc_dispatch(rows, dest, n_out):
    t_total, dw = rows.shape
    nc, nw = _sc_workers()
    per_w = t_total // nw
    mesh = plsc.VectorSubcoreMesh(core_axis_name="c", subcore_axis_name="s")

    @functools.partial(
        pl.kernel,
        out_type=jax.ShapeDtypeStruct((n_out, dw), rows.dtype),
        mesh=mesh,
        scratch_types=[pltpu.VMEM((SC_WINDOW,), jnp.int32) for _ in range(TOP_K)]
        + [pltpu.VMEM((SC_WINDOW, dw), rows.dtype), pltpu.SemaphoreType.DMA],
        name="sc_dispatch",
    )
    def k(x_hbm, i_hbm, o_hbm, i0, i1, i2, i3, rows_v, sem):
        wid = lax.axis_index("s") * nc + lax.axis_index("c")
        idx_bufs = (i0, i1, i2, i3)

        @pl.loop(0, per_w // SC_WINDOW)
        def _(j):
            base = wid * per_w + j * SC_WINDOW
            for kk in range(TOP_K):
                pltpu.sync_copy(i_hbm.at[pl.ds(kk * t_total + base, SC_WINDOW)], idx_bufs[kk])
            pltpu.sync_copy(x_hbm.at[pl.ds(base, SC_WINDOW)], rows_v)
            copies = [pltpu.async_copy(rows_v, o_hbm.at[idx_bufs[kk]], sem) for kk in range(TOP_K)]
            for cp in copies:
                cp.wait()

    return k(rows, dest)


def _sc_gather(table, idx):
    n = idx.shape[0]
    dw = table.shape[1]
    nc, nw = _sc_workers()
    per_w = n // nw
    mesh = plsc.VectorSubcoreMesh(core_axis_name="c", subcore_axis_name="s")

    @functools.partial(
        pl.kernel,
        out_type=jax.ShapeDtypeStruct((n, dw), table.dtype),
        mesh=mesh,
        scratch_types=[pltpu.VMEM((SC_WINDOW,), jnp.int32),
                       pltpu.VMEM((SC_WINDOW, dw), table.dtype),
                       pltpu.SemaphoreType.DMA],
        name="sc_gather",
    )
    def k(t_hbm, i_hbm, o_hbm, idx_v, rows_v, sem):
        wid = lax.axis_index("s") * nc + lax.axis_index("c")

        @pl.loop(0, per_w // SC_WINDOW)
        def _(j):
            base = wid * per_w + j * SC_WINDOW
            pltpu.sync_copy(i_hbm.at[pl.ds(base, SC_WINDOW)], idx_v)
            pltpu.async_copy(t_hbm.at[idx_v], rows_v, sem).wait()
            pltpu.sync_copy(rows_v, o_hbm.at[pl.ds(base, SC_WINDOW)])

    return k(table, idx)


def _expert_kernel(te_ref, tv_ref, xs_ref, w1_ref, w2_ref, b1g_ref, b1l_ref, b2_ref, perm_ref, y_ref,
                   w1g_scr, w1l_scr, w2_scr):
    i = pl.program_id(0)
    valid = tv_ref[i]
    f32 = jnp.float32
    bf16 = jnp.bfloat16
    expert_changed = (i == 0) | (te_ref[i] != te_ref[jnp.maximum(i - 1, 0)])

    @pl.when(valid == 0)
    def _():
        y_ref[...] = jnp.zeros_like(y_ref)

    @pl.when((valid > 0) & expert_changed)
    def _():
        w2_scr[...] = w2_ref[0].astype(bf16)
        perm = perm_ref[...]
        for c in range(2 * D_MODEL // PERM_BLOCK):
            blk = w1_ref[0, :, c * PERM_BLOCK:(c + 1) * PERM_BLOCK].astype(bf16)
            r = jnp.dot(blk, perm, preferred_element_type=f32).astype(bf16)
            half = PERM_BLOCK // 2
            w1g_scr[:, c * half:(c + 1) * half] = r[:, :half]
            w1l_scr[:, c * half:(c + 1) * half] = r[:, half:]

    def expert_rows(n_rows):
        a, b = _unpack_rows(xs_ref[:n_rows, :])
        keep = lax.broadcasted_iota(jnp.int32, (n_rows, 1), 0) < valid
        x = jnp.where(keep, jnp.concatenate([a, b], axis=1), 0.0).astype(bf16)
        hg = jnp.dot(x, w1g_scr[...], preferred_element_type=f32) + b1g_ref[0]
        hl = jnp.dot(x, w1l_scr[...], preferred_element_type=f32) + b1l_ref[0]
        xg = jnp.minimum(hg, SWIGLU_LIMIT)
        xl = jnp.clip(hl, -SWIGLU_LIMIT, SWIGLU_LIMIT)
        act = xg * _sigmoid(SWIGLU_ALPHA * xg) * (xl + 1.0)
        y = jnp.dot(act.astype(bf16), w2_scr[...], preferred_element_type=f32) + b2_ref[0]
        y_ref[:n_rows, :] = _pack_rows(y)
        if n_rows < TM:
            y_ref[n_rows:, :] = jnp.zeros((TM - n_rows, HALF), jnp.int32)

    @pl.when(valid > TM // 2)
    def _():
        expert_rows(TM)

    @pl.when((valid > 0) & (valid <= TM // 2))
    def _():
        expert_rows(TM // 2)


def _experts(xs, tile_expert, tile_valid, w1, w2, b1g, b1l, b2, perm):
    n_slots = xs.shape[0]
    n_tiles = n_slots // TM
    d_ff = w2.shape[1]
    wmap = lambda i, te, tv: (te[i], 0, 0)
    grid_spec = pltpu.PrefetchScalarGridSpec(
        num_scalar_prefetch=2,
        grid=(n_tiles,),
        in_specs=[
            pl.BlockSpec((TM, HALF), lambda i, te, tv: (i, 0)),
            pl.BlockSpec((1, D_MODEL, 2 * d_ff), wmap),
            pl.BlockSpec((1, d_ff, D_MODEL), wmap),
            pl.BlockSpec((1, 1, d_ff), wmap),
            pl.BlockSpec((1, 1, d_ff), wmap),
            pl.BlockSpec((1, 1, D_MODEL), wmap),
            pl.BlockSpec((PERM_BLOCK, PERM_BLOCK), lambda i, te, tv: (0, 0)),
        ],
        out_specs=pl.BlockSpec((TM, HALF), lambda i, te, tv: (i, 0)),
        scratch_shapes=[
            pltpu.VMEM((D_MODEL, d_ff), jnp.bfloat16),
            pltpu.VMEM((D_MODEL, d_ff), jnp.bfloat16),
            pltpu.VMEM((d_ff, D_MODEL), jnp.bfloat16),
        ],
    )
    return pl.pallas_call(
        _expert_kernel,
        grid_spec=grid_spec,
        out_shape=jax.ShapeDtypeStruct((n_slots, HALF), jnp.int32),
        compiler_params=pltpu.CompilerParams(
            dimension_semantics=("arbitrary",),
            vmem_limit_bytes=VMEM_LIMIT),
        name="experts",
    )(tile_expert, tile_valid, xs, w1, w2, b1g, b1l, b2, perm)


def _combine_kernel(h_ref, yk_ref, gate_ref, g_ref, b_ref, *rest):
    o_ref = rest[-1]
    gates = gate_ref[...]
    acc_a = None
    acc_b = None
    for k in range(TOP_K):
        a, b = _unpack_rows(yk_ref[k])
        gk = gates[:, k:k + 1]
        acc_a = gk * a if acc_a is None else acc_a + gk * a
        acc_b = gk * b if acc_b is None else acc_b + gk * b
    ffn = jnp.concatenate([acc_a, acc_b], axis=1)
    o_ref[...] = _layer_norm(ALPHA * h_ref[...] + ffn, g_ref[...], b_ref[...])


def _combine(h1f, yk, gates, g2, b2, out_prev, part, t_all):
    t_part = h1f.shape[0]
    blk0 = part * (t_part // TC3)
    row = lambda i: (i, 0)
    const = lambda i: (0, 0)
    in_specs = [
        pl.BlockSpec((TC3, D_MODEL), row),
        pl.BlockSpec((TOP_K, TC3, HALF), lambda i: (0, i, 0)),
        pl.BlockSpec((TC3, TOP_K), row),
        pl.BlockSpec((1, D_MODEL), const),
        pl.BlockSpec((1, D_MODEL), const),
    ]
    args = [h1f, yk, gates, g2, b2]
    aliases = {}
    if out_prev is not None:
        in_specs.append(pl.BlockSpec(memory_space=pl.ANY))
        args.append(out_prev)
        aliases = {len(args) - 1: 0}
    return pl.pallas_call(
        _combine_kernel,
        grid=(t_part // TC3,),
        in_specs=in_specs,
        out_specs=pl.BlockSpec((TC3, D_MODEL), lambda i: (i + blk0, 0)),
        out_shape=jax.ShapeDtypeStruct((t_all, D_MODEL), jnp.float32),
        input_output_aliases=aliases,
        compiler_params=pltpu.CompilerParams(
            dimension_semantics=("arbitrary",),
            vmem_limit_bytes=VMEM_LIMIT),
        name="combine",
    )(*args)


def kernel(x, w_in, lb_logits, hgrn_norm_g, gmlp_ln_g, gmlp_ln_b, gmlp_ws, gmlp_bs, w_out, ln1_g, ln1_b, router_w, router_b, exp_w1, exp_b1, exp_w2, exp_b2, ln2_g, ln2_b):
    batch, seq, d = x.shape
    assert d == D_MODEL and seq % TB == 0 and w_in.shape[0] == 1
    t_total = batch * seq
    f32 = jnp.float32
    bf16 = jnp.bfloat16

    lb = jnp.cumsum(jax.nn.softmax(lb_logits.astype(f32), axis=0), axis=0)[0:1]
    chunk_id = jnp.arange(GMLP_BLOCK) // CHUNK
    wm = jnp.where((chunk_id[None, :] <= chunk_id[:, None])[None], gmlp_ws[0], 0.0).astype(bf16)
    rwt = router_w[0].T
    rwh = rwt.astype(bf16)
    rwl = (rwt - rwh.astype(f32)).astype(bf16)
    tri = (jnp.arange(CHUNK)[None, :] <= jnp.arange(CHUNK)[:, None]).astype(bf16)
    upp = (jnp.arange(TB)[:, None] < jnp.arange(TB)[None, :]).astype(bf16)

    lane = jnp.arange(PERM_BLOCK)
    src = jnp.where(lane < PERM_BLOCK // 2, 2 * lane, 2 * (lane - PERM_BLOCK // 2) + 1)
    perm = (jnp.arange(PERM_BLOCK)[:, None] == src[None, :]).astype(bf16)
    xt = x.reshape(t_total, d)
    win = w_in[0].astype(bf16)
    wout = w_out[0].astype(bf16)
    b1g, b1l, b2e = exp_b1[0][:, None, 0::2], exp_b1[0][:, None, 1::2], exp_b2[0][:, None, :]
    experts = jnp.arange(N_EXPERTS, dtype=jnp.int32)

    assert batch % N_PARTS == 0
    pb = batch // N_PARTS
    t_part = pb * seq
    n_slots = t_part * TOP_K + N_EXPERTS * TM
    n_tiles = n_slots // TM
    out = None
    for part in range(N_PARTS):
        h1f, h1p, idx, gates, rank, cnt = _mixer(
            xt, win, lb, hgrn_norm_g[0:1], gmlp_ln_g[0:1], gmlp_ln_b[0:1], wm, gmlp_bs[0].T, wout,
            ln1_g[0:1], ln1_b[0:1], rwh, rwl, router_b[0][:, None], tri, upp, seq, pb, part * pb)

        counts = cnt[:, 0].astype(jnp.int32)
        padded = ((counts + TM - 1) // TM) * TM
        ends = jnp.cumsum(padded)
        starts = ends - padded
        start_of = jnp.sum(jnp.where(idx[..., None] == experts, starts, 0), axis=-1)
        dest = start_of + rank
        dest_kt = dest.transpose(1, 0, 2).reshape(TOP_K * t_part)
        gates_tk = gates.transpose(0, 2, 1).reshape(t_part, TOP_K)
        tile_row = jnp.arange(n_tiles, dtype=jnp.int32) * TM
        tile_expert = jnp.minimum(
            jnp.sum((tile_row[:, None] >= ends[None, :]).astype(jnp.int32), axis=1), N_EXPERTS - 1)
        tile_valid = jnp.clip(starts[tile_expert] + counts[tile_expert] - tile_row, 0, TM)

        xs = _sc_dispatch(h1p, dest_kt, n_slots)
        y = _experts(xs, tile_expert, tile_valid.astype(jnp.int32), exp_w1[0], exp_w2[0],
                     b1g, b1l, b2e, perm)
        yk = _sc_gather(y, dest_kt).reshape(TOP_K, t_part, HALF)
        out = _combine(h1f, yk, gates_tk, ln2_g[0:1], ln2_b[0:1], out, part, t_total)
    return out.reshape(batch, seq, d)
```

```python
import functools
import math

import jax
import jax.numpy as jnp
from jax import lax
from jax.experimental import pallas as pl
from jax.experimental.pallas import tpu as pltpu
from jax.experimental.pallas import tpu_sc as plsc

D_MODEL = 1024
CHUNK = 64
SUB = 16
N_SUB = CHUNK // SUB
MAX_SUB_DECAY = 60.0
HGRN_WIDTH = 512
HGRN_HEADS = 4
HEAD_DIM = 128
GMLP_WIDTH = 512
GMLP_BLOCK = 128
GMLP_GROUPS = 4
IN_COLS = 3072
N_EXPERTS = 32
TOP_K = 4
SWIGLU_LIMIT = 7.0
SWIGLU_ALPHA = 1.702
ALPHA = 2.0 ** 0.25
EPS = 1e-5
HALF = D_MODEL // 2

TB = 512
TM = 512
TC3 = 512
N_PARTS = 2
SC_WINDOW = 128
PROJ_GROUP = 256
PERM_BLOCK = 256
VMEM_LIMIT = 56 * 1024 * 1024

_NT = (((1,), (1,)), ((), ()))


def _sigmoid(x):
    z = jnp.exp(-jnp.abs(x))
    r = 1.0 / (1.0 + z)
    return jnp.where(x >= 0, r, z * r)


def _gelu(x):
    return 0.5 * x * (1.0 + lax.erf(x * (1.0 / math.sqrt(2.0))))


def _layer_norm(x, g, b):
    mu = jnp.mean(x, axis=-1, keepdims=True)
    xc = x - mu
    var = jnp.mean(xc * xc, axis=-1, keepdims=True)
    return xc * lax.rsqrt(var + EPS) * g + b


def _pack_rows(h):
    a = h[:, :HALF].astype(jnp.bfloat16).astype(jnp.float32)
    b = h[:, HALF:].astype(jnp.bfloat16).astype(jnp.float32)
    au = lax.bitcast_convert_type(a, jnp.uint32) >> 16
    bu = lax.bitcast_convert_type(b, jnp.uint32) & jnp.uint32(0xFFFF0000)
    return lax.bitcast_convert_type(au | bu, jnp.int32)


def _unpack_rows(w):
    u = lax.bitcast_convert_type(w, jnp.uint32)
    a = lax.bitcast_convert_type(u << 16, jnp.float32)
    b = lax.bitcast_convert_type(u & jnp.uint32(0xFFFF0000), jnp.float32)
    return a, b


def _mixer_kernel(x_ref, xn_ref, win_ref, lb_ref, hg_ref, lng_ref, lnb_ref, wm_ref, bst_ref,
                  wout_ref, l1g_ref, l1b_ref, rwh_ref, rwl_ref, rb_ref, tri_ref, upp_ref,
                  h1f_ref, h1p_ref, idx_ref, gate_ref, rank_ref, cnt_ref,
                  p_scr, lf_scr, kk_scr, g_scr, kc_scr, o_scr, st_scr, carry_scr,
                  amat_scr, qg_scr, kd_scr, upd_scr, mix_scr):
    b = pl.program_id(0)
    t = pl.program_id(1)
    step = b * pl.num_programs(1) + t
    f32 = jnp.float32
    bf16 = jnp.bfloat16

    @pl.when(t == 0)
    def _():
        st_scr[...] = jnp.zeros_like(st_scr)

    @pl.when((b == 0) & (t == 0))
    def _():
        carry_scr[...] = jnp.zeros_like(carry_scr)

    def project(xb, group):
        cols = slice(group * PROJ_GROUP, (group + 1) * PROJ_GROUP)
        p_scr[:, cols] = jnp.dot(xb, win_ref[:, cols], preferred_element_type=f32)

    def hgrn_gates():
        fl = p_scr[:, HGRN_WIDTH:2 * HGRN_WIDTH]
        lb = lb_ref[...]
        z = jnp.exp(-jnp.abs(fl))
        r = 1.0 / (1.0 + z)
        zr = z * r
        pos = fl >= 0
        lf_scr[...] = jnp.log(lb + (1.0 - lb) * jnp.where(pos, r, zr))
        kk_scr[...] = (1.0 - lb) * jnp.where(pos, zr, r)

    @pl.when(step == 0)
    def _():
        xb0 = x_ref[...].astype(bf16)
        for group in range(IN_COLS // PROJ_GROUP):
            project(xb0, group)
        hgrn_gates()

    x = x_ref[...]

    tri = tri_ref[...]
    row64 = lax.broadcasted_iota(jnp.int32, (CHUNK, CHUNK), 0)
    col64 = lax.broadcasted_iota(jnp.int32, (CHUNK, CHUNK), 1)
    lane_sub = lax.broadcasted_iota(jnp.int32, (SUB, CHUNK), 1)
    scale = HEAD_DIM ** -0.5

    def pad_rows(piece, lo_r):
        parts = []
        if lo_r > 0:
            parts.append(jnp.zeros((lo_r, HEAD_DIM), f32))
        parts.append(piece)
        rest = CHUNK - lo_r - piece.shape[0]
        if rest > 0:
            parts.append(jnp.zeros((rest, HEAD_DIM), f32))
        return jnp.concatenate(parts, axis=0) if len(parts) > 1 else piece

    def intra_factorised(q, k, gh):
        qts, kts = [], []
        for a in range(N_SUB):
            lo_r, hi_r = a * SUB, (a + 1) * SUB
            if a == 0:
                qa = q[:hi_r] * jnp.exp(gh[:hi_r])
                ka = k[:hi_r] * jnp.exp(-gh[:hi_r])
            else:
                ra = gh[lo_r - 1:lo_r]
                qa = q[lo_r:hi_r] * jnp.exp(gh[lo_r:hi_r] - ra)
                ka = k[:hi_r] * jnp.exp(ra - gh[:hi_r])
            qts.append(pad_rows(qa, lo_r))
            kts.append(pad_rows(ka, 0))
        a_mat = lax.dot_general(jnp.concatenate(qts, axis=1).astype(bf16),
                                jnp.concatenate(kts, axis=1).astype(bf16), _NT,
                                preferred_element_type=f32)
        return jnp.where(row64 >= col64, a_mat, 0.0)

    def chunks_factorised():
        n_chunks = TB // CHUNK
        heads = [(c, h) for c in range(n_chunks) for h in range(HGRN_HEADS)]

        def cols(h, base=0):
            return slice(base + h * HEAD_DIM, base + (h + 1) * HEAD_DIM)

        for c in range(n_chunks):
            rows = slice(c * CHUNK, (c + 1) * CHUNK)
            lf = lf_scr[rows, :]
            hi = lf.astype(bf16)
            lo = (lf - hi.astype(f32)).astype(bf16)
            gg = jnp.dot(tri, jnp.concatenate([hi, lo], axis=1), preferred_element_type=f32)
            lf_scr[rows, :] = gg[:, :HGRN_WIDTH] + gg[:, HGRN_WIDTH:]
        for c, h in heads:
            rows = slice(c * CHUNK, (c + 1) * CHUNK)
            q = p_scr[rows, cols(h)]
            k = kk_scr[rows, cols(h)]
            gh = lf_scr[rows, cols(h)]
            amat_scr[c * HGRN_HEADS + h] = intra_factorised(q, k, gh).astype(bf16)
            qg_scr[rows, cols(h)] = (q * jnp.exp(gh)).astype(bf16)
            kd_scr[rows, cols(h)] = (k * jnp.exp(gh[CHUNK - 1:CHUNK] - gh)).astype(bf16)
        for c, h in heads:
            rows = slice(c * CHUNK, (c + 1) * CHUNK)
            v = p_scr[rows, cols(h, 2 * HGRN_WIDTH)]
            o_scr[rows, cols(h)] = jnp.dot(amat_scr[c * HGRN_HEADS + h], v.astype(bf16),
                                           preferred_element_type=f32)
            upd_scr[c * HGRN_HEADS + h] = jnp.dot(v.T.astype(bf16), kd_scr[rows, cols(h)],
                                                  preferred_element_type=f32)
        states = [st_scr[h] for h in range(HGRN_HEADS)]
        for c, h in heads:
            rows = slice(c * CHUNK, (c + 1) * CHUNK)
            st = states[h]
            o_inter = lax.dot_general(qg_scr[rows, cols(h)], st.astype(bf16), _NT,
                                      preferred_element_type=f32)
            o_scr[rows, cols(h)] = (o_scr[rows, cols(h)] + o_inter) * scale
            gl = lf_scr[(c + 1) * CHUNK - 1:(c + 1) * CHUNK, cols(h)]
            states[h] = st * jnp.exp(gl) + upd_scr[c * HGRN_HEADS + h]
        for h in range(HGRN_HEADS):
            st_scr[h] = states[h]

    def intra_exact_diagonal(q, k, gh, cs):
        qts, kts = [], []
        for a in range(1, N_SUB):
            lo_r, hi_r = a * SUB, (a + 1) * SUB
            ra = g_scr[lo_r - 1:lo_r, cs]
            qts.append(pad_rows(q[lo_r:hi_r] * jnp.exp(gh[lo_r:hi_r] - ra), lo_r))
            kts.append(pad_rows(k[:lo_r] * jnp.exp(ra - gh[:lo_r]), 0))
        a_off = lax.dot_general(jnp.concatenate(qts, axis=1).astype(bf16),
                                jnp.concatenate(kts, axis=1).astype(bf16), _NT,
                                preferred_element_type=f32)
        diag_rows = []
        for a in range(N_SUB):
            lo_r = a * SUB
            gs = gh[lo_r:lo_r + SUB]
            qs = q[lo_r:lo_r + SUB]
            blk = jnp.zeros((SUB, CHUNK), f32)
            for jl in range(SUB):
                j = lo_r + jl
                gj = g_scr[j:j + 1, cs]
                kj = kc_scr[j:j + 1, cs]
                e = jnp.exp(jnp.minimum(gs - gj, 0.0))
                col = jnp.sum(qs * (kj * e), axis=-1, keepdims=True)
                blk = jnp.where(lane_sub == j, col, blk)
            diag_rows.append(blk)
        a_diag = jnp.concatenate(diag_rows, axis=0)
        return a_off + jnp.where(row64 >= col64, a_diag, 0.0)

    def make_chunk_body(intra):
        def chunk_body(c, carry):
            r0 = pl.multiple_of(c * CHUNK, CHUNK)
            rows = pl.ds(r0, CHUNK)
            lf = lf_scr[rows, :]
            hi = lf.astype(bf16)
            lo = (lf - hi.astype(f32)).astype(bf16)
            gg = jnp.dot(tri, jnp.concatenate([hi, lo], axis=1), preferred_element_type=f32)
            g_all = gg[:, :HGRN_WIDTH] + gg[:, HGRN_WIDTH:]
            g_scr[...] = g_all
            kc_scr[...] = kk_scr[rows, :]
            for h in range(HGRN_HEADS):
                cs = slice(h * HEAD_DIM, (h + 1) * HEAD_DIM)
                q = p_scr[rows, h * HEAD_DIM:(h + 1) * HEAD_DIM]
                v = p_scr[rows, 2 * HGRN_WIDTH + h * HEAD_DIM:2 * HGRN_WIDTH + (h + 1) * HEAD_DIM]
                k = kc_scr[:, cs]
                gh = g_all[:, cs]
                st = st_scr[h]
                o_inter = lax.dot_general((q * jnp.exp(gh)).astype(bf16), st.astype(bf16), _NT,
                                          preferred_element_type=f32)
                a_mat = intra(q, k, gh, cs)
                o = jnp.dot(a_mat.astype(bf16), v.astype(bf16), preferred_element_type=f32) + o_inter
                o_scr[rows, h * HEAD_DIM:(h + 1) * HEAD_DIM] = o * scale

                gl = g_scr[CHUNK - 1:CHUNK, cs]
                kd = k * jnp.exp(gl - gh)
                upd = jnp.dot(v.T.astype(bf16), kd.astype(bf16), preferred_element_type=f32)
                st_scr[h] = st * jnp.exp(gl) + upd
            return carry
        return chunk_body

    lf_all = lf_scr[...]
    sub_decay = -jnp.sum(lf_all.reshape(TB // SUB, SUB, HGRN_WIDTH), axis=1)
    bounded = jnp.max(sub_decay) <= MAX_SUB_DECAY

    @pl.when(bounded)
    def _():
        chunks_factorised()

    @pl.when(jnp.logical_not(bounded))
    def _():
        lax.fori_loop(0, TB // CHUNK, make_chunk_body(intra_exact_diagonal), 0)

    xnb = xn_ref[...].astype(bf16)
    f_groups = list(range(HGRN_WIDTH // PROJ_GROUP, 2 * HGRN_WIDTH // PROJ_GROUP))
    pending = f_groups + [g for g in range(IN_COLS // PROJ_GROUP) if g not in f_groups]

    def project_next(count):
        for _ in range(count):
            project(xnb, pending.pop(0))

    project_next(3 * HGRN_WIDTH // PROJ_GROUP)
    for w in range(TB // GMLP_BLOCK):
        rows = slice(w * GMLP_BLOCK, (w + 1) * GMLP_BLOCK)
        o = o_scr[rows, :]
        ms = jnp.mean(o * o, axis=-1, keepdims=True)
        gate = p_scr[rows, 3 * HGRN_WIDTH:4 * HGRN_WIDTH]
        y_rec = o * lax.rsqrt(ms + EPS) * hg_ref[...] * (gate * _sigmoid(gate))
        u = _gelu(p_scr[rows, 4 * HGRN_WIDTH:4 * HGRN_WIDTH + GMLP_WIDTH])
        vn = _layer_norm(_gelu(p_scr[rows, 4 * HGRN_WIDTH + GMLP_WIDTH:]), lng_ref[...], lnb_ref[...])
        vnb = vn.astype(bf16)
        cols = []
        for g in range(GMLP_GROUPS):
            s = jnp.dot(wm_ref[g], vnb[:, g * HEAD_DIM:(g + 1) * HEAD_DIM],
                        preferred_element_type=f32) + bst_ref[:, g:g + 1]
            cols.append(s)
        y_sg = u * jnp.concatenate(cols, axis=1)
        mix_scr[rows, :] = jnp.concatenate([y_rec, y_sg], axis=1).astype(bf16)

    mix = jnp.dot(mix_scr[...], wout_ref[...], preferred_element_type=f32)
    hgrn_gates()
    h1 = _layer_norm(ALPHA * x + mix, l1g_ref[...], l1b_ref[...])
    h1f_ref[...] = h1
    h1p_ref[...] = _pack_rows(h1)
    project_next(2)

    hh = h1.astype(bf16)
    hl = (h1 - hh.astype(f32)).astype(bf16)
    rwh = rwh_ref[...]
    logits = (lax.dot_general(rwh, hh, _NT, preferred_element_type=f32)
              + lax.dot_general(rwh, hl, _NT, preferred_element_type=f32)
              + lax.dot_general(rwl_ref[...], hh, _NT, preferred_element_type=f32)
              + rb_ref[...])
    project_next(4)
    assert not pending

    e_iota = lax.broadcasted_iota(jnp.int32, (N_EXPERTS, TB), 0)
    work = logits
    vals, idxs, hots = [], [], []
    for _ in range(TOP_K):
        m = jnp.max(work, axis=0, keepdims=True)
        ik = jnp.min(jnp.where(work == m, e_iota, N_EXPERTS), axis=0, keepdims=True)
        hot = e_iota == ik
        work = jnp.where(hot, -jnp.inf, work)
        vals.append(m)
        idxs.append(ik)
        hots.append(hot)
    exps = [jnp.exp(vk - vals[0]) for vk in vals]
    denom = exps[0] + exps[1] + exps[2] + exps[3]
    gate_ref[0] = jnp.concatenate([ek / denom for ek in exps], axis=0)
    idx_ref[0] = jnp.concatenate(idxs, axis=0)

    hot_any = jnp.where(hots[0] | hots[1] | hots[2] | hots[3], 1.0, 0.0)
    prefix = jnp.dot(hot_any.astype(bf16), upp_ref[...], preferred_element_type=f32)
    base = prefix + carry_scr[:, 0:1]
    ranks = [jnp.sum(jnp.where(hk, base, 0.0), axis=0, keepdims=True) for hk in hots]
    rank_ref[0] = jnp.concatenate(ranks, axis=0).astype(jnp.int32)
    new_carry = carry_scr[...] + jnp.sum(hot_any, axis=1, keepdims=True)
    carry_scr[...] = new_carry
    cnt_ref[...] = new_carry


def _mixer(xt, win, lb, hg, lng, lnb, wm, bst, wout, l1g, l1b, rwh, rwl, rb, tri, upp,
           seq, batch, batch0):
    nt = seq // TB
    t_total = batch * seq
    nblk = batch * nt
    const2 = lambda b, t: (0, 0)
    const3 = lambda b, t: (0, 0, 0)
    row_blk = lambda b, t: (b * nt + t, 0)
    meta_blk = lambda b, t: (b * nt + t, 0, 0)
    once = dict(pipeline_mode=pl.Buffered(1))
    in_specs = [
        pl.BlockSpec((TB, D_MODEL), lambda b, t: ((b + batch0) * nt + t, 0)),
        pl.BlockSpec((TB, D_MODEL),
                     lambda b, t: (batch0 * nt + jnp.minimum(b * nt + t + 1, nblk - 1), 0)),
        pl.BlockSpec((D_MODEL, IN_COLS), const2, **once),
        pl.BlockSpec((1, HGRN_WIDTH), const2),
        pl.BlockSpec((1, HGRN_WIDTH), const2),
        pl.BlockSpec((1, GMLP_WIDTH), const2),
        pl.BlockSpec((1, GMLP_WIDTH), const2),
        pl.BlockSpec((GMLP_GROUPS, GMLP_BLOCK, GMLP_BLOCK), const3),
        pl.BlockSpec((GMLP_BLOCK, GMLP_GROUPS), const2),
        pl.BlockSpec((D_MODEL, D_MODEL), const2, **once),
        pl.BlockSpec((1, D_MODEL), const2),
        pl.BlockSpec((1, D_MODEL), const2),
        pl.BlockSpec((N_EXPERTS, D_MODEL), const2),
        pl.BlockSpec((N_EXPERTS, D_MODEL), const2),
        pl.BlockSpec((N_EXPERTS, 1), const2),
        pl.BlockSpec((CHUNK, CHUNK), const2),
        pl.BlockSpec((TB, TB), const2, **once),
    ]
    out_shape = [
        jax.ShapeDtypeStruct((t_total, D_MODEL), jnp.float32),
        jax.ShapeDtypeStruct((t_total, HALF), jnp.int32),
        jax.ShapeDtypeStruct((nblk, TOP_K, TB), jnp.int32),
        jax.ShapeDtypeStruct((nblk, TOP_K, TB), jnp.float32),
        jax.ShapeDtypeStruct((nblk, TOP_K, TB), jnp.int32),
        jax.ShapeDtypeStruct((N_EXPERTS, 128), jnp.float32),
    ]
    out_specs = [
        pl.BlockSpec((TB, D_MODEL), row_blk),
        pl.BlockSpec((TB, HALF), row_blk),
        pl.BlockSpec((1, TOP_K, TB), meta_blk),
        pl.BlockSpec((1, TOP_K, TB), meta_blk),
        pl.BlockSpec((1, TOP_K, TB), meta_blk),
        pl.BlockSpec((N_EXPERTS, 128), const2),
    ]
    scratch = [
        pltpu.VMEM((TB, IN_COLS), jnp.float32),
        pltpu.VMEM((TB, HGRN_WIDTH), jnp.float32),
        pltpu.VMEM((TB, HGRN_WIDTH), jnp.float32),
        pltpu.VMEM((CHUNK, HGRN_WIDTH), jnp.float32),
        pltpu.VMEM((CHUNK, HGRN_WIDTH), jnp.float32),
        pltpu.VMEM((TB, HGRN_WIDTH), jnp.float32),
        pltpu.VMEM((HGRN_HEADS, HEAD_DIM, HEAD_DIM), jnp.float32),
        pltpu.VMEM((N_EXPERTS, 128), jnp.float32),
        pltpu.VMEM((TB // CHUNK * HGRN_HEADS, CHUNK, CHUNK), jnp.bfloat16),
        pltpu.VMEM((TB, HGRN_WIDTH), jnp.bfloat16),
        pltpu.VMEM((TB, HGRN_WIDTH), jnp.bfloat16),
        pltpu.VMEM((TB // CHUNK * HGRN_HEADS, HEAD_DIM, HEAD_DIM), jnp.float32),
        pltpu.VMEM((TB, D_MODEL), jnp.bfloat16),
    ]
    return pl.pallas_call(
        _mixer_kernel,
        grid=(batch, nt),
        in_specs=in_specs,
        out_specs=out_specs,
        out_shape=out_shape,
        scratch_shapes=scratch,
        compiler_params=pltpu.CompilerParams(
            dimension_semantics=("arbitrary", "arbitrary"),
            vmem_limit_bytes=VMEM_LIMIT),
        name="mixer",
    )(xt, xt, win, lb, hg, lng, lnb, wm, bst, wout, l1g, l1b, rwh, rwl, rb, tri, upp)


def _sc_workers():
    info = plsc.get_sparse_core_info()
    return info.num_cores, info.num_cores * info.num_subcores


def _sc_dispatch(rows, dest, n_out):
    t_total, dw = rows.shape
    nc, nw = _sc_workers()
    per_w = t_total // nw
    mesh = plsc.VectorSubcoreMesh(core_axis_name="c", subcore_axis_name="s")

    @functools.partial(
        pl.kernel,
        out_type=jax.ShapeDtypeStruct((n_out, dw), rows.dtype),
        mesh=mesh,
        scratch_types=[pltpu.VMEM((SC_WINDOW,), jnp.int32) for _ in range(TOP_K)]
        + [pltpu.VMEM((SC_WINDOW, dw), rows.dtype), pltpu.SemaphoreType.DMA],
        name="sc_dispatch",
    )
    def k(x_hbm, i_hbm, o_hbm, i0, i1, i2, i3, rows_v, sem):
        wid = lax.axis_index("s") * nc + lax.axis_index("c")
        idx_bufs = (i0, i1, i2, i3)

        @pl.loop(0, per_w // SC_WINDOW)
        def _(j):
            base = wid * per_w + j * SC_WINDOW
            for kk in range(TOP_K):
                pltpu.sync_copy(i_hbm.at[pl.ds(kk * t_total + base, SC_WINDOW)], idx_bufs[kk])
            pltpu.sync_copy(x_hbm.at[pl.ds(base, SC_WINDOW)], rows_v)
            copies = [pltpu.async_copy(rows_v, o_hbm.at[idx_bufs[kk]], sem) for kk in range(TOP_K)]
            for cp in copies:
                cp.wait()

    return k(rows, dest)


def _sc_gather(table, idx):
    n = idx.shape[0]
    dw = table.shape[1]
    nc, nw = _sc_workers()
    per_w = n // nw
    mesh = plsc.VectorSubcoreMesh(core_axis_name="c", subcore_axis_name="s")

    @functools.partial(
        pl.kernel,
        out_type=jax.ShapeDtypeStruct((n, dw), table.dtype),
        mesh=mesh,
        scratch_types=[pltpu.VMEM((SC_WINDOW,), jnp.int32),
                       pltpu.VMEM((SC_WINDOW, dw), table.dtype),
                       pltpu.SemaphoreType.DMA],
        name="sc_gather",
    )
    def k(t_hbm, i_hbm, o_hbm, idx_v, rows_v, sem):
        wid = lax.axis_index("s") * nc + lax.axis_index("c")

        @pl.loop(0, per_w // SC_WINDOW)
        def _(j):
            base = wid * per_w + j * SC_WINDOW
            pltpu.sync_copy(i_hbm.at[pl.ds(base, SC_WINDOW)], idx_v)
            pltpu.async_copy(t_hbm.at[idx_v], rows_v, sem).wait()
            pltpu.sync_copy(rows_v, o_hbm.at[pl.ds(base, SC_WINDOW)])

    return k(table, idx)


def _expert_kernel(te_ref, tv_ref, tn_ref, xs_ref, w1_hbm, w2_hbm, b1g_ref, b1l_ref, b2_ref,
                   perm_ref, y_ref, w1f_scr, w2f_scr, w1g_scr, w1l_scr, w2_scr, sems):
    i = pl.program_id(0)
    valid = tv_ref[i]
    expert = te_ref[i]
    f32 = jnp.float32
    bf16 = jnp.bfloat16
    expert_changed = (i == 0) | (expert != te_ref[jnp.maximum(i - 1, 0)])

    def weight_copies(e):
        return (pltpu.make_async_copy(w1_hbm.at[e], w1f_scr, sems.at[0]),
                pltpu.make_async_copy(w2_hbm.at[e], w2f_scr, sems.at[1]))

    @pl.when(valid == 0)
    def _():
        y_ref[...] = jnp.zeros_like(y_ref)

    @pl.when(i == 0)
    def _():
        for cp in weight_copies(expert):
            cp.start()

    @pl.when((valid > 0) & expert_changed)
    def _():
        for cp in weight_copies(expert):
            cp.wait()
        w2_scr[...] = w2f_scr[...].astype(bf16)
        perm = perm_ref[...]
        for c in range(2 * D_MODEL // PERM_BLOCK):
            blk = w1f_scr[:, c * PERM_BLOCK:(c + 1) * PERM_BLOCK].astype(bf16)
            r = jnp.dot(blk, perm, preferred_element_type=f32).astype(bf16)
            half = PERM_BLOCK // 2
            w1g_scr[:, c * half:(c + 1) * half] = r[:, :half]
            w1l_scr[:, c * half:(c + 1) * half] = r[:, half:]

        @pl.when(tn_ref[i] != expert)
        def _():
            for cp in weight_copies(tn_ref[i]):
                cp.start()

    def expert_rows(n_rows):
        a, b = _unpack_rows(xs_ref[:n_rows, :])
        keep = lax.broadcasted_iota(jnp.int32, (n_rows, 1), 0) < valid
        x = jnp.where(keep, jnp.concatenate([a, b], axis=1), 0.0).astype(bf16)
        hg = jnp.dot(x, w1g_scr[...], preferred_element_type=f32) + b1g_ref[0]
        hl = jnp.dot(x, w1l_scr[...], preferred_element_type=f32) + b1l_ref[0]
        xg = jnp.minimum(hg, SWIGLU_LIMIT)
        xl = jnp.clip(hl, -SWIGLU_LIMIT, SWIGLU_LIMIT)
        act = xg * _sigmoid(SWIGLU_ALPHA * xg) * (xl + 1.0)
        y = jnp.dot(act.astype(bf16), w2_scr[...], preferred_element_type=f32) + b2_ref[0]
        y_ref[:n_rows, :] = _pack_rows(y)
        if n_rows < TM:
            y_ref[n_rows:, :] = jnp.zeros((TM - n_rows, HALF), jnp.int32)

    @pl.when(valid > TM // 2)
    def _():
        expert_rows(TM)

    @pl.when((valid > 0) & (valid <= TM // 2))
    def _():
        expert_rows(TM // 2)


def _experts(xs, tile_expert, tile_valid, tile_next, w1, w2, b1g, b1l, b2, perm):
    n_slots = xs.shape[0]
    n_tiles = n_slots // TM
    d_ff = w2.shape[1]
    wmap = lambda i, te, tv, tn: (te[i], 0, 0)
    grid_spec = pltpu.PrefetchScalarGridSpec(
        num_scalar_prefetch=3,
        grid=(n_tiles,),
        in_specs=[
            pl.BlockSpec((TM, HALF), lambda i, te, tv, tn: (i, 0)),
            pl.BlockSpec(memory_space=pl.ANY),
            pl.BlockSpec(memory_space=pl.ANY),
            pl.BlockSpec((1, 1, d_ff), wmap),
            pl.BlockSpec((1, 1, d_ff), wmap),
            pl.BlockSpec((1, 1, D_MODEL), wmap),
            pl.BlockSpec((PERM_BLOCK, PERM_BLOCK), lambda i, te, tv, tn: (0, 0)),
        ],
        out_specs=pl.BlockSpec((TM, HALF), lambda i, te, tv, tn: (i, 0)),
        scratch_shapes=[
            pltpu.VMEM((D_MODEL, 2 * d_ff), jnp.float32),
            pltpu.VMEM((d_ff, D_MODEL), jnp.float32),
            pltpu.VMEM((D_MODEL, d_ff), jnp.bfloat16),
            pltpu.VMEM((D_MODEL, d_ff), jnp.bfloat16),
            pltpu.VMEM((d_ff, D_MODEL), jnp.bfloat16),
            pltpu.SemaphoreType.DMA((2,)),
        ],
    )
    return pl.pallas_call(
        _expert_kernel,
        grid_spec=grid_spec,
        out_shape=jax.ShapeDtypeStruct((n_slots, HALF), jnp.int32),
        compiler_params=pltpu.CompilerParams(
            dimension_semantics=("arbitrary",),
            vmem_limit_bytes=VMEM_LIMIT),
        name="experts",
    )(tile_expert, tile_valid, tile_next, xs, w1, w2, b1g, b1l, b2, perm)


def _combine_kernel(h_ref, yk_ref, gate_ref, g_ref, b_ref, *rest):
    o_ref = rest[-1]
    gates = gate_ref[...]
    acc_a = None
    acc_b = None
    for k in range(TOP_K):
        a, b = _unpack_rows(yk_ref[k])
        gk = gates[:, k:k + 1]
        acc_a = gk * a if acc_a is None else acc_a + gk * a
        acc_b = gk * b if acc_b is None else acc_b + gk * b
    ffn = jnp.concatenate([acc_a, acc_b], axis=1)
    o_ref[...] = _layer_norm(ALPHA * h_ref[...] + ffn, g_ref[...], b_ref[...])


def _combine(h1f, yk, gates, g2, b2, out_prev, part, t_all):
    t_part = h1f.shape[0]
    blk0 = part * (t_part // TC3)
    row = lambda i: (i, 0)
    const = lambda i: (0, 0)
    in_specs = [
        pl.BlockSpec((TC3, D_MODEL), row),
        pl.BlockSpec((TOP_K, TC3, HALF), lambda i: (0, i, 0)),
        pl.BlockSpec((TC3, TOP_K), row),
        pl.BlockSpec((1, D_MODEL), const),
        pl.BlockSpec((1, D_MODEL), const),
    ]
    args = [h1f, yk, gates, g2, b2]
    aliases = {}
    if out_prev is not None:
        in_specs.append(pl.BlockSpec(memory_space=pl.ANY))
        args.append(out_prev)
        aliases = {len(args) - 1: 0}
    return pl.pallas_call(
        _combine_kernel,
        grid=(t_part // TC3,),
        in_specs=in_specs,
        out_specs=pl.BlockSpec((TC3, D_MODEL), lambda i: (i + blk0, 0)),
        out_shape=jax.ShapeDtypeStruct((t_all, D_MODEL), jnp.float32),
        input_output_aliases=aliases,
        compiler_params=pltpu.CompilerParams(
            dimension_semantics=("arbitrary",),
            vmem_limit_bytes=VMEM_LIMIT),
        name="combine",
    )(*args)


def kernel(x, w_in, lb_logits, hgrn_norm_g, gmlp_ln_g, gmlp_ln_b, gmlp_ws, gmlp_bs, w_out, ln1_g, ln1_b, router_w, router_b, exp_w1, exp_b1, exp_w2, exp_b2, ln2_g, ln2_b):
    batch, seq, d = x.shape
    assert d == D_MODEL and seq % TB == 0 and w_in.shape[0] == 1
    t_total = batch * seq
    f32 = jnp.float32
    bf16 = jnp.bfloat16

    lb = jnp.cumsum(jax.nn.softmax(lb_logits.astype(f32), axis=0), axis=0)[0:1]
    chunk_id = jnp.arange(GMLP_BLOCK) // CHUNK
    wm = jnp.where((chunk_id[None, :] <= chunk_id[:, None])[None], gmlp_ws[0], 0.0).astype(bf16)
    rwt = router_w[0].T
    rwh = rwt.astype(bf16)
    rwl = (rwt - rwh.astype(f32)).astype(bf16)
    tri = (jnp.arange(CHUNK)[None, :] <= jnp.arange(CHUNK)[:, None]).astype(bf16)
    upp = (jnp.arange(TB)[:, None] < jnp.arange(TB)[None, :]).astype(bf16)

    lane = jnp.arange(PERM_BLOCK)
    src = jnp.where(lane < PERM_BLOCK // 2, 2 * lane, 2 * (lane - PERM_BLOCK // 2) + 1)
    perm = (jnp.arange(PERM_BLOCK)[:, None] == src[None, :]).astype(bf16)
    xt = x.reshape(t_total, d)
    win = w_in[0].astype(bf16)
    wout = w_out[0].astype(bf16)
    b1g, b1l, b2e = exp_b1[0][:, None, 0::2], exp_b1[0][:, None, 1::2], exp_b2[0][:, None, :]
    experts = jnp.arange(N_EXPERTS, dtype=jnp.int32)

    assert batch % N_PARTS == 0
    pb = batch // N_PARTS
    t_part = pb * seq
    n_slots = t_part * TOP_K + N_EXPERTS * TM
    n_tiles = n_slots // TM
    out = None
    for part in range(N_PARTS):
        h1f, h1p, idx, gates, rank, cnt = _mixer(
            xt, win, lb, hgrn_norm_g[0:1], gmlp_ln_g[0:1], gmlp_ln_b[0:1], wm, gmlp_bs[0].T, wout,
            ln1_g[0:1], ln1_b[0:1], rwh, rwl, router_b[0][:, None], tri, upp, seq, pb, part * pb)

        counts = cnt[:, 0].astype(jnp.int32)
        padded = ((counts + TM - 1) // TM) * TM
        ends = jnp.cumsum(padded)
        starts = ends - padded
        start_of = jnp.sum(jnp.where(idx[..., None] == experts, starts, 0), axis=-1)
        dest = start_of + rank
        dest_kt = dest.transpose(1, 0, 2).reshape(TOP_K * t_part)
        gates_tk = gates.transpose(0, 2, 1).reshape(t_part, TOP_K)
        tile_row = jnp.arange(n_tiles, dtype=jnp.int32) * TM
        tile_expert = jnp.minimum(
            jnp.sum((tile_row[:, None] >= ends[None, :]).astype(jnp.int32), axis=1), N_EXPERTS - 1)
        tile_valid = jnp.clip(starts[tile_expert] + counts[tile_expert] - tile_row, 0, TM)
        later = lax.cummin(jnp.where(counts > 0, experts, N_EXPERTS), reverse=True)
        nxt = jnp.concatenate([later[1:], jnp.full((1,), N_EXPERTS, jnp.int32)])
        nxt = jnp.where(nxt == N_EXPERTS, experts, nxt)
        tile_next = nxt[tile_expert]

        xs = _sc_dispatch(h1p, dest_kt, n_slots)
        y = _experts(xs, tile_expert, tile_valid.astype(jnp.int32), tile_next.astype(jnp.int32),
                     exp_w1[0], exp_w2[0], b1g, b1l, b2e, perm)
        yk = _sc_gather(y, dest_kt).reshape(TOP_K, t_part, HALF)
        out = _combine(h1f, yk, gates_tk, ln2_g[0:1], ln2_b[0:1], out, part, t_total)
    return out.reshape(batch, seq, d)
```

```python
import functools
import math

import jax
import jax.numpy as jnp
from jax import lax
from jax.experimental import pallas as pl
from jax.experimental.pallas import tpu as pltpu
from jax.experimental.pallas import tpu_sc as plsc

D_MODEL = 1024
CHUNK = 64
SUB = 16
N_SUB = CHUNK // SUB
MAX_SUB_DECAY = 60.0
HGRN_WIDTH = 512
HGRN_HEADS = 4
HEAD_DIM = 128
GMLP_WIDTH = 512
GMLP_BLOCK = 128
GMLP_GROUPS = 4
IN_COLS = 3072
N_EXPERTS = 32
TOP_K = 4
SWIGLU_LIMIT = 7.0
SWIGLU_ALPHA = 1.702
ALPHA = 2.0 ** 0.25
EPS = 1e-5
HALF = D_MODEL // 2

TB = 512
TM = 1024
ROW_STEP = 256
TC3 = 512
N_PARTS = 2
SC_WINDOW = 128
PROJ_GROUP = 256
PERM_BLOCK = 256
VMEM_LIMIT = 56 * 1024 * 1024

_NT = (((1,), (1,)), ((), ()))


def _sigmoid(x):
    z = jnp.exp(-jnp.abs(x))
    r = 1.0 / (1.0 + z)
    return jnp.where(x >= 0, r, z * r)


def _gelu(x):
    return 0.5 * x * (1.0 + lax.erf(x * (1.0 / math.sqrt(2.0))))


def _layer_norm(x, g, b):
    mu = jnp.mean(x, axis=-1, keepdims=True)
    xc = x - mu
    var = jnp.mean(xc * xc, axis=-1, keepdims=True)
    return xc * lax.rsqrt(var + EPS) * g + b


def _pack_rows(h):
    a = h[:, :HALF].astype(jnp.bfloat16).astype(jnp.float32)
    b = h[:, HALF:].astype(jnp.bfloat16).astype(jnp.float32)
    au = lax.bitcast_convert_type(a, jnp.uint32) >> 16
    bu = lax.bitcast_convert_type(b, jnp.uint32) & jnp.uint32(0xFFFF0000)
    return lax.bitcast_convert_type(au | bu, jnp.int32)


def _unpack_rows(w):
    u = lax.bitcast_convert_type(w, jnp.uint32)
    a = lax.bitcast_convert_type(u << 16, jnp.float32)
    b = lax.bitcast_convert_type(u & jnp.uint32(0xFFFF0000), jnp.float32)
    return a, b


def _mixer_kernel(x_ref, xn_ref, win_ref, lb_ref, hg_ref, lng_ref, lnb_ref, wm_ref, bst_ref,
                  wout_ref, l1g_ref, l1b_ref, rwh_ref, rwl_ref, rb_ref, tri_ref, upp_ref,
                  h1f_ref, h1p_ref, idx_ref, gate_ref, rank_ref, cnt_ref,
                  p_scr, lf_scr, kk_scr, g_scr, kc_scr, o_scr, st_scr, carry_scr,
                  amat_scr, qg_scr, kd_scr, upd_scr, mix_scr):
    b = pl.program_id(0)
    t = pl.program_id(1)
    step = b * pl.num_programs(1) + t
    f32 = jnp.float32
    bf16 = jnp.bfloat16

    @pl.when(t == 0)
    def _():
        st_scr[...] = jnp.zeros_like(st_scr)

    @pl.when((b == 0) & (t == 0))
    def _():
        carry_scr[...] = jnp.zeros_like(carry_scr)

    def project(xb, group):
        cols = slice(group * PROJ_GROUP, (group + 1) * PROJ_GROUP)
        p_scr[:, cols] = jnp.dot(xb, win_ref[:, cols], preferred_element_type=f32)

    def hgrn_gates():
        fl = p_scr[:, HGRN_WIDTH:2 * HGRN_WIDTH]
        lb = lb_ref[...]
        z = jnp.exp(-jnp.abs(fl))
        r = 1.0 / (1.0 + z)
        zr = z * r
        pos = fl >= 0
        lf_scr[...] = jnp.log(lb + (1.0 - lb) * jnp.where(pos, r, zr))
        kk_scr[...] = (1.0 - lb) * jnp.where(pos, zr, r)

    @pl.when(step == 0)
    def _():
        xb0 = x_ref[...].astype(bf16)
        for group in range(IN_COLS // PROJ_GROUP):
            project(xb0, group)
        hgrn_gates()

    x = x_ref[...]

    tri = tri_ref[...]
    row64 = lax.broadcasted_iota(jnp.int32, (CHUNK, CHUNK), 0)
    col64 = lax.broadcasted_iota(jnp.int32, (CHUNK, CHUNK), 1)
    lane_sub = lax.broadcasted_iota(jnp.int32, (SUB, CHUNK), 1)
    scale = HEAD_DIM ** -0.5

    def pad_rows(piece, lo_r):
        parts = []
        if lo_r > 0:
            parts.append(jnp.zeros((lo_r, HEAD_DIM), f32))
        parts.append(piece)
        rest = CHUNK - lo_r - piece.shape[0]
        if rest > 0:
            parts.append(jnp.zeros((rest, HEAD_DIM), f32))
        return jnp.concatenate(parts, axis=0) if len(parts) > 1 else piece

    def intra_factorised(q, k, gh):
        qts, kts = [], []
        for a in range(N_SUB):
            lo_r, hi_r = a * SUB, (a + 1) * SUB
            if a == 0:
                qa = q[:hi_r] * jnp.exp(gh[:hi_r])
                ka = k[:hi_r] * jnp.exp(-gh[:hi_r])
            else:
                ra = gh[lo_r - 1:lo_r]
                qa = q[lo_r:hi_r] * jnp.exp(gh[lo_r:hi_r] - ra)
                ka = k[:hi_r] * jnp.exp(ra - gh[:hi_r])
            qts.append(pad_rows(qa, lo_r))
            kts.append(pad_rows(ka, 0))
        a_mat = lax.dot_general(jnp.concatenate(qts, axis=1).astype(bf16),
                                jnp.concatenate(kts, axis=1).astype(bf16), _NT,
                                preferred_element_type=f32)
        return jnp.where(row64 >= col64, a_mat, 0.0)

    def chunks_factorised():
        n_chunks = TB // CHUNK
        heads = [(c, h) for c in range(n_chunks) for h in range(HGRN_HEADS)]

        def cols(h, base=0):
            return slice(base + h * HEAD_DIM, base + (h + 1) * HEAD_DIM)

        for c in range(n_chunks):
            rows = slice(c * CHUNK, (c + 1) * CHUNK)
            lf = lf_scr[rows, :]
            hi = lf.astype(bf16)
            lo = (lf - hi.astype(f32)).astype(bf16)
            gg = jnp.dot(tri, jnp.concatenate([hi, lo], axis=1), preferred_element_type=f32)
            lf_scr[rows, :] = gg[:, :HGRN_WIDTH] + gg[:, HGRN_WIDTH:]
        for c, h in heads:
            rows = slice(c * CHUNK, (c + 1) * CHUNK)
            q = p_scr[rows, cols(h)]
            k = kk_scr[rows, cols(h)]
            gh = lf_scr[rows, cols(h)]
            amat_scr[c * HGRN_HEADS + h] = intra_factorised(q, k, gh).astype(bf16)
            qg_scr[rows, cols(h)] = (q * jnp.exp(gh)).astype(bf16)
            kd_scr[rows, cols(h)] = (k * jnp.exp(gh[CHUNK - 1:CHUNK] - gh)).astype(bf16)
        for c, h in heads:
            rows = slice(c * CHUNK, (c + 1) * CHUNK)
            v = p_scr[rows, cols(h, 2 * HGRN_WIDTH)]
            o_scr[rows, cols(h)] = jnp.dot(amat_scr[c * HGRN_HEADS + h], v.astype(bf16),
                                           preferred_element_type=f32)
            upd_scr[c * HGRN_HEADS + h] = jnp.dot(v.T.astype(bf16), kd_scr[rows, cols(h)],
                                                  preferred_element_type=f32)
        states = [st_scr[h] for h in range(HGRN_HEADS)]
        for c, h in heads:
            rows = slice(c * CHUNK, (c + 1) * CHUNK)
            st = states[h]
            o_inter = lax.dot_general(qg_scr[rows, cols(h)], st.astype(bf16), _NT,
                                      preferred_element_type=f32)
            o_scr[rows, cols(h)] = (o_scr[rows, cols(h)] + o_inter) * scale
            gl = lf_scr[(c + 1) * CHUNK - 1:(c + 1) * CHUNK, cols(h)]
            states[h] = st * jnp.exp(gl) + upd_scr[c * HGRN_HEADS + h]
        for h in range(HGRN_HEADS):
            st_scr[h] = states[h]

    def intra_exact_diagonal(q, k, gh, cs):
        qts, kts = [], []
        for a in range(1, N_SUB):
            lo_r, hi_r = a * SUB, (a + 1) * SUB
            ra = g_scr[lo_r - 1:lo_r, cs]
            qts.append(pad_rows(q[lo_r:hi_r] * jnp.exp(gh[lo_r:hi_r] - ra), lo_r))
            kts.append(pad_rows(k[:lo_r] * jnp.exp(ra - gh[:lo_r]), 0))
        a_off = lax.dot_general(jnp.concatenate(qts, axis=1).astype(bf16),
                                jnp.concatenate(kts, axis=1).astype(bf16), _NT,
                                preferred_element_type=f32)
        diag_rows = []
        for a in range(N_SUB):
            lo_r = a * SUB
            gs = gh[lo_r:lo_r + SUB]
            qs = q[lo_r:lo_r + SUB]
            blk = jnp.zeros((SUB, CHUNK), f32)
            for jl in range(SUB):
                j = lo_r + jl
                gj = g_scr[j:j + 1, cs]
                kj = kc_scr[j:j + 1, cs]
                e = jnp.exp(jnp.minimum(gs - gj, 0.0))
                col = jnp.sum(qs * (kj * e), axis=-1, keepdims=True)
                blk = jnp.where(lane_sub == j, col, blk)
            diag_rows.append(blk)
        a_diag = jnp.concatenate(diag_rows, axis=0)
        return a_off + jnp.where(row64 >= col64, a_diag, 0.0)

    def make_chunk_body(intra):
        def chunk_body(c, carry):
            r0 = pl.multiple_of(c * CHUNK, CHUNK)
            rows = pl.ds(r0, CHUNK)
            lf = lf_scr[rows, :]
            hi = lf.astype(bf16)
            lo = (lf - hi.astype(f32)).astype(bf16)
            gg = jnp.dot(tri, jnp.concatenate([hi, lo], axis=1), preferred_element_type=f32)
            g_all = gg[:, :HGRN_WIDTH] + gg[:, HGRN_WIDTH:]
            g_scr[...] = g_all
            kc_scr[...] = kk_scr[rows, :]
            for h in range(HGRN_HEADS):
                cs = slice(h * HEAD_DIM, (h + 1) * HEAD_DIM)
                q = p_scr[rows, h * HEAD_DIM:(h + 1) * HEAD_DIM]
                v = p_scr[rows, 2 * HGRN_WIDTH + h * HEAD_DIM:2 * HGRN_WIDTH + (h + 1) * HEAD_DIM]
                k = kc_scr[:, cs]
                gh = g_all[:, cs]
                st = st_scr[h]
                o_inter = lax.dot_general((q * jnp.exp(gh)).astype(bf16), st.astype(bf16), _NT,
                                          preferred_element_type=f32)
                a_mat = intra(q, k, gh, cs)
                o = jnp.dot(a_mat.astype(bf16), v.astype(bf16), preferred_element_type=f32) + o_inter
                o_scr[rows, h * HEAD_DIM:(h + 1) * HEAD_DIM] = o * scale

                gl = g_scr[CHUNK - 1:CHUNK, cs]
                kd = k * jnp.exp(gl - gh)
                upd = jnp.dot(v.T.astype(bf16), kd.astype(bf16), preferred_element_type=f32)
                st_scr[h] = st * jnp.exp(gl) + upd
            return carry
        return chunk_body

    lf_all = lf_scr[...]
    sub_decay = -jnp.sum(lf_all.reshape(TB // SUB, SUB, HGRN_WIDTH), axis=1)
    bounded = jnp.max(sub_decay) <= MAX_SUB_DECAY

    @pl.when(bounded)
    def _():
        chunks_factorised()

    @pl.when(jnp.logical_not(bounded))
    def _():
        lax.fori_loop(0, TB // CHUNK, make_chunk_body(intra_exact_diagonal), 0)

    xnb = xn_ref[...].astype(bf16)
    f_groups = list(range(HGRN_WIDTH // PROJ_GROUP, 2 * HGRN_WIDTH // PROJ_GROUP))
    pending = f_groups + [g for g in range(IN_COLS // PROJ_GROUP) if g not in f_groups]

    def project_next(count):
        for _ in range(count):
            project(xnb, pending.pop(0))

    project_next(3 * HGRN_WIDTH // PROJ_GROUP)
    for w in range(TB // GMLP_BLOCK):
        rows = slice(w * GMLP_BLOCK, (w + 1) * GMLP_BLOCK)
        o = o_scr[rows, :]
        ms = jnp.mean(o * o, axis=-1, keepdims=True)
        gate = p_scr[rows, 3 * HGRN_WIDTH:4 * HGRN_WIDTH]
        y_rec = o * lax.rsqrt(ms + EPS) * hg_ref[...] * (gate * _sigmoid(gate))
        u = _gelu(p_scr[rows, 4 * HGRN_WIDTH:4 * HGRN_WIDTH + GMLP_WIDTH])
        vn = _layer_norm(_gelu(p_scr[rows, 4 * HGRN_WIDTH + GMLP_WIDTH:]), lng_ref[...], lnb_ref[...])
        vnb = vn.astype(bf16)
        cols = []
        for g in range(GMLP_GROUPS):
            s = jnp.dot(wm_ref[g], vnb[:, g * HEAD_DIM:(g + 1) * HEAD_DIM],
                        preferred_element_type=f32) + bst_ref[:, g:g + 1]
            cols.append(s)
        y_sg = u * jnp.concatenate(cols, axis=1)
        mix_scr[rows, :] = jnp.concatenate([y_rec, y_sg], axis=1).astype(bf16)

    mix = jnp.dot(mix_scr[...], wout_ref[...], preferred_element_type=f32)
    hgrn_gates()
    h1 = _layer_norm(ALPHA * x + mix, l1g_ref[...], l1b_ref[...])
    h1f_ref[...] = h1
    h1p_ref[...] = _pack_rows(h1)
    project_next(2)

    hh = h1.astype(bf16)
    hl = (h1 - hh.astype(f32)).astype(bf16)
    rwh = rwh_ref[...]
    logits = (lax.dot_general(rwh, hh, _NT, preferred_element_type=f32)
              + lax.dot_general(rwh, hl, _NT, preferred_element_type=f32)
              + lax.dot_general(rwl_ref[...], hh, _NT, preferred_element_type=f32)
              + rb_ref[...])
    project_next(4)
    assert not pending

    e_iota = lax.broadcasted_iota(jnp.int32, (N_EXPERTS, TB), 0)
    work = logits
    vals, idxs, hots = [], [], []
    for _ in range(TOP_K):
        m = jnp.max(work, axis=0, keepdims=True)
        ik = jnp.min(jnp.where(work == m, e_iota, N_EXPERTS), axis=0, keepdims=True)
        hot = e_iota == ik
        work = jnp.where(hot, -jnp.inf, work)
        vals.append(m)
        idxs.append(ik)
        hots.append(hot)
    exps = [jnp.exp(vk - vals[0]) for vk in vals]
    denom = exps[0] + exps[1] + exps[2] + exps[3]
    gate_ref[...] = jnp.concatenate([ek / denom for ek in exps], axis=0)
    idx_ref[...] = jnp.concatenate(idxs, axis=0)

    hot_any = jnp.where(hots[0] | hots[1] | hots[2] | hots[3], 1.0, 0.0)
    prefix = jnp.dot(hot_any.astype(bf16), upp_ref[...], preferred_element_type=f32)
    base = prefix + carry_scr[:, 0:1]
    ranks = [jnp.sum(jnp.where(hk, base, 0.0), axis=0, keepdims=True) for hk in hots]
    rank_ref[...] = jnp.concatenate(ranks, axis=0).astype(jnp.int32)
    new_carry = carry_scr[...] + jnp.sum(hot_any, axis=1, keepdims=True)
    carry_scr[...] = new_carry
    cnt_ref[...] = new_carry


def _mixer(xt, win, lb, hg, lng, lnb, wm, bst, wout, l1g, l1b, rwh, rwl, rb, tri, upp,
           seq, batch, batch0):
    nt = seq // TB
    t_total = batch * seq
    nblk = batch * nt
    const2 = lambda b, t: (0, 0)
    const3 = lambda b, t: (0, 0, 0)
    row_blk = lambda b, t: (b * nt + t, 0)
    meta_blk = lambda b, t: (0, b * nt + t)
    once = dict(pipeline_mode=pl.Buffered(1))
    in_specs = [
        pl.BlockSpec((TB, D_MODEL), lambda b, t: ((b + batch0) * nt + t, 0)),
        pl.BlockSpec((TB, D_MODEL),
                     lambda b, t: (batch0 * nt + jnp.minimum(b * nt + t + 1, nblk - 1), 0)),
        pl.BlockSpec((D_MODEL, IN_COLS), const2, **once),
        pl.BlockSpec((1, HGRN_WIDTH), const2),
        pl.BlockSpec((1, HGRN_WIDTH), const2),
        pl.BlockSpec((1, GMLP_WIDTH), const2),
        pl.BlockSpec((1, GMLP_WIDTH), const2),
        pl.BlockSpec((GMLP_GROUPS, GMLP_BLOCK, GMLP_BLOCK), const3),
        pl.BlockSpec((GMLP_BLOCK, GMLP_GROUPS), const2),
        pl.BlockSpec((D_MODEL, D_MODEL), const2, **once),
        pl.BlockSpec((1, D_MODEL), const2),
        pl.BlockSpec((1, D_MODEL), const2),
        pl.BlockSpec((N_EXPERTS, D_MODEL), const2),
        pl.BlockSpec((N_EXPERTS, D_MODEL), const2),
        pl.BlockSpec((N_EXPERTS, 1), const2),
        pl.BlockSpec((CHUNK, CHUNK), const2),
        pl.BlockSpec((TB, TB), const2, **once),
    ]
    out_shape = [
        jax.ShapeDtypeStruct((t_total, D_MODEL), jnp.float32),
        jax.ShapeDtypeStruct((t_total, HALF), jnp.int32),
        jax.ShapeDtypeStruct((TOP_K, t_total), jnp.int32),
        jax.ShapeDtypeStruct((TOP_K, t_total), jnp.float32),
        jax.ShapeDtypeStruct((TOP_K, t_total), jnp.int32),
        jax.ShapeDtypeStruct((N_EXPERTS, 128), jnp.float32),
    ]
    out_specs = [
        pl.BlockSpec((TB, D_MODEL), row_blk),
        pl.BlockSpec((TB, HALF), row_blk),
        pl.BlockSpec((TOP_K, TB), meta_blk),
        pl.BlockSpec((TOP_K, TB), meta_blk),
        pl.BlockSpec((TOP_K, TB), meta_blk),
        pl.BlockSpec((N_EXPERTS, 128), const2),
    ]
    scratch = [
        pltpu.VMEM((TB, IN_COLS), jnp.float32),
        pltpu.VMEM((TB, HGRN_WIDTH), jnp.float32),
        pltpu.VMEM((TB, HGRN_WIDTH), jnp.float32),
        pltpu.VMEM((CHUNK, HGRN_WIDTH), jnp.float32),
        pltpu.VMEM((CHUNK, HGRN_WIDTH), jnp.float32),
        pltpu.VMEM((TB, HGRN_WIDTH), jnp.float32),
        pltpu.VMEM((HGRN_HEADS, HEAD_DIM, HEAD_DIM), jnp.float32),
        pltpu.VMEM((N_EXPERTS, 128), jnp.float32),
        pltpu.VMEM((TB // CHUNK * HGRN_HEADS, CHUNK, CHUNK), jnp.bfloat16),
        pltpu.VMEM((TB, HGRN_WIDTH), jnp.bfloat16),
        pltpu.VMEM((TB, HGRN_WIDTH), jnp.bfloat16),
        pltpu.VMEM((TB // CHUNK * HGRN_HEADS, HEAD_DIM, HEAD_DIM), jnp.float32),
        pltpu.VMEM((TB, D_MODEL), jnp.bfloat16),
    ]
    return pl.pallas_call(
        _mixer_kernel,
        grid=(batch, nt),
        in_specs=in_specs,
        out_specs=out_specs,
        out_shape=out_shape,
        scratch_shapes=scratch,
        compiler_params=pltpu.CompilerParams(
            dimension_semantics=("arbitrary", "arbitrary"),
            vmem_limit_bytes=VMEM_LIMIT),
        name="mixer",
    )(xt, xt, win, lb, hg, lng, lnb, wm, bst, wout, l1g, l1b, rwh, rwl, rb, tri, upp)


def _sc_workers():
    info = plsc.get_sparse_core_info()
    return info.num_cores, info.num_cores * info.num_subcores


def _sc_dispatch(rows, dest, n_out):
    t_total, dw = rows.shape
    nc, nw = _sc_workers()
    per_w = t_total // nw
    mesh = plsc.VectorSubcoreMesh(core_axis_name="c", subcore_axis_name="s")

    @functools.partial(
        pl.kernel,
        out_type=jax.ShapeDtypeStruct((n_out, dw), rows.dtype),
        mesh=mesh,
        scratch_types=[pltpu.VMEM((SC_WINDOW,), jnp.int32) for _ in range(TOP_K)]
        + [pltpu.VMEM((SC_WINDOW, dw), rows.dtype), pltpu.SemaphoreType.DMA],
        name="sc_dispatch",
    )
    def k(x_hbm, i_hbm, o_hbm, i0, i1, i2, i3, rows_v, sem):
        wid = lax.axis_index("s") * nc + lax.axis_index("c")
        idx_bufs = (i0, i1, i2, i3)

        @pl.loop(0, per_w // SC_WINDOW)
        def _(j):
            base = wid * per_w + j * SC_WINDOW
            for kk in range(TOP_K):
                pltpu.sync_copy(i_hbm.at[kk, pl.ds(base, SC_WINDOW)], idx_bufs[kk])
            pltpu.sync_copy(x_hbm.at[pl.ds(base, SC_WINDOW)], rows_v)
            copies = [pltpu.async_copy(rows_v, o_hbm.at[idx_bufs[kk]], sem) for kk in range(TOP_K)]
            for cp in copies:
                cp.wait()

    return k(rows, dest)


def _sc_gather(table, idx):
    n_k, t_total = idx.shape
    n = n_k * t_total
    dw = table.shape[1]
    nc, nw = _sc_workers()
    per_w = n // nw
    w_per_k = nw // n_k
    mesh = plsc.VectorSubcoreMesh(core_axis_name="c", subcore_axis_name="s")

    @functools.partial(
        pl.kernel,
        out_type=jax.ShapeDtypeStruct((n, dw), table.dtype),
        mesh=mesh,
        scratch_types=[pltpu.VMEM((SC_WINDOW,), jnp.int32),
                       pltpu.VMEM((SC_WINDOW, dw), table.dtype),
                       pltpu.SemaphoreType.DMA],
        name="sc_gather",
    )
    def k(t_hbm, i_hbm, o_hbm, idx_v, rows_v, sem):
        wid = lax.axis_index("s") * nc + lax.axis_index("c")

        row = wid // w_per_k
        col0 = (wid % w_per_k) * per_w

        @pl.loop(0, per_w // SC_WINDOW)
        def _(j):
            col = col0 + j * SC_WINDOW
            pltpu.sync_copy(i_hbm.at[row, pl.ds(col, SC_WINDOW)], idx_v)
            pltpu.async_copy(t_hbm.at[idx_v], rows_v, sem).wait()
            pltpu.sync_copy(rows_v, o_hbm.at[pl.ds(row * t_total + col, SC_WINDOW)])

    return k(table, idx)


def _expert_kernel(te_ref, tv_ref, tn_ref, xs_ref, w1_hbm, w2_hbm, b1g_ref, b1l_ref, b2_ref,
                   perm_ref, y_ref, w1f_scr, w2f_scr, w1g_scr, w1l_scr, w2_scr, sems):
    i = pl.program_id(0)
    valid = tv_ref[i]
    expert = te_ref[i]
    f32 = jnp.float32
    bf16 = jnp.bfloat16
    expert_changed = (i == 0) | (expert != te_ref[jnp.maximum(i - 1, 0)])

    def weight_copies(e):
        return (pltpu.make_async_copy(w1_hbm.at[e], w1f_scr, sems.at[0]),
                pltpu.make_async_copy(w2_hbm.at[e], w2f_scr, sems.at[1]))

    @pl.when(valid == 0)
    def _():
        y_ref[...] = jnp.zeros_like(y_ref)

    @pl.when(i == 0)
    def _():
        for cp in weight_copies(expert):
            cp.start()

    @pl.when((valid > 0) & expert_changed)
    def _():
        for cp in weight_copies(expert):
            cp.wait()
        w2_scr[...] = w2f_scr[...].astype(bf16)
        perm = perm_ref[...]
        for c in range(2 * D_MODEL // PERM_BLOCK):
            blk = w1f_scr[:, c * PERM_BLOCK:(c + 1) * PERM_BLOCK].astype(bf16)
            r = jnp.dot(blk, perm, preferred_element_type=f32).astype(bf16)
            half = PERM_BLOCK // 2
            w1g_scr[:, c * half:(c + 1) * half] = r[:, :half]
            w1l_scr[:, c * half:(c + 1) * half] = r[:, half:]

        @pl.when(tn_ref[i] != expert)
        def _():
            for cp in weight_copies(tn_ref[i]):
                cp.start()

    def expert_rows(n_rows):
        a, b = _unpack_rows(xs_ref[:n_rows, :])
        keep = lax.broadcasted_iota(jnp.int32, (n_rows, 1), 0) < valid
        x = jnp.where(keep, jnp.concatenate([a, b], axis=1), 0.0).astype(bf16)
        hg = jnp.dot(x, w1g_scr[...], preferred_element_type=f32) + b1g_ref[0]
        hl = jnp.dot(x, w1l_scr[...], preferred_element_type=f32) + b1l_ref[0]
        xg = jnp.minimum(hg, SWIGLU_LIMIT)
        xl = jnp.clip(hl, -SWIGLU_LIMIT, SWIGLU_LIMIT)
        act = xg * _sigmoid(SWIGLU_ALPHA * xg) * (xl + 1.0)
        y = jnp.dot(act.astype(bf16), w2_scr[...], preferred_element_type=f32) + b2_ref[0]
        y_ref[:n_rows, :] = _pack_rows(y)
        if n_rows < TM:
            y_ref[n_rows:, :] = jnp.zeros((TM - n_rows, HALF), jnp.int32)

    for blocks in range(TM // ROW_STEP, 0, -1):
        @pl.when((valid > (blocks - 1) * ROW_STEP) & (valid <= blocks * ROW_STEP))
        def _(blocks=blocks):
            expert_rows(blocks * ROW_STEP)


def _experts(xs, tile_expert, tile_valid, tile_next, w1, w2, b1g, b1l, b2, perm):
    n_slots = xs.shape[0]
    n_tiles = n_slots // TM
    d_ff = w2.shape[1]
    wmap = lambda i, te, tv, tn: (te[i], 0, 0)
    grid_spec = pltpu.PrefetchScalarGridSpec(
        num_scalar_prefetch=3,
        grid=(n_tiles,),
        in_specs=[
            pl.BlockSpec((TM, HALF), lambda i, te, tv, tn: (i, 0)),
            pl.BlockSpec(memory_space=pl.ANY),
            pl.BlockSpec(memory_space=pl.ANY),
            pl.BlockSpec((1, 1, d_ff), wmap),
            pl.BlockSpec((1, 1, d_ff), wmap),
            pl.BlockSpec((1, 1, D_MODEL), wmap),
            pl.BlockSpec((PERM_BLOCK, PERM_BLOCK), lambda i, te, tv, tn: (0, 0)),
        ],
        out_specs=pl.BlockSpec((TM, HALF), lambda i, te, tv, tn: (i, 0)),
        scratch_shapes=[
            pltpu.VMEM((D_MODEL, 2 * d_ff), jnp.float32),
            pltpu.VMEM((d_ff, D_MODEL), jnp.float32),
            pltpu.VMEM((D_MODEL, d_ff), jnp.bfloat16),
            pltpu.VMEM((D_MODEL, d_ff), jnp.bfloat16),
            pltpu.VMEM((d_ff, D_MODEL), jnp.bfloat16),
            pltpu.SemaphoreType.DMA((2,)),
        ],
    )
    return pl.pallas_call(
        _expert_kernel,
        grid_spec=grid_spec,
        out_shape=jax.ShapeDtypeStruct((n_slots, HALF), jnp.int32),
        compiler_params=pltpu.CompilerParams(
            dimension_semantics=("arbitrary",),
            vmem_limit_bytes=VMEM_LIMIT),
        name="experts",
    )(tile_expert, tile_valid, tile_next, xs, w1, w2, b1g, b1l, b2, perm)


def _combine_kernel(h_ref, yk_ref, gate_ref, g_ref, b_ref, *rest):
    o_ref = rest[-1]
    gates = gate_ref[...].T
    acc_a = None
    acc_b = None
    for k in range(TOP_K):
        a, b = _unpack_rows(yk_ref[k])
        gk = gates[:, k:k + 1]
        acc_a = gk * a if acc_a is None else acc_a + gk * a
        acc_b = gk * b if acc_b is None else acc_b + gk * b
    ffn = jnp.concatenate([acc_a, acc_b], axis=1)
    o_ref[...] = _layer_norm(ALPHA * h_ref[...] + ffn, g_ref[...], b_ref[...])


def _combine(h1f, yk, gates, g2, b2, out_prev, part, t_all):
    t_part = h1f.shape[0]
    blk0 = part * (t_part // TC3)
    row = lambda i: (i, 0)
    const = lambda i: (0, 0)
    in_specs = [
        pl.BlockSpec((TC3, D_MODEL), row),
        pl.BlockSpec((TOP_K, TC3, HALF), lambda i: (0, i, 0)),
        pl.BlockSpec((TOP_K, TC3), lambda i: (0, i)),
        pl.BlockSpec((1, D_MODEL), const),
        pl.BlockSpec((1, D_MODEL), const),
    ]
    args = [h1f, yk, gates, g2, b2]
    aliases = {}
    if out_prev is not None:
        in_specs.append(pl.BlockSpec(memory_space=pl.ANY))
        args.append(out_prev)
        aliases = {len(args) - 1: 0}
    return pl.pallas_call(
        _combine_kernel,
        grid=(t_part // TC3,),
        in_specs=in_specs,
        out_specs=pl.BlockSpec((TC3, D_MODEL), lambda i: (i + blk0, 0)),
        out_shape=jax.ShapeDtypeStruct((t_all, D_MODEL), jnp.float32),
        input_output_aliases=aliases,
        compiler_params=pltpu.CompilerParams(
            dimension_semantics=("arbitrary",),
            vmem_limit_bytes=VMEM_LIMIT),
        name="combine",
    )(*args)


def kernel(x, w_in, lb_logits, hgrn_norm_g, gmlp_ln_g, gmlp_ln_b, gmlp_ws, gmlp_bs, w_out, ln1_g, ln1_b, router_w, router_b, exp_w1, exp_b1, exp_w2, exp_b2, ln2_g, ln2_b):
    batch, seq, d = x.shape
    assert d == D_MODEL and seq % TB == 0 and w_in.shape[0] == 1
    t_total = batch * seq
    f32 = jnp.float32
    bf16 = jnp.bfloat16

    lb = jnp.cumsum(jax.nn.softmax(lb_logits.astype(f32), axis=0), axis=0)[0:1]
    chunk_id = jnp.arange(GMLP_BLOCK) // CHUNK
    wm = jnp.where((chunk_id[None, :] <= chunk_id[:, None])[None], gmlp_ws[0], 0.0).astype(bf16)
    rwt = router_w[0].T
    rwh = rwt.astype(bf16)
    rwl = (rwt - rwh.astype(f32)).astype(bf16)
    tri = (jnp.arange(CHUNK)[None, :] <= jnp.arange(CHUNK)[:, None]).astype(bf16)
    upp = (jnp.arange(TB)[:, None] < jnp.arange(TB)[None, :]).astype(bf16)

    lane = jnp.arange(PERM_BLOCK)
    src = jnp.where(lane < PERM_BLOCK // 2, 2 * lane, 2 * (lane - PERM_BLOCK // 2) + 1)
    perm = (jnp.arange(PERM_BLOCK)[:, None] == src[None, :]).astype(bf16)
    xt = x.reshape(t_total, d)
    win = w_in[0].astype(bf16)
    wout = w_out[0].astype(bf16)
    b1g, b1l, b2e = exp_b1[0][:, None, 0::2], exp_b1[0][:, None, 1::2], exp_b2[0][:, None, :]
    experts = jnp.arange(N_EXPERTS, dtype=jnp.int32)

    assert batch % N_PARTS == 0
    pb = batch // N_PARTS
    t_part = pb * seq
    n_slots = t_part * TOP_K + N_EXPERTS * TM
    n_tiles = n_slots // TM
    out = None
    for part in range(N_PARTS):
        h1f, h1p, idx, gates, rank, cnt = _mixer(
            xt, win, lb, hgrn_norm_g[0:1], gmlp_ln_g[0:1], gmlp_ln_b[0:1], wm, gmlp_bs[0].T, wout,
            ln1_g[0:1], ln1_b[0:1], rwh, rwl, router_b[0][:, None], tri, upp, seq, pb, part * pb)

        counts = cnt[:, 0].astype(jnp.int32)
        padded = ((counts + TM - 1) // TM) * TM
        ends = jnp.cumsum(padded)
        starts = ends - padded
        dest = rank
        for e in range(N_EXPERTS):
            dest = dest + jnp.where(idx == e, starts[e], 0)
        tile_row = jnp.arange(n_tiles, dtype=jnp.int32) * TM
        tile_expert = jnp.minimum(
            jnp.sum((tile_row[:, None] >= ends[None, :]).astype(jnp.int32), axis=1), N_EXPERTS - 1)
        tile_valid = jnp.clip(starts[tile_expert] + counts[tile_expert] - tile_row, 0, TM)
        owns_later = (experts[None, :] > experts[:, None]) & (counts[None, :] > 0)
        nxt = jnp.min(jnp.where(owns_later, experts[None, :], N_EXPERTS), axis=1)
        nxt = jnp.where(nxt == N_EXPERTS, experts, nxt)
        tile_next = nxt[tile_expert]

        xs = _sc_dispatch(h1p, dest, n_slots)
        y = _experts(xs, tile_expert, tile_valid.astype(jnp.int32), tile_next.astype(jnp.int32),
                     exp_w1[0], exp_w2[0], b1g, b1l, b2e, perm)
        yk = _sc_gather(y, dest).reshape(TOP_K, t_part, HALF)
        out = _combine(h1f, yk, gates, ln2_g[0:1], ln2_b[0:1], out, part, t_total)
    return out.reshape(batch, seq, d)
```

```python
import functools
import math

import jax
import jax.numpy as jnp
from jax import lax
from jax.experimental import pallas as pl
from jax.experimental.pallas import tpu as pltpu
from jax.experimental.pallas import tpu_sc as plsc

D_MODEL = 1024
CHUNK = 64
SUB = 16
N_SUB = CHUNK // SUB
MAX_SUB_DECAY = 60.0
HGRN_WIDTH = 512
HGRN_HEADS = 4
HEAD_DIM = 128
GMLP_WIDTH = 512
GMLP_BLOCK = 128
GMLP_GROUPS = 4
IN_COLS = 3072
N_EXPERTS = 32
TOP_K = 4
SWIGLU_LIMIT = 7.0
SWIGLU_ALPHA = 1.702
ALPHA = 2.0 ** 0.25
EPS = 1e-5
HALF = D_MODEL // 2

TB = 512
TM = 1024
ROW_STEP = 256
PLAN_CHUNK = 2048
TC3 = 512
N_PARTS = 2
SC_WINDOW = 128
PROJ_GROUP = 256
PERM_BLOCK = 256
VMEM_LIMIT = 56 * 1024 * 1024

_NT = (((1,), (1,)), ((), ()))


def _sigmoid(x):
    z = jnp.exp(-jnp.abs(x))
    r = 1.0 / (1.0 + z)
    return jnp.where(x >= 0, r, z * r)


def _gelu(x):
    return 0.5 * x * (1.0 + lax.erf(x * (1.0 / math.sqrt(2.0))))


def _layer_norm(x, g, b):
    mu = jnp.mean(x, axis=-1, keepdims=True)
    xc = x - mu
    var = jnp.mean(xc * xc, axis=-1, keepdims=True)
    return xc * lax.rsqrt(var + EPS) * g + b


def _pack_rows(h):
    a = h[:, :HALF].astype(jnp.bfloat16).astype(jnp.float32)
    b = h[:, HALF:].astype(jnp.bfloat16).astype(jnp.float32)
    au = lax.bitcast_convert_type(a, jnp.uint32) >> 16
    bu = lax.bitcast_convert_type(b, jnp.uint32) & jnp.uint32(0xFFFF0000)
    return lax.bitcast_convert_type(au | bu, jnp.int32)


def _unpack_rows(w):
    u = lax.bitcast_convert_type(w, jnp.uint32)
    a = lax.bitcast_convert_type(u << 16, jnp.float32)
    b = lax.bitcast_convert_type(u & jnp.uint32(0xFFFF0000), jnp.float32)
    return a, b


def _mixer_kernel(x_ref, xn_ref, win_ref, lb_ref, hg_ref, lng_ref, lnb_ref, wm_ref, bst_ref,
                  wout_ref, l1g_ref, l1b_ref, rwh_ref, rwl_ref, rb_ref, tri_ref, upp_ref,
                  h1f_ref, h1p_ref, idx_ref, gate_ref, rank_ref, cnt_ref,
                  p_scr, lf_scr, kk_scr, g_scr, kc_scr, o_scr, st_scr, carry_scr,
                  amat_scr, qg_scr, kd_scr, upd_scr, mix_scr):
    b = pl.program_id(0)
    t = pl.program_id(1)
    step = b * pl.num_programs(1) + t
    f32 = jnp.float32
    bf16 = jnp.bfloat16

    @pl.when(t == 0)
    def _():
        st_scr[...] = jnp.zeros_like(st_scr)

    @pl.when((b == 0) & (t == 0))
    def _():
        carry_scr[...] = jnp.zeros_like(carry_scr)

    def project(xb, group):
        cols = slice(group * PROJ_GROUP, (group + 1) * PROJ_GROUP)
        p_scr[:, cols] = jnp.dot(xb, win_ref[:, cols], preferred_element_type=f32)

    def hgrn_gates():
        fl = p_scr[:, HGRN_WIDTH:2 * HGRN_WIDTH]
        lb = lb_ref[...]
        z = jnp.exp(-jnp.abs(fl))
        r = 1.0 / (1.0 + z)
        zr = z * r
        pos = fl >= 0
        lf_scr[...] = jnp.log(lb + (1.0 - lb) * jnp.where(pos, r, zr))
        kk_scr[...] = (1.0 - lb) * jnp.where(pos, zr, r)

    @pl.when(step == 0)
    def _():
        xb0 = x_ref[...].astype(bf16)
        for group in range(IN_COLS // PROJ_GROUP):
            project(xb0, group)
        hgrn_gates()

    x = x_ref[...]

    tri = tri_ref[...]
    row64 = lax.broadcasted_iota(jnp.int32, (CHUNK, CHUNK), 0)
    col64 = lax.broadcasted_iota(jnp.int32, (CHUNK, CHUNK), 1)
    lane_sub = lax.broadcasted_iota(jnp.int32, (SUB, CHUNK), 1)
    scale = HEAD_DIM ** -0.5

    def pad_rows(piece, lo_r):
        parts = []
        if lo_r > 0:
            parts.append(jnp.zeros((lo_r, HEAD_DIM), f32))
        parts.append(piece)
        rest = CHUNK - lo_r - piece.shape[0]
        if rest > 0:
            parts.append(jnp.zeros((rest, HEAD_DIM), f32))
        return jnp.concatenate(parts, axis=0) if len(parts) > 1 else piece

    def intra_factorised(q, k, gh):
        qts, kts = [], []
        for a in range(N_SUB):
            lo_r, hi_r = a * SUB, (a + 1) * SUB
            if a == 0:
                qa = q[:hi_r] * jnp.exp(gh[:hi_r])
                ka = k[:hi_r] * jnp.exp(-gh[:hi_r])
            else:
                ra = gh[lo_r - 1:lo_r]
                qa = q[lo_r:hi_r] * jnp.exp(gh[lo_r:hi_r] - ra)
                ka = k[:hi_r] * jnp.exp(ra - gh[:hi_r])
            qts.append(pad_rows(qa, lo_r))
            kts.append(pad_rows(ka, 0))
        a_mat = lax.dot_general(jnp.concatenate(qts, axis=1).astype(bf16),
                                jnp.concatenate(kts, axis=1).astype(bf16), _NT,
                                preferred_element_type=f32)
        return jnp.where(row64 >= col64, a_mat, 0.0)

    def chunks_factorised():
        n_chunks = TB // CHUNK
        heads = [(c, h) for c in range(n_chunks) for h in range(HGRN_HEADS)]

        def cols(h, base=0):
            return slice(base + h * HEAD_DIM, base + (h + 1) * HEAD_DIM)

        for c in range(n_chunks):
            rows = slice(c * CHUNK, (c + 1) * CHUNK)
            lf = lf_scr[rows, :]
            hi = lf.astype(bf16)
            lo = (lf - hi.astype(f32)).astype(bf16)
            gg = jnp.dot(tri, jnp.concatenate([hi, lo], axis=1), preferred_element_type=f32)
            lf_scr[rows, :] = gg[:, :HGRN_WIDTH] + gg[:, HGRN_WIDTH:]
        for c, h in heads:
            rows = slice(c * CHUNK, (c + 1) * CHUNK)
            q = p_scr[rows, cols(h)]
            k = kk_scr[rows, cols(h)]
            gh = lf_scr[rows, cols(h)]
            amat_scr[c * HGRN_HEADS + h] = intra_factorised(q, k, gh).astype(bf16)
            qg_scr[rows, cols(h)] = (q * jnp.exp(gh)).astype(bf16)
            kd_scr[rows, cols(h)] = (k * jnp.exp(gh[CHUNK - 1:CHUNK] - gh)).astype(bf16)
        for c, h in heads:
            rows = slice(c * CHUNK, (c + 1) * CHUNK)
            v = p_scr[rows, cols(h, 2 * HGRN_WIDTH)]
            o_scr[rows, cols(h)] = jnp.dot(amat_scr[c * HGRN_HEADS + h], v.astype(bf16),
                                           preferred_element_type=f32)
            upd_scr[c * HGRN_HEADS + h] = jnp.dot(v.T.astype(bf16), kd_scr[rows, cols(h)],
                                                  preferred_element_type=f32)
        states = [st_scr[h] for h in range(HGRN_HEADS)]
        for c, h in heads:
            rows = slice(c * CHUNK, (c + 1) * CHUNK)
            st = states[h]
            o_inter = lax.dot_general(qg_scr[rows, cols(h)], st.astype(bf16), _NT,
                                      preferred_element_type=f32)
            o_scr[rows, cols(h)] = (o_scr[rows, cols(h)] + o_inter) * scale
            gl = lf_scr[(c + 1) * CHUNK - 1:(c + 1) * CHUNK, cols(h)]
            states[h] = st * jnp.exp(gl) + upd_scr[c * HGRN_HEADS + h]
        for h in range(HGRN_HEADS):
            st_scr[h] = states[h]

    def intra_exact_diagonal(q, k, gh, cs):
        qts, kts = [], []
        for a in range(1, N_SUB):
            lo_r, hi_r = a * SUB, (a + 1) * SUB
            ra = g_scr[lo_r - 1:lo_r, cs]
            qts.append(pad_rows(q[lo_r:hi_r] * jnp.exp(gh[lo_r:hi_r] - ra), lo_r))
            kts.append(pad_rows(k[:lo_r] * jnp.exp(ra - gh[:lo_r]), 0))
        a_off = lax.dot_general(jnp.concatenate(qts, axis=1).astype(bf16),
                                jnp.concatenate(kts, axis=1).astype(bf16), _NT,
                                preferred_element_type=f32)
        diag_rows = []
        for a in range(N_SUB):
            lo_r = a * SUB
            gs = gh[lo_r:lo_r + SUB]
            qs = q[lo_r:lo_r + SUB]
            blk = jnp.zeros((SUB, CHUNK), f32)
            for jl in range(SUB):
                j = lo_r + jl
                gj = g_scr[j:j + 1, cs]
                kj = kc_scr[j:j + 1, cs]
                e = jnp.exp(jnp.minimum(gs - gj, 0.0))
                col = jnp.sum(qs * (kj * e), axis=-1, keepdims=True)
                blk = jnp.where(lane_sub == j, col, blk)
            diag_rows.append(blk)
        a_diag = jnp.concatenate(diag_rows, axis=0)
        return a_off + jnp.where(row64 >= col64, a_diag, 0.0)

    def make_chunk_body(intra):
        def chunk_body(c, carry):
            r0 = pl.multiple_of(c * CHUNK, CHUNK)
            rows = pl.ds(r0, CHUNK)
            lf = lf_scr[rows, :]
            hi = lf.astype(bf16)
            lo = (lf - hi.astype(f32)).astype(bf16)
            gg = jnp.dot(tri, jnp.concatenate([hi, lo], axis=1), preferred_element_type=f32)
            g_all = gg[:, :HGRN_WIDTH] + gg[:, HGRN_WIDTH:]
            g_scr[...] = g_all
            kc_scr[...] = kk_scr[rows, :]
            for h in range(HGRN_HEADS):
                cs = slice(h * HEAD_DIM, (h + 1) * HEAD_DIM)
                q = p_scr[rows, h * HEAD_DIM:(h + 1) * HEAD_DIM]
                v = p_scr[rows, 2 * HGRN_WIDTH + h * HEAD_DIM:2 * HGRN_WIDTH + (h + 1) * HEAD_DIM]
                k = kc_scr[:, cs]
                gh = g_all[:, cs]
                st = st_scr[h]
                o_inter = lax.dot_general((q * jnp.exp(gh)).astype(bf16), st.astype(bf16), _NT,
                                          preferred_element_type=f32)
                a_mat = intra(q, k, gh, cs)
                o = jnp.dot(a_mat.astype(bf16), v.astype(bf16), preferred_element_type=f32) + o_inter
                o_scr[rows, h * HEAD_DIM:(h + 1) * HEAD_DIM] = o * scale

                gl = g_scr[CHUNK - 1:CHUNK, cs]
                kd = k * jnp.exp(gl - gh)
                upd = jnp.dot(v.T.astype(bf16), kd.astype(bf16), preferred_element_type=f32)
                st_scr[h] = st * jnp.exp(gl) + upd
            return carry
        return chunk_body

    lf_all = lf_scr[...]
    sub_decay = -jnp.sum(lf_all.reshape(TB // SUB, SUB, HGRN_WIDTH), axis=1)
    bounded = jnp.max(sub_decay) <= MAX_SUB_DECAY

    @pl.when(bounded)
    def _():
        chunks_factorised()

    @pl.when(jnp.logical_not(bounded))
    def _():
        lax.fori_loop(0, TB // CHUNK, make_chunk_body(intra_exact_diagonal), 0)

    xnb = xn_ref[...].astype(bf16)
    f_groups = list(range(HGRN_WIDTH // PROJ_GROUP, 2 * HGRN_WIDTH // PROJ_GROUP))
    pending = f_groups + [g for g in range(IN_COLS // PROJ_GROUP) if g not in f_groups]

    def project_next(count):
        for _ in range(count):
            project(xnb, pending.pop(0))

    project_next(3 * HGRN_WIDTH // PROJ_GROUP)
    for w in range(TB // GMLP_BLOCK):
        rows = slice(w * GMLP_BLOCK, (w + 1) * GMLP_BLOCK)
        o = o_scr[rows, :]
        ms = jnp.mean(o * o, axis=-1, keepdims=True)
        gate = p_scr[rows, 3 * HGRN_WIDTH:4 * HGRN_WIDTH]
        y_rec = o * lax.rsqrt(ms + EPS) * hg_ref[...] * (gate * _sigmoid(gate))
        u = _gelu(p_scr[rows, 4 * HGRN_WIDTH:4 * HGRN_WIDTH + GMLP_WIDTH])
        vn = _layer_norm(_gelu(p_scr[rows, 4 * HGRN_WIDTH + GMLP_WIDTH:]), lng_ref[...], lnb_ref[...])
        vnb = vn.astype(bf16)
        cols = []
        for g in range(GMLP_GROUPS):
            s = jnp.dot(wm_ref[g], vnb[:, g * HEAD_DIM:(g + 1) * HEAD_DIM],
                        preferred_element_type=f32) + bst_ref[:, g:g + 1]
            cols.append(s)
        y_sg = u * jnp.concatenate(cols, axis=1)
        mix_scr[rows, :] = jnp.concatenate([y_rec, y_sg], axis=1).astype(bf16)

    mix = jnp.dot(mix_scr[...], wout_ref[...], preferred_element_type=f32)
    hgrn_gates()
    h1 = _layer_norm(ALPHA * x + mix, l1g_ref[...], l1b_ref[...])
    h1f_ref[...] = h1
    h1p_ref[...] = _pack_rows(h1)
    project_next(2)

    hh = h1.astype(bf16)
    hl = (h1 - hh.astype(f32)).astype(bf16)
    rwh = rwh_ref[...]
    logits = (lax.dot_general(rwh, hh, _NT, preferred_element_type=f32)
              + lax.dot_general(rwh, hl, _NT, preferred_element_type=f32)
              + lax.dot_general(rwl_ref[...], hh, _NT, preferred_element_type=f32)
              + rb_ref[...])
    project_next(4)
    assert not pending

    e_iota = lax.broadcasted_iota(jnp.int32, (N_EXPERTS, TB), 0)
    work = logits
    vals, idxs, hots = [], [], []
    for _ in range(TOP_K):
        m = jnp.max(work, axis=0, keepdims=True)
        ik = jnp.min(jnp.where(work == m, e_iota, N_EXPERTS), axis=0, keepdims=True)
        hot = e_iota == ik
        work = jnp.where(hot, -jnp.inf, work)
        vals.append(m)
        idxs.append(ik)
        hots.append(hot)
    exps = [jnp.exp(vk - vals[0]) for vk in vals]
    denom = exps[0] + exps[1] + exps[2] + exps[3]
    gate_ref[...] = jnp.concatenate([ek / denom for ek in exps], axis=0)
    idx_ref[...] = jnp.concatenate(idxs, axis=0)

    hot_any = jnp.where(hots[0] | hots[1] | hots[2] | hots[3], 1.0, 0.0)
    prefix = jnp.dot(hot_any.astype(bf16), upp_ref[...], preferred_element_type=f32)
    base = prefix + carry_scr[:, 0:1]
    ranks = [jnp.sum(jnp.where(hk, base, 0.0), axis=0, keepdims=True) for hk in hots]
    rank_ref[...] = jnp.concatenate(ranks, axis=0).astype(jnp.int32)
    new_carry = carry_scr[...] + jnp.sum(hot_any, axis=1, keepdims=True)
    carry_scr[...] = new_carry
    cnt_ref[...] = new_carry


def _mixer(xt, win, lb, hg, lng, lnb, wm, bst, wout, l1g, l1b, rwh, rwl, rb, tri, upp,
           seq, batch, batch0):
    nt = seq // TB
    t_total = batch * seq
    nblk = batch * nt
    const2 = lambda b, t: (0, 0)
    const3 = lambda b, t: (0, 0, 0)
    row_blk = lambda b, t: (b * nt + t, 0)
    meta_blk = lambda b, t: (0, b * nt + t)
    once = dict(pipeline_mode=pl.Buffered(1))
    in_specs = [
        pl.BlockSpec((TB, D_MODEL), lambda b, t: ((b + batch0) * nt + t, 0)),
        pl.BlockSpec((TB, D_MODEL),
                     lambda b, t: (batch0 * nt + jnp.minimum(b * nt + t + 1, nblk - 1), 0)),
        pl.BlockSpec((D_MODEL, IN_COLS), const2, **once),
        pl.BlockSpec((1, HGRN_WIDTH), const2),
        pl.BlockSpec((1, HGRN_WIDTH), const2),
        pl.BlockSpec((1, GMLP_WIDTH), const2),
        pl.BlockSpec((1, GMLP_WIDTH), const2),
        pl.BlockSpec((GMLP_GROUPS, GMLP_BLOCK, GMLP_BLOCK), const3),
        pl.BlockSpec((GMLP_BLOCK, GMLP_GROUPS), const2),
        pl.BlockSpec((D_MODEL, D_MODEL), const2, **once),
        pl.BlockSpec((1, D_MODEL), const2),
        pl.BlockSpec((1, D_MODEL), const2),
        pl.BlockSpec((N_EXPERTS, D_MODEL), const2),
        pl.BlockSpec((N_EXPERTS, D_MODEL), const2),
        pl.BlockSpec((N_EXPERTS, 1), const2),
        pl.BlockSpec((CHUNK, CHUNK), const2),
        pl.BlockSpec((TB, TB), const2, **once),
    ]
    out_shape = [
        jax.ShapeDtypeStruct((t_total, D_MODEL), jnp.float32),
        jax.ShapeDtypeStruct((t_total, HALF), jnp.int32),
        jax.ShapeDtypeStruct((TOP_K, t_total), jnp.int32),
        jax.ShapeDtypeStruct((TOP_K, t_total), jnp.float32),
        jax.ShapeDtypeStruct((TOP_K, t_total), jnp.int32),
        jax.ShapeDtypeStruct((N_EXPERTS, 128), jnp.float32),
    ]
    out_specs = [
        pl.BlockSpec((TB, D_MODEL), row_blk),
        pl.BlockSpec((TB, HALF), row_blk),
        pl.BlockSpec((TOP_K, TB), meta_blk),
        pl.BlockSpec((TOP_K, TB), meta_blk),
        pl.BlockSpec((TOP_K, TB), meta_blk),
        pl.BlockSpec((N_EXPERTS, 128), const2),
    ]
    scratch = [
        pltpu.VMEM((TB, IN_COLS), jnp.float32),
        pltpu.VMEM((TB, HGRN_WIDTH), jnp.float32),
        pltpu.VMEM((TB, HGRN_WIDTH), jnp.float32),
        pltpu.VMEM((CHUNK, HGRN_WIDTH), jnp.float32),
        pltpu.VMEM((CHUNK, HGRN_WIDTH), jnp.float32),
        pltpu.VMEM((TB, HGRN_WIDTH), jnp.float32),
        pltpu.VMEM((HGRN_HEADS, HEAD_DIM, HEAD_DIM), jnp.float32),
        pltpu.VMEM((N_EXPERTS, 128), jnp.float32),
        pltpu.VMEM((TB // CHUNK * HGRN_HEADS, CHUNK, CHUNK), jnp.bfloat16),
        pltpu.VMEM((TB, HGRN_WIDTH), jnp.bfloat16),
        pltpu.VMEM((TB, HGRN_WIDTH), jnp.bfloat16),
        pltpu.VMEM((TB // CHUNK * HGRN_HEADS, HEAD_DIM, HEAD_DIM), jnp.float32),
        pltpu.VMEM((TB, D_MODEL), jnp.bfloat16),
    ]
    return pl.pallas_call(
        _mixer_kernel,
        grid=(batch, nt),
        in_specs=in_specs,
        out_specs=out_specs,
        out_shape=out_shape,
        scratch_shapes=scratch,
        compiler_params=pltpu.CompilerParams(
            dimension_semantics=("arbitrary", "arbitrary"),
            vmem_limit_bytes=VMEM_LIMIT),
        name="mixer",
    )(xt, xt, win, lb, hg, lng, lnb, wm, bst, wout, l1g, l1b, rwh, rwl, rb, tri, upp)


def _plan_kernel(cnt_ref, idx_ref, rank_ref, dest_ref, tiles_ref):
    f32 = jnp.float32
    n_e = N_EXPERTS
    e_sub = lax.broadcasted_iota(jnp.int32, (n_e, n_e), 0)
    e_lane = lax.broadcasted_iota(jnp.int32, (n_e, n_e), 1)
    counts = cnt_ref[:, 0:1]
    padded = jnp.floor((counts + (TM - 1)) * (1.0 / TM)) * TM
    as_row = lambda col: jnp.sum(jnp.where(e_sub == e_lane, col, 0.0), axis=0, keepdims=True)
    padded_row = as_row(padded)
    counts_row = as_row(counts)
    ends = jnp.sum(jnp.where(e_lane <= e_sub, padded_row, 0.0), axis=1, keepdims=True)
    starts = ends - padded
    owns_later = (e_lane > e_sub) & (counts_row > 0.0)
    nxt = jnp.min(jnp.where(owns_later, e_lane, n_e), axis=1, keepdims=True)
    own = lax.broadcasted_iota(jnp.int32, (n_e, 1), 0)
    nxt = jnp.where(nxt == n_e, own, nxt).astype(f32)

    n_lanes = tiles_ref.shape[1]
    tile_row = lax.broadcasted_iota(jnp.int32, (n_e, n_lanes), 1).astype(f32) * TM
    e_of = lax.broadcasted_iota(jnp.int32, (n_e, n_lanes), 0)
    tile_e = jnp.minimum(jnp.sum(jnp.where(tile_row >= ends, 1, 0), axis=0, keepdims=True), n_e - 1)
    mine = e_of == tile_e
    pick = lambda col: jnp.sum(jnp.where(mine, col, 0.0), axis=0, keepdims=True)
    valid = jnp.clip(pick(starts + counts) - tile_row[0:1], 0.0, float(TM))
    rows = [tile_e, valid.astype(jnp.int32), pick(nxt).astype(jnp.int32)]
    rows.append(jnp.zeros((tiles_ref.shape[0] - len(rows), n_lanes), jnp.int32))
    tiles_ref[...] = jnp.concatenate(rows, axis=0)

    chunk = PLAN_CHUNK
    e_chunk = lax.broadcasted_iota(jnp.int32, (n_e, chunk), 0)

    def body(c, carry):
        lanes = pl.ds(pl.multiple_of(c * chunk, chunk), chunk)
        for k in range(TOP_K):
            hit = e_chunk == idx_ref[k:k + 1, lanes]
            start_of = jnp.sum(jnp.where(hit, starts, 0.0), axis=0, keepdims=True)
            dest_ref[k:k + 1, lanes] = rank_ref[k:k + 1, lanes] + start_of.astype(jnp.int32)
        return carry

    lax.fori_loop(0, idx_ref.shape[1] // chunk, body, 0)


def _plan(cnt, idx, rank, n_tiles):
    t_part = idx.shape[1]
    n_lanes = -(-n_tiles // 128) * 128
    full = lambda shape: pl.BlockSpec(shape, lambda i: (0,) * len(shape))
    dest, tiles = pl.pallas_call(
        _plan_kernel,
        grid=(1,),
        in_specs=[full(cnt.shape), full(idx.shape), full(rank.shape)],
        out_specs=[full(idx.shape), full((8, n_lanes))],
        out_shape=[jax.ShapeDtypeStruct((TOP_K, t_part), jnp.int32),
                   jax.ShapeDtypeStruct((8, n_lanes), jnp.int32)],
        compiler_params=pltpu.CompilerParams(
            dimension_semantics=("arbitrary",), vmem_limit_bytes=VMEM_LIMIT),
        name="plan",
    )(cnt, idx, rank)
    return dest, tiles[0, :n_tiles], tiles[1, :n_tiles], tiles[2, :n_tiles]


def _sc_workers():
    info = plsc.get_sparse_core_info()
    return info.num_cores, info.num_cores * info.num_subcores


def _sc_dispatch(rows, dest, n_out):
    t_total, dw = rows.shape
    nc, nw = _sc_workers()
    per_w = t_total // nw
    mesh = plsc.VectorSubcoreMesh(core_axis_name="c", subcore_axis_name="s")

    @functools.partial(
        pl.kernel,
        out_type=jax.ShapeDtypeStruct((n_out, dw), rows.dtype),
        mesh=mesh,
        scratch_types=[pltpu.VMEM((SC_WINDOW,), jnp.int32) for _ in range(TOP_K)]
        + [pltpu.VMEM((SC_WINDOW, dw), rows.dtype), pltpu.SemaphoreType.DMA],
        name="sc_dispatch",
    )
    def k(x_hbm, i_hbm, o_hbm, i0, i1, i2, i3, rows_v, sem):
        wid = lax.axis_index("s") * nc + lax.axis_index("c")
        idx_bufs = (i0, i1, i2, i3)

        @pl.loop(0, per_w // SC_WINDOW)
        def _(j):
            base = wid * per_w + j * SC_WINDOW
            for kk in range(TOP_K):
                pltpu.sync_copy(i_hbm.at[kk, pl.ds(base, SC_WINDOW)], idx_bufs[kk])
            pltpu.sync_copy(x_hbm.at[pl.ds(base, SC_WINDOW)], rows_v)
            copies = [pltpu.async_copy(rows_v, o_hbm.at[idx_bufs[kk]], sem) for kk in range(TOP_K)]
            for cp in copies:
                cp.wait()

    return k(rows, dest)


def _sc_gather(table, idx):
    n_k, t_total = idx.shape
    n = n_k * t_total
    dw = table.shape[1]
    nc, nw = _sc_workers()
    per_w = n // nw
    w_per_k = nw // n_k
    mesh = plsc.VectorSubcoreMesh(core_axis_name="c", subcore_axis_name="s")

    @functools.partial(
        pl.kernel,
        out_type=jax.ShapeDtypeStruct((n, dw), table.dtype),
        mesh=mesh,
        scratch_types=[pltpu.VMEM((SC_WINDOW,), jnp.int32),
                       pltpu.VMEM((SC_WINDOW, dw), table.dtype),
                       pltpu.SemaphoreType.DMA],
        name="sc_gather",
    )
    def k(t_hbm, i_hbm, o_hbm, idx_v, rows_v, sem):
        wid = lax.axis_index("s") * nc + lax.axis_index("c")

        row = wid // w_per_k
        col0 = (wid % w_per_k) * per_w

        @pl.loop(0, per_w // SC_WINDOW)
        def _(j):
            col = col0 + j * SC_WINDOW
            pltpu.sync_copy(i_hbm.at[row, pl.ds(col, SC_WINDOW)], idx_v)
            pltpu.async_copy(t_hbm.at[idx_v], rows_v, sem).wait()
            pltpu.sync_copy(rows_v, o_hbm.at[pl.ds(row * t_total + col, SC_WINDOW)])

    return k(table, idx)


def _expert_kernel(te_ref, tv_ref, tn_ref, xs_ref, w1_hbm, w2_hbm, b1g_ref, b1l_ref, b2_ref,
                   perm_ref, y_ref, w1f_scr, w2f_scr, w1g_scr, w1l_scr, w2_scr, sems):
    i = pl.program_id(0)
    valid = tv_ref[i]
    expert = te_ref[i]
    f32 = jnp.float32
    bf16 = jnp.bfloat16
    expert_changed = (i == 0) | (expert != te_ref[jnp.maximum(i - 1, 0)])

    def weight_copies(e):
        return (pltpu.make_async_copy(w1_hbm.at[e], w1f_scr, sems.at[0]),
                pltpu.make_async_copy(w2_hbm.at[e], w2f_scr, sems.at[1]))

    @pl.when(valid == 0)
    def _():
        y_ref[...] = jnp.zeros_like(y_ref)

    @pl.when(i == 0)
    def _():
        for cp in weight_copies(expert):
            cp.start()

    @pl.when((valid > 0) & expert_changed)
    def _():
        for cp in weight_copies(expert):
            cp.wait()
        w2_scr[...] = w2f_scr[...].astype(bf16)
        perm = perm_ref[...]
        for c in range(2 * D_MODEL // PERM_BLOCK):
            blk = w1f_scr[:, c * PERM_BLOCK:(c + 1) * PERM_BLOCK].astype(bf16)
            r = jnp.dot(blk, perm, preferred_element_type=f32).astype(bf16)
            half = PERM_BLOCK // 2
            w1g_scr[:, c * half:(c + 1) * half] = r[:, :half]
            w1l_scr[:, c * half:(c + 1) * half] = r[:, half:]

        @pl.when(tn_ref[i] != expert)
        def _():
            for cp in weight_copies(tn_ref[i]):
                cp.start()

    def expert_rows(n_rows):
        a, b = _unpack_rows(xs_ref[:n_rows, :])
        keep = lax.broadcasted_iota(jnp.int32, (n_rows, 1), 0) < valid
        x = jnp.where(keep, jnp.concatenate([a, b], axis=1), 0.0).astype(bf16)
        hg = jnp.dot(x, w1g_scr[...], preferred_element_type=f32) + b1g_ref[0]
        hl = jnp.dot(x, w1l_scr[...], preferred_element_type=f32) + b1l_ref[0]
        xg = jnp.minimum(hg, SWIGLU_LIMIT)
        xl = jnp.clip(hl, -SWIGLU_LIMIT, SWIGLU_LIMIT)
        act = xg * _sigmoid(SWIGLU_ALPHA * xg) * (xl + 1.0)
        y = jnp.dot(act.astype(bf16), w2_scr[...], preferred_element_type=f32) + b2_ref[0]
        y_ref[:n_rows, :] = _pack_rows(y)
        if n_rows < TM:
            y_ref[n_rows:, :] = jnp.zeros((TM - n_rows, HALF), jnp.int32)

    for blocks in range(TM // ROW_STEP, 0, -1):
        @pl.when((valid > (blocks - 1) * ROW_STEP) & (valid <= blocks * ROW_STEP))
        def _(blocks=blocks):
            expert_rows(blocks * ROW_STEP)


def _experts(xs, tile_expert, tile_valid, tile_next, w1, w2, b1g, b1l, b2, perm):
    n_slots = xs.shape[0]
    n_tiles = n_slots // TM
    d_ff = w2.shape[1]
    wmap = lambda i, te, tv, tn: (te[i], 0, 0)
    grid_spec = pltpu.PrefetchScalarGridSpec(
        num_scalar_prefetch=3,
        grid=(n_tiles,),
        in_specs=[
            pl.BlockSpec((TM, HALF), lambda i, te, tv, tn: (i, 0)),
            pl.BlockSpec(memory_space=pl.ANY),
            pl.BlockSpec(memory_space=pl.ANY),
            pl.BlockSpec((1, 1, d_ff), wmap),
            pl.BlockSpec((1, 1, d_ff), wmap),
            pl.BlockSpec((1, 1, D_MODEL), wmap),
            pl.BlockSpec((PERM_BLOCK, PERM_BLOCK), lambda i, te, tv, tn: (0, 0)),
        ],
        out_specs=pl.BlockSpec((TM, HALF), lambda i, te, tv, tn: (i, 0)),
        scratch_shapes=[
            pltpu.VMEM((D_MODEL, 2 * d_ff), jnp.float32),
            pltpu.VMEM((d_ff, D_MODEL), jnp.float32),
            pltpu.VMEM((D_MODEL, d_ff), jnp.bfloat16),
            pltpu.VMEM((D_MODEL, d_ff), jnp.bfloat16),
            pltpu.VMEM((d_ff, D_MODEL), jnp.bfloat16),
            pltpu.SemaphoreType.DMA((2,)),
        ],
    )
    return pl.pallas_call(
        _expert_kernel,
        grid_spec=grid_spec,
        out_shape=jax.ShapeDtypeStruct((n_slots, HALF), jnp.int32),
        compiler_params=pltpu.CompilerParams(
            dimension_semantics=("arbitrary",),
            vmem_limit_bytes=VMEM_LIMIT),
        name="experts",
    )(tile_expert, tile_valid, tile_next, xs, w1, w2, b1g, b1l, b2, perm)


def _combine_kernel(h_ref, yk_ref, gate_ref, g_ref, b_ref, *rest):
    o_ref = rest[-1]
    gates = gate_ref[...].T
    acc_a = None
    acc_b = None
    for k in range(TOP_K):
        a, b = _unpack_rows(yk_ref[k])
        gk = gates[:, k:k + 1]
        acc_a = gk * a if acc_a is None else acc_a + gk * a
        acc_b = gk * b if acc_b is None else acc_b + gk * b
    ffn = jnp.concatenate([acc_a, acc_b], axis=1)
    o_ref[...] = _layer_norm(ALPHA * h_ref[...] + ffn, g_ref[...], b_ref[...])


def _combine(h1f, yk, gates, g2, b2, out_prev, part, t_all):
    t_part = h1f.shape[0]
    blk0 = part * (t_part // TC3)
    row = lambda i: (i, 0)
    const = lambda i: (0, 0)
    in_specs = [
        pl.BlockSpec((TC3, D_MODEL), row),
        pl.BlockSpec((TOP_K, TC3, HALF), lambda i: (0, i, 0)),
        pl.BlockSpec((TOP_K, TC3), lambda i: (0, i)),
        pl.BlockSpec((1, D_MODEL), const),
        pl.BlockSpec((1, D_MODEL), const),
    ]
    args = [h1f, yk, gates, g2, b2]
    aliases = {}
    if out_prev is not None:
        in_specs.append(pl.BlockSpec(memory_space=pl.ANY))
        args.append(out_prev)
        aliases = {len(args) - 1: 0}
    return pl.pallas_call(
        _combine_kernel,
        grid=(t_part // TC3,),
        in_specs=in_specs,
        out_specs=pl.BlockSpec((TC3, D_MODEL), lambda i: (i + blk0, 0)),
        out_shape=jax.ShapeDtypeStruct((t_all, D_MODEL), jnp.float32),
        input_output_aliases=aliases,
        compiler_params=pltpu.CompilerParams(
            dimension_semantics=("arbitrary",),
            vmem_limit_bytes=VMEM_LIMIT),
        name="combine",
    )(*args)


def kernel(x, w_in, lb_logits, hgrn_norm_g, gmlp_ln_g, gmlp_ln_b, gmlp_ws, gmlp_bs, w_out, ln1_g, ln1_b, router_w, router_b, exp_w1, exp_b1, exp_w2, exp_b2, ln2_g, ln2_b):
    batch, seq, d = x.shape
    assert d == D_MODEL and seq % TB == 0 and w_in.shape[0] == 1
    t_total = batch * seq
    f32 = jnp.float32
    bf16 = jnp.bfloat16

    lb = jnp.cumsum(jax.nn.softmax(lb_logits.astype(f32), axis=0), axis=0)[0:1]
    chunk_id = jnp.arange(GMLP_BLOCK) // CHUNK
    wm = jnp.where((chunk_id[None, :] <= chunk_id[:, None])[None], gmlp_ws[0], 0.0).astype(bf16)
    rwt = router_w[0].T
    rwh = rwt.astype(bf16)
    rwl = (rwt - rwh.astype(f32)).astype(bf16)
    tri = (jnp.arange(CHUNK)[None, :] <= jnp.arange(CHUNK)[:, None]).astype(bf16)
    upp = (jnp.arange(TB)[:, None] < jnp.arange(TB)[None, :]).astype(bf16)

    lane = jnp.arange(PERM_BLOCK)
    src = jnp.where(lane < PERM_BLOCK // 2, 2 * lane, 2 * (lane - PERM_BLOCK // 2) + 1)
    perm = (jnp.arange(PERM_BLOCK)[:, None] == src[None, :]).astype(bf16)
    xt = x.reshape(t_total, d)
    win = w_in[0].astype(bf16)
    wout = w_out[0].astype(bf16)
    b1g, b1l, b2e = exp_b1[0][:, None, 0::2], exp_b1[0][:, None, 1::2], exp_b2[0][:, None, :]

    assert batch % N_PARTS == 0
    pb = batch // N_PARTS
    t_part = pb * seq
    n_slots = t_part * TOP_K + N_EXPERTS * TM
    n_tiles = n_slots // TM
    out = None
    for part in range(N_PARTS):
        h1f, h1p, idx, gates, rank, cnt = _mixer(
            xt, win, lb, hgrn_norm_g[0:1], gmlp_ln_g[0:1], gmlp_ln_b[0:1], wm, gmlp_bs[0].T, wout,
            ln1_g[0:1], ln1_b[0:1], rwh, rwl, router_b[0][:, None], tri, upp, seq, pb, part * pb)

        dest, tile_expert, tile_valid, tile_next = _plan(cnt, idx, rank, n_tiles)
        xs = _sc_dispatch(h1p, dest, n_slots)
        y = _experts(xs, tile_expert, tile_valid, tile_next,
                     exp_w1[0], exp_w2[0], b1g, b1l, b2e, perm)
        yk = _sc_gather(y, dest).reshape(TOP_K, t_part, HALF)
        out = _combine(h1f, yk, gates, ln2_g[0:1], ln2_b[0:1], out, part, t_total)
    return out.reshape(batch, seq, d)
```

```python
import functools
import math

import jax
import jax.numpy as jnp
from jax import lax
from jax.experimental import pallas as pl
from jax.experimental.pallas import tpu as pltpu
from jax.experimental.pallas import tpu_sc as plsc

D_MODEL = 1024
CHUNK = 64
SUB = 16
N_SUB = CHUNK // SUB
MAX_SUB_DECAY = 60.0
HGRN_WIDTH = 512
HGRN_HEADS = 4
HEAD_DIM = 128
GMLP_WIDTH = 512
GMLP_BLOCK = 128
GMLP_GROUPS = 4
IN_COLS = 3072
N_EXPERTS = 32
TOP_K = 4
SWIGLU_LIMIT = 7.0
SWIGLU_ALPHA = 1.702
ALPHA = 2.0 ** 0.25
EPS = 1e-5
HALF = D_MODEL // 2

TB = 512
TM = 1024
ROW_STEP = 256
PLAN_CHUNK = 2048
TC3 = 512
N_PARTS = 2
SC_WINDOW = 128
PROJ_GROUP = 256
PERM_BLOCK = 256
VMEM_LIMIT = 56 * 1024 * 1024

_NT = (((1,), (1,)), ((), ()))


def _sigmoid(x):
    z = jnp.exp(-jnp.abs(x))
    r = 1.0 / (1.0 + z)
    return jnp.where(x >= 0, r, z * r)


def _gelu(x):
    return 0.5 * x * (1.0 + lax.erf(x * (1.0 / math.sqrt(2.0))))


def _layer_norm(x, g, b):
    mu = jnp.mean(x, axis=-1, keepdims=True)
    xc = x - mu
    var = jnp.mean(xc * xc, axis=-1, keepdims=True)
    return xc * lax.rsqrt(var + EPS) * g + b


def _pack_rows(h):
    a = h[:, :HALF].astype(jnp.bfloat16).astype(jnp.float32)
    b = h[:, HALF:].astype(jnp.bfloat16).astype(jnp.float32)
    au = lax.bitcast_convert_type(a, jnp.uint32) >> 16
    bu = lax.bitcast_convert_type(b, jnp.uint32) & jnp.uint32(0xFFFF0000)
    return lax.bitcast_convert_type(au | bu, jnp.int32)


def _unpack_rows(w):
    u = lax.bitcast_convert_type(w, jnp.uint32)
    a = lax.bitcast_convert_type(u << 16, jnp.float32)
    b = lax.bitcast_convert_type(u & jnp.uint32(0xFFFF0000), jnp.float32)
    return a, b


def _mixer_kernel(x_ref, xn_ref, win_ref, lb_ref, hg_ref, lng_ref, lnb_ref, wm_ref, bst_ref,
                  wout_ref, l1g_ref, l1b_ref, rwh_ref, rwl_ref, rb_ref, tri_ref, upp_ref,
                  h1f_ref, h1p_ref, idx_ref, gate_ref, rank_ref, cnt_ref,
                  p_scr, lf_scr, kk_scr, g_scr, kc_scr, o_scr, st_scr, carry_scr,
                  amat_scr, qg_scr, kd_scr, upd_scr, mix_scr):
    b = pl.program_id(0)
    t = pl.program_id(1)
    step = b * pl.num_programs(1) + t
    f32 = jnp.float32
    bf16 = jnp.bfloat16

    @pl.when(t == 0)
    def _():
        st_scr[...] = jnp.zeros_like(st_scr)

    @pl.when((b == 0) & (t == 0))
    def _():
        carry_scr[...] = jnp.zeros_like(carry_scr)

    def project(xb, group):
        cols = slice(group * PROJ_GROUP, (group + 1) * PROJ_GROUP)
        p_scr[:, cols] = jnp.dot(xb, win_ref[:, cols], preferred_element_type=f32)

    def hgrn_gates():
        fl = p_scr[:, HGRN_WIDTH:2 * HGRN_WIDTH]
        lb = lb_ref[...]
        z = jnp.exp(-jnp.abs(fl))
        r = 1.0 / (1.0 + z)
        zr = z * r
        pos = fl >= 0
        lf_scr[...] = jnp.log(lb + (1.0 - lb) * jnp.where(pos, r, zr))
        kk_scr[...] = (1.0 - lb) * jnp.where(pos, zr, r)

    @pl.when(step == 0)
    def _():
        xb0 = x_ref[...].astype(bf16)
        for group in range(IN_COLS // PROJ_GROUP):
            project(xb0, group)
        hgrn_gates()

    tri = tri_ref[...]
    row64 = lax.broadcasted_iota(jnp.int32, (CHUNK, CHUNK), 0)
    col64 = lax.broadcasted_iota(jnp.int32, (CHUNK, CHUNK), 1)
    lane_sub = lax.broadcasted_iota(jnp.int32, (SUB, CHUNK), 1)
    scale = HEAD_DIM ** -0.5

    def pad_rows(piece, lo_r):
        parts = []
        if lo_r > 0:
            parts.append(jnp.zeros((lo_r, HEAD_DIM), f32))
        parts.append(piece)
        rest = CHUNK - lo_r - piece.shape[0]
        if rest > 0:
            parts.append(jnp.zeros((rest, HEAD_DIM), f32))
        return jnp.concatenate(parts, axis=0) if len(parts) > 1 else piece

    def intra_factorised(q, k, gh):
        qts, kts = [], []
        for a in range(N_SUB):
            lo_r, hi_r = a * SUB, (a + 1) * SUB
            if a == 0:
                qa = q[:hi_r] * jnp.exp(gh[:hi_r])
                ka = k[:hi_r] * jnp.exp(-gh[:hi_r])
            else:
                ra = gh[lo_r - 1:lo_r]
                qa = q[lo_r:hi_r] * jnp.exp(gh[lo_r:hi_r] - ra)
                ka = k[:hi_r] * jnp.exp(ra - gh[:hi_r])
            qts.append(pad_rows(qa, lo_r))
            kts.append(pad_rows(ka, 0))
        a_mat = lax.dot_general(jnp.concatenate(qts, axis=1).astype(bf16),
                                jnp.concatenate(kts, axis=1).astype(bf16), _NT,
                                preferred_element_type=f32)
        return jnp.where(row64 >= col64, a_mat, 0.0)

    def chunks_factorised():
        n_chunks = TB // CHUNK
        heads = [(c, h) for c in range(n_chunks) for h in range(HGRN_HEADS)]

        def cols(h, base=0):
            return slice(base + h * HEAD_DIM, base + (h + 1) * HEAD_DIM)

        for c in range(n_chunks):
            rows = slice(c * CHUNK, (c + 1) * CHUNK)
            lf = lf_scr[rows, :]
            hi = lf.astype(bf16)
            lo = (lf - hi.astype(f32)).astype(bf16)
            gg = jnp.dot(tri, jnp.concatenate([hi, lo], axis=1), preferred_element_type=f32)
            lf_scr[rows, :] = gg[:, :HGRN_WIDTH] + gg[:, HGRN_WIDTH:]
        for c, h in heads:
            rows = slice(c * CHUNK, (c + 1) * CHUNK)
            q = p_scr[rows, cols(h)]
            k = kk_scr[rows, cols(h)]
            gh = lf_scr[rows, cols(h)]
            amat_scr[c * HGRN_HEADS + h] = intra_factorised(q, k, gh).astype(bf16)
            qg_scr[rows, cols(h)] = (q * jnp.exp(gh)).astype(bf16)
            kd_scr[rows, cols(h)] = (k * jnp.exp(gh[CHUNK - 1:CHUNK] - gh)).astype(bf16)
        for c, h in heads:
            rows = slice(c * CHUNK, (c + 1) * CHUNK)
            v = p_scr[rows, cols(h, 2 * HGRN_WIDTH)]
            o_scr[rows, cols(h)] = jnp.dot(amat_scr[c * HGRN_HEADS + h], v.astype(bf16),
                                           preferred_element_type=f32)
            upd_scr[c * HGRN_HEADS + h] = jnp.dot(v.T.astype(bf16), kd_scr[rows, cols(h)],
                                                  preferred_element_type=f32)
        states = [st_scr[h] for h in range(HGRN_HEADS)]
        for c, h in heads:
            rows = slice(c * CHUNK, (c + 1) * CHUNK)
            st = states[h]
            o_inter = lax.dot_general(qg_scr[rows, cols(h)], st.astype(bf16), _NT,
                                      preferred_element_type=f32)
            o_scr[rows, cols(h)] = (o_scr[rows, cols(h)] + o_inter) * scale
            gl = lf_scr[(c + 1) * CHUNK - 1:(c + 1) * CHUNK, cols(h)]
            states[h] = st * jnp.exp(gl) + upd_scr[c * HGRN_HEADS + h]
        for h in range(HGRN_HEADS):
            st_scr[h] = states[h]

    def intra_exact_diagonal(q, k, gh, cs):
        qts, kts = [], []
        for a in range(1, N_SUB):
            lo_r, hi_r = a * SUB, (a + 1) * SUB
            ra = g_scr[lo_r - 1:lo_r, cs]
            qts.append(pad_rows(q[lo_r:hi_r] * jnp.exp(gh[lo_r:hi_r] - ra), lo_r))
            kts.append(pad_rows(k[:lo_r] * jnp.exp(ra - gh[:lo_r]), 0))
        a_off = lax.dot_general(jnp.concatenate(qts, axis=1).astype(bf16),
                                jnp.concatenate(kts, axis=1).astype(bf16), _NT,
                                preferred_element_type=f32)
        diag_rows = []
        for a in range(N_SUB):
            lo_r = a * SUB
            gs = gh[lo_r:lo_r + SUB]
            qs = q[lo_r:lo_r + SUB]
            blk = jnp.zeros((SUB, CHUNK), f32)
            for jl in range(SUB):
                j = lo_r + jl
                gj = g_scr[j:j + 1, cs]
                kj = kc_scr[j:j + 1, cs]
                e = jnp.exp(jnp.minimum(gs - gj, 0.0))
                col = jnp.sum(qs * (kj * e), axis=-1, keepdims=True)
                blk = jnp.where(lane_sub == j, col, blk)
            diag_rows.append(blk)
        a_diag = jnp.concatenate(diag_rows, axis=0)
        return a_off + jnp.where(row64 >= col64, a_diag, 0.0)

    def make_chunk_body(intra):
        def chunk_body(c, carry):
            r0 = pl.multiple_of(c * CHUNK, CHUNK)
            rows = pl.ds(r0, CHUNK)
            lf = lf_scr[rows, :]
            hi = lf.astype(bf16)
            lo = (lf - hi.astype(f32)).astype(bf16)
            gg = jnp.dot(tri, jnp.concatenate([hi, lo], axis=1), preferred_element_type=f32)
            g_all = gg[:, :HGRN_WIDTH] + gg[:, HGRN_WIDTH:]
            g_scr[...] = g_all
            kc_scr[...] = kk_scr[rows, :]
            for h in range(HGRN_HEADS):
                cs = slice(h * HEAD_DIM, (h + 1) * HEAD_DIM)
                q = p_scr[rows, h * HEAD_DIM:(h + 1) * HEAD_DIM]
                v = p_scr[rows, 2 * HGRN_WIDTH + h * HEAD_DIM:2 * HGRN_WIDTH + (h + 1) * HEAD_DIM]
                k = kc_scr[:, cs]
                gh = g_all[:, cs]
                st = st_scr[h]
                o_inter = lax.dot_general((q * jnp.exp(gh)).astype(bf16), st.astype(bf16), _NT,
                                          preferred_element_type=f32)
                a_mat = intra(q, k, gh, cs)
                o = jnp.dot(a_mat.astype(bf16), v.astype(bf16), preferred_element_type=f32) + o_inter
                o_scr[rows, h * HEAD_DIM:(h + 1) * HEAD_DIM] = o * scale

                gl = g_scr[CHUNK - 1:CHUNK, cs]
                kd = k * jnp.exp(gl - gh)
                upd = jnp.dot(v.T.astype(bf16), kd.astype(bf16), preferred_element_type=f32)
                st_scr[h] = st * jnp.exp(gl) + upd
            return carry
        return chunk_body

    lf_all = lf_scr[...]
    sub_decay = -jnp.sum(lf_all.reshape(TB // SUB, SUB, HGRN_WIDTH), axis=1)
    bounded = jnp.max(sub_decay) <= MAX_SUB_DECAY

    f_groups = list(range(HGRN_WIDTH // PROJ_GROUP, 2 * HGRN_WIDTH // PROJ_GROUP))
    qi_groups = [g for g in range(3 * HGRN_WIDTH // PROJ_GROUP) if g not in f_groups]
    rest_groups = list(range(3 * HGRN_WIDTH // PROJ_GROUP, IN_COLS // PROJ_GROUP))

    def gmlp_stage(xnb):
        for g in f_groups:
            project(xnb, g)
        for w in range(TB // GMLP_BLOCK):
            rows = slice(w * GMLP_BLOCK, (w + 1) * GMLP_BLOCK)
            u = _gelu(p_scr[rows, 4 * HGRN_WIDTH:4 * HGRN_WIDTH + GMLP_WIDTH])
            vn = _layer_norm(_gelu(p_scr[rows, 4 * HGRN_WIDTH + GMLP_WIDTH:]),
                             lng_ref[...], lnb_ref[...])
            vnb = vn.astype(bf16)
            cols = []
            for g in range(GMLP_GROUPS):
                s = jnp.dot(wm_ref[g], vnb[:, g * HEAD_DIM:(g + 1) * HEAD_DIM],
                            preferred_element_type=f32) + bst_ref[:, g:g + 1]
                cols.append(s)
            mix_scr[rows, HGRN_WIDTH:] = (u * jnp.concatenate(cols, axis=1)).astype(bf16)

    def output_stage(xnb):
        for g in qi_groups:
            project(xnb, g)
        for w in range(TB // GMLP_BLOCK):
            rows = slice(w * GMLP_BLOCK, (w + 1) * GMLP_BLOCK)
            o = o_scr[rows, :]
            ms = jnp.mean(o * o, axis=-1, keepdims=True)
            gate = p_scr[rows, 3 * HGRN_WIDTH:4 * HGRN_WIDTH]
            y_rec = o * lax.rsqrt(ms + EPS) * hg_ref[...] * (gate * _sigmoid(gate))
            mix_scr[rows, :HGRN_WIDTH] = y_rec.astype(bf16)

        mix = jnp.dot(mix_scr[...], wout_ref[...], preferred_element_type=f32)
        hgrn_gates()
        h1 = _layer_norm(ALPHA * x_ref[...] + mix, l1g_ref[...], l1b_ref[...])
        h1f_ref[...] = h1
        h1p_ref[...] = _pack_rows(h1)
        for g in rest_groups[:2]:
            project(xnb, g)

        hh = h1.astype(bf16)
        hl = (h1 - hh.astype(f32)).astype(bf16)
        rwh = rwh_ref[...]
        logits = (lax.dot_general(rwh, hh, _NT, preferred_element_type=f32)
                  + lax.dot_general(rwh, hl, _NT, preferred_element_type=f32)
                  + lax.dot_general(rwl_ref[...], hh, _NT, preferred_element_type=f32)
                  + rb_ref[...])
        for g in rest_groups[2:]:
            project(xnb, g)

        e_iota = lax.broadcasted_iota(jnp.int32, (N_EXPERTS, TB), 0)
        work = logits
        vals, idxs, hots = [], [], []
        for _ in range(TOP_K):
            m = jnp.max(work, axis=0, keepdims=True)
            ik = jnp.min(jnp.where(work == m, e_iota, N_EXPERTS), axis=0, keepdims=True)
            hot = e_iota == ik
            work = jnp.where(hot, -jnp.inf, work)
            vals.append(m)
            idxs.append(ik)
            hots.append(hot)
        exps = [jnp.exp(vk - vals[0]) for vk in vals]
        denom = exps[0] + exps[1] + exps[2] + exps[3]
        gate_ref[...] = jnp.concatenate([ek / denom for ek in exps], axis=0)
        idx_ref[...] = jnp.concatenate(idxs, axis=0)

        hot_any = jnp.where(hots[0] | hots[1] | hots[2] | hots[3], 1.0, 0.0)
        prefix = jnp.dot(hot_any.astype(bf16), upp_ref[...], preferred_element_type=f32)
        base = prefix + carry_scr[:, 0:1]
        ranks = [jnp.sum(jnp.where(hk, base, 0.0), axis=0, keepdims=True) for hk in hots]
        rank_ref[...] = jnp.concatenate(ranks, axis=0).astype(jnp.int32)
        new_carry = carry_scr[...] + jnp.sum(hot_any, axis=1, keepdims=True)
        carry_scr[...] = new_carry
        cnt_ref[...] = new_carry

    @pl.when(bounded)
    def _():
        xnb = xn_ref[...].astype(bf16)
        gmlp_stage(xnb)
        chunks_factorised()
        output_stage(xnb)

    @pl.when(jnp.logical_not(bounded))
    def _():
        xnb = xn_ref[...].astype(bf16)
        lax.fori_loop(0, TB // CHUNK, make_chunk_body(intra_exact_diagonal), 0)
        gmlp_stage(xnb)
        output_stage(xnb)


def _mixer(xt, win, lb, hg, lng, lnb, wm, bst, wout, l1g, l1b, rwh, rwl, rb, tri, upp,
           seq, batch, batch0):
    nt = seq // TB
    t_total = batch * seq
    nblk = batch * nt
    const2 = lambda b, t: (0, 0)
    const3 = lambda b, t: (0, 0, 0)
    row_blk = lambda b, t: (b * nt + t, 0)
    meta_blk = lambda b, t: (0, b * nt + t)
    once = dict(pipeline_mode=pl.Buffered(1))
    in_specs = [
        pl.BlockSpec((TB, D_MODEL), lambda b, t: ((b + batch0) * nt + t, 0)),
        pl.BlockSpec((TB, D_MODEL),
                     lambda b, t: (batch0 * nt + jnp.minimum(b * nt + t + 1, nblk - 1), 0)),
        pl.BlockSpec((D_MODEL, IN_COLS), const2, **once),
        pl.BlockSpec((1, HGRN_WIDTH), const2),
        pl.BlockSpec((1, HGRN_WIDTH), const2),
        pl.BlockSpec((1, GMLP_WIDTH), const2),
        pl.BlockSpec((1, GMLP_WIDTH), const2),
        pl.BlockSpec((GMLP_GROUPS, GMLP_BLOCK, GMLP_BLOCK), const3),
        pl.BlockSpec((GMLP_BLOCK, GMLP_GROUPS), const2),
        pl.BlockSpec((D_MODEL, D_MODEL), const2, **once),
        pl.BlockSpec((1, D_MODEL), const2),
        pl.BlockSpec((1, D_MODEL), const2),
        pl.BlockSpec((N_EXPERTS, D_MODEL), const2),
        pl.BlockSpec((N_EXPERTS, D_MODEL), const2),
        pl.BlockSpec((N_EXPERTS, 1), const2),
        pl.BlockSpec((CHUNK, CHUNK), const2),
        pl.BlockSpec((TB, TB), const2, **once),
    ]
    out_shape = [
        jax.ShapeDtypeStruct((t_total, D_MODEL), jnp.float32),
        jax.ShapeDtypeStruct((t_total, HALF), jnp.int32),
        jax.ShapeDtypeStruct((TOP_K, t_total), jnp.int32),
        jax.ShapeDtypeStruct((TOP_K, t_total), jnp.float32),
        jax.ShapeDtypeStruct((TOP_K, t_total), jnp.int32),
        jax.ShapeDtypeStruct((N_EXPERTS, 128), jnp.float32),
    ]
    out_specs = [
        pl.BlockSpec((TB, D_MODEL), row_blk),
        pl.BlockSpec((TB, HALF), row_blk),
        pl.BlockSpec((TOP_K, TB), meta_blk),
        pl.BlockSpec((TOP_K, TB), meta_blk),
        pl.BlockSpec((TOP_K, TB), meta_blk),
        pl.BlockSpec((N_EXPERTS, 128), const2),
    ]
    scratch = [
        pltpu.VMEM((TB, IN_COLS), jnp.float32),
        pltpu.VMEM((TB, HGRN_WIDTH), jnp.float32),
        pltpu.VMEM((TB, HGRN_WIDTH), jnp.float32),
        pltpu.VMEM((CHUNK, HGRN_WIDTH), jnp.float32),
        pltpu.VMEM((CHUNK, HGRN_WIDTH), jnp.float32),
        pltpu.VMEM((TB, HGRN_WIDTH), jnp.float32),
        pltpu.VMEM((HGRN_HEADS, HEAD_DIM, HEAD_DIM), jnp.float32),
        pltpu.VMEM((N_EXPERTS, 128), jnp.float32),
        pltpu.VMEM((TB // CHUNK * HGRN_HEADS, CHUNK, CHUNK), jnp.bfloat16),
        pltpu.VMEM((TB, HGRN_WIDTH), jnp.bfloat16),
        pltpu.VMEM((TB, HGRN_WIDTH), jnp.bfloat16),
        pltpu.VMEM((TB // CHUNK * HGRN_HEADS, HEAD_DIM, HEAD_DIM), jnp.float32),
        pltpu.VMEM((TB, D_MODEL), jnp.bfloat16),
    ]
    return pl.pallas_call(
        _mixer_kernel,
        grid=(batch, nt),
        in_specs=in_specs,
        out_specs=out_specs,
        out_shape=out_shape,
        scratch_shapes=scratch,
        compiler_params=pltpu.CompilerParams(
            dimension_semantics=("arbitrary", "arbitrary"),
            vmem_limit_bytes=VMEM_LIMIT),
        name="mixer",
    )(xt, xt, win, lb, hg, lng, lnb, wm, bst, wout, l1g, l1b, rwh, rwl, rb, tri, upp)


def _plan_kernel(cnt_ref, idx_ref, rank_ref, dest_ref, tiles_ref):
    f32 = jnp.float32
    n_e = N_EXPERTS
    e_sub = lax.broadcasted_iota(jnp.int32, (n_e, n_e), 0)
    e_lane = lax.broadcasted_iota(jnp.int32, (n_e, n_e), 1)
    counts = cnt_ref[:, 0:1]
    padded = jnp.floor((counts + (TM - 1)) * (1.0 / TM)) * TM
    as_row = lambda col: jnp.sum(jnp.where(e_sub == e_lane, col, 0.0), axis=0, keepdims=True)
    padded_row = as_row(padded)
    counts_row = as_row(counts)
    ends = jnp.sum(jnp.where(e_lane <= e_sub, padded_row, 0.0), axis=1, keepdims=True)
    starts = ends - padded
    owns_later = (e_lane > e_sub) & (counts_row > 0.0)
    nxt = jnp.min(jnp.where(owns_later, e_lane, n_e), axis=1, keepdims=True)
    own = lax.broadcasted_iota(jnp.int32, (n_e, 1), 0)
    nxt = jnp.where(nxt == n_e, own, nxt).astype(f32)

    n_lanes = tiles_ref.shape[1]
    tile_row = lax.broadcasted_iota(jnp.int32, (n_e, n_lanes), 1).astype(f32) * TM
    e_of = lax.broadcasted_iota(jnp.int32, (n_e, n_lanes), 0)
    tile_e = jnp.minimum(jnp.sum(jnp.where(tile_row >= ends, 1, 0), axis=0, keepdims=True), n_e - 1)
    mine = e_of == tile_e
    pick = lambda col: jnp.sum(jnp.where(mine, col, 0.0), axis=0, keepdims=True)
    valid = jnp.clip(pick(starts + counts) - tile_row[0:1], 0.0, float(TM))
    rows = [tile_e, valid.astype(jnp.int32), pick(nxt).astype(jnp.int32)]
    rows.append(jnp.zeros((tiles_ref.shape[0] - len(rows), n_lanes), jnp.int32))
    tiles_ref[...] = jnp.concatenate(rows, axis=0)

    chunk = PLAN_CHUNK
    e_chunk = lax.broadcasted_iota(jnp.int32, (n_e, chunk), 0)

    def body(c, carry):
        lanes = pl.ds(pl.multiple_of(c * chunk, chunk), chunk)
        for k in range(TOP_K):
            hit = e_chunk == idx_ref[k:k + 1, lanes]
            start_of = jnp.sum(jnp.where(hit, starts, 0.0), axis=0, keepdims=True)
            dest_ref[k:k + 1, lanes] = rank_ref[k:k + 1, lanes] + start_of.astype(jnp.int32)
        return carry

    lax.fori_loop(0, idx_ref.shape[1] // chunk, body, 0)


def _plan(cnt, idx, rank, n_tiles):
    t_part = idx.shape[1]
    n_lanes = -(-n_tiles // 128) * 128
    full = lambda shape: pl.BlockSpec(shape, lambda i: (0,) * len(shape))
    dest, tiles = pl.pallas_call(
        _plan_kernel,
        grid=(1,),
        in_specs=[full(cnt.shape), full(idx.shape), full(rank.shape)],
        out_specs=[full(idx.shape), full((8, n_lanes))],
        out_shape=[jax.ShapeDtypeStruct((TOP_K, t_part), jnp.int32),
                   jax.ShapeDtypeStruct((8, n_lanes), jnp.int32)],
        compiler_params=pltpu.CompilerParams(
            dimension_semantics=("arbitrary",), vmem_limit_bytes=VMEM_LIMIT),
        name="plan",
    )(cnt, idx, rank)
    return dest, tiles[0, :n_tiles], tiles[1, :n_tiles], tiles[2, :n_tiles]


def _sc_workers():
    info = plsc.get_sparse_core_info()
    return info.num_cores, info.num_cores * info.num_subcores


def _sc_dispatch(rows, dest, n_out):
    t_total, dw = rows.shape
    nc, nw = _sc_workers()
    per_w = t_total // nw
    mesh = plsc.VectorSubcoreMesh(core_axis_name="c", subcore_axis_name="s")

    @functools.partial(
        pl.kernel,
        out_type=jax.ShapeDtypeStruct((n_out, dw), rows.dtype),
        mesh=mesh,
        scratch_types=[pltpu.VMEM((SC_WINDOW,), jnp.int32) for _ in range(TOP_K)]
        + [pltpu.VMEM((SC_WINDOW, dw), rows.dtype), pltpu.SemaphoreType.DMA],
        name="sc_dispatch",
    )
    def k(x_hbm, i_hbm, o_hbm, i0, i1, i2, i3, rows_v, sem):
        wid = lax.axis_index("s") * nc + lax.axis_index("c")
        idx_bufs = (i0, i1, i2, i3)

        @pl.loop(0, per_w // SC_WINDOW)
        def _(j):
            base = wid * per_w + j * SC_WINDOW
            for kk in range(TOP_K):
                pltpu.sync_copy(i_hbm.at[kk, pl.ds(base, SC_WINDOW)], idx_bufs[kk])
            pltpu.sync_copy(x_hbm.at[pl.ds(base, SC_WINDOW)], rows_v)
            copies = [pltpu.async_copy(rows_v, o_hbm.at[idx_bufs[kk]], sem) for kk in range(TOP_K)]
            for cp in copies:
                cp.wait()

    return k(rows, dest)


def _sc_gather(table, idx):
    n_k, t_total = idx.shape
    n = n_k * t_total
    dw = table.shape[1]
    nc, nw = _sc_workers()
    per_w = n // nw
    w_per_k = nw // n_k
    mesh = plsc.VectorSubcoreMesh(core_axis_name="c", subcore_axis_name="s")

    @functools.partial(
        pl.kernel,
        out_type=jax.ShapeDtypeStruct((n, dw), table.dtype),
        mesh=mesh,
        scratch_types=[pltpu.VMEM((SC_WINDOW,), jnp.int32),
                       pltpu.VMEM((SC_WINDOW, dw), table.dtype),
                       pltpu.SemaphoreType.DMA],
        name="sc_gather",
    )
    def k(t_hbm, i_hbm, o_hbm, idx_v, rows_v, sem):
        wid = lax.axis_index("s") * nc + lax.axis_index("c")

        row = wid // w_per_k
        col0 = (wid % w_per_k) * per_w

        @pl.loop(0, per_w // SC_WINDOW)
        def _(j):
            col = col0 + j * SC_WINDOW
            pltpu.sync_copy(i_hbm.at[row, pl.ds(col, SC_WINDOW)], idx_v)
            pltpu.async_copy(t_hbm.at[idx_v], rows_v, sem).wait()
            pltpu.sync_copy(rows_v, o_hbm.at[pl.ds(row * t_total + col, SC_WINDOW)])

    return k(table, idx)


def _expert_kernel(te_ref, tv_ref, tn_ref, xs_ref, w1_hbm, w2_hbm, b1g_ref, b1l_ref, b2_ref,
                   perm_ref, y_ref, w1f_scr, w2f_scr, w1g_scr, w1l_scr, w2_scr, sems):
    i = pl.program_id(0)
    valid = tv_ref[i]
    expert = te_ref[i]
    f32 = jnp.float32
    bf16 = jnp.bfloat16
    expert_changed = (i == 0) | (expert != te_ref[jnp.maximum(i - 1, 0)])

    def weight_copies(e):
        return (pltpu.make_async_copy(w1_hbm.at[e], w1f_scr, sems.at[0]),
                pltpu.make_async_copy(w2_hbm.at[e], w2f_scr, sems.at[1]))

    @pl.when(valid == 0)
    def _():
        y_ref[...] = jnp.zeros_like(y_ref)

    @pl.when(i == 0)
    def _():
        for cp in weight_copies(expert):
            cp.start()

    @pl.when((valid > 0) & expert_changed)
    def _():
        for cp in weight_copies(expert):
            cp.wait()
        w2_scr[...] = w2f_scr[...].astype(bf16)
        perm = perm_ref[...]
        for c in range(2 * D_MODEL // PERM_BLOCK):
            blk = w1f_scr[:, c * PERM_BLOCK:(c + 1) * PERM_BLOCK].astype(bf16)
            r = jnp.dot(blk, perm, preferred_element_type=f32).astype(bf16)
            half = PERM_BLOCK // 2
            w1g_scr[:, c * half:(c + 1) * half] = r[:, :half]
            w1l_scr[:, c * half:(c + 1) * half] = r[:, half:]

        @pl.when(tn_ref[i] != expert)
        def _():
            for cp in weight_copies(tn_ref[i]):
                cp.start()

    def expert_rows(n_rows):
        a, b = _unpack_rows(xs_ref[:n_rows, :])
        keep = lax.broadcasted_iota(jnp.int32, (n_rows, 1), 0) < valid
        x = jnp.where(keep, jnp.concatenate([a, b], axis=1), 0.0).astype(bf16)
        hg = jnp.dot(x, w1g_scr[...], preferred_element_type=f32) + b1g_ref[0]
        hl = jnp.dot(x, w1l_scr[...], preferred_element_type=f32) + b1l_ref[0]
        xg = jnp.minimum(hg, SWIGLU_LIMIT)
        xl = jnp.clip(hl, -SWIGLU_LIMIT, SWIGLU_LIMIT)
        act = xg * _sigmoid(SWIGLU_ALPHA * xg) * (xl + 1.0)
        y = jnp.dot(act.astype(bf16), w2_scr[...], preferred_element_type=f32) + b2_ref[0]
        y_ref[:n_rows, :] = _pack_rows(y)
        if n_rows < TM:
            y_ref[n_rows:, :] = jnp.zeros((TM - n_rows, HALF), jnp.int32)

    for blocks in range(TM // ROW_STEP, 0, -1):
        @pl.when((valid > (blocks - 1) * ROW_STEP) & (valid <= blocks * ROW_STEP))
        def _(blocks=blocks):
            expert_rows(blocks * ROW_STEP)


def _experts(xs, tile_expert, tile_valid, tile_next, w1, w2, b1g, b1l, b2, perm):
    n_slots = xs.shape[0]
    n_tiles = n_slots // TM
    d_ff = w2.shape[1]
    wmap = lambda i, te, tv, tn: (te[i], 0, 0)
    grid_spec = pltpu.PrefetchScalarGridSpec(
        num_scalar_prefetch=3,
        grid=(n_tiles,),
        in_specs=[
            pl.BlockSpec((TM, HALF), lambda i, te, tv, tn: (i, 0)),
            pl.BlockSpec(memory_space=pl.ANY),
            pl.BlockSpec(memory_space=pl.ANY),
            pl.BlockSpec((1, 1, d_ff), wmap),
            pl.BlockSpec((1, 1, d_ff), wmap),
            pl.BlockSpec((1, 1, D_MODEL), wmap),
            pl.BlockSpec((PERM_BLOCK, PERM_BLOCK), lambda i, te, tv, tn: (0, 0)),
        ],
        out_specs=pl.BlockSpec((TM, HALF), lambda i, te, tv, tn: (i, 0)),
        scratch_shapes=[
            pltpu.VMEM((D_MODEL, 2 * d_ff), jnp.float32),
            pltpu.VMEM((d_ff, D_MODEL), jnp.float32),
            pltpu.VMEM((D_MODEL, d_ff), jnp.bfloat16),
            pltpu.VMEM((D_MODEL, d_ff), jnp.bfloat16),
            pltpu.VMEM((d_ff, D_MODEL), jnp.bfloat16),
            pltpu.SemaphoreType.DMA((2,)),
        ],
    )
    return pl.pallas_call(
        _expert_kernel,
        grid_spec=grid_spec,
        out_shape=jax.ShapeDtypeStruct((n_slots, HALF), jnp.int32),
        compiler_params=pltpu.CompilerParams(
            dimension_semantics=("arbitrary",),
            vmem_limit_bytes=VMEM_LIMIT),
        name="experts",
    )(tile_expert, tile_valid, tile_next, xs, w1, w2, b1g, b1l, b2, perm)


def _combine_kernel(h_ref, yk_ref, gate_ref, g_ref, b_ref, *rest):
    o_ref = rest[-1]
    gates = gate_ref[...].T
    acc_a = None
    acc_b = None
    for k in range(TOP_K):
        a, b = _unpack_rows(yk_ref[k])
        gk = gates[:, k:k + 1]
        acc_a = gk * a if acc_a is None else acc_a + gk * a
        acc_b = gk * b if acc_b is None else acc_b + gk * b
    ffn = jnp.concatenate([acc_a, acc_b], axis=1)
    o_ref[...] = _layer_norm(ALPHA * h_ref[...] + ffn, g_ref[...], b_ref[...])


def _combine(h1f, yk, gates, g2, b2, out_prev, part, t_all):
    t_part = h1f.shape[0]
    blk0 = part * (t_part // TC3)
    row = lambda i: (i, 0)
    const = lambda i: (0, 0)
    in_specs = [
        pl.BlockSpec((TC3, D_MODEL), row),
        pl.BlockSpec((TOP_K, TC3, HALF), lambda i: (0, i, 0)),
        pl.BlockSpec((TOP_K, TC3), lambda i: (0, i)),
        pl.BlockSpec((1, D_MODEL), const),
        pl.BlockSpec((1, D_MODEL), const),
    ]
    args = [h1f, yk, gates, g2, b2]
    aliases = {}
    if out_prev is not None:
        in_specs.append(pl.BlockSpec(memory_space=pl.ANY))
        args.append(out_prev)
        aliases = {len(args) - 1: 0}
    return pl.pallas_call(
        _combine_kernel,
        grid=(t_part // TC3,),
        in_specs=in_specs,
        out_specs=pl.BlockSpec((TC3, D_MODEL), lambda i: (i + blk0, 0)),
        out_shape=jax.ShapeDtypeStruct((t_all, D_MODEL), jnp.float32),
        input_output_aliases=aliases,
        compiler_params=pltpu.CompilerParams(
            dimension_semantics=("arbitrary",),
            vmem_limit_bytes=VMEM_LIMIT),
        name="combine",
    )(*args)


def kernel(x, w_in, lb_logits, hgrn_norm_g, gmlp_ln_g, gmlp_ln_b, gmlp_ws, gmlp_bs, w_out, ln1_g, ln1_b, router_w, router_b, exp_w1, exp_b1, exp_w2, exp_b2, ln2_g, ln2_b):
    batch, seq, d = x.shape
    assert d == D_MODEL and seq % TB == 0 and w_in.shape[0] == 1
    t_total = batch * seq
    f32 = jnp.float32
    bf16 = jnp.bfloat16

    lb = jnp.cumsum(jax.nn.softmax(lb_logits.astype(f32), axis=0), axis=0)[0:1]
    chunk_id = jnp.arange(GMLP_BLOCK) // CHUNK
    wm = jnp.where((chunk_id[None, :] <= chunk_id[:, None])[None], gmlp_ws[0], 0.0).astype(bf16)
    rwt = router_w[0].T
    rwh = rwt.astype(bf16)
    rwl = (rwt - rwh.astype(f32)).astype(bf16)
    tri = (jnp.arange(CHUNK)[None, :] <= jnp.arange(CHUNK)[:, None]).astype(bf16)
    upp = (jnp.arange(TB)[:, None] < jnp.arange(TB)[None, :]).astype(bf16)

    lane = jnp.arange(PERM_BLOCK)
    src = jnp.where(lane < PERM_BLOCK // 2, 2 * lane, 2 * (lane - PERM_BLOCK // 2) + 1)
    perm = (jnp.arange(PERM_BLOCK)[:, None] == src[None, :]).astype(bf16)
    xt = x.reshape(t_total, d)
    win = w_in[0].astype(bf16)
    wout = w_out[0].astype(bf16)
    b1g, b1l, b2e = exp_b1[0][:, None, 0::2], exp_b1[0][:, None, 1::2], exp_b2[0][:, None, :]

    assert batch % N_PARTS == 0
    pb = batch // N_PARTS
    t_part = pb * seq
    n_slots = t_part * TOP_K + N_EXPERTS * TM
    n_tiles = n_slots // TM
    out = None
    for part in range(N_PARTS):
        h1f, h1p, idx, gates, rank, cnt = _mixer(
            xt, win, lb, hgrn_norm_g[0:1], gmlp_ln_g[0:1], gmlp_ln_b[0:1], wm, gmlp_bs[0].T, wout,
            ln1_g[0:1], ln1_b[0:1], rwh, rwl, router_b[0][:, None], tri, upp, seq, pb, part * pb)

        dest, tile_expert, tile_valid, tile_next = _plan(cnt, idx, rank, n_tiles)
        xs = _sc_dispatch(h1p, dest, n_slots)
        y = _experts(xs, tile_expert, tile_valid, tile_next,
                     exp_w1[0], exp_w2[0], b1g, b1l, b2e, perm)
        yk = _sc_gather(y, dest).reshape(TOP_K, t_part, HALF)
        out = _combine(h1f, yk, gates, ln2_g[0:1], ln2_b[0:1], out, part, t_total)
    return out.reshape(batch, seq, d)
```

```python
import functools
import math

import jax
import jax.numpy as jnp
from jax import lax
from jax.experimental import pallas as pl
from jax.experimental.pallas import tpu as pltpu
from jax.experimental.pallas import tpu_sc as plsc

D_MODEL = 1024
CHUNK = 64
SUB = 16
N_SUB = CHUNK // SUB
MAX_SUB_DECAY = 60.0
HGRN_WIDTH = 512
HGRN_HEADS = 4
HEAD_DIM = 128
GMLP_WIDTH = 512
GMLP_BLOCK = 128
GMLP_GROUPS = 4
IN_COLS = 3072
N_EXPERTS = 32
TOP_K = 4
SWIGLU_LIMIT = 7.0
SWIGLU_ALPHA = 1.702
ALPHA = 2.0 ** 0.25
EPS = 1e-5
HALF = D_MODEL // 2

TB = 512
TM = 1024
ROW_STEP = 256
PLAN_CHUNK = 2048
TC3 = 512
LAST_PART_NUM, LAST_PART_DEN = 1, 4
SC_WINDOW = 128
PROJ_GROUP = 256
PERM_BLOCK = 256
VMEM_LIMIT = 56 * 1024 * 1024

_NT = (((1,), (1,)), ((), ()))


def _sigmoid(x):
    z = jnp.exp(-jnp.abs(x))
    r = 1.0 / (1.0 + z)
    return jnp.where(x >= 0, r, z * r)


def _gelu(x):
    return 0.5 * x * (1.0 + lax.erf(x * (1.0 / math.sqrt(2.0))))


def _layer_norm(x, g, b):
    mu = jnp.mean(x, axis=-1, keepdims=True)
    xc = x - mu
    var = jnp.mean(xc * xc, axis=-1, keepdims=True)
    return xc * lax.rsqrt(var + EPS) * g + b


def _pack_rows(h):
    a = h[:, :HALF].astype(jnp.bfloat16).astype(jnp.float32)
    b = h[:, HALF:].astype(jnp.bfloat16).astype(jnp.float32)
    au = lax.bitcast_convert_type(a, jnp.uint32) >> 16
    bu = lax.bitcast_convert_type(b, jnp.uint32) & jnp.uint32(0xFFFF0000)
    return lax.bitcast_convert_type(au | bu, jnp.int32)


def _unpack_rows(w):
    u = lax.bitcast_convert_type(w, jnp.uint32)
    a = lax.bitcast_convert_type(u << 16, jnp.float32)
    b = lax.bitcast_convert_type(u & jnp.uint32(0xFFFF0000), jnp.float32)
    return a, b


def _mixer_kernel(x_ref, xn_ref, win_ref, lb_ref, hg_ref, lng_ref, lnb_ref, wm_ref, bst_ref,
                  wout_ref, l1g_ref, l1b_ref, rwh_ref, rwl_ref, rb_ref, tri_ref, upp_ref,
                  h1f_ref, h1p_ref, idx_ref, gate_ref, rank_ref, cnt_ref,
                  p_scr, lf_scr, kk_scr, g_scr, kc_scr, o_scr, st_scr, carry_scr,
                  amat_scr, qg_scr, kd_scr, upd_scr, mix_scr):
    b = pl.program_id(0)
    t = pl.program_id(1)
    step = b * pl.num_programs(1) + t
    f32 = jnp.float32
    bf16 = jnp.bfloat16

    @pl.when(t == 0)
    def _():
        st_scr[...] = jnp.zeros_like(st_scr)

    @pl.when((b == 0) & (t == 0))
    def _():
        carry_scr[...] = jnp.zeros_like(carry_scr)

    def project(xb, group):
        cols = slice(group * PROJ_GROUP, (group + 1) * PROJ_GROUP)
        p_scr[:, cols] = jnp.dot(xb, win_ref[:, cols], preferred_element_type=f32)

    def hgrn_gates():
        fl = p_scr[:, HGRN_WIDTH:2 * HGRN_WIDTH]
        lb = lb_ref[...]
        z = jnp.exp(-jnp.abs(fl))
        r = 1.0 / (1.0 + z)
        zr = z * r
        pos = fl >= 0
        lf_scr[...] = jnp.log(lb + (1.0 - lb) * jnp.where(pos, r, zr))
        kk_scr[...] = (1.0 - lb) * jnp.where(pos, zr, r)

    @pl.when(step == 0)
    def _():
        xb0 = x_ref[...].astype(bf16)
        for group in range(IN_COLS // PROJ_GROUP):
            project(xb0, group)
        hgrn_gates()

    tri = tri_ref[...]
    row64 = lax.broadcasted_iota(jnp.int32, (CHUNK, CHUNK), 0)
    col64 = lax.broadcasted_iota(jnp.int32, (CHUNK, CHUNK), 1)
    lane_sub = lax.broadcasted_iota(jnp.int32, (SUB, CHUNK), 1)
    scale = HEAD_DIM ** -0.5

    def pad_rows(piece, lo_r):
        parts = []
        if lo_r > 0:
            parts.append(jnp.zeros((lo_r, HEAD_DIM), f32))
        parts.append(piece)
        rest = CHUNK - lo_r - piece.shape[0]
        if rest > 0:
            parts.append(jnp.zeros((rest, HEAD_DIM), f32))
        return jnp.concatenate(parts, axis=0) if len(parts) > 1 else piece

    def intra_factorised(q, k, gh):
        qts, kts = [], []
        for a in range(N_SUB):
            lo_r, hi_r = a * SUB, (a + 1) * SUB
            if a == 0:
                qa = q[:hi_r] * jnp.exp(gh[:hi_r])
                ka = k[:hi_r] * jnp.exp(-gh[:hi_r])
            else:
                ra = gh[lo_r - 1:lo_r]
                qa = q[lo_r:hi_r] * jnp.exp(gh[lo_r:hi_r] - ra)
                ka = k[:hi_r] * jnp.exp(ra - gh[:hi_r])
            qts.append(pad_rows(qa, lo_r))
            kts.append(pad_rows(ka, 0))
        a_mat = lax.dot_general(jnp.concatenate(qts, axis=1).astype(bf16),
                                jnp.concatenate(kts, axis=1).astype(bf16), _NT,
                                preferred_element_type=f32)
        return jnp.where(row64 >= col64, a_mat, 0.0)

    def chunks_factorised():
        n_chunks = TB // CHUNK
        heads = [(c, h) for c in range(n_chunks) for h in range(HGRN_HEADS)]

        def cols(h, base=0):
            return slice(base + h * HEAD_DIM, base + (h + 1) * HEAD_DIM)

        for c in range(n_chunks):
            rows = slice(c * CHUNK, (c + 1) * CHUNK)
            lf = lf_scr[rows, :]
            hi = lf.astype(bf16)
            lo = (lf - hi.astype(f32)).astype(bf16)
            gg = jnp.dot(tri, jnp.concatenate([hi, lo], axis=1), preferred_element_type=f32)
            lf_scr[rows, :] = gg[:, :HGRN_WIDTH] + gg[:, HGRN_WIDTH:]
        for c, h in heads:
            rows = slice(c * CHUNK, (c + 1) * CHUNK)
            q = p_scr[rows, cols(h)]
            k = kk_scr[rows, cols(h)]
            gh = lf_scr[rows, cols(h)]
            amat_scr[c * HGRN_HEADS + h] = intra_factorised(q, k, gh).astype(bf16)
            qg_scr[rows, cols(h)] = (q * jnp.exp(gh)).astype(bf16)
            kd_scr[rows, cols(h)] = (k * jnp.exp(gh[CHUNK - 1:CHUNK] - gh)).astype(bf16)
        for c, h in heads:
            rows = slice(c * CHUNK, (c + 1) * CHUNK)
            v = p_scr[rows, cols(h, 2 * HGRN_WIDTH)]
            o_scr[rows, cols(h)] = jnp.dot(amat_scr[c * HGRN_HEADS + h], v.astype(bf16),
                                           preferred_element_type=f32)
            upd_scr[c * HGRN_HEADS + h] = jnp.dot(v.T.astype(bf16), kd_scr[rows, cols(h)],
                                                  preferred_element_type=f32)
        states = [st_scr[h] for h in range(HGRN_HEADS)]
        for c, h in heads:
            rows = slice(c * CHUNK, (c + 1) * CHUNK)
            st = states[h]
            o_inter = lax.dot_general(qg_scr[rows, cols(h)], st.astype(bf16), _NT,
                                      preferred_element_type=f32)
            o_scr[rows, cols(h)] = (o_scr[rows, cols(h)] + o_inter) * scale
            gl = lf_scr[(c + 1) * CHUNK - 1:(c + 1) * CHUNK, cols(h)]
            states[h] = st * jnp.exp(gl) + upd_scr[c * HGRN_HEADS + h]
        for h in range(HGRN_HEADS):
            st_scr[h] = states[h]

    def intra_exact_diagonal(q, k, gh, cs):
        qts, kts = [], []
        for a in range(1, N_SUB):
            lo_r, hi_r = a * SUB, (a + 1) * SUB
            ra = g_scr[lo_r - 1:lo_r, cs]
            qts.append(pad_rows(q[lo_r:hi_r] * jnp.exp(gh[lo_r:hi_r] - ra), lo_r))
            kts.append(pad_rows(k[:lo_r] * jnp.exp(ra - gh[:lo_r]), 0))
        a_off = lax.dot_general(jnp.concatenate(qts, axis=1).astype(bf16),
                                jnp.concatenate(kts, axis=1).astype(bf16), _NT,
                                preferred_element_type=f32)
        diag_rows = []
        for a in range(N_SUB):
            lo_r = a * SUB
            gs = gh[lo_r:lo_r + SUB]
            qs = q[lo_r:lo_r + SUB]
            blk = jnp.zeros((SUB, CHUNK), f32)
            for jl in range(SUB):
                j = lo_r + jl
                gj = g_scr[j:j + 1, cs]
                kj = kc_scr[j:j + 1, cs]
                e = jnp.exp(jnp.minimum(gs - gj, 0.0))
                col = jnp.sum(qs * (kj * e), axis=-1, keepdims=True)
                blk = jnp.where(lane_sub == j, col, blk)
            diag_rows.append(blk)
        a_diag = jnp.concatenate(diag_rows, axis=0)
        return a_off + jnp.where(row64 >= col64, a_diag, 0.0)

    def make_chunk_body(intra):
        def chunk_body(c, carry):
            r0 = pl.multiple_of(c * CHUNK, CHUNK)
            rows = pl.ds(r0, CHUNK)
            lf = lf_scr[rows, :]
            hi = lf.astype(bf16)
            lo = (lf - hi.astype(f32)).astype(bf16)
            gg = jnp.dot(tri, jnp.concatenate([hi, lo], axis=1), preferred_element_type=f32)
            g_all = gg[:, :HGRN_WIDTH] + gg[:, HGRN_WIDTH:]
            g_scr[...] = g_all
            kc_scr[...] = kk_scr[rows, :]
            for h in range(HGRN_HEADS):
                cs = slice(h * HEAD_DIM, (h + 1) * HEAD_DIM)
                q = p_scr[rows, h * HEAD_DIM:(h + 1) * HEAD_DIM]
                v = p_scr[rows, 2 * HGRN_WIDTH + h * HEAD_DIM:2 * HGRN_WIDTH + (h + 1) * HEAD_DIM]
                k = kc_scr[:, cs]
                gh = g_all[:, cs]
                st = st_scr[h]
                o_inter = lax.dot_general((q * jnp.exp(gh)).astype(bf16), st.astype(bf16), _NT,
                                          preferred_element_type=f32)
                a_mat = intra(q, k, gh, cs)
                o = jnp.dot(a_mat.astype(bf16), v.astype(bf16), preferred_element_type=f32) + o_inter
                o_scr[rows, h * HEAD_DIM:(h + 1) * HEAD_DIM] = o * scale

                gl = g_scr[CHUNK - 1:CHUNK, cs]
                kd = k * jnp.exp(gl - gh)
                upd = jnp.dot(v.T.astype(bf16), kd.astype(bf16), preferred_element_type=f32)
                st_scr[h] = st * jnp.exp(gl) + upd
            return carry
        return chunk_body

    lf_all = lf_scr[...]
    sub_decay = -jnp.sum(lf_all.reshape(TB // SUB, SUB, HGRN_WIDTH), axis=1)
    bounded = jnp.max(sub_decay) <= MAX_SUB_DECAY

    f_groups = list(range(HGRN_WIDTH // PROJ_GROUP, 2 * HGRN_WIDTH // PROJ_GROUP))
    qi_groups = [g for g in range(3 * HGRN_WIDTH // PROJ_GROUP) if g not in f_groups]
    rest_groups = list(range(3 * HGRN_WIDTH // PROJ_GROUP, IN_COLS // PROJ_GROUP))

    def gmlp_stage(xnb):
        for g in f_groups:
            project(xnb, g)
        for w in range(TB // GMLP_BLOCK):
            rows = slice(w * GMLP_BLOCK, (w + 1) * GMLP_BLOCK)
            u = _gelu(p_scr[rows, 4 * HGRN_WIDTH:4 * HGRN_WIDTH + GMLP_WIDTH])
            vn = _layer_norm(_gelu(p_scr[rows, 4 * HGRN_WIDTH + GMLP_WIDTH:]),
                             lng_ref[...], lnb_ref[...])
            vnb = vn.astype(bf16)
            cols = []
            for g in range(GMLP_GROUPS):
                s = jnp.dot(wm_ref[g], vnb[:, g * HEAD_DIM:(g + 1) * HEAD_DIM],
                            preferred_element_type=f32) + bst_ref[:, g:g + 1]
                cols.append(s)
            mix_scr[rows, HGRN_WIDTH:] = (u * jnp.concatenate(cols, axis=1)).astype(bf16)

    def output_stage(xnb):
        for g in qi_groups:
            project(xnb, g)
        for w in range(TB // GMLP_BLOCK):
            rows = slice(w * GMLP_BLOCK, (w + 1) * GMLP_BLOCK)
            o = o_scr[rows, :]
            ms = jnp.mean(o * o, axis=-1, keepdims=True)
            gate = p_scr[rows, 3 * HGRN_WIDTH:4 * HGRN_WIDTH]
            y_rec = o * lax.rsqrt(ms + EPS) * hg_ref[...] * (gate * _sigmoid(gate))
            mix_scr[rows, :HGRN_WIDTH] = y_rec.astype(bf16)

        mix = jnp.dot(mix_scr[...], wout_ref[...], preferred_element_type=f32)
        hgrn_gates()
        h1 = _layer_norm(ALPHA * x_ref[...] + mix, l1g_ref[...], l1b_ref[...])
        h1f_ref[...] = h1
        h1p_ref[...] = _pack_rows(h1)
        for g in rest_groups[:2]:
            project(xnb, g)

        hh = h1.astype(bf16)
        hl = (h1 - hh.astype(f32)).astype(bf16)
        rwh = rwh_ref[...]
        logits = (lax.dot_general(rwh, hh, _NT, preferred_element_type=f32)
                  + lax.dot_general(rwh, hl, _NT, preferred_element_type=f32)
                  + lax.dot_general(rwl_ref[...], hh, _NT, preferred_element_type=f32)
                  + rb_ref[...])
        for g in rest_groups[2:]:
            project(xnb, g)

        e_iota = lax.broadcasted_iota(jnp.int32, (N_EXPERTS, TB), 0)
        work = logits
        vals, idxs, hots = [], [], []
        for _ in range(TOP_K):
            m = jnp.max(work, axis=0, keepdims=True)
            ik = jnp.min(jnp.where(work == m, e_iota, N_EXPERTS), axis=0, keepdims=True)
            hot = e_iota == ik
            work = jnp.where(hot, -jnp.inf, work)
            vals.append(m)
            idxs.append(ik)
            hots.append(hot)
        exps = [jnp.exp(vk - vals[0]) for vk in vals]
        denom = exps[0] + exps[1] + exps[2] + exps[3]
        gate_ref[...] = jnp.concatenate([ek / denom for ek in exps], axis=0)
        idx_ref[...] = jnp.concatenate(idxs, axis=0)

        hot_any = jnp.where(hots[0] | hots[1] | hots[2] | hots[3], 1.0, 0.0)
        prefix = jnp.dot(hot_any.astype(bf16), upp_ref[...], preferred_element_type=f32)
        base = prefix + carry_scr[:, 0:1]
        ranks = [jnp.sum(jnp.where(hk, base, 0.0), axis=0, keepdims=True) for hk in hots]
        rank_ref[...] = jnp.concatenate(ranks, axis=0).astype(jnp.int32)
        new_carry = carry_scr[...] + jnp.sum(hot_any, axis=1, keepdims=True)
        carry_scr[...] = new_carry
        cnt_ref[...] = new_carry

    @pl.when(bounded)
    def _():
        xnb = xn_ref[...].astype(bf16)
        gmlp_stage(xnb)
        chunks_factorised()
        output_stage(xnb)

    @pl.when(jnp.logical_not(bounded))
    def _():
        xnb = xn_ref[...].astype(bf16)
        lax.fori_loop(0, TB // CHUNK, make_chunk_body(intra_exact_diagonal), 0)
        gmlp_stage(xnb)
        output_stage(xnb)


def _mixer(xt, win, lb, hg, lng, lnb, wm, bst, wout, l1g, l1b, rwh, rwl, rb, tri, upp,
           seq, batch, batch0):
    nt = seq // TB
    t_total = batch * seq
    nblk = batch * nt
    const2 = lambda b, t: (0, 0)
    const3 = lambda b, t: (0, 0, 0)
    row_blk = lambda b, t: (b * nt + t, 0)
    meta_blk = lambda b, t: (0, b * nt + t)
    once = dict(pipeline_mode=pl.Buffered(1))
    in_specs = [
        pl.BlockSpec((TB, D_MODEL), lambda b, t: ((b + batch0) * nt + t, 0)),
        pl.BlockSpec((TB, D_MODEL),
                     lambda b, t: (batch0 * nt + jnp.minimum(b * nt + t + 1, nblk - 1), 0)),
        pl.BlockSpec((D_MODEL, IN_COLS), const2, **once),
        pl.BlockSpec((1, HGRN_WIDTH), const2),
        pl.BlockSpec((1, HGRN_WIDTH), const2),
        pl.BlockSpec((1, GMLP_WIDTH), const2),
        pl.BlockSpec((1, GMLP_WIDTH), const2),
        pl.BlockSpec((GMLP_GROUPS, GMLP_BLOCK, GMLP_BLOCK), const3),
        pl.BlockSpec((GMLP_BLOCK, GMLP_GROUPS), const2),
        pl.BlockSpec((D_MODEL, D_MODEL), const2, **once),
        pl.BlockSpec((1, D_MODEL), const2),
        pl.BlockSpec((1, D_MODEL), const2),
        pl.BlockSpec((N_EXPERTS, D_MODEL), const2),
        pl.BlockSpec((N_EXPERTS, D_MODEL), const2),
        pl.BlockSpec((N_EXPERTS, 1), const2),
        pl.BlockSpec((CHUNK, CHUNK), const2),
        pl.BlockSpec((TB, TB), const2, **once),
    ]
    out_shape = [
        jax.ShapeDtypeStruct((t_total, D_MODEL), jnp.float32),
        jax.ShapeDtypeStruct((t_total, HALF), jnp.int32),
        jax.ShapeDtypeStruct((TOP_K, t_total), jnp.int32),
        jax.ShapeDtypeStruct((TOP_K, t_total), jnp.float32),
        jax.ShapeDtypeStruct((TOP_K, t_total), jnp.int32),
        jax.ShapeDtypeStruct((N_EXPERTS, 128), jnp.float32),
    ]
    out_specs = [
        pl.BlockSpec((TB, D_MODEL), row_blk),
        pl.BlockSpec((TB, HALF), row_blk),
        pl.BlockSpec((TOP_K, TB), meta_blk),
        pl.BlockSpec((TOP_K, TB), meta_blk),
        pl.BlockSpec((TOP_K, TB), meta_blk),
        pl.BlockSpec((N_EXPERTS, 128), const2),
    ]
    scratch = [
        pltpu.VMEM((TB, IN_COLS), jnp.float32),
        pltpu.VMEM((TB, HGRN_WIDTH), jnp.float32),
        pltpu.VMEM((TB, HGRN_WIDTH), jnp.float32),
        pltpu.VMEM((CHUNK, HGRN_WIDTH), jnp.float32),
        pltpu.VMEM((CHUNK, HGRN_WIDTH), jnp.float32),
        pltpu.VMEM((TB, HGRN_WIDTH), jnp.float32),
        pltpu.VMEM((HGRN_HEADS, HEAD_DIM, HEAD_DIM), jnp.float32),
        pltpu.VMEM((N_EXPERTS, 128), jnp.float32),
        pltpu.VMEM((TB // CHUNK * HGRN_HEADS, CHUNK, CHUNK), jnp.bfloat16),
        pltpu.VMEM((TB, HGRN_WIDTH), jnp.bfloat16),
        pltpu.VMEM((TB, HGRN_WIDTH), jnp.bfloat16),
        pltpu.VMEM((TB // CHUNK * HGRN_HEADS, HEAD_DIM, HEAD_DIM), jnp.float32),
        pltpu.VMEM((TB, D_MODEL), jnp.bfloat16),
    ]
    return pl.pallas_call(
        _mixer_kernel,
        grid=(batch, nt),
        in_specs=in_specs,
        out_specs=out_specs,
        out_shape=out_shape,
        scratch_shapes=scratch,
        compiler_params=pltpu.CompilerParams(
            dimension_semantics=("arbitrary", "arbitrary"),
            vmem_limit_bytes=VMEM_LIMIT),
        name="mixer",
    )(xt, xt, win, lb, hg, lng, lnb, wm, bst, wout, l1g, l1b, rwh, rwl, rb, tri, upp)


def _plan_kernel(cnt_ref, idx_ref, rank_ref, dest_ref, tiles_ref):
    f32 = jnp.float32
    n_e = N_EXPERTS
    e_sub = lax.broadcasted_iota(jnp.int32, (n_e, n_e), 0)
    e_lane = lax.broadcasted_iota(jnp.int32, (n_e, n_e), 1)
    counts = cnt_ref[:, 0:1]
    padded = jnp.floor((counts + (TM - 1)) * (1.0 / TM)) * TM
    as_row = lambda col: jnp.sum(jnp.where(e_sub == e_lane, col, 0.0), axis=0, keepdims=True)
    padded_row = as_row(padded)
    counts_row = as_row(counts)
    ends = jnp.sum(jnp.where(e_lane <= e_sub, padded_row, 0.0), axis=1, keepdims=True)
    starts = ends - padded
    owns_later = (e_lane > e_sub) & (counts_row > 0.0)
    nxt = jnp.min(jnp.where(owns_later, e_lane, n_e), axis=1, keepdims=True)
    own = lax.broadcasted_iota(jnp.int32, (n_e, 1), 0)
    nxt = jnp.where(nxt == n_e, own, nxt).astype(f32)

    n_lanes = tiles_ref.shape[1]
    tile_row = lax.broadcasted_iota(jnp.int32, (n_e, n_lanes), 1).astype(f32) * TM
    e_of = lax.broadcasted_iota(jnp.int32, (n_e, n_lanes), 0)
    tile_e = jnp.minimum(jnp.sum(jnp.where(tile_row >= ends, 1, 0), axis=0, keepdims=True), n_e - 1)
    mine = e_of == tile_e
    pick = lambda col: jnp.sum(jnp.where(mine, col, 0.0), axis=0, keepdims=True)
    valid = jnp.clip(pick(starts + counts) - tile_row[0:1], 0.0, float(TM))
    rows = [tile_e, valid.astype(jnp.int32), pick(nxt).astype(jnp.int32)]
    rows.append(jnp.zeros((tiles_ref.shape[0] - len(rows), n_lanes), jnp.int32))
    tiles_ref[...] = jnp.concatenate(rows, axis=0)

    chunk = PLAN_CHUNK
    e_chunk = lax.broadcasted_iota(jnp.int32, (n_e, chunk), 0)

    def body(c, carry):
        lanes = pl.ds(pl.multiple_of(c * chunk, chunk), chunk)
        for k in range(TOP_K):
            hit = e_chunk == idx_ref[k:k + 1, lanes]
            start_of = jnp.sum(jnp.where(hit, starts, 0.0), axis=0, keepdims=True)
            dest_ref[k:k + 1, lanes] = rank_ref[k:k + 1, lanes] + start_of.astype(jnp.int32)
        return carry

    lax.fori_loop(0, idx_ref.shape[1] // chunk, body, 0)


def _plan(cnt, idx, rank, n_tiles):
    t_part = idx.shape[1]
    n_lanes = -(-n_tiles // 128) * 128
    full = lambda shape: pl.BlockSpec(shape, lambda i: (0,) * len(shape))
    dest, tiles = pl.pallas_call(
        _plan_kernel,
        grid=(1,),
        in_specs=[full(cnt.shape), full(idx.shape), full(rank.shape)],
        out_specs=[full(idx.shape), full((8, n_lanes))],
        out_shape=[jax.ShapeDtypeStruct((TOP_K, t_part), jnp.int32),
                   jax.ShapeDtypeStruct((8, n_lanes), jnp.int32)],
        compiler_params=pltpu.CompilerParams(
            dimension_semantics=("arbitrary",), vmem_limit_bytes=VMEM_LIMIT),
        name="plan",
    )(cnt, idx, rank)
    return dest, tiles[0, :n_tiles], tiles[1, :n_tiles], tiles[2, :n_tiles]


def _sc_workers():
    info = plsc.get_sparse_core_info()
    return info.num_cores, info.num_cores * info.num_subcores


def _sc_dispatch(rows, dest, n_out):
    t_total, dw = rows.shape
    nc, nw = _sc_workers()
    per_w = t_total // nw
    mesh = plsc.VectorSubcoreMesh(core_axis_name="c", subcore_axis_name="s")

    @functools.partial(
        pl.kernel,
        out_type=jax.ShapeDtypeStruct((n_out, dw), rows.dtype),
        mesh=mesh,
        scratch_types=[pltpu.VMEM((SC_WINDOW,), jnp.int32) for _ in range(TOP_K)]
        + [pltpu.VMEM((SC_WINDOW, dw), rows.dtype), pltpu.SemaphoreType.DMA],
        name="sc_dispatch",
    )
    def k(x_hbm, i_hbm, o_hbm, i0, i1, i2, i3, rows_v, sem):
        wid = lax.axis_index("s") * nc + lax.axis_index("c")
        idx_bufs = (i0, i1, i2, i3)

        @pl.loop(0, per_w // SC_WINDOW)
        def _(j):
            base = wid * per_w + j * SC_WINDOW
            for kk in range(TOP_K):
                pltpu.sync_copy(i_hbm.at[kk, pl.ds(base, SC_WINDOW)], idx_bufs[kk])
            pltpu.sync_copy(x_hbm.at[pl.ds(base, SC_WINDOW)], rows_v)
            copies = [pltpu.async_copy(rows_v, o_hbm.at[idx_bufs[kk]], sem) for kk in range(TOP_K)]
            for cp in copies:
                cp.wait()

    return k(rows, dest)


def _sc_gather(table, idx):
    n_k, t_total = idx.shape
    n = n_k * t_total
    dw = table.shape[1]
    nc, nw = _sc_workers()
    per_w = n // nw
    w_per_k = nw // n_k
    mesh = plsc.VectorSubcoreMesh(core_axis_name="c", subcore_axis_name="s")

    @functools.partial(
        pl.kernel,
        out_type=jax.ShapeDtypeStruct((n, dw), table.dtype),
        mesh=mesh,
        scratch_types=[pltpu.VMEM((SC_WINDOW,), jnp.int32),
                       pltpu.VMEM((SC_WINDOW, dw), table.dtype),
                       pltpu.SemaphoreType.DMA],
        name="sc_gather",
    )
    def k(t_hbm, i_hbm, o_hbm, idx_v, rows_v, sem):
        wid = lax.axis_index("s") * nc + lax.axis_index("c")

        row = wid // w_per_k
        col0 = (wid % w_per_k) * per_w

        @pl.loop(0, per_w // SC_WINDOW)
        def _(j):
            col = col0 + j * SC_WINDOW
            pltpu.sync_copy(i_hbm.at[row, pl.ds(col, SC_WINDOW)], idx_v)
            pltpu.async_copy(t_hbm.at[idx_v], rows_v, sem).wait()
            pltpu.sync_copy(rows_v, o_hbm.at[pl.ds(row * t_total + col, SC_WINDOW)])

    return k(table, idx)


def _expert_kernel(te_ref, tv_ref, tn_ref, xs_ref, w1_hbm, w2_hbm, b1g_ref, b1l_ref, b2_ref,
                   perm_ref, y_ref, w1f_scr, w2f_scr, w1g_scr, w1l_scr, w2_scr, sems):
    i = pl.program_id(0)
    valid = tv_ref[i]
    expert = te_ref[i]
    f32 = jnp.float32
    bf16 = jnp.bfloat16
    expert_changed = (i == 0) | (expert != te_ref[jnp.maximum(i - 1, 0)])

    def weight_copies(e):
        return (pltpu.make_async_copy(w1_hbm.at[e], w1f_scr, sems.at[0]),
                pltpu.make_async_copy(w2_hbm.at[e], w2f_scr, sems.at[1]))

    @pl.when(valid == 0)
    def _():
        y_ref[...] = jnp.zeros_like(y_ref)

    @pl.when(i == 0)
    def _():
        for cp in weight_copies(expert):
            cp.start()

    @pl.when((valid > 0) & expert_changed)
    def _():
        for cp in weight_copies(expert):
            cp.wait()
        w2_scr[...] = w2f_scr[...].astype(bf16)
        perm = perm_ref[...]
        for c in range(2 * D_MODEL // PERM_BLOCK):
            blk = w1f_scr[:, c * PERM_BLOCK:(c + 1) * PERM_BLOCK].astype(bf16)
            r = jnp.dot(blk, perm, preferred_element_type=f32).astype(bf16)
            half = PERM_BLOCK // 2
            w1g_scr[:, c * half:(c + 1) * half] = r[:, :half]
            w1l_scr[:, c * half:(c + 1) * half] = r[:, half:]

        @pl.when(tn_ref[i] != expert)
        def _():
            for cp in weight_copies(tn_ref[i]):
                cp.start()

    def expert_rows(n_rows):
        a, b = _unpack_rows(xs_ref[:n_rows, :])
        keep = lax.broadcasted_iota(jnp.int32, (n_rows, 1), 0) < valid
        x = jnp.where(keep, jnp.concatenate([a, b], axis=1), 0.0).astype(bf16)
        hg = jnp.dot(x, w1g_scr[...], preferred_element_type=f32) + b1g_ref[0]
        hl = jnp.dot(x, w1l_scr[...], preferred_element_type=f32) + b1l_ref[0]
        xg = jnp.minimum(hg, SWIGLU_LIMIT)
        xl = jnp.clip(hl, -SWIGLU_LIMIT, SWIGLU_LIMIT)
        act = xg * _sigmoid(SWIGLU_ALPHA * xg) * (xl + 1.0)
        y = jnp.dot(act.astype(bf16), w2_scr[...], preferred_element_type=f32) + b2_ref[0]
        y_ref[:n_rows, :] = _pack_rows(y)
        if n_rows < TM:
            y_ref[n_rows:, :] = jnp.zeros((TM - n_rows, HALF), jnp.int32)

    for blocks in range(TM // ROW_STEP, 0, -1):
        @pl.when((valid > (blocks - 1) * ROW_STEP) & (valid <= blocks * ROW_STEP))
        def _(blocks=blocks):
            expert_rows(blocks * ROW_STEP)


def _experts(xs, tile_expert, tile_valid, tile_next, w1, w2, b1g, b1l, b2, perm):
    n_slots = xs.shape[0]
    n_tiles = n_slots // TM
    d_ff = w2.shape[1]
    wmap = lambda i, te, tv, tn: (te[i], 0, 0)
    grid_spec = pltpu.PrefetchScalarGridSpec(
        num_scalar_prefetch=3,
        grid=(n_tiles,),
        in_specs=[
            pl.BlockSpec((TM, HALF), lambda i, te, tv, tn: (i, 0)),
            pl.BlockSpec(memory_space=pl.ANY),
            pl.BlockSpec(memory_space=pl.ANY),
            pl.BlockSpec((1, 1, d_ff), wmap),
            pl.BlockSpec((1, 1, d_ff), wmap),
            pl.BlockSpec((1, 1, D_MODEL), wmap),
            pl.BlockSpec((PERM_BLOCK, PERM_BLOCK), lambda i, te, tv, tn: (0, 0)),
        ],
        out_specs=pl.BlockSpec((TM, HALF), lambda i, te, tv, tn: (i, 0)),
        scratch_shapes=[
            pltpu.VMEM((D_MODEL, 2 * d_ff), jnp.float32),
            pltpu.VMEM((d_ff, D_MODEL), jnp.float32),
            pltpu.VMEM((D_MODEL, d_ff), jnp.bfloat16),
            pltpu.VMEM((D_MODEL, d_ff), jnp.bfloat16),
            pltpu.VMEM((d_ff, D_MODEL), jnp.bfloat16),
            pltpu.SemaphoreType.DMA((2,)),
        ],
    )
    return pl.pallas_call(
        _expert_kernel,
        grid_spec=grid_spec,
        out_shape=jax.ShapeDtypeStruct((n_slots, HALF), jnp.int32),
        compiler_params=pltpu.CompilerParams(
            dimension_semantics=("arbitrary",),
            vmem_limit_bytes=VMEM_LIMIT),
        name="experts",
    )(tile_expert, tile_valid, tile_next, xs, w1, w2, b1g, b1l, b2, perm)


def _combine_kernel(h_ref, yk_ref, gate_ref, g_ref, b_ref, *rest):
    o_ref = rest[-1]
    gates = gate_ref[...].T
    acc_a = None
    acc_b = None
    for k in range(TOP_K):
        a, b = _unpack_rows(yk_ref[k])
        gk = gates[:, k:k + 1]
        acc_a = gk * a if acc_a is None else acc_a + gk * a
        acc_b = gk * b if acc_b is None else acc_b + gk * b
    ffn = jnp.concatenate([acc_a, acc_b], axis=1)
    o_ref[...] = _layer_norm(ALPHA * h_ref[...] + ffn, g_ref[...], b_ref[...])


def _combine(h1f, yk, gates, g2, b2, out_prev, row0, t_all):
    t_part = h1f.shape[0]
    blk0 = row0 // TC3
    row = lambda i: (i, 0)
    const = lambda i: (0, 0)
    in_specs = [
        pl.BlockSpec((TC3, D_MODEL), row),
        pl.BlockSpec((TOP_K, TC3, HALF), lambda i: (0, i, 0)),
        pl.BlockSpec((TOP_K, TC3), lambda i: (0, i)),
        pl.BlockSpec((1, D_MODEL), const),
        pl.BlockSpec((1, D_MODEL), const),
    ]
    args = [h1f, yk, gates, g2, b2]
    aliases = {}
    if out_prev is not None:
        in_specs.append(pl.BlockSpec(memory_space=pl.ANY))
        args.append(out_prev)
        aliases = {len(args) - 1: 0}
    return pl.pallas_call(
        _combine_kernel,
        grid=(t_part // TC3,),
        in_specs=in_specs,
        out_specs=pl.BlockSpec((TC3, D_MODEL), lambda i: (i + blk0, 0)),
        out_shape=jax.ShapeDtypeStruct((t_all, D_MODEL), jnp.float32),
        input_output_aliases=aliases,
        compiler_params=pltpu.CompilerParams(
            dimension_semantics=("arbitrary",),
            vmem_limit_bytes=VMEM_LIMIT),
        name="combine",
    )(*args)


def kernel(x, w_in, lb_logits, hgrn_norm_g, gmlp_ln_g, gmlp_ln_b, gmlp_ws, gmlp_bs, w_out, ln1_g, ln1_b, router_w, router_b, exp_w1, exp_b1, exp_w2, exp_b2, ln2_g, ln2_b):
    batch, seq, d = x.shape
    assert d == D_MODEL and seq % TB == 0 and w_in.shape[0] == 1
    t_total = batch * seq
    f32 = jnp.float32
    bf16 = jnp.bfloat16

    lb = jnp.cumsum(jax.nn.softmax(lb_logits.astype(f32), axis=0), axis=0)[0:1]
    chunk_id = jnp.arange(GMLP_BLOCK) // CHUNK
    wm = jnp.where((chunk_id[None, :] <= chunk_id[:, None])[None], gmlp_ws[0], 0.0).astype(bf16)
    rwt = router_w[0].T
    rwh = rwt.astype(bf16)
    rwl = (rwt - rwh.astype(f32)).astype(bf16)
    tri = (jnp.arange(CHUNK)[None, :] <= jnp.arange(CHUNK)[:, None]).astype(bf16)
    upp = (jnp.arange(TB)[:, None] < jnp.arange(TB)[None, :]).astype(bf16)

    lane = jnp.arange(PERM_BLOCK)
    src = jnp.where(lane < PERM_BLOCK // 2, 2 * lane, 2 * (lane - PERM_BLOCK // 2) + 1)
    perm = (jnp.arange(PERM_BLOCK)[:, None] == src[None, :]).astype(bf16)
    xt = x.reshape(t_total, d)
    win = w_in[0].astype(bf16)
    wout = w_out[0].astype(bf16)
    b1g, b1l, b2e = exp_b1[0][:, None, 0::2], exp_b1[0][:, None, 1::2], exp_b2[0][:, None, :]

    last = max(batch * LAST_PART_NUM // LAST_PART_DEN, 1) if batch > 1 else 0
    part_sizes = [pb for pb in (batch - last, last) if pb > 0]
    out = None
    b0 = 0
    for pb in part_sizes:
        t_part = pb * seq
        n_slots = t_part * TOP_K + N_EXPERTS * TM
        n_tiles = n_slots // TM
        h1f, h1p, idx, gates, rank, cnt = _mixer(
            xt, win, lb, hgrn_norm_g[0:1], gmlp_ln_g[0:1], gmlp_ln_b[0:1], wm, gmlp_bs[0].T, wout,
            ln1_g[0:1], ln1_b[0:1], rwh, rwl, router_b[0][:, None], tri, upp, seq, pb, b0)

        dest, tile_expert, tile_valid, tile_next = _plan(cnt, idx, rank, n_tiles)
        xs = _sc_dispatch(h1p, dest, n_slots)
        y = _experts(xs, tile_expert, tile_valid, tile_next,
                     exp_w1[0], exp_w2[0], b1g, b1l, b2e, perm)
        yk = _sc_gather(y, dest).reshape(TOP_K, t_part, HALF)
        out = _combine(h1f, yk, gates, ln2_g[0:1], ln2_b[0:1], out, b0 * seq, t_total)
        b0 += pb
    return out.reshape(batch, seq, d)
```

```python
import functools
import math

import jax
import jax.numpy as jnp
from jax import lax
from jax.experimental import pallas as pl
from jax.experimental.pallas import tpu as pltpu
from jax.experimental.pallas import tpu_sc as plsc

D_MODEL = 1024
CHUNK = 64
SUB = 16
N_SUB = CHUNK // SUB
MAX_SUB_DECAY = 60.0
HGRN_WIDTH = 512
HGRN_HEADS = 4
HEAD_DIM = 128
GMLP_WIDTH = 512
GMLP_BLOCK = 128
GMLP_GROUPS = 4
IN_COLS = 3072
N_EXPERTS = 32
TOP_K = 4
SWIGLU_LIMIT = 7.0
SWIGLU_ALPHA = 1.702
ALPHA = 2.0 ** 0.25
EPS = 1e-5
HALF = D_MODEL // 2

TB = 512
TM = 1024
ROW_STEP = 256
MLP_ROWS = 256
PLAN_CHUNK = 2048
TC3 = 512
LAST_PART_NUM, LAST_PART_DEN = 3, 16
SC_WINDOW = 128
PROJ_GROUP = 256
PERM_BLOCK = 256
VMEM_LIMIT = 56 * 1024 * 1024

_NT = (((1,), (1,)), ((), ()))


def _sigmoid(x):
    z = jnp.exp(-jnp.abs(x))
    r = 1.0 / (1.0 + z)
    return jnp.where(x >= 0, r, z * r)


def _gelu(x):
    return 0.5 * x * (1.0 + lax.erf(x * (1.0 / math.sqrt(2.0))))


def _layer_norm(x, g, b):
    mu = jnp.mean(x, axis=-1, keepdims=True)
    xc = x - mu
    var = jnp.mean(xc * xc, axis=-1, keepdims=True)
    return xc * lax.rsqrt(var + EPS) * g + b


def _pack_rows(h):
    a = h[:, :HALF].astype(jnp.bfloat16).astype(jnp.float32)
    b = h[:, HALF:].astype(jnp.bfloat16).astype(jnp.float32)
    au = lax.bitcast_convert_type(a, jnp.uint32) >> 16
    bu = lax.bitcast_convert_type(b, jnp.uint32) & jnp.uint32(0xFFFF0000)
    return lax.bitcast_convert_type(au | bu, jnp.int32)


def _unpack_rows(w):
    u = lax.bitcast_convert_type(w, jnp.uint32)
    a = lax.bitcast_convert_type(u << 16, jnp.float32)
    b = lax.bitcast_convert_type(u & jnp.uint32(0xFFFF0000), jnp.float32)
    return a, b


def _mixer_kernel(x_ref, xn_ref, win_ref, lb_ref, hg_ref, lng_ref, lnb_ref, wm_ref, bst_ref,
                  wout_ref, l1g_ref, l1b_ref, rwh_ref, rwl_ref, rb_ref, tri_ref, upp_ref,
                  h1f_ref, h1p_ref, idx_ref, gate_ref, rank_ref, cnt_ref,
                  p_scr, lf_scr, kk_scr, g_scr, kc_scr, o_scr, st_scr, carry_scr,
                  amat_scr, qg_scr, kd_scr, upd_scr, mix_scr):
    b = pl.program_id(0)
    t = pl.program_id(1)
    step = b * pl.num_programs(1) + t
    f32 = jnp.float32
    bf16 = jnp.bfloat16

    @pl.when(t == 0)
    def _():
        st_scr[...] = jnp.zeros_like(st_scr)

    @pl.when((b == 0) & (t == 0))
    def _():
        carry_scr[...] = jnp.zeros_like(carry_scr)

    def project(xb, group):
        cols = slice(group * PROJ_GROUP, (group + 1) * PROJ_GROUP)
        p_scr[:, cols] = jnp.dot(xb, win_ref[:, cols], preferred_element_type=f32)

    def hgrn_gates():
        fl = p_scr[:, HGRN_WIDTH:2 * HGRN_WIDTH]
        lb = lb_ref[...]
        z = jnp.exp(-jnp.abs(fl))
        r = 1.0 / (1.0 + z)
        zr = z * r
        pos = fl >= 0
        lf_scr[...] = jnp.log(lb + (1.0 - lb) * jnp.where(pos, r, zr))
        kk_scr[...] = (1.0 - lb) * jnp.where(pos, zr, r)

    @pl.when(step == 0)
    def _():
        xb0 = x_ref[...].astype(bf16)
        for group in range(IN_COLS // PROJ_GROUP):
            project(xb0, group)
        hgrn_gates()

    tri = tri_ref[...]
    row64 = lax.broadcasted_iota(jnp.int32, (CHUNK, CHUNK), 0)
    col64 = lax.broadcasted_iota(jnp.int32, (CHUNK, CHUNK), 1)
    lane_sub = lax.broadcasted_iota(jnp.int32, (SUB, CHUNK), 1)
    scale = HEAD_DIM ** -0.5

    def pad_rows(piece, lo_r):
        parts = []
        if lo_r > 0:
            parts.append(jnp.zeros((lo_r, HEAD_DIM), f32))
        parts.append(piece)
        rest = CHUNK - lo_r - piece.shape[0]
        if rest > 0:
            parts.append(jnp.zeros((rest, HEAD_DIM), f32))
        return jnp.concatenate(parts, axis=0) if len(parts) > 1 else piece

    def intra_factorised(q, k, gh):
        qts, kts = [], []
        for a in range(N_SUB):
            lo_r, hi_r = a * SUB, (a + 1) * SUB
            if a == 0:
                qa = q[:hi_r] * jnp.exp(gh[:hi_r])
                ka = k[:hi_r] * jnp.exp(-gh[:hi_r])
            else:
                ra = gh[lo_r - 1:lo_r]
                qa = q[lo_r:hi_r] * jnp.exp(gh[lo_r:hi_r] - ra)
                ka = k[:hi_r] * jnp.exp(ra - gh[:hi_r])
            qts.append(pad_rows(qa, lo_r))
            kts.append(pad_rows(ka, 0))
        a_mat = lax.dot_general(jnp.concatenate(qts, axis=1).astype(bf16),
                                jnp.concatenate(kts, axis=1).astype(bf16), _NT,
                                preferred_element_type=f32)
        return jnp.where(row64 >= col64, a_mat, 0.0)

    def chunks_factorised():
        n_chunks = TB // CHUNK
        heads = [(c, h) for c in range(n_chunks) for h in range(HGRN_HEADS)]

        def cols(h, base=0):
            return slice(base + h * HEAD_DIM, base + (h + 1) * HEAD_DIM)

        for c in range(n_chunks):
            rows = slice(c * CHUNK, (c + 1) * CHUNK)
            lf = lf_scr[rows, :]
            hi = lf.astype(bf16)
            lo = (lf - hi.astype(f32)).astype(bf16)
            gg = jnp.dot(tri, jnp.concatenate([hi, lo], axis=1), preferred_element_type=f32)
            lf_scr[rows, :] = gg[:, :HGRN_WIDTH] + gg[:, HGRN_WIDTH:]
        for c, h in heads:
            rows = slice(c * CHUNK, (c + 1) * CHUNK)
            q = p_scr[rows, cols(h)]
            k = kk_scr[rows, cols(h)]
            gh = lf_scr[rows, cols(h)]
            amat_scr[c * HGRN_HEADS + h] = intra_factorised(q, k, gh).astype(bf16)
            qg_scr[rows, cols(h)] = (q * jnp.exp(gh)).astype(bf16)
            kd_scr[rows, cols(h)] = (k * jnp.exp(gh[CHUNK - 1:CHUNK] - gh)).astype(bf16)
        for c, h in heads:
            rows = slice(c * CHUNK, (c + 1) * CHUNK)
            v = p_scr[rows, cols(h, 2 * HGRN_WIDTH)]
            o_scr[rows, cols(h)] = jnp.dot(amat_scr[c * HGRN_HEADS + h], v.astype(bf16),
                                           preferred_element_type=f32)
            upd_scr[c * HGRN_HEADS + h] = jnp.dot(v.T.astype(bf16), kd_scr[rows, cols(h)],
                                                  preferred_element_type=f32)
        states = [st_scr[h] for h in range(HGRN_HEADS)]
        for c, h in heads:
            rows = slice(c * CHUNK, (c + 1) * CHUNK)
            st = states[h]
            o_inter = lax.dot_general(qg_scr[rows, cols(h)], st.astype(bf16), _NT,
                                      preferred_element_type=f32)
            o_scr[rows, cols(h)] = (o_scr[rows, cols(h)] + o_inter) * scale
            gl = lf_scr[(c + 1) * CHUNK - 1:(c + 1) * CHUNK, cols(h)]
            states[h] = st * jnp.exp(gl) + upd_scr[c * HGRN_HEADS + h]
        for h in range(HGRN_HEADS):
            st_scr[h] = states[h]

    def intra_exact_diagonal(q, k, gh, cs):
        qts, kts = [], []
        for a in range(1, N_SUB):
            lo_r, hi_r = a * SUB, (a + 1) * SUB
            ra = g_scr[lo_r - 1:lo_r, cs]
            qts.append(pad_rows(q[lo_r:hi_r] * jnp.exp(gh[lo_r:hi_r] - ra), lo_r))
            kts.append(pad_rows(k[:lo_r] * jnp.exp(ra - gh[:lo_r]), 0))
        a_off = lax.dot_general(jnp.concatenate(qts, axis=1).astype(bf16),
                                jnp.concatenate(kts, axis=1).astype(bf16), _NT,
                                preferred_element_type=f32)
        diag_rows = []
        for a in range(N_SUB):
            lo_r = a * SUB
            gs = gh[lo_r:lo_r + SUB]
            qs = q[lo_r:lo_r + SUB]
            blk = jnp.zeros((SUB, CHUNK), f32)
            for jl in range(SUB):
                j = lo_r + jl
                gj = g_scr[j:j + 1, cs]
                kj = kc_scr[j:j + 1, cs]
                e = jnp.exp(jnp.minimum(gs - gj, 0.0))
                col = jnp.sum(qs * (kj * e), axis=-1, keepdims=True)
                blk = jnp.where(lane_sub == j, col, blk)
            diag_rows.append(blk)
        a_diag = jnp.concatenate(diag_rows, axis=0)
        return a_off + jnp.where(row64 >= col64, a_diag, 0.0)

    def make_chunk_body(intra):
        def chunk_body(c, carry):
            r0 = pl.multiple_of(c * CHUNK, CHUNK)
            rows = pl.ds(r0, CHUNK)
            lf = lf_scr[rows, :]
            hi = lf.astype(bf16)
            lo = (lf - hi.astype(f32)).astype(bf16)
            gg = jnp.dot(tri, jnp.concatenate([hi, lo], axis=1), preferred_element_type=f32)
            g_all = gg[:, :HGRN_WIDTH] + gg[:, HGRN_WIDTH:]
            g_scr[...] = g_all
            kc_scr[...] = kk_scr[rows, :]
            for h in range(HGRN_HEADS):
                cs = slice(h * HEAD_DIM, (h + 1) * HEAD_DIM)
                q = p_scr[rows, h * HEAD_DIM:(h + 1) * HEAD_DIM]
                v = p_scr[rows, 2 * HGRN_WIDTH + h * HEAD_DIM:2 * HGRN_WIDTH + (h + 1) * HEAD_DIM]
                k = kc_scr[:, cs]
                gh = g_all[:, cs]
                st = st_scr[h]
                o_inter = lax.dot_general((q * jnp.exp(gh)).astype(bf16), st.astype(bf16), _NT,
                                          preferred_element_type=f32)
                a_mat = intra(q, k, gh, cs)
                o = jnp.dot(a_mat.astype(bf16), v.astype(bf16), preferred_element_type=f32) + o_inter
                o_scr[rows, h * HEAD_DIM:(h + 1) * HEAD_DIM] = o * scale

                gl = g_scr[CHUNK - 1:CHUNK, cs]
                kd = k * jnp.exp(gl - gh)
                upd = jnp.dot(v.T.astype(bf16), kd.astype(bf16), preferred_element_type=f32)
                st_scr[h] = st * jnp.exp(gl) + upd
            return carry
        return chunk_body

    lf_all = lf_scr[...]
    sub_decay = -jnp.sum(lf_all.reshape(TB // SUB, SUB, HGRN_WIDTH), axis=1)
    bounded = jnp.max(sub_decay) <= MAX_SUB_DECAY

    f_groups = list(range(HGRN_WIDTH // PROJ_GROUP, 2 * HGRN_WIDTH // PROJ_GROUP))
    qi_groups = [g for g in range(3 * HGRN_WIDTH // PROJ_GROUP) if g not in f_groups]
    rest_groups = list(range(3 * HGRN_WIDTH // PROJ_GROUP, IN_COLS // PROJ_GROUP))

    def gmlp_stage(xnb):
        for g in f_groups:
            project(xnb, g)
        for w in range(TB // GMLP_BLOCK):
            rows = slice(w * GMLP_BLOCK, (w + 1) * GMLP_BLOCK)
            u = _gelu(p_scr[rows, 4 * HGRN_WIDTH:4 * HGRN_WIDTH + GMLP_WIDTH])
            vn = _layer_norm(_gelu(p_scr[rows, 4 * HGRN_WIDTH + GMLP_WIDTH:]),
                             lng_ref[...], lnb_ref[...])
            vnb = vn.astype(bf16)
            cols = []
            for g in range(GMLP_GROUPS):
                s = jnp.dot(wm_ref[g], vnb[:, g * HEAD_DIM:(g + 1) * HEAD_DIM],
                            preferred_element_type=f32) + bst_ref[:, g:g + 1]
                cols.append(s)
            mix_scr[rows, HGRN_WIDTH:] = (u * jnp.concatenate(cols, axis=1)).astype(bf16)

    def output_stage(xnb):
        for g in qi_groups:
            project(xnb, g)
        for w in range(TB // GMLP_BLOCK):
            rows = slice(w * GMLP_BLOCK, (w + 1) * GMLP_BLOCK)
            o = o_scr[rows, :]
            ms = jnp.mean(o * o, axis=-1, keepdims=True)
            gate = p_scr[rows, 3 * HGRN_WIDTH:4 * HGRN_WIDTH]
            y_rec = o * lax.rsqrt(ms + EPS) * hg_ref[...] * (gate * _sigmoid(gate))
            mix_scr[rows, :HGRN_WIDTH] = y_rec.astype(bf16)

        mix = jnp.dot(mix_scr[...], wout_ref[...], preferred_element_type=f32)
        hgrn_gates()
        h1 = _layer_norm(ALPHA * x_ref[...] + mix, l1g_ref[...], l1b_ref[...])
        h1f_ref[...] = h1
        h1p_ref[...] = _pack_rows(h1)
        for g in rest_groups[:2]:
            project(xnb, g)

        hh = h1.astype(bf16)
        hl = (h1 - hh.astype(f32)).astype(bf16)
        rwh = rwh_ref[...]
        logits = (lax.dot_general(rwh, hh, _NT, preferred_element_type=f32)
                  + lax.dot_general(rwh, hl, _NT, preferred_element_type=f32)
                  + lax.dot_general(rwl_ref[...], hh, _NT, preferred_element_type=f32)
                  + rb_ref[...])
        for g in rest_groups[2:]:
            project(xnb, g)

        e_iota = lax.broadcasted_iota(jnp.int32, (N_EXPERTS, TB), 0)
        work = logits
        vals, idxs, hots = [], [], []
        for _ in range(TOP_K):
            m = jnp.max(work, axis=0, keepdims=True)
            ik = jnp.min(jnp.where(work == m, e_iota, N_EXPERTS), axis=0, keepdims=True)
            hot = e_iota == ik
            work = jnp.where(hot, -jnp.inf, work)
            vals.append(m)
            idxs.append(ik)
            hots.append(hot)
        exps = [jnp.exp(vk - vals[0]) for vk in vals]
        denom = exps[0] + exps[1] + exps[2] + exps[3]
        gate_ref[...] = jnp.concatenate([ek / denom for ek in exps], axis=0)
        idx_ref[...] = jnp.concatenate(idxs, axis=0)

        hot_any = jnp.where(hots[0] | hots[1] | hots[2] | hots[3], 1.0, 0.0)
        prefix = jnp.dot(hot_any.astype(bf16), upp_ref[...], preferred_element_type=f32)
        base = prefix + carry_scr[:, 0:1]
        ranks = [jnp.sum(jnp.where(hk, base, 0.0), axis=0, keepdims=True) for hk in hots]
        rank_ref[...] = jnp.concatenate(ranks, axis=0).astype(jnp.int32)
        new_carry = carry_scr[...] + jnp.sum(hot_any, axis=1, keepdims=True)
        carry_scr[...] = new_carry
        cnt_ref[...] = new_carry

    @pl.when(bounded)
    def _():
        xnb = xn_ref[...].astype(bf16)
        gmlp_stage(xnb)
        chunks_factorised()
        output_stage(xnb)

    @pl.when(jnp.logical_not(bounded))
    def _():
        xnb = xn_ref[...].astype(bf16)
        lax.fori_loop(0, TB // CHUNK, make_chunk_body(intra_exact_diagonal), 0)
        gmlp_stage(xnb)
        output_stage(xnb)


def _mixer(xt, win, lb, hg, lng, lnb, wm, bst, wout, l1g, l1b, rwh, rwl, rb, tri, upp,
           seq, batch, batch0):
    nt = seq // TB
    t_total = batch * seq
    nblk = batch * nt
    const2 = lambda b, t: (0, 0)
    const3 = lambda b, t: (0, 0, 0)
    row_blk = lambda b, t: (b * nt + t, 0)
    meta_blk = lambda b, t: (0, b * nt + t)
    once = dict(pipeline_mode=pl.Buffered(1))
    in_specs = [
        pl.BlockSpec((TB, D_MODEL), lambda b, t: ((b + batch0) * nt + t, 0)),
        pl.BlockSpec((TB, D_MODEL),
                     lambda b, t: (batch0 * nt + jnp.minimum(b * nt + t + 1, nblk - 1), 0)),
        pl.BlockSpec((D_MODEL, IN_COLS), const2, **once),
        pl.BlockSpec((1, HGRN_WIDTH), const2),
        pl.BlockSpec((1, HGRN_WIDTH), const2),
        pl.BlockSpec((1, GMLP_WIDTH), const2),
        pl.BlockSpec((1, GMLP_WIDTH), const2),
        pl.BlockSpec((GMLP_GROUPS, GMLP_BLOCK, GMLP_BLOCK), const3),
        pl.BlockSpec((GMLP_BLOCK, GMLP_GROUPS), const2),
        pl.BlockSpec((D_MODEL, D_MODEL), const2, **once),
        pl.BlockSpec((1, D_MODEL), const2),
        pl.BlockSpec((1, D_MODEL), const2),
        pl.BlockSpec((N_EXPERTS, D_MODEL), const2),
        pl.BlockSpec((N_EXPERTS, D_MODEL), const2),
        pl.BlockSpec((N_EXPERTS, 1), const2),
        pl.BlockSpec((CHUNK, CHUNK), const2),
        pl.BlockSpec((TB, TB), const2, **once),
    ]
    out_shape = [
        jax.ShapeDtypeStruct((t_total, D_MODEL), jnp.float32),
        jax.ShapeDtypeStruct((t_total, HALF), jnp.int32),
        jax.ShapeDtypeStruct((TOP_K, t_total), jnp.int32),
        jax.ShapeDtypeStruct((TOP_K, t_total), jnp.float32),
        jax.ShapeDtypeStruct((TOP_K, t_total), jnp.int32),
        jax.ShapeDtypeStruct((N_EXPERTS, 128), jnp.float32),
    ]
    out_specs = [
        pl.BlockSpec((TB, D_MODEL), row_blk),
        pl.BlockSpec((TB, HALF), row_blk),
        pl.BlockSpec((TOP_K, TB), meta_blk),
        pl.BlockSpec((TOP_K, TB), meta_blk),
        pl.BlockSpec((TOP_K, TB), meta_blk),
        pl.BlockSpec((N_EXPERTS, 128), const2),
    ]
    scratch = [
        pltpu.VMEM((TB, IN_COLS), jnp.float32),
        pltpu.VMEM((TB, HGRN_WIDTH), jnp.float32),
        pltpu.VMEM((TB, HGRN_WIDTH), jnp.float32),
        pltpu.VMEM((CHUNK, HGRN_WIDTH), jnp.float32),
        pltpu.VMEM((CHUNK, HGRN_WIDTH), jnp.float32),
        pltpu.VMEM((TB, HGRN_WIDTH), jnp.float32),
        pltpu.VMEM((HGRN_HEADS, HEAD_DIM, HEAD_DIM), jnp.float32),
        pltpu.VMEM((N_EXPERTS, 128), jnp.float32),
        pltpu.VMEM((TB // CHUNK * HGRN_HEADS, CHUNK, CHUNK), jnp.bfloat16),
        pltpu.VMEM((TB, HGRN_WIDTH), jnp.bfloat16),
        pltpu.VMEM((TB, HGRN_WIDTH), jnp.bfloat16),
        pltpu.VMEM((TB // CHUNK * HGRN_HEADS, HEAD_DIM, HEAD_DIM), jnp.float32),
        pltpu.VMEM((TB, D_MODEL), jnp.bfloat16),
    ]
    return pl.pallas_call(
        _mixer_kernel,
        grid=(batch, nt),
        in_specs=in_specs,
        out_specs=out_specs,
        out_shape=out_shape,
        scratch_shapes=scratch,
        compiler_params=pltpu.CompilerParams(
            dimension_semantics=("arbitrary", "arbitrary"),
            vmem_limit_bytes=VMEM_LIMIT),
        name="mixer",
    )(xt, xt, win, lb, hg, lng, lnb, wm, bst, wout, l1g, l1b, rwh, rwl, rb, tri, upp)


def _plan_kernel(cnt_ref, idx_ref, rank_ref, dest_ref, tiles_ref):
    f32 = jnp.float32
    n_e = N_EXPERTS
    e_sub = lax.broadcasted_iota(jnp.int32, (n_e, n_e), 0)
    e_lane = lax.broadcasted_iota(jnp.int32, (n_e, n_e), 1)
    counts = cnt_ref[:, 0:1]
    padded = jnp.floor((counts + (TM - 1)) * (1.0 / TM)) * TM
    as_row = lambda col: jnp.sum(jnp.where(e_sub == e_lane, col, 0.0), axis=0, keepdims=True)
    padded_row = as_row(padded)
    counts_row = as_row(counts)
    ends = jnp.sum(jnp.where(e_lane <= e_sub, padded_row, 0.0), axis=1, keepdims=True)
    starts = ends - padded
    owns_later = (e_lane > e_sub) & (counts_row > 0.0)
    nxt = jnp.min(jnp.where(owns_later, e_lane, n_e), axis=1, keepdims=True)
    own = lax.broadcasted_iota(jnp.int32, (n_e, 1), 0)
    nxt = jnp.where(nxt == n_e, own, nxt).astype(f32)

    n_lanes = tiles_ref.shape[1]
    tile_row = lax.broadcasted_iota(jnp.int32, (n_e, n_lanes), 1).astype(f32) * TM
    e_of = lax.broadcasted_iota(jnp.int32, (n_e, n_lanes), 0)
    tile_e = jnp.minimum(jnp.sum(jnp.where(tile_row >= ends, 1, 0), axis=0, keepdims=True), n_e - 1)
    mine = e_of == tile_e
    pick = lambda col: jnp.sum(jnp.where(mine, col, 0.0), axis=0, keepdims=True)
    valid = jnp.clip(pick(starts + counts) - tile_row[0:1], 0.0, float(TM))
    rows = [tile_e, valid.astype(jnp.int32), pick(nxt).astype(jnp.int32)]
    rows.append(jnp.zeros((tiles_ref.shape[0] - len(rows), n_lanes), jnp.int32))
    tiles_ref[...] = jnp.concatenate(rows, axis=0)

    chunk = PLAN_CHUNK
    e_chunk = lax.broadcasted_iota(jnp.int32, (n_e, chunk), 0)

    def body(c, carry):
        lanes = pl.ds(pl.multiple_of(c * chunk, chunk), chunk)
        for k in range(TOP_K):
            hit = e_chunk == idx_ref[k:k + 1, lanes]
            start_of = jnp.sum(jnp.where(hit, starts, 0.0), axis=0, keepdims=True)
            dest_ref[k:k + 1, lanes] = rank_ref[k:k + 1, lanes] + start_of.astype(jnp.int32)
        return carry

    lax.fori_loop(0, idx_ref.shape[1] // chunk, body, 0)


def _plan(cnt, idx, rank, n_tiles):
    t_part = idx.shape[1]
    n_lanes = -(-n_tiles // 128) * 128
    full = lambda shape: pl.BlockSpec(shape, lambda i: (0,) * len(shape))
    dest, tiles = pl.pallas_call(
        _plan_kernel,
        grid=(1,),
        in_specs=[full(cnt.shape), full(idx.shape), full(rank.shape)],
        out_specs=[full(idx.shape), full((8, n_lanes))],
        out_shape=[jax.ShapeDtypeStruct((TOP_K, t_part), jnp.int32),
                   jax.ShapeDtypeStruct((8, n_lanes), jnp.int32)],
        compiler_params=pltpu.CompilerParams(
            dimension_semantics=("arbitrary",), vmem_limit_bytes=VMEM_LIMIT),
        name="plan",
    )(cnt, idx, rank)
    return dest, tiles[0, :n_tiles], tiles[1, :n_tiles], tiles[2, :n_tiles]


def _sc_workers():
    info = plsc.get_sparse_core_info()
    return info.num_cores, info.num_cores * info.num_subcores


def _sc_dispatch(rows, dest, n_out):
    t_total, dw = rows.shape
    nc, nw = _sc_workers()
    per_w = t_total // nw
    mesh = plsc.VectorSubcoreMesh(core_axis_name="c", subcore_axis_name="s")

    @functools.partial(
        pl.kernel,
        out_type=jax.ShapeDtypeStruct((n_out, dw), rows.dtype),
        mesh=mesh,
        scratch_types=[pltpu.VMEM((SC_WINDOW,), jnp.int32) for _ in range(TOP_K)]
        + [pltpu.VMEM((SC_WINDOW, dw), rows.dtype), pltpu.SemaphoreType.DMA],
        name="sc_dispatch",
    )
    def k(x_hbm, i_hbm, o_hbm, i0, i1, i2, i3, rows_v, sem):
        wid = lax.axis_index("s") * nc + lax.axis_index("c")
        idx_bufs = (i0, i1, i2, i3)

        @pl.loop(0, per_w // SC_WINDOW)
        def _(j):
            base = wid * per_w + j * SC_WINDOW
            for kk in range(TOP_K):
                pltpu.sync_copy(i_hbm.at[kk, pl.ds(base, SC_WINDOW)], idx_bufs[kk])
            pltpu.sync_copy(x_hbm.at[pl.ds(base, SC_WINDOW)], rows_v)
            copies = [pltpu.async_copy(rows_v, o_hbm.at[idx_bufs[kk]], sem) for kk in range(TOP_K)]
            for cp in copies:
                cp.wait()

    return k(rows, dest)


def _sc_gather(table, idx):
    n_k, t_total = idx.shape
    n = n_k * t_total
    dw = table.shape[1]
    nc, nw = _sc_workers()
    per_w = n // nw
    w_per_k = nw // n_k
    mesh = plsc.VectorSubcoreMesh(core_axis_name="c", subcore_axis_name="s")

    @functools.partial(
        pl.kernel,
        out_type=jax.ShapeDtypeStruct((n, dw), table.dtype),
        mesh=mesh,
        scratch_types=[pltpu.VMEM((SC_WINDOW,), jnp.int32),
                       pltpu.VMEM((SC_WINDOW, dw), table.dtype),
                       pltpu.SemaphoreType.DMA],
        name="sc_gather",
    )
    def k(t_hbm, i_hbm, o_hbm, idx_v, rows_v, sem):
        wid = lax.axis_index("s") * nc + lax.axis_index("c")

        row = wid // w_per_k
        col0 = (wid % w_per_k) * per_w

        @pl.loop(0, per_w // SC_WINDOW)
        def _(j):
            col = col0 + j * SC_WINDOW
            pltpu.sync_copy(i_hbm.at[row, pl.ds(col, SC_WINDOW)], idx_v)
            pltpu.async_copy(t_hbm.at[idx_v], rows_v, sem).wait()
            pltpu.sync_copy(rows_v, o_hbm.at[pl.ds(row * t_total + col, SC_WINDOW)])

    return k(table, idx)


def _expert_kernel(te_ref, tv_ref, tn_ref, xs_ref, w1_hbm, w2_hbm, b1g_ref, b1l_ref, b2_ref,
                   perm_ref, y_ref, w1f_scr, w2f_scr, w1g_scr, w1l_scr, w2_scr, sems):
    i = pl.program_id(0)
    valid = tv_ref[i]
    expert = te_ref[i]
    f32 = jnp.float32
    bf16 = jnp.bfloat16
    expert_changed = (i == 0) | (expert != te_ref[jnp.maximum(i - 1, 0)])

    def weight_copies(e):
        return (pltpu.make_async_copy(w1_hbm.at[e], w1f_scr, sems.at[0]),
                pltpu.make_async_copy(w2_hbm.at[e], w2f_scr, sems.at[1]))

    @pl.when(valid == 0)
    def _():
        y_ref[...] = jnp.zeros_like(y_ref)

    @pl.when(i == 0)
    def _():
        for cp in weight_copies(expert):
            cp.start()

    @pl.when((valid > 0) & expert_changed)
    def _():
        for cp in weight_copies(expert):
            cp.wait()
        w2_scr[...] = w2f_scr[...].astype(bf16)
        perm = perm_ref[...]
        for c in range(2 * D_MODEL // PERM_BLOCK):
            blk = w1f_scr[:, c * PERM_BLOCK:(c + 1) * PERM_BLOCK].astype(bf16)
            r = jnp.dot(blk, perm, preferred_element_type=f32).astype(bf16)
            half = PERM_BLOCK // 2
            w1g_scr[:, c * half:(c + 1) * half] = r[:, :half]
            w1l_scr[:, c * half:(c + 1) * half] = r[:, half:]

        @pl.when(tn_ref[i] != expert)
        def _():
            for cp in weight_copies(tn_ref[i]):
                cp.start()

    def expert_rows(n_rows):
        blocks = [slice(r0, min(r0 + MLP_ROWS, n_rows)) for r0 in range(0, n_rows, MLP_ROWS)]

        def up(rows):
            a, b = _unpack_rows(xs_ref[rows, :])
            keep = lax.broadcasted_iota(jnp.int32, (rows.stop - rows.start, 1), 0) < valid - rows.start
            x = jnp.where(keep, jnp.concatenate([a, b], axis=1), 0.0).astype(bf16)
            return (jnp.dot(x, w1g_scr[...], preferred_element_type=f32) + b1g_ref[0],
                    jnp.dot(x, w1l_scr[...], preferred_element_type=f32) + b1l_ref[0])

        def activation(hg, hl):
            xg = jnp.minimum(hg, SWIGLU_LIMIT)
            xl = jnp.clip(hl, -SWIGLU_LIMIT, SWIGLU_LIMIT)
            return (xg * _sigmoid(SWIGLU_ALPHA * xg) * (xl + 1.0)).astype(bf16)

        def down(rows, act):
            y = jnp.dot(act, w2_scr[...], preferred_element_type=f32) + b2_ref[0]
            y_ref[rows, :] = _pack_rows(y)

        hidden = up(blocks[0])
        for i in range(1, len(blocks)):
            nxt_hidden = up(blocks[i])
            down(blocks[i - 1], activation(*hidden))
            hidden = nxt_hidden
        down(blocks[-1], activation(*hidden))
        if n_rows < TM:
            y_ref[n_rows:, :] = jnp.zeros((TM - n_rows, HALF), jnp.int32)

    for blocks in range(TM // ROW_STEP, 0, -1):
        @pl.when((valid > (blocks - 1) * ROW_STEP) & (valid <= blocks * ROW_STEP))
        def _(blocks=blocks):
            expert_rows(blocks * ROW_STEP)


def _experts(xs, tile_expert, tile_valid, tile_next, w1, w2, b1g, b1l, b2, perm):
    n_slots = xs.shape[0]
    n_tiles = n_slots // TM
    d_ff = w2.shape[1]
    wmap = lambda i, te, tv, tn: (te[i], 0, 0)
    grid_spec = pltpu.PrefetchScalarGridSpec(
        num_scalar_prefetch=3,
        grid=(n_tiles,),
        in_specs=[
            pl.BlockSpec((TM, HALF), lambda i, te, tv, tn: (i, 0)),
            pl.BlockSpec(memory_space=pl.ANY),
            pl.BlockSpec(memory_space=pl.ANY),
            pl.BlockSpec((1, 1, d_ff), wmap),
            pl.BlockSpec((1, 1, d_ff), wmap),
            pl.BlockSpec((1, 1, D_MODEL), wmap),
            pl.BlockSpec((PERM_BLOCK, PERM_BLOCK), lambda i, te, tv, tn: (0, 0)),
        ],
        out_specs=pl.BlockSpec((TM, HALF), lambda i, te, tv, tn: (i, 0)),
        scratch_shapes=[
            pltpu.VMEM((D_MODEL, 2 * d_ff), jnp.float32),
            pltpu.VMEM((d_ff, D_MODEL), jnp.float32),
            pltpu.VMEM((D_MODEL, d_ff), jnp.bfloat16),
            pltpu.VMEM((D_MODEL, d_ff), jnp.bfloat16),
            pltpu.VMEM((d_ff, D_MODEL), jnp.bfloat16),
            pltpu.SemaphoreType.DMA((2,)),
        ],
    )
    return pl.pallas_call(
        _expert_kernel,
        grid_spec=grid_spec,
        out_shape=jax.ShapeDtypeStruct((n_slots, HALF), jnp.int32),
        compiler_params=pltpu.CompilerParams(
            dimension_semantics=("arbitrary",),
            vmem_limit_bytes=VMEM_LIMIT),
        name="experts",
    )(tile_expert, tile_valid, tile_next, xs, w1, w2, b1g, b1l, b2, perm)


def _combine_kernel(h_ref, yk_ref, gate_ref, g_ref, b_ref, *rest):
    o_ref = rest[-1]
    gates = gate_ref[...].T
    acc_a = None
    acc_b = None
    for k in range(TOP_K):
        a, b = _unpack_rows(yk_ref[k])
        gk = gates[:, k:k + 1]
        acc_a = gk * a if acc_a is None else acc_a + gk * a
        acc_b = gk * b if acc_b is None else acc_b + gk * b
    ffn = jnp.concatenate([acc_a, acc_b], axis=1)
    o_ref[...] = _layer_norm(ALPHA * h_ref[...] + ffn, g_ref[...], b_ref[...])


def _combine(h1f, yk, gates, g2, b2, out_prev, row0, t_all):
    t_part = h1f.shape[0]
    blk0 = row0 // TC3
    row = lambda i: (i, 0)
    const = lambda i: (0, 0)
    in_specs = [
        pl.BlockSpec((TC3, D_MODEL), row),
        pl.BlockSpec((TOP_K, TC3, HALF), lambda i: (0, i, 0)),
        pl.BlockSpec((TOP_K, TC3), lambda i: (0, i)),
        pl.BlockSpec((1, D_MODEL), const),
        pl.BlockSpec((1, D_MODEL), const),
    ]
    args = [h1f, yk, gates, g2, b2]
    aliases = {}
    if out_prev is not None:
        in_specs.append(pl.BlockSpec(memory_space=pl.ANY))
        args.append(out_prev)
        aliases = {len(args) - 1: 0}
    return pl.pallas_call(
        _combine_kernel,
        grid=(t_part // TC3,),
        in_specs=in_specs,
        out_specs=pl.BlockSpec((TC3, D_MODEL), lambda i: (i + blk0, 0)),
        out_shape=jax.ShapeDtypeStruct((t_all, D_MODEL), jnp.float32),
        input_output_aliases=aliases,
        compiler_params=pltpu.CompilerParams(
            dimension_semantics=("arbitrary",),
            vmem_limit_bytes=VMEM_LIMIT),
        name="combine",
    )(*args)


def kernel(x, w_in, lb_logits, hgrn_norm_g, gmlp_ln_g, gmlp_ln_b, gmlp_ws, gmlp_bs, w_out, ln1_g, ln1_b, router_w, router_b, exp_w1, exp_b1, exp_w2, exp_b2, ln2_g, ln2_b):
    batch, seq, d = x.shape
    assert d == D_MODEL and seq % TB == 0 and w_in.shape[0] == 1
    t_total = batch * seq
    f32 = jnp.float32
    bf16 = jnp.bfloat16

    lb = jnp.cumsum(jax.nn.softmax(lb_logits.astype(f32), axis=0), axis=0)[0:1]
    chunk_id = jnp.arange(GMLP_BLOCK) // CHUNK
    wm = jnp.where((chunk_id[None, :] <= chunk_id[:, None])[None], gmlp_ws[0], 0.0).astype(bf16)
    rwt = router_w[0].T
    rwh = rwt.astype(bf16)
    rwl = (rwt - rwh.astype(f32)).astype(bf16)
    tri = (jnp.arange(CHUNK)[None, :] <= jnp.arange(CHUNK)[:, None]).astype(bf16)
    upp = (jnp.arange(TB)[:, None] < jnp.arange(TB)[None, :]).astype(bf16)

    lane = jnp.arange(PERM_BLOCK)
    src = jnp.where(lane < PERM_BLOCK // 2, 2 * lane, 2 * (lane - PERM_BLOCK // 2) + 1)
    perm = (jnp.arange(PERM_BLOCK)[:, None] == src[None, :]).astype(bf16)
    xt = x.reshape(t_total, d)
    win = w_in[0].astype(bf16)
    wout = w_out[0].astype(bf16)
    b1g, b1l, b2e = exp_b1[0][:, None, 0::2], exp_b1[0][:, None, 1::2], exp_b2[0][:, None, :]

    last = max(batch * LAST_PART_NUM // LAST_PART_DEN, 1) if batch > 1 else 0
    part_sizes = [pb for pb in (batch - last, last) if pb > 0]
    out = None
    b0 = 0
    for pb in part_sizes:
        t_part = pb * seq
        n_slots = t_part * TOP_K + N_EXPERTS * TM
        n_tiles = n_slots // TM
        h1f, h1p, idx, gates, rank, cnt = _mixer(
            xt, win, lb, hgrn_norm_g[0:1], gmlp_ln_g[0:1], gmlp_ln_b[0:1], wm, gmlp_bs[0].T, wout,
            ln1_g[0:1], ln1_b[0:1], rwh, rwl, router_b[0][:, None], tri, upp, seq, pb, b0)

        dest, tile_expert, tile_valid, tile_next = _plan(cnt, idx, rank, n_tiles)
        xs = _sc_dispatch(h1p, dest, n_slots)
        y = _experts(xs, tile_expert, tile_valid, tile_next,
                     exp_w1[0], exp_w2[0], b1g, b1l, b2e, perm)
        yk = _sc_gather(y, dest).reshape(TOP_K, t_part, HALF)
        out = _combine(h1f, yk, gates, ln2_g[0:1], ln2_b[0:1], out, b0 * seq, t_total)
        b0 += pb
    return out.reshape(batch, seq, d)
```

```python
import functools
import math

import jax
import jax.numpy as jnp
from jax import lax
from jax.experimental import pallas as pl
from jax.experimental.pallas import tpu as pltpu
from jax.experimental.pallas import tpu_sc as plsc

D_MODEL = 1024
CHUNK = 64
SUB = 16
N_SUB = CHUNK // SUB
MAX_SUB_DECAY = 86.0
HGRN_WIDTH = 512
HGRN_HEADS = 4
HEAD_DIM = 128
GMLP_WIDTH = 512
GMLP_BLOCK = 128
GMLP_GROUPS = 4
IN_COLS = 3072
N_EXPERTS = 32
TOP_K = 4
SWIGLU_LIMIT = 7.0
SWIGLU_ALPHA = 1.702
ALPHA = 2.0 ** 0.25
EPS = 1e-5
HALF = D_MODEL // 2

TB = 512
TM = 1024
ROW_STEP = 256
PLAN_CHUNK = 2048
TC3 = 512
LAST_PART_NUM, LAST_PART_DEN = 1, 4
SC_WINDOW = 128
PROJ_GROUP = 256
PERM_BLOCK = 256
VMEM_LIMIT = 56 * 1024 * 1024

_NT = (((1,), (1,)), ((), ()))


def _sigmoid(x):
    z = jnp.exp(-jnp.abs(x))
    r = 1.0 / (1.0 + z)
    return jnp.where(x >= 0, r, z * r)


def _gelu(x):
    return 0.5 * x * (1.0 + lax.erf(x * (1.0 / math.sqrt(2.0))))


def _layer_norm(x, g, b):
    mu = jnp.mean(x, axis=-1, keepdims=True)
    xc = x - mu
    var = jnp.mean(xc * xc, axis=-1, keepdims=True)
    return xc * lax.rsqrt(var + EPS) * g + b


def _pack_rows(h):
    a = h[:, :HALF].astype(jnp.bfloat16).astype(jnp.float32)
    b = h[:, HALF:].astype(jnp.bfloat16).astype(jnp.float32)
    au = lax.bitcast_convert_type(a, jnp.uint32) >> 16
    bu = lax.bitcast_convert_type(b, jnp.uint32) & jnp.uint32(0xFFFF0000)
    return lax.bitcast_convert_type(au | bu, jnp.int32)


def _unpack_rows(w):
    u = lax.bitcast_convert_type(w, jnp.uint32)
    a = lax.bitcast_convert_type(u << 16, jnp.float32)
    b = lax.bitcast_convert_type(u & jnp.uint32(0xFFFF0000), jnp.float32)
    return a, b


def _mixer_kernel(x_ref, xn_ref, win_ref, lb_ref, hg_ref, lng_ref, lnb_ref, wm_ref, bst_ref,
                  wout_ref, l1g_ref, l1b_ref, rwh_ref, rwl_ref, rb_ref, tri_ref, upp_ref,
                  h1f_ref, h1p_ref, idx_ref, gate_ref, rank_ref, cnt_ref,
                  p_scr, lf_scr, kk_scr, g_scr, kc_scr, o_scr, st_scr, carry_scr,
                  amat_scr, qg_scr, kd_scr, upd_scr, mix_scr):
    b = pl.program_id(0)
    t = pl.program_id(1)
    step = b * pl.num_programs(1) + t
    f32 = jnp.float32
    bf16 = jnp.bfloat16

    @pl.when(t == 0)
    def _():
        st_scr[...] = jnp.zeros_like(st_scr)

    @pl.when((b == 0) & (t == 0))
    def _():
        carry_scr[...] = jnp.zeros_like(carry_scr)

    def project(xb, group):
        cols = slice(group * PROJ_GROUP, (group + 1) * PROJ_GROUP)
        p_scr[:, cols] = jnp.dot(xb, win_ref[:, cols], preferred_element_type=f32)

    def hgrn_gates():
        fl = p_scr[:, HGRN_WIDTH:2 * HGRN_WIDTH]
        lb = lb_ref[...]
        z = jnp.exp(-jnp.abs(fl))
        r = 1.0 / (1.0 + z)
        zr = z * r
        pos = fl >= 0
        lf_scr[...] = jnp.log2(lb + (1.0 - lb) * jnp.where(pos, r, zr))
        kk_scr[...] = (1.0 - lb) * jnp.where(pos, zr, r)

    @pl.when(step == 0)
    def _():
        xb0 = x_ref[...].astype(bf16)
        for group in range(IN_COLS // PROJ_GROUP):
            project(xb0, group)
        hgrn_gates()

    tri = tri_ref[...]
    row64 = lax.broadcasted_iota(jnp.int32, (CHUNK, CHUNK), 0)
    col64 = lax.broadcasted_iota(jnp.int32, (CHUNK, CHUNK), 1)
    lane_sub = lax.broadcasted_iota(jnp.int32, (SUB, CHUNK), 1)
    scale = HEAD_DIM ** -0.5

    def pad_rows(piece, lo_r):
        parts = []
        if lo_r > 0:
            parts.append(jnp.zeros((lo_r, HEAD_DIM), f32))
        parts.append(piece)
        rest = CHUNK - lo_r - piece.shape[0]
        if rest > 0:
            parts.append(jnp.zeros((rest, HEAD_DIM), f32))
        return jnp.concatenate(parts, axis=0) if len(parts) > 1 else piece

    def intra_factorised(q, k, gh):
        qts, kts = [], []
        for a in range(N_SUB):
            lo_r, hi_r = a * SUB, (a + 1) * SUB
            if a == 0:
                qa = q[:hi_r] * jnp.exp2(gh[:hi_r])
                ka = k[:hi_r] * jnp.exp2(-gh[:hi_r])
            else:
                ra = gh[lo_r - 1:lo_r]
                qa = q[lo_r:hi_r] * jnp.exp2(gh[lo_r:hi_r] - ra)
                ka = k[:hi_r] * jnp.exp2(ra - gh[:hi_r])
            qts.append(pad_rows(qa, lo_r))
            kts.append(pad_rows(ka, 0))
        a_mat = lax.dot_general(jnp.concatenate(qts, axis=1).astype(bf16),
                                jnp.concatenate(kts, axis=1).astype(bf16), _NT,
                                preferred_element_type=f32)
        return jnp.where(row64 >= col64, a_mat, 0.0)

    def chunks_factorised():
        n_chunks = TB // CHUNK
        heads = [(c, h) for c in range(n_chunks) for h in range(HGRN_HEADS)]

        def cols(h, base=0):
            return slice(base + h * HEAD_DIM, base + (h + 1) * HEAD_DIM)

        for c in range(n_chunks):
            rows = slice(c * CHUNK, (c + 1) * CHUNK)
            lf = lf_scr[rows, :]
            hi = lf.astype(bf16)
            lo = (lf - hi.astype(f32)).astype(bf16)
            gg = jnp.dot(tri, jnp.concatenate([hi, lo], axis=1), preferred_element_type=f32)
            lf_scr[rows, :] = gg[:, :HGRN_WIDTH] + gg[:, HGRN_WIDTH:]
        for c, h in heads:
            rows = slice(c * CHUNK, (c + 1) * CHUNK)
            q = p_scr[rows, cols(h)]
            k = kk_scr[rows, cols(h)]
            gh = lf_scr[rows, cols(h)]
            amat_scr[c * HGRN_HEADS + h] = intra_factorised(q, k, gh).astype(bf16)
            qg_scr[rows, cols(h)] = (q * jnp.exp2(gh)).astype(bf16)
            kd_scr[rows, cols(h)] = (k * jnp.exp2(gh[CHUNK - 1:CHUNK] - gh)).astype(bf16)
        for c, h in heads:
            rows = slice(c * CHUNK, (c + 1) * CHUNK)
            v = p_scr[rows, cols(h, 2 * HGRN_WIDTH)]
            o_scr[rows, cols(h)] = jnp.dot(amat_scr[c * HGRN_HEADS + h], v.astype(bf16),
                                           preferred_element_type=f32)
            upd_scr[c * HGRN_HEADS + h] = jnp.dot(v.T.astype(bf16), kd_scr[rows, cols(h)],
                                                  preferred_element_type=f32)
        states = [st_scr[h] for h in range(HGRN_HEADS)]
        for c, h in heads:
            rows = slice(c * CHUNK, (c + 1) * CHUNK)
            st = states[h]
            o_inter = lax.dot_general(qg_scr[rows, cols(h)], st.astype(bf16), _NT,
                                      preferred_element_type=f32)
            o_scr[rows, cols(h)] = (o_scr[rows, cols(h)] + o_inter) * scale
            gl = lf_scr[(c + 1) * CHUNK - 1:(c + 1) * CHUNK, cols(h)]
            states[h] = st * jnp.exp2(gl) + upd_scr[c * HGRN_HEADS + h]
        for h in range(HGRN_HEADS):
            st_scr[h] = states[h]

    def intra_exact_diagonal(q, k, gh, cs):
        qts, kts = [], []
        for a in range(1, N_SUB):
            lo_r, hi_r = a * SUB, (a + 1) * SUB
            ra = g_scr[lo_r - 1:lo_r, cs]
            qts.append(pad_rows(q[lo_r:hi_r] * jnp.exp2(gh[lo_r:hi_r] - ra), lo_r))
            kts.append(pad_rows(k[:lo_r] * jnp.exp2(ra - gh[:lo_r]), 0))
        a_off = lax.dot_general(jnp.concatenate(qts, axis=1).astype(bf16),
                                jnp.concatenate(kts, axis=1).astype(bf16), _NT,
                                preferred_element_type=f32)
        diag_rows = []
        for a in range(N_SUB):
            lo_r = a * SUB
            gs = gh[lo_r:lo_r + SUB]
            qs = q[lo_r:lo_r + SUB]
            blk = jnp.zeros((SUB, CHUNK), f32)
            for jl in range(SUB):
                j = lo_r + jl
                gj = g_scr[j:j + 1, cs]
                kj = kc_scr[j:j + 1, cs]
                e = jnp.exp2(jnp.minimum(gs - gj, 0.0))
                col = jnp.sum(qs * (kj * e), axis=-1, keepdims=True)
                blk = jnp.where(lane_sub == j, col, blk)
            diag_rows.append(blk)
        a_diag = jnp.concatenate(diag_rows, axis=0)
        return a_off + jnp.where(row64 >= col64, a_diag, 0.0)

    def make_chunk_body(intra):
        def chunk_body(c, carry):
            r0 = pl.multiple_of(c * CHUNK, CHUNK)
            rows = pl.ds(r0, CHUNK)
            lf = lf_scr[rows, :]
            hi = lf.astype(bf16)
            lo = (lf - hi.astype(f32)).astype(bf16)
            gg = jnp.dot(tri, jnp.concatenate([hi, lo], axis=1), preferred_element_type=f32)
            g_all = gg[:, :HGRN_WIDTH] + gg[:, HGRN_WIDTH:]
            g_scr[...] = g_all
            kc_scr[...] = kk_scr[rows, :]
            for h in range(HGRN_HEADS):
                cs = slice(h * HEAD_DIM, (h + 1) * HEAD_DIM)
                q = p_scr[rows, h * HEAD_DIM:(h + 1) * HEAD_DIM]
                v = p_scr[rows, 2 * HGRN_WIDTH + h * HEAD_DIM:2 * HGRN_WIDTH + (h + 1) * HEAD_DIM]
                k = kc_scr[:, cs]
                gh = g_all[:, cs]
                st = st_scr[h]
                o_inter = lax.dot_general((q * jnp.exp2(gh)).astype(bf16), st.astype(bf16), _NT,
                                          preferred_element_type=f32)
                a_mat = intra(q, k, gh, cs)
                o = jnp.dot(a_mat.astype(bf16), v.astype(bf16), preferred_element_type=f32) + o_inter
                o_scr[rows, h * HEAD_DIM:(h + 1) * HEAD_DIM] = o * scale

                gl = g_scr[CHUNK - 1:CHUNK, cs]
                kd = k * jnp.exp2(gl - gh)
                upd = jnp.dot(v.T.astype(bf16), kd.astype(bf16), preferred_element_type=f32)
                st_scr[h] = st * jnp.exp2(gl) + upd
            return carry
        return chunk_body

    lf_all = lf_scr[...]
    sub_decay = -jnp.sum(lf_all.reshape(TB // SUB, SUB, HGRN_WIDTH), axis=1)
    bounded = jnp.max(sub_decay) <= MAX_SUB_DECAY

    f_groups = list(range(HGRN_WIDTH // PROJ_GROUP, 2 * HGRN_WIDTH // PROJ_GROUP))
    qi_groups = [g for g in range(3 * HGRN_WIDTH // PROJ_GROUP) if g not in f_groups]
    rest_groups = list(range(3 * HGRN_WIDTH // PROJ_GROUP, IN_COLS // PROJ_GROUP))

    def gmlp_stage(xnb):
        for g in f_groups:
            project(xnb, g)
        for w in range(TB // GMLP_BLOCK):
            rows = slice(w * GMLP_BLOCK, (w + 1) * GMLP_BLOCK)
            u = _gelu(p_scr[rows, 4 * HGRN_WIDTH:4 * HGRN_WIDTH + GMLP_WIDTH])
            vn = _layer_norm(_gelu(p_scr[rows, 4 * HGRN_WIDTH + GMLP_WIDTH:]),
                             lng_ref[...], lnb_ref[...])
            vnb = vn.astype(bf16)
            cols = []
            for g in range(GMLP_GROUPS):
                s = jnp.dot(wm_ref[g], vnb[:, g * HEAD_DIM:(g + 1) * HEAD_DIM],
                            preferred_element_type=f32) + bst_ref[:, g:g + 1]
                cols.append(s)
            mix_scr[rows, HGRN_WIDTH:] = (u * jnp.concatenate(cols, axis=1)).astype(bf16)

    def output_stage(xnb):
        for g in qi_groups:
            project(xnb, g)
        for w in range(TB // GMLP_BLOCK):
            rows = slice(w * GMLP_BLOCK, (w + 1) * GMLP_BLOCK)
            o = o_scr[rows, :]
            ms = jnp.mean(o * o, axis=-1, keepdims=True)
            gate = p_scr[rows, 3 * HGRN_WIDTH:4 * HGRN_WIDTH]
            y_rec = o * lax.rsqrt(ms + EPS) * hg_ref[...] * (gate * _sigmoid(gate))
            mix_scr[rows, :HGRN_WIDTH] = y_rec.astype(bf16)

        mix = jnp.dot(mix_scr[...], wout_ref[...], preferred_element_type=f32)
        hgrn_gates()
        h1 = _layer_norm(ALPHA * x_ref[...] + mix, l1g_ref[...], l1b_ref[...])
        h1f_ref[...] = h1
        h1p_ref[...] = _pack_rows(h1)
        for g in rest_groups[:2]:
            project(xnb, g)

        hh = h1.astype(bf16)
        hl = (h1 - hh.astype(f32)).astype(bf16)
        rwh = rwh_ref[...]
        by_hh = lax.dot_general(jnp.concatenate([rwh, rwl_ref[...]], axis=0), hh, _NT,
                                preferred_element_type=f32)
        logits = (by_hh[:N_EXPERTS] + by_hh[N_EXPERTS:]
                  + lax.dot_general(rwh, hl, _NT, preferred_element_type=f32) + rb_ref[...])
        for g in rest_groups[2:]:
            project(xnb, g)

        e_iota = lax.broadcasted_iota(jnp.int32, (N_EXPERTS, TB), 0)
        work = logits
        vals, idxs, hots = [], [], []
        for _ in range(TOP_K):
            m = jnp.max(work, axis=0, keepdims=True)
            ik = jnp.min(jnp.where(work == m, e_iota, N_EXPERTS), axis=0, keepdims=True)
            hot = e_iota == ik
            work = jnp.where(hot, -jnp.inf, work)
            vals.append(m)
            idxs.append(ik)
            hots.append(hot)
        exps = [jnp.exp(vk - vals[0]) for vk in vals]
        denom = exps[0] + exps[1] + exps[2] + exps[3]
        gate_ref[...] = jnp.concatenate([ek / denom for ek in exps], axis=0)
        idx_ref[...] = jnp.concatenate(idxs, axis=0)

        hot_any = jnp.where(hots[0] | hots[1] | hots[2] | hots[3], 1.0, 0.0)
        prefix = jnp.dot(hot_any.astype(bf16), upp_ref[...], preferred_element_type=f32)
        base = prefix + carry_scr[:, 0:1]
        ranks = [jnp.sum(jnp.where(hk, base, 0.0), axis=0, keepdims=True) for hk in hots]
        rank_ref[...] = jnp.concatenate(ranks, axis=0).astype(jnp.int32)
        new_carry = carry_scr[...] + jnp.sum(hot_any, axis=1, keepdims=True)
        carry_scr[...] = new_carry
        cnt_ref[...] = new_carry

    @pl.when(bounded)
    def _():
        xnb = xn_ref[...].astype(bf16)
        gmlp_stage(xnb)
        chunks_factorised()
        output_stage(xnb)

    @pl.when(jnp.logical_not(bounded))
    def _():
        xnb = xn_ref[...].astype(bf16)
        lax.fori_loop(0, TB // CHUNK, make_chunk_body(intra_exact_diagonal), 0)
        gmlp_stage(xnb)
        output_stage(xnb)


def _mixer(xt, win, lb, hg, lng, lnb, wm, bst, wout, l1g, l1b, rwh, rwl, rb, tri, upp,
           seq, batch, batch0):
    nt = seq // TB
    t_total = batch * seq
    nblk = batch * nt
    const2 = lambda b, t: (0, 0)
    const3 = lambda b, t: (0, 0, 0)
    row_blk = lambda b, t: (b * nt + t, 0)
    meta_blk = lambda b, t: (0, b * nt + t)
    once = dict(pipeline_mode=pl.Buffered(1))
    in_specs = [
        pl.BlockSpec((TB, D_MODEL), lambda b, t: ((b + batch0) * nt + t, 0)),
        pl.BlockSpec((TB, D_MODEL),
                     lambda b, t: (batch0 * nt + jnp.minimum(b * nt + t + 1, nblk - 1), 0)),
        pl.BlockSpec((D_MODEL, IN_COLS), const2, **once),
        pl.BlockSpec((1, HGRN_WIDTH), const2),
        pl.BlockSpec((1, HGRN_WIDTH), const2),
        pl.BlockSpec((1, GMLP_WIDTH), const2),
        pl.BlockSpec((1, GMLP_WIDTH), const2),
        pl.BlockSpec((GMLP_GROUPS, GMLP_BLOCK, GMLP_BLOCK), const3),
        pl.BlockSpec((GMLP_BLOCK, GMLP_GROUPS), const2),
        pl.BlockSpec((D_MODEL, D_MODEL), const2, **once),
        pl.BlockSpec((1, D_MODEL), const2),
        pl.BlockSpec((1, D_MODEL), const2),
        pl.BlockSpec((N_EXPERTS, D_MODEL), const2),
        pl.BlockSpec((N_EXPERTS, D_MODEL), const2),
        pl.BlockSpec((N_EXPERTS, 1), const2),
        pl.BlockSpec((CHUNK, CHUNK), const2),
        pl.BlockSpec((TB, TB), const2, **once),
    ]
    out_shape = [
        jax.ShapeDtypeStruct((t_total, D_MODEL), jnp.float32),
        jax.ShapeDtypeStruct((t_total, HALF), jnp.int32),
        jax.ShapeDtypeStruct((TOP_K, t_total), jnp.int32),
        jax.ShapeDtypeStruct((TOP_K, t_total), jnp.float32),
        jax.ShapeDtypeStruct((TOP_K, t_total), jnp.int32),
        jax.ShapeDtypeStruct((N_EXPERTS, 128), jnp.float32),
    ]
    out_specs = [
        pl.BlockSpec((TB, D_MODEL), row_blk),
        pl.BlockSpec((TB, HALF), row_blk),
        pl.BlockSpec((TOP_K, TB), meta_blk),
        pl.BlockSpec((TOP_K, TB), meta_blk),
        pl.BlockSpec((TOP_K, TB), meta_blk),
        pl.BlockSpec((N_EXPERTS, 128), const2),
    ]
    scratch = [
        pltpu.VMEM((TB, IN_COLS), jnp.float32),
        pltpu.VMEM((TB, HGRN_WIDTH), jnp.float32),
        pltpu.VMEM((TB, HGRN_WIDTH), jnp.float32),
        pltpu.VMEM((CHUNK, HGRN_WIDTH), jnp.float32),
        pltpu.VMEM((CHUNK, HGRN_WIDTH), jnp.float32),
        pltpu.VMEM((TB, HGRN_WIDTH), jnp.float32),
        pltpu.VMEM((HGRN_HEADS, HEAD_DIM, HEAD_DIM), jnp.float32),
        pltpu.VMEM((N_EXPERTS, 128), jnp.float32),
        pltpu.VMEM((TB // CHUNK * HGRN_HEADS, CHUNK, CHUNK), jnp.bfloat16),
        pltpu.VMEM((TB, HGRN_WIDTH), jnp.bfloat16),
        pltpu.VMEM((TB, HGRN_WIDTH), jnp.bfloat16),
        pltpu.VMEM((TB // CHUNK * HGRN_HEADS, HEAD_DIM, HEAD_DIM), jnp.float32),
        pltpu.VMEM((TB, D_MODEL), jnp.bfloat16),
    ]
    return pl.pallas_call(
        _mixer_kernel,
        grid=(batch, nt),
        in_specs=in_specs,
        out_specs=out_specs,
        out_shape=out_shape,
        scratch_shapes=scratch,
        compiler_params=pltpu.CompilerParams(
            dimension_semantics=("arbitrary", "arbitrary"),
            vmem_limit_bytes=VMEM_LIMIT),
        name="mixer",
    )(xt, xt, win, lb, hg, lng, lnb, wm, bst, wout, l1g, l1b, rwh, rwl, rb, tri, upp)


def _plan_kernel(cnt_ref, idx_ref, rank_ref, dest_ref, tiles_ref):
    f32 = jnp.float32
    n_e = N_EXPERTS
    e_sub = lax.broadcasted_iota(jnp.int32, (n_e, n_e), 0)
    e_lane = lax.broadcasted_iota(jnp.int32, (n_e, n_e), 1)
    counts = cnt_ref[:, 0:1]
    padded = jnp.floor((counts + (TM - 1)) * (1.0 / TM)) * TM
    as_row = lambda col: jnp.sum(jnp.where(e_sub == e_lane, col, 0.0), axis=0, keepdims=True)
    padded_row = as_row(padded)
    counts_row = as_row(counts)
    ends = jnp.sum(jnp.where(e_lane <= e_sub, padded_row, 0.0), axis=1, keepdims=True)
    starts = ends - padded
    owns_later = (e_lane > e_sub) & (counts_row > 0.0)
    nxt = jnp.min(jnp.where(owns_later, e_lane, n_e), axis=1, keepdims=True)
    own = lax.broadcasted_iota(jnp.int32, (n_e, 1), 0)
    nxt = jnp.where(nxt == n_e, own, nxt).astype(f32)

    n_lanes = tiles_ref.shape[1]
    tile_row = lax.broadcasted_iota(jnp.int32, (n_e, n_lanes), 1).astype(f32) * TM
    e_of = lax.broadcasted_iota(jnp.int32, (n_e, n_lanes), 0)
    tile_e = jnp.minimum(jnp.sum(jnp.where(tile_row >= ends, 1, 0), axis=0, keepdims=True), n_e - 1)
    mine = e_of == tile_e
    pick = lambda col: jnp.sum(jnp.where(mine, col, 0.0), axis=0, keepdims=True)
    valid = jnp.clip(pick(starts + counts) - tile_row[0:1], 0.0, float(TM))
    rows = [tile_e, valid.astype(jnp.int32), pick(nxt).astype(jnp.int32)]
    rows.append(jnp.zeros((tiles_ref.shape[0] - len(rows), n_lanes), jnp.int32))
    tiles_ref[...] = jnp.concatenate(rows, axis=0)

    chunk = PLAN_CHUNK
    e_chunk = lax.broadcasted_iota(jnp.int32, (n_e, chunk), 0)

    def body(c, carry):
        lanes = pl.ds(pl.multiple_of(c * chunk, chunk), chunk)
        for k in range(TOP_K):
            hit = e_chunk == idx_ref[k:k + 1, lanes]
            start_of = jnp.sum(jnp.where(hit, starts, 0.0), axis=0, keepdims=True)
            dest_ref[k:k + 1, lanes] = rank_ref[k:k + 1, lanes] + start_of.astype(jnp.int32)
        return carry

    lax.fori_loop(0, idx_ref.shape[1] // chunk, body, 0)


def _plan(cnt, idx, rank, n_tiles):
    t_part = idx.shape[1]
    n_lanes = -(-n_tiles // 128) * 128
    full = lambda shape: pl.BlockSpec(shape, lambda i: (0,) * len(shape))
    dest, tiles = pl.pallas_call(
        _plan_kernel,
        grid=(1,),
        in_specs=[full(cnt.shape), full(idx.shape), full(rank.shape)],
        out_specs=[full(idx.shape), full((8, n_lanes))],
        out_shape=[jax.ShapeDtypeStruct((TOP_K, t_part), jnp.int32),
                   jax.ShapeDtypeStruct((8, n_lanes), jnp.int32)],
        compiler_params=pltpu.CompilerParams(
            dimension_semantics=("arbitrary",), vmem_limit_bytes=VMEM_LIMIT),
        name="plan",
    )(cnt, idx, rank)
    return dest, tiles[0, :n_tiles], tiles[1, :n_tiles], tiles[2, :n_tiles]


def _sc_workers():
    info = plsc.get_sparse_core_info()
    return info.num_cores, info.num_cores * info.num_subcores


def _sc_dispatch(rows, dest, n_out):
    t_total, dw = rows.shape
    nc, nw = _sc_workers()
    per_w = t_total // nw
    mesh = plsc.VectorSubcoreMesh(core_axis_name="c", subcore_axis_name="s")

    @functools.partial(
        pl.kernel,
        out_type=jax.ShapeDtypeStruct((n_out, dw), rows.dtype),
        mesh=mesh,
        scratch_types=[pltpu.VMEM((SC_WINDOW,), jnp.int32) for _ in range(TOP_K)]
        + [pltpu.VMEM((SC_WINDOW, dw), rows.dtype), pltpu.SemaphoreType.DMA],
        name="sc_dispatch",
    )
    def k(x_hbm, i_hbm, o_hbm, i0, i1, i2, i3, rows_v, sem):
        wid = lax.axis_index("s") * nc + lax.axis_index("c")
        idx_bufs = (i0, i1, i2, i3)

        @pl.loop(0, per_w // SC_WINDOW)
        def _(j):
            base = wid * per_w + j * SC_WINDOW
            for kk in range(TOP_K):
                pltpu.sync_copy(i_hbm.at[kk, pl.ds(base, SC_WINDOW)], idx_bufs[kk])
            pltpu.sync_copy(x_hbm.at[pl.ds(base, SC_WINDOW)], rows_v)
            copies = [pltpu.async_copy(rows_v, o_hbm.at[idx_bufs[kk]], sem) for kk in range(TOP_K)]
            for cp in copies:
                cp.wait()

    return k(rows, dest)


def _sc_gather(table, idx):
    n_k, t_total = idx.shape
    n = n_k * t_total
    dw = table.shape[1]
    nc, nw = _sc_workers()
    per_w = n // nw
    w_per_k = nw // n_k
    mesh = plsc.VectorSubcoreMesh(core_axis_name="c", subcore_axis_name="s")

    @functools.partial(
        pl.kernel,
        out_type=jax.ShapeDtypeStruct((n, dw), table.dtype),
        mesh=mesh,
        scratch_types=[pltpu.VMEM((SC_WINDOW,), jnp.int32),
                       pltpu.VMEM((SC_WINDOW, dw), table.dtype),
                       pltpu.SemaphoreType.DMA],
        name="sc_gather",
    )
    def k(t_hbm, i_hbm, o_hbm, idx_v, rows_v, sem):
        wid = lax.axis_index("s") * nc + lax.axis_index("c")

        row = wid // w_per_k
        col0 = (wid % w_per_k) * per_w

        @pl.loop(0, per_w // SC_WINDOW)
        def _(j):
            col = col0 + j * SC_WINDOW
            pltpu.sync_copy(i_hbm.at[row, pl.ds(col, SC_WINDOW)], idx_v)
            pltpu.async_copy(t_hbm.at[idx_v], rows_v, sem).wait()
            pltpu.sync_copy(rows_v, o_hbm.at[pl.ds(row * t_total + col, SC_WINDOW)])

    return k(table, idx)


def _expert_kernel(te_ref, tv_ref, tn_ref, xs_ref, w1_hbm, w2_hbm, b1g_ref, b1l_ref, b2_ref,
                   perm_ref, y_ref, w1f_scr, w2f_scr, w1g_scr, w1l_scr, w2_scr, sems):
    i = pl.program_id(0)
    valid = tv_ref[i]
    expert = te_ref[i]
    f32 = jnp.float32
    bf16 = jnp.bfloat16
    expert_changed = (i == 0) | (expert != te_ref[jnp.maximum(i - 1, 0)])

    def weight_copies(e):
        return (pltpu.make_async_copy(w1_hbm.at[e], w1f_scr, sems.at[0]),
                pltpu.make_async_copy(w2_hbm.at[e], w2f_scr, sems.at[1]))

    @pl.when(valid == 0)
    def _():
        y_ref[...] = jnp.zeros_like(y_ref)

    @pl.when(i == 0)
    def _():
        for cp in weight_copies(expert):
            cp.start()

    @pl.when((valid > 0) & expert_changed)
    def _():
        for cp in weight_copies(expert):
            cp.wait()
        w2_scr[...] = w2f_scr[...].astype(bf16)
        perm = perm_ref[...]
        for c in range(2 * D_MODEL // PERM_BLOCK):
            blk = w1f_scr[:, c * PERM_BLOCK:(c + 1) * PERM_BLOCK].astype(bf16)
            r = jnp.dot(blk, perm, preferred_element_type=f32).astype(bf16)
            half = PERM_BLOCK // 2
            w1g_scr[:, c * half:(c + 1) * half] = r[:, :half]
            w1l_scr[:, c * half:(c + 1) * half] = r[:, half:]

        @pl.when(tn_ref[i] != expert)
        def _():
            for cp in weight_copies(tn_ref[i]):
                cp.start()

    def expert_rows(n_rows):
        a, b = _unpack_rows(xs_ref[:n_rows, :])
        keep = lax.broadcasted_iota(jnp.int32, (n_rows, 1), 0) < valid
        x = jnp.where(keep, jnp.concatenate([a, b], axis=1), 0.0).astype(bf16)
        hg = jnp.dot(x, w1g_scr[...], preferred_element_type=f32) + b1g_ref[0]
        hl = jnp.dot(x, w1l_scr[...], preferred_element_type=f32) + b1l_ref[0]
        xg = jnp.minimum(hg, SWIGLU_LIMIT)
        xl = jnp.clip(hl, -SWIGLU_LIMIT, SWIGLU_LIMIT)
        act = xg * _sigmoid(SWIGLU_ALPHA * xg) * (xl + 1.0)
        y = jnp.dot(act.astype(bf16), w2_scr[...], preferred_element_type=f32) + b2_ref[0]
        y_ref[:n_rows, :] = _pack_rows(y)
        if n_rows < TM:
            y_ref[n_rows:, :] = jnp.zeros((TM - n_rows, HALF), jnp.int32)

    for blocks in range(TM // ROW_STEP, 0, -1):
        @pl.when((valid > (blocks - 1) * ROW_STEP) & (valid <= blocks * ROW_STEP))
        def _(blocks=blocks):
            expert_rows(blocks * ROW_STEP)


def _experts(xs, tile_expert, tile_valid, tile_next, w1, w2, b1g, b1l, b2, perm):
    n_slots = xs.shape[0]
    n_tiles = n_slots // TM
    d_ff = w2.shape[1]
    wmap = lambda i, te, tv, tn: (te[i], 0, 0)
    grid_spec = pltpu.PrefetchScalarGridSpec(
        num_scalar_prefetch=3,
        grid=(n_tiles,),
        in_specs=[
            pl.BlockSpec((TM, HALF), lambda i, te, tv, tn: (i, 0)),
            pl.BlockSpec(memory_space=pl.ANY),
            pl.BlockSpec(memory_space=pl.ANY),
            pl.BlockSpec((1, 1, d_ff), wmap),
            pl.BlockSpec((1, 1, d_ff), wmap),
            pl.BlockSpec((1, 1, D_MODEL), wmap),
            pl.BlockSpec((PERM_BLOCK, PERM_BLOCK), lambda i, te, tv, tn: (0, 0)),
        ],
        out_specs=pl.BlockSpec((TM, HALF), lambda i, te, tv, tn: (i, 0)),
        scratch_shapes=[
            pltpu.VMEM((D_MODEL, 2 * d_ff), jnp.float32),
            pltpu.VMEM((d_ff, D_MODEL), jnp.float32),
            pltpu.VMEM((D_MODEL, d_ff), jnp.bfloat16),
            pltpu.VMEM((D_MODEL, d_ff), jnp.bfloat16),
            pltpu.VMEM((d_ff, D_MODEL), jnp.bfloat16),
            pltpu.SemaphoreType.DMA((2,)),
        ],
    )
    return pl.pallas_call(
        _expert_kernel,
        grid_spec=grid_spec,
        out_shape=jax.ShapeDtypeStruct((n_slots, HALF), jnp.int32),
        compiler_params=pltpu.CompilerParams(
            dimension_semantics=("arbitrary",),
            vmem_limit_bytes=VMEM_LIMIT),
        name="experts",
    )(tile_expert, tile_valid, tile_next, xs, w1, w2, b1g, b1l, b2, perm)


def _combine_kernel(h_ref, yk_ref, gate_ref, g_ref, b_ref, *rest):
    o_ref = rest[-1]
    gates = gate_ref[...].T
    acc_a = None
    acc_b = None
    for k in range(TOP_K):
        a, b = _unpack_rows(yk_ref[k])
        gk = gates[:, k:k + 1]
        acc_a = gk * a if acc_a is None else acc_a + gk * a
        acc_b = gk * b if acc_b is None else acc_b + gk * b
    ffn = jnp.concatenate([acc_a, acc_b], axis=1)
    o_ref[...] = _layer_norm(ALPHA * h_ref[...] + ffn, g_ref[...], b_ref[...])


def _combine(h1f, yk, gates, g2, b2, out_prev, row0, t_all):
    t_part = h1f.shape[0]
    blk0 = row0 // TC3
    row = lambda i: (i, 0)
    const = lambda i: (0, 0)
    in_specs = [
        pl.BlockSpec((TC3, D_MODEL), row),
        pl.BlockSpec((TOP_K, TC3, HALF), lambda i: (0, i, 0)),
        pl.BlockSpec((TOP_K, TC3), lambda i: (0, i)),
        pl.BlockSpec((1, D_MODEL), const),
        pl.BlockSpec((1, D_MODEL), const),
    ]
    args = [h1f, yk, gates, g2, b2]
    aliases = {}
    if out_prev is not None:
        in_specs.append(pl.BlockSpec(memory_space=pl.ANY))
        args.append(out_prev)
        aliases = {len(args) - 1: 0}
    return pl.pallas_call(
        _combine_kernel,
        grid=(t_part // TC3,),
        in_specs=in_specs,
        out_specs=pl.BlockSpec((TC3, D_MODEL), lambda i: (i + blk0, 0)),
        out_shape=jax.ShapeDtypeStruct((t_all, D_MODEL), jnp.float32),
        input_output_aliases=aliases,
        compiler_params=pltpu.CompilerParams(
            dimension_semantics=("arbitrary",),
            vmem_limit_bytes=VMEM_LIMIT),
        name="combine",
    )(*args)


def kernel(x, w_in, lb_logits, hgrn_norm_g, gmlp_ln_g, gmlp_ln_b, gmlp_ws, gmlp_bs, w_out, ln1_g, ln1_b, router_w, router_b, exp_w1, exp_b1, exp_w2, exp_b2, ln2_g, ln2_b):
    batch, seq, d = x.shape
    assert d == D_MODEL and seq % TB == 0 and w_in.shape[0] == 1
    t_total = batch * seq
    f32 = jnp.float32
    bf16 = jnp.bfloat16

    lb = jnp.cumsum(jax.nn.softmax(lb_logits.astype(f32), axis=0), axis=0)[0:1]
    chunk_id = jnp.arange(GMLP_BLOCK) // CHUNK
    wm = jnp.where((chunk_id[None, :] <= chunk_id[:, None])[None], gmlp_ws[0], 0.0).astype(bf16)
    rwt = router_w[0].T
    rwh = rwt.astype(bf16)
    rwl = (rwt - rwh.astype(f32)).astype(bf16)
    tri = (jnp.arange(CHUNK)[None, :] <= jnp.arange(CHUNK)[:, None]).astype(bf16)
    upp = (jnp.arange(TB)[:, None] < jnp.arange(TB)[None, :]).astype(bf16)

    lane = jnp.arange(PERM_BLOCK)
    src = jnp.where(lane < PERM_BLOCK // 2, 2 * lane, 2 * (lane - PERM_BLOCK // 2) + 1)
    perm = (jnp.arange(PERM_BLOCK)[:, None] == src[None, :]).astype(bf16)
    xt = x.reshape(t_total, d)
    win = w_in[0].astype(bf16)
    wout = w_out[0].astype(bf16)
    b1g, b1l, b2e = exp_b1[0][:, None, 0::2], exp_b1[0][:, None, 1::2], exp_b2[0][:, None, :]

    last = max(batch * LAST_PART_NUM // LAST_PART_DEN, 1) if batch > 1 else 0
    part_sizes = [pb for pb in (batch - last, last) if pb > 0]
    out = None
    b0 = 0
    for pb in part_sizes:
        t_part = pb * seq
        n_slots = t_part * TOP_K + N_EXPERTS * TM
        n_tiles = n_slots // TM
        h1f, h1p, idx, gates, rank, cnt = _mixer(
            xt, win, lb, hgrn_norm_g[0:1], gmlp_ln_g[0:1], gmlp_ln_b[0:1], wm, gmlp_bs[0].T, wout,
            ln1_g[0:1], ln1_b[0:1], rwh, rwl, router_b[0][:, None], tri, upp, seq, pb, b0)

        dest, tile_expert, tile_valid, tile_next = _plan(cnt, idx, rank, n_tiles)
        xs = _sc_dispatch(h1p, dest, n_slots)
        y = _experts(xs, tile_expert, tile_valid, tile_next,
                     exp_w1[0], exp_w2[0], b1g, b1l, b2e, perm)
        yk = _sc_gather(y, dest).reshape(TOP_K, t_part, HALF)
        out = _combine(h1f, yk, gates, ln2_g[0:1], ln2_b[0:1], out, b0 * seq, t_total)
        b0 += pb
    return out.reshape(batch, seq, d)
```

```python
import functools
import math

import jax
import jax.numpy as jnp
from jax import lax
from jax.experimental import pallas as pl
from jax.experimental.pallas import tpu as pltpu
from jax.experimental.pallas import tpu_sc as plsc

D_MODEL = 1024
CHUNK = 64
SUB = 16
N_SUB = CHUNK // SUB
MAX_SUB_DECAY = 86.0
HGRN_WIDTH = 512
HGRN_HEADS = 4
HEAD_DIM = 128
GMLP_WIDTH = 512
GMLP_BLOCK = 128
GMLP_GROUPS = 4
IN_COLS = 3072
N_EXPERTS = 32
TOP_K = 4
SWIGLU_LIMIT = 7.0
SWIGLU_ALPHA = 1.702
ALPHA = 2.0 ** 0.25
EPS = 1e-5
HALF = D_MODEL // 2

TB = 512
TM = 1024
ROW_STEP = 256
PLAN_CHUNK = 2048
TC3 = 512
LAST_PART_NUM, LAST_PART_DEN = 1, 4
SC_WINDOW = 128
PROJ_GROUP = 256
PERM_BLOCK = 256
VMEM_LIMIT = 56 * 1024 * 1024

_NT = (((1,), (1,)), ((), ()))


def _sigmoid(x):
    return 0.5 * jnp.tanh(0.5 * x) + 0.5


def _gelu(x):
    return 0.5 * x * (1.0 + lax.erf(x * (1.0 / math.sqrt(2.0))))


def _layer_norm(x, g, b):
    mu = jnp.mean(x, axis=-1, keepdims=True)
    xc = x - mu
    var = jnp.mean(xc * xc, axis=-1, keepdims=True)
    return xc * lax.rsqrt(var + EPS) * g + b


def _pack_rounded(hr):
    au = lax.bitcast_convert_type(hr[:, :HALF], jnp.uint32) >> 16
    bu = lax.bitcast_convert_type(hr[:, HALF:], jnp.uint32) & jnp.uint32(0xFFFF0000)
    return lax.bitcast_convert_type(au | bu, jnp.int32)


def _pack_rows(h):
    return _pack_rounded(h.astype(jnp.bfloat16).astype(jnp.float32))


def _unpack_rows(w):
    u = lax.bitcast_convert_type(w, jnp.uint32)
    a = lax.bitcast_convert_type(u << 16, jnp.float32)
    b = lax.bitcast_convert_type(u & jnp.uint32(0xFFFF0000), jnp.float32)
    return a, b


def _mixer_kernel(x_ref, xn_ref, win_ref, lb_ref, hg_ref, lng_ref, lnb_ref, wm_ref, bst_ref,
                  wout_ref, l1g_ref, l1b_ref, rwh_ref, rwl_ref, rb_ref, tri_ref, upp_ref,
                  h1f_ref, h1p_ref, idx_ref, gate_ref, rank_ref, cnt_ref,
                  p_scr, lf_scr, kk_scr, g_scr, kc_scr, o_scr, st_scr, carry_scr,
                  amat_scr, qg_scr, kd_scr, upd_scr, mix_scr):
    b = pl.program_id(0)
    t = pl.program_id(1)
    step = b * pl.num_programs(1) + t
    f32 = jnp.float32
    bf16 = jnp.bfloat16

    @pl.when(t == 0)
    def _():
        st_scr[...] = jnp.zeros_like(st_scr)

    @pl.when((b == 0) & (t == 0))
    def _():
        carry_scr[...] = jnp.zeros_like(carry_scr)

    def project(xb, group):
        cols = slice(group * PROJ_GROUP, (group + 1) * PROJ_GROUP)
        p_scr[:, cols] = jnp.dot(xb, win_ref[:, cols], preferred_element_type=f32)

    def hgrn_gates():
        fl = p_scr[:, HGRN_WIDTH:2 * HGRN_WIDTH]
        z = jnp.exp(-jnp.abs(fl))
        r = 1.0 / (1.0 + z)
        k = (1.0 - lb_ref[...]) * jnp.where(fl >= 0, z * r, r)
        kk_scr[...] = k
        lf_scr[...] = jnp.log2(1.0 - k)

    @pl.when(step == 0)
    def _():
        xb0 = x_ref[...].astype(bf16)
        for group in range(IN_COLS // PROJ_GROUP):
            project(xb0, group)
        hgrn_gates()

    tri = tri_ref[...]
    row64 = lax.broadcasted_iota(jnp.int32, (CHUNK, CHUNK), 0)
    col64 = lax.broadcasted_iota(jnp.int32, (CHUNK, CHUNK), 1)
    lane_sub = lax.broadcasted_iota(jnp.int32, (SUB, CHUNK), 1)
    scale = HEAD_DIM ** -0.5

    def pad_rows(piece, lo_r):
        parts = []
        if lo_r > 0:
            parts.append(jnp.zeros((lo_r, HEAD_DIM), f32))
        parts.append(piece)
        rest = CHUNK - lo_r - piece.shape[0]
        if rest > 0:
            parts.append(jnp.zeros((rest, HEAD_DIM), f32))
        return jnp.concatenate(parts, axis=0) if len(parts) > 1 else piece

    def intra_factorised(q, k, gh):
        qts, kts = [], []
        for a in range(N_SUB):
            lo_r, hi_r = a * SUB, (a + 1) * SUB
            if a == 0:
                qa = q[:hi_r] * jnp.exp2(gh[:hi_r])
                ka = k[:hi_r] * jnp.exp2(-gh[:hi_r])
            else:
                ra = gh[lo_r - 1:lo_r]
                qa = q[lo_r:hi_r] * jnp.exp2(gh[lo_r:hi_r] - ra)
                ka = k[:hi_r] * jnp.exp2(ra - gh[:hi_r])
            qts.append(pad_rows(qa, lo_r))
            kts.append(pad_rows(ka, 0))
        a_mat = lax.dot_general(jnp.concatenate(qts, axis=1).astype(bf16),
                                jnp.concatenate(kts, axis=1).astype(bf16), _NT,
                                preferred_element_type=f32)
        return jnp.where(row64 >= col64, a_mat, 0.0)

    def chunks_factorised():
        n_chunks = TB // CHUNK
        heads = [(c, h) for c in range(n_chunks) for h in range(HGRN_HEADS)]

        def cols(h, base=0):
            return slice(base + h * HEAD_DIM, base + (h + 1) * HEAD_DIM)

        for c in range(n_chunks):
            rows = slice(c * CHUNK, (c + 1) * CHUNK)
            lf = lf_scr[rows, :]
            hi = lf.astype(bf16)
            lo = (lf - hi.astype(f32)).astype(bf16)
            gg = jnp.dot(tri, jnp.concatenate([hi, lo], axis=1), preferred_element_type=f32)
            lf_scr[rows, :] = gg[:, :HGRN_WIDTH] + gg[:, HGRN_WIDTH:]
        for c, h in heads:
            rows = slice(c * CHUNK, (c + 1) * CHUNK)
            q = p_scr[rows, cols(h)]
            k = kk_scr[rows, cols(h)]
            gh = lf_scr[rows, cols(h)]
            amat_scr[c * HGRN_HEADS + h] = intra_factorised(q, k, gh).astype(bf16)
            qg_scr[rows, cols(h)] = (q * jnp.exp2(gh)).astype(bf16)
            kd_scr[rows, cols(h)] = (k * jnp.exp2(gh[CHUNK - 1:CHUNK] - gh)).astype(bf16)
        for c, h in heads:
            rows = slice(c * CHUNK, (c + 1) * CHUNK)
            v = p_scr[rows, cols(h, 2 * HGRN_WIDTH)]
            o_scr[rows, cols(h)] = jnp.dot(amat_scr[c * HGRN_HEADS + h], v.astype(bf16),
                                           preferred_element_type=f32)
            upd_scr[c * HGRN_HEADS + h] = jnp.dot(v.T.astype(bf16), kd_scr[rows, cols(h)],
                                                  preferred_element_type=f32)
        states = [st_scr[h] for h in range(HGRN_HEADS)]
        for c, h in heads:
            rows = slice(c * CHUNK, (c + 1) * CHUNK)
            st = states[h]
            o_inter = lax.dot_general(qg_scr[rows, cols(h)], st.astype(bf16), _NT,
                                      preferred_element_type=f32)
            o_scr[rows, cols(h)] = (o_scr[rows, cols(h)] + o_inter) * scale
            gl = lf_scr[(c + 1) * CHUNK - 1:(c + 1) * CHUNK, cols(h)]
            states[h] = st * jnp.exp2(gl) + upd_scr[c * HGRN_HEADS + h]
        for h in range(HGRN_HEADS):
            st_scr[h] = states[h]

    def intra_exact_diagonal(q, k, gh, cs):
        qts, kts = [], []
        for a in range(1, N_SUB):
            lo_r, hi_r = a * SUB, (a + 1) * SUB
            ra = g_scr[lo_r - 1:lo_r, cs]
            qts.append(pad_rows(q[lo_r:hi_r] * jnp.exp2(gh[lo_r:hi_r] - ra), lo_r))
            kts.append(pad_rows(k[:lo_r] * jnp.exp2(ra - gh[:lo_r]), 0))
        a_off = lax.dot_general(jnp.concatenate(qts, axis=1).astype(bf16),
                                jnp.concatenate(kts, axis=1).astype(bf16), _NT,
                                preferred_element_type=f32)
        diag_rows = []
        for a in range(N_SUB):
            lo_r = a * SUB
            gs = gh[lo_r:lo_r + SUB]
            qs = q[lo_r:lo_r + SUB]
            blk = jnp.zeros((SUB, CHUNK), f32)
            for jl in range(SUB):
                j = lo_r + jl
                gj = g_scr[j:j + 1, cs]
                kj = kc_scr[j:j + 1, cs]
                e = jnp.exp2(jnp.minimum(gs - gj, 0.0))
                col = jnp.sum(qs * (kj * e), axis=-1, keepdims=True)
                blk = jnp.where(lane_sub == j, col, blk)
            diag_rows.append(blk)
        a_diag = jnp.concatenate(diag_rows, axis=0)
        return a_off + jnp.where(row64 >= col64, a_diag, 0.0)

    def make_chunk_body(intra):
        def chunk_body(c, carry):
            r0 = pl.multiple_of(c * CHUNK, CHUNK)
            rows = pl.ds(r0, CHUNK)
            lf = lf_scr[rows, :]
            hi = lf.astype(bf16)
            lo = (lf - hi.astype(f32)).astype(bf16)
            gg = jnp.dot(tri, jnp.concatenate([hi, lo], axis=1), preferred_element_type=f32)
            g_all = gg[:, :HGRN_WIDTH] + gg[:, HGRN_WIDTH:]
            g_scr[...] = g_all
            kc_scr[...] = kk_scr[rows, :]
            for h in range(HGRN_HEADS):
                cs = slice(h * HEAD_DIM, (h + 1) * HEAD_DIM)
                q = p_scr[rows, h * HEAD_DIM:(h + 1) * HEAD_DIM]
                v = p_scr[rows, 2 * HGRN_WIDTH + h * HEAD_DIM:2 * HGRN_WIDTH + (h + 1) * HEAD_DIM]
                k = kc_scr[:, cs]
                gh = g_all[:, cs]
                st = st_scr[h]
                o_inter = lax.dot_general((q * jnp.exp2(gh)).astype(bf16), st.astype(bf16), _NT,
                                          preferred_element_type=f32)
                a_mat = intra(q, k, gh, cs)
                o = jnp.dot(a_mat.astype(bf16), v.astype(bf16), preferred_element_type=f32) + o_inter
                o_scr[rows, h * HEAD_DIM:(h + 1) * HEAD_DIM] = o * scale

                gl = g_scr[CHUNK - 1:CHUNK, cs]
                kd = k * jnp.exp2(gl - gh)
                upd = jnp.dot(v.T.astype(bf16), kd.astype(bf16), preferred_element_type=f32)
                st_scr[h] = st * jnp.exp2(gl) + upd
            return carry
        return chunk_body

    lf_all = lf_scr[...]
    sub_decay = -jnp.sum(lf_all.reshape(TB // SUB, SUB, HGRN_WIDTH), axis=1)
    bounded = jnp.max(sub_decay) <= MAX_SUB_DECAY

    f_groups = list(range(HGRN_WIDTH // PROJ_GROUP, 2 * HGRN_WIDTH // PROJ_GROUP))
    qi_groups = [g for g in range(3 * HGRN_WIDTH // PROJ_GROUP) if g not in f_groups]
    rest_groups = list(range(3 * HGRN_WIDTH // PROJ_GROUP, IN_COLS // PROJ_GROUP))

    def gmlp_stage(xnb):
        for g in f_groups:
            project(xnb, g)
        for w in range(TB // GMLP_BLOCK):
            rows = slice(w * GMLP_BLOCK, (w + 1) * GMLP_BLOCK)
            u = _gelu(p_scr[rows, 4 * HGRN_WIDTH:4 * HGRN_WIDTH + GMLP_WIDTH])
            vn = _layer_norm(_gelu(p_scr[rows, 4 * HGRN_WIDTH + GMLP_WIDTH:]),
                             lng_ref[...], lnb_ref[...])
            vnb = vn.astype(bf16)
            cols = []
            for g in range(GMLP_GROUPS):
                s = jnp.dot(wm_ref[g], vnb[:, g * HEAD_DIM:(g + 1) * HEAD_DIM],
                            preferred_element_type=f32) + bst_ref[:, g:g + 1]
                cols.append(s)
            mix_scr[rows, HGRN_WIDTH:] = (u * jnp.concatenate(cols, axis=1)).astype(bf16)

    def output_stage(xnb):
        for g in qi_groups:
            project(xnb, g)
        for w in range(TB // GMLP_BLOCK):
            rows = slice(w * GMLP_BLOCK, (w + 1) * GMLP_BLOCK)
            o = o_scr[rows, :]
            ms = jnp.mean(o * o, axis=-1, keepdims=True)
            gate = p_scr[rows, 3 * HGRN_WIDTH:4 * HGRN_WIDTH]
            y_rec = o * lax.rsqrt(ms + EPS) * hg_ref[...] * (gate * _sigmoid(gate))
            mix_scr[rows, :HGRN_WIDTH] = y_rec.astype(bf16)

        mix = jnp.dot(mix_scr[...], wout_ref[...], preferred_element_type=f32)
        hgrn_gates()
        h1 = _layer_norm(ALPHA * x_ref[...] + mix, l1g_ref[...], l1b_ref[...])
        h1f_ref[...] = h1
        hh = h1.astype(bf16)
        h1_rounded = hh.astype(f32)
        h1p_ref[...] = _pack_rounded(h1_rounded)
        for g in rest_groups[:2]:
            project(xnb, g)

        hl = (h1 - h1_rounded).astype(bf16)
        rwh = rwh_ref[...]
        by_hh = lax.dot_general(jnp.concatenate([rwh, rwl_ref[...]], axis=0), hh, _NT,
                                preferred_element_type=f32)
        logits = (by_hh[:N_EXPERTS] + by_hh[N_EXPERTS:]
                  + lax.dot_general(rwh, hl, _NT, preferred_element_type=f32) + rb_ref[...])
        for g in rest_groups[2:]:
            project(xnb, g)

        e_iota = lax.broadcasted_iota(jnp.int32, (N_EXPERTS, TB), 0)
        work = logits
        vals, idxs, hots = [], [], []
        for _ in range(TOP_K):
            m = jnp.max(work, axis=0, keepdims=True)
            ik = jnp.min(jnp.where(work == m, e_iota, N_EXPERTS), axis=0, keepdims=True)
            hot = e_iota == ik
            work = jnp.where(hot, -jnp.inf, work)
            vals.append(m)
            idxs.append(ik)
            hots.append(hot)
        exps = [jnp.exp(vk - vals[0]) for vk in vals]
        denom = exps[0] + exps[1] + exps[2] + exps[3]
        gate_ref[...] = jnp.concatenate([ek / denom for ek in exps], axis=0)
        idx_ref[...] = jnp.concatenate(idxs, axis=0)

        hot_any = jnp.where(hots[0] | hots[1] | hots[2] | hots[3], 1.0, 0.0)
        prefix = jnp.dot(hot_any.astype(bf16), upp_ref[...], preferred_element_type=f32)
        base = prefix + carry_scr[:, 0:1]
        ranks = [jnp.sum(jnp.where(hk, base, 0.0), axis=0, keepdims=True) for hk in hots]
        rank_ref[...] = jnp.concatenate(ranks, axis=0).astype(jnp.int32)
        new_carry = carry_scr[...] + jnp.sum(hot_any, axis=1, keepdims=True)
        carry_scr[...] = new_carry
        cnt_ref[...] = new_carry

    @pl.when(bounded)
    def _():
        xnb = xn_ref[...].astype(bf16)
        gmlp_stage(xnb)
        chunks_factorised()
        output_stage(xnb)

    @pl.when(jnp.logical_not(bounded))
    def _():
        xnb = xn_ref[...].astype(bf16)
        lax.fori_loop(0, TB // CHUNK, make_chunk_body(intra_exact_diagonal), 0)
        gmlp_stage(xnb)
        output_stage(xnb)


def _mixer(xt, win, lb, hg, lng, lnb, wm, bst, wout, l1g, l1b, rwh, rwl, rb, tri, upp,
           seq, batch, batch0):
    nt = seq // TB
    t_total = batch * seq
    nblk = batch * nt
    const2 = lambda b, t: (0, 0)
    const3 = lambda b, t: (0, 0, 0)
    row_blk = lambda b, t: (b * nt + t, 0)
    meta_blk = lambda b, t: (0, b * nt + t)
    once = dict(pipeline_mode=pl.Buffered(1))
    in_specs = [
        pl.BlockSpec((TB, D_MODEL), lambda b, t: ((b + batch0) * nt + t, 0)),
        pl.BlockSpec((TB, D_MODEL),
                     lambda b, t: (batch0 * nt + jnp.minimum(b * nt + t + 1, nblk - 1), 0)),
        pl.BlockSpec((D_MODEL, IN_COLS), const2, **once),
        pl.BlockSpec((1, HGRN_WIDTH), const2),
        pl.BlockSpec((1, HGRN_WIDTH), const2),
        pl.BlockSpec((1, GMLP_WIDTH), const2),
        pl.BlockSpec((1, GMLP_WIDTH), const2),
        pl.BlockSpec((GMLP_GROUPS, GMLP_BLOCK, GMLP_BLOCK), const3),
        pl.BlockSpec((GMLP_BLOCK, GMLP_GROUPS), const2),
        pl.BlockSpec((D_MODEL, D_MODEL), const2, **once),
        pl.BlockSpec((1, D_MODEL), const2),
        pl.BlockSpec((1, D_MODEL), const2),
        pl.BlockSpec((N_EXPERTS, D_MODEL), const2),
        pl.BlockSpec((N_EXPERTS, D_MODEL), const2),
        pl.BlockSpec((N_EXPERTS, 1), const2),
        pl.BlockSpec((CHUNK, CHUNK), const2),
        pl.BlockSpec((TB, TB), const2, **once),
    ]
    out_shape = [
        jax.ShapeDtypeStruct((t_total, D_MODEL), jnp.float32),
        jax.ShapeDtypeStruct((t_total, HALF), jnp.int32),
        jax.ShapeDtypeStruct((TOP_K, t_total), jnp.int32),
        jax.ShapeDtypeStruct((TOP_K, t_total), jnp.float32),
        jax.ShapeDtypeStruct((TOP_K, t_total), jnp.int32),
        jax.ShapeDtypeStruct((N_EXPERTS, 128), jnp.float32),
    ]
    out_specs = [
        pl.BlockSpec((TB, D_MODEL), row_blk),
        pl.BlockSpec((TB, HALF), row_blk),
        pl.BlockSpec((TOP_K, TB), meta_blk),
        pl.BlockSpec((TOP_K, TB), meta_blk),
        pl.BlockSpec((TOP_K, TB), meta_blk),
        pl.BlockSpec((N_EXPERTS, 128), const2),
    ]
    scratch = [
        pltpu.VMEM((TB, IN_COLS), jnp.float32),
        pltpu.VMEM((TB, HGRN_WIDTH), jnp.float32),
        pltpu.VMEM((TB, HGRN_WIDTH), jnp.float32),
        pltpu.VMEM((CHUNK, HGRN_WIDTH), jnp.float32),
        pltpu.VMEM((CHUNK, HGRN_WIDTH), jnp.float32),
        pltpu.VMEM((TB, HGRN_WIDTH), jnp.float32),
        pltpu.VMEM((HGRN_HEADS, HEAD_DIM, HEAD_DIM), jnp.float32),
        pltpu.VMEM((N_EXPERTS, 128), jnp.float32),
        pltpu.VMEM((TB // CHUNK * HGRN_HEADS, CHUNK, CHUNK), jnp.bfloat16),
        pltpu.VMEM((TB, HGRN_WIDTH), jnp.bfloat16),
        pltpu.VMEM((TB, HGRN_WIDTH), jnp.bfloat16),
        pltpu.VMEM((TB // CHUNK * HGRN_HEADS, HEAD_DIM, HEAD_DIM), jnp.float32),
        pltpu.VMEM((TB, D_MODEL), jnp.bfloat16),
    ]
    return pl.pallas_call(
        _mixer_kernel,
        grid=(batch, nt),
        in_specs=in_specs,
        out_specs=out_specs,
        out_shape=out_shape,
        scratch_shapes=scratch,
        compiler_params=pltpu.CompilerParams(
            dimension_semantics=("arbitrary", "arbitrary"),
            vmem_limit_bytes=VMEM_LIMIT),
        name="mixer",
    )(xt, xt, win, lb, hg, lng, lnb, wm, bst, wout, l1g, l1b, rwh, rwl, rb, tri, upp)


def _plan_kernel(cnt_ref, idx_ref, rank_ref, dest_ref, tiles_ref):
    f32 = jnp.float32
    n_e = N_EXPERTS
    e_sub = lax.broadcasted_iota(jnp.int32, (n_e, n_e), 0)
    e_lane = lax.broadcasted_iota(jnp.int32, (n_e, n_e), 1)
    counts = cnt_ref[:, 0:1]
    padded = jnp.floor((counts + (TM - 1)) * (1.0 / TM)) * TM
    as_row = lambda col: jnp.sum(jnp.where(e_sub == e_lane, col, 0.0), axis=0, keepdims=True)
    padded_row = as_row(padded)
    counts_row = as_row(counts)
    ends = jnp.sum(jnp.where(e_lane <= e_sub, padded_row, 0.0), axis=1, keepdims=True)
    starts = ends - padded
    owns_later = (e_lane > e_sub) & (counts_row > 0.0)
    nxt = jnp.min(jnp.where(owns_later, e_lane, n_e), axis=1, keepdims=True)
    own = lax.broadcasted_iota(jnp.int32, (n_e, 1), 0)
    nxt = jnp.where(nxt == n_e, own, nxt).astype(f32)

    n_lanes = tiles_ref.shape[1]
    tile_row = lax.broadcasted_iota(jnp.int32, (n_e, n_lanes), 1).astype(f32) * TM
    e_of = lax.broadcasted_iota(jnp.int32, (n_e, n_lanes), 0)
    tile_e = jnp.minimum(jnp.sum(jnp.where(tile_row >= ends, 1, 0), axis=0, keepdims=True), n_e - 1)
    mine = e_of == tile_e
    pick = lambda col: jnp.sum(jnp.where(mine, col, 0.0), axis=0, keepdims=True)
    valid = jnp.clip(pick(starts + counts) - tile_row[0:1], 0.0, float(TM))
    rows = [tile_e, valid.astype(jnp.int32), pick(nxt).astype(jnp.int32)]
    rows.append(jnp.zeros((tiles_ref.shape[0] - len(rows), n_lanes), jnp.int32))
    tiles_ref[...] = jnp.concatenate(rows, axis=0)

    chunk = PLAN_CHUNK
    e_chunk = lax.broadcasted_iota(jnp.int32, (n_e, chunk), 0)

    def body(c, carry):
        lanes = pl.ds(pl.multiple_of(c * chunk, chunk), chunk)
        for k in range(TOP_K):
            hit = e_chunk == idx_ref[k:k + 1, lanes]
            start_of = jnp.sum(jnp.where(hit, starts, 0.0), axis=0, keepdims=True)
            dest_ref[k:k + 1, lanes] = rank_ref[k:k + 1, lanes] + start_of.astype(jnp.int32)
        return carry

    lax.fori_loop(0, idx_ref.shape[1] // chunk, body, 0)


def _plan(cnt, idx, rank, n_tiles):
    t_part = idx.shape[1]
    n_lanes = -(-n_tiles // 128) * 128
    full = lambda shape: pl.BlockSpec(shape, lambda i: (0,) * len(shape))
    dest, tiles = pl.pallas_call(
        _plan_kernel,
        grid=(1,),
        in_specs=[full(cnt.shape), full(idx.shape), full(rank.shape)],
        out_specs=[full(idx.shape), full((8, n_lanes))],
        out_shape=[jax.ShapeDtypeStruct((TOP_K, t_part), jnp.int32),
                   jax.ShapeDtypeStruct((8, n_lanes), jnp.int32)],
        compiler_params=pltpu.CompilerParams(
            dimension_semantics=("arbitrary",), vmem_limit_bytes=VMEM_LIMIT),
        name="plan",
    )(cnt, idx, rank)
    return dest, tiles[0, :n_tiles], tiles[1, :n_tiles], tiles[2, :n_tiles]


def _sc_workers():
    info = plsc.get_sparse_core_info()
    return info.num_cores, info.num_cores * info.num_subcores


def _sc_dispatch(rows, dest, n_out):
    t_total, dw = rows.shape
    nc, nw = _sc_workers()
    per_w = t_total // nw
    mesh = plsc.VectorSubcoreMesh(core_axis_name="c", subcore_axis_name="s")

    @functools.partial(
        pl.kernel,
        out_type=jax.ShapeDtypeStruct((n_out, dw), rows.dtype),
        mesh=mesh,
        scratch_types=[pltpu.VMEM((SC_WINDOW,), jnp.int32) for _ in range(TOP_K)]
        + [pltpu.VMEM((SC_WINDOW, dw), rows.dtype), pltpu.SemaphoreType.DMA],
        name="sc_dispatch",
    )
    def k(x_hbm, i_hbm, o_hbm, i0, i1, i2, i3, rows_v, sem):
        wid = lax.axis_index("s") * nc + lax.axis_index("c")
        idx_bufs = (i0, i1, i2, i3)

        @pl.loop(0, per_w // SC_WINDOW)
        def _(j):
            base = wid * per_w + j * SC_WINDOW
            for kk in range(TOP_K):
                pltpu.sync_copy(i_hbm.at[kk, pl.ds(base, SC_WINDOW)], idx_bufs[kk])
            pltpu.sync_copy(x_hbm.at[pl.ds(base, SC_WINDOW)], rows_v)
            copies = [pltpu.async_copy(rows_v, o_hbm.at[idx_bufs[kk]], sem) for kk in range(TOP_K)]
            for cp in copies:
                cp.wait()

    return k(rows, dest)


def _sc_gather(table, idx):
    n_k, t_total = idx.shape
    n = n_k * t_total
    dw = table.shape[1]
    nc, nw = _sc_workers()
    per_w = n // nw
    w_per_k = nw // n_k
    mesh = plsc.VectorSubcoreMesh(core_axis_name="c", subcore_axis_name="s")

    @functools.partial(
        pl.kernel,
        out_type=jax.ShapeDtypeStruct((n, dw), table.dtype),
        mesh=mesh,
        scratch_types=[pltpu.VMEM((SC_WINDOW,), jnp.int32),
                       pltpu.VMEM((SC_WINDOW, dw), table.dtype),
                       pltpu.SemaphoreType.DMA],
        name="sc_gather",
    )
    def k(t_hbm, i_hbm, o_hbm, idx_v, rows_v, sem):
        wid = lax.axis_index("s") * nc + lax.axis_index("c")

        row = wid // w_per_k
        col0 = (wid % w_per_k) * per_w

        @pl.loop(0, per_w // SC_WINDOW)
        def _(j):
            col = col0 + j * SC_WINDOW
            pltpu.sync_copy(i_hbm.at[row, pl.ds(col, SC_WINDOW)], idx_v)
            pltpu.async_copy(t_hbm.at[idx_v], rows_v, sem).wait()
            pltpu.sync_copy(rows_v, o_hbm.at[pl.ds(row * t_total + col, SC_WINDOW)])

    return k(table, idx)


def _expert_kernel(te_ref, tv_ref, tn_ref, xs_ref, w1_hbm, w2_hbm, b1g_ref, b1l_ref, b2_ref,
                   perm_ref, y_ref, w1f_scr, w2f_scr, w1g_scr, w1l_scr, w2_scr, sems):
    i = pl.program_id(0)
    valid = tv_ref[i]
    expert = te_ref[i]
    f32 = jnp.float32
    bf16 = jnp.bfloat16
    expert_changed = (i == 0) | (expert != te_ref[jnp.maximum(i - 1, 0)])

    def weight_copies(e):
        return (pltpu.make_async_copy(w1_hbm.at[e], w1f_scr, sems.at[0]),
                pltpu.make_async_copy(w2_hbm.at[e], w2f_scr, sems.at[1]))

    @pl.when(valid == 0)
    def _():
        y_ref[...] = jnp.zeros_like(y_ref)

    @pl.when(i == 0)
    def _():
        for cp in weight_copies(expert):
            cp.start()

    @pl.when((valid > 0) & expert_changed)
    def _():
        for cp in weight_copies(expert):
            cp.wait()
        w2_scr[...] = w2f_scr[...].astype(bf16)
        perm = perm_ref[...]
        for c in range(2 * D_MODEL // PERM_BLOCK):
            blk = w1f_scr[:, c * PERM_BLOCK:(c + 1) * PERM_BLOCK].astype(bf16)
            r = jnp.dot(blk, perm, preferred_element_type=f32).astype(bf16)
            half = PERM_BLOCK // 2
            w1g_scr[:, c * half:(c + 1) * half] = r[:, :half]
            w1l_scr[:, c * half:(c + 1) * half] = r[:, half:]

        @pl.when(tn_ref[i] != expert)
        def _():
            for cp in weight_copies(tn_ref[i]):
                cp.start()

    def expert_rows(n_rows):
        a, b = _unpack_rows(xs_ref[:n_rows, :])
        keep = lax.broadcasted_iota(jnp.int32, (n_rows, 1), 0) < valid
        x = jnp.where(keep, jnp.concatenate([a, b], axis=1), 0.0).astype(bf16)
        hg = jnp.dot(x, w1g_scr[...], preferred_element_type=f32) + b1g_ref[0]
        hl = jnp.dot(x, w1l_scr[...], preferred_element_type=f32) + b1l_ref[0]
        xg = jnp.minimum(hg, SWIGLU_LIMIT)
        xl = jnp.clip(hl, -SWIGLU_LIMIT, SWIGLU_LIMIT)
        act = xg * _sigmoid(SWIGLU_ALPHA * xg) * (xl + 1.0)
        y = jnp.dot(act.astype(bf16), w2_scr[...], preferred_element_type=f32) + b2_ref[0]
        y_ref[:n_rows, :] = _pack_rows(y)
        if n_rows < TM:
            y_ref[n_rows:, :] = jnp.zeros((TM - n_rows, HALF), jnp.int32)

    for blocks in range(TM // ROW_STEP, 0, -1):
        @pl.when((valid > (blocks - 1) * ROW_STEP) & (valid <= blocks * ROW_STEP))
        def _(blocks=blocks):
            expert_rows(blocks * ROW_STEP)


def _experts(xs, tile_expert, tile_valid, tile_next, w1, w2, b1g, b1l, b2, perm):
    n_slots = xs.shape[0]
    n_tiles = n_slots // TM
    d_ff = w2.shape[1]
    wmap = lambda i, te, tv, tn: (te[i], 0, 0)
    grid_spec = pltpu.PrefetchScalarGridSpec(
        num_scalar_prefetch=3,
        grid=(n_tiles,),
        in_specs=[
            pl.BlockSpec((TM, HALF), lambda i, te, tv, tn: (i, 0)),
            pl.BlockSpec(memory_space=pl.ANY),
            pl.BlockSpec(memory_space=pl.ANY),
            pl.BlockSpec((1, 1, d_ff), wmap),
            pl.BlockSpec((1, 1, d_ff), wmap),
            pl.BlockSpec((1, 1, D_MODEL), wmap),
            pl.BlockSpec((PERM_BLOCK, PERM_BLOCK), lambda i, te, tv, tn: (0, 0)),
        ],
        out_specs=pl.BlockSpec((TM, HALF), lambda i, te, tv, tn: (i, 0)),
        scratch_shapes=[
            pltpu.VMEM((D_MODEL, 2 * d_ff), jnp.float32),
            pltpu.VMEM((d_ff, D_MODEL), jnp.float32),
            pltpu.VMEM((D_MODEL, d_ff), jnp.bfloat16),
            pltpu.VMEM((D_MODEL, d_ff), jnp.bfloat16),
            pltpu.VMEM((d_ff, D_MODEL), jnp.bfloat16),
            pltpu.SemaphoreType.DMA((2,)),
        ],
    )
    return pl.pallas_call(
        _expert_kernel,
        grid_spec=grid_spec,
        out_shape=jax.ShapeDtypeStruct((n_slots, HALF), jnp.int32),
        compiler_params=pltpu.CompilerParams(
            dimension_semantics=("arbitrary",),
            vmem_limit_bytes=VMEM_LIMIT),
        name="experts",
    )(tile_expert, tile_valid, tile_next, xs, w1, w2, b1g, b1l, b2, perm)


def _combine_kernel(h_ref, yk_ref, gate_ref, g_ref, b_ref, *rest):
    o_ref = rest[-1]
    gates = gate_ref[...].T
    acc_a = None
    acc_b = None
    for k in range(TOP_K):
        a, b = _unpack_rows(yk_ref[k])
        gk = gates[:, k:k + 1]
        acc_a = gk * a if acc_a is None else acc_a + gk * a
        acc_b = gk * b if acc_b is None else acc_b + gk * b
    ffn = jnp.concatenate([acc_a, acc_b], axis=1)
    o_ref[...] = _layer_norm(ALPHA * h_ref[...] + ffn, g_ref[...], b_ref[...])


def _combine(h1f, yk, gates, g2, b2, out_prev, row0, t_all):
    t_part = h1f.shape[0]
    blk0 = row0 // TC3
    row = lambda i: (i, 0)
    const = lambda i: (0, 0)
    in_specs = [
        pl.BlockSpec((TC3, D_MODEL), row),
        pl.BlockSpec((TOP_K, TC3, HALF), lambda i: (0, i, 0)),
        pl.BlockSpec((TOP_K, TC3), lambda i: (0, i)),
        pl.BlockSpec((1, D_MODEL), const),
        pl.BlockSpec((1, D_MODEL), const),
    ]
    args = [h1f, yk, gates, g2, b2]
    aliases = {}
    if out_prev is not None:
        in_specs.append(pl.BlockSpec(memory_space=pl.ANY))
        args.append(out_prev)
        aliases = {len(args) - 1: 0}
    return pl.pallas_call(
        _combine_kernel,
        grid=(t_part // TC3,),
        in_specs=in_specs,
        out_specs=pl.BlockSpec((TC3, D_MODEL), lambda i: (i + blk0, 0)),
        out_shape=jax.ShapeDtypeStruct((t_all, D_MODEL), jnp.float32),
        input_output_aliases=aliases,
        compiler_params=pltpu.CompilerParams(
            dimension_semantics=("arbitrary",),
            vmem_limit_bytes=VMEM_LIMIT),
        name="combine",
    )(*args)


def kernel(x, w_in, lb_logits, hgrn_norm_g, gmlp_ln_g, gmlp_ln_b, gmlp_ws, gmlp_bs, w_out, ln1_g, ln1_b, router_w, router_b, exp_w1, exp_b1, exp_w2, exp_b2, ln2_g, ln2_b):
    batch, seq, d = x.shape
    assert d == D_MODEL and seq % TB == 0 and w_in.shape[0] == 1
    t_total = batch * seq
    f32 = jnp.float32
    bf16 = jnp.bfloat16

    lb = jnp.cumsum(jax.nn.softmax(lb_logits.astype(f32), axis=0), axis=0)[0:1]
    chunk_id = jnp.arange(GMLP_BLOCK) // CHUNK
    wm = jnp.where((chunk_id[None, :] <= chunk_id[:, None])[None], gmlp_ws[0], 0.0).astype(bf16)
    rwt = router_w[0].T
    rwh = rwt.astype(bf16)
    rwl = (rwt - rwh.astype(f32)).astype(bf16)
    tri = (jnp.arange(CHUNK)[None, :] <= jnp.arange(CHUNK)[:, None]).astype(bf16)
    upp = (jnp.arange(TB)[:, None] < jnp.arange(TB)[None, :]).astype(bf16)

    lane = jnp.arange(PERM_BLOCK)
    src = jnp.where(lane < PERM_BLOCK // 2, 2 * lane, 2 * (lane - PERM_BLOCK // 2) + 1)
    perm = (jnp.arange(PERM_BLOCK)[:, None] == src[None, :]).astype(bf16)
    xt = x.reshape(t_total, d)
    win = w_in[0].astype(bf16)
    wout = w_out[0].astype(bf16)
    b1g, b1l, b2e = exp_b1[0][:, None, 0::2], exp_b1[0][:, None, 1::2], exp_b2[0][:, None, :]

    last = max(batch * LAST_PART_NUM // LAST_PART_DEN, 1) if batch > 1 else 0
    part_sizes = [pb for pb in (batch - last, last) if pb > 0]
    out = None
    b0 = 0
    for pb in part_sizes:
        t_part = pb * seq
        n_slots = t_part * TOP_K + N_EXPERTS * TM
        n_tiles = n_slots // TM
        h1f, h1p, idx, gates, rank, cnt = _mixer(
            xt, win, lb, hgrn_norm_g[0:1], gmlp_ln_g[0:1], gmlp_ln_b[0:1], wm, gmlp_bs[0].T, wout,
            ln1_g[0:1], ln1_b[0:1], rwh, rwl, router_b[0][:, None], tri, upp, seq, pb, b0)

        dest, tile_expert, tile_valid, tile_next = _plan(cnt, idx, rank, n_tiles)
        xs = _sc_dispatch(h1p, dest, n_slots)
        y = _experts(xs, tile_expert, tile_valid, tile_next,
                     exp_w1[0], exp_w2[0], b1g, b1l, b2e, perm)
        yk = _sc_gather(y, dest).reshape(TOP_K, t_part, HALF)
        out = _combine(h1f, yk, gates, ln2_g[0:1], ln2_b[0:1], out, b0 * seq, t_total)
        b0 += pb
    return out.reshape(batch, seq, d)
```

```python
import functools
import math

import jax
import jax.numpy as jnp
from jax import lax
from jax.experimental import pallas as pl
from jax.experimental.pallas import tpu as pltpu
from jax.experimental.pallas import tpu_sc as plsc

D_MODEL = 1024
CHUNK = 64
SUB = 16
N_SUB = CHUNK // SUB
MAX_SUB_DECAY = 86.0
HGRN_WIDTH = 512
HGRN_HEADS = 4
HEAD_DIM = 128
GMLP_WIDTH = 512
GMLP_BLOCK = 128
GMLP_GROUPS = 4
IN_COLS = 3072
N_EXPERTS = 32
TOP_K = 4
SWIGLU_LIMIT = 7.0
SWIGLU_ALPHA = 1.702
ALPHA = 2.0 ** 0.25
EPS = 1e-5
HALF = D_MODEL // 2

TB = 512
TM = 1024
ROW_STEP = 256
PLAN_CHUNK = 2048
TC3 = 512
LAST_PART_NUM, LAST_PART_DEN = 1, 4
SC_WINDOW = 128
PROJ_GROUP = 256
PERM_BLOCK = 256
VMEM_LIMIT = 56 * 1024 * 1024

_NT = (((1,), (1,)), ((), ()))


def _gelu(x):
    return 0.5 * x * (1.0 + lax.erf(x * (1.0 / math.sqrt(2.0))))


def _layer_norm(x, g, b):
    mu = jnp.mean(x, axis=-1, keepdims=True)
    xc = x - mu
    var = jnp.mean(xc * xc, axis=-1, keepdims=True)
    return xc * lax.rsqrt(var + EPS) * g + b


def _pack_rounded(hr):
    au = lax.bitcast_convert_type(hr[:, :HALF], jnp.uint32) >> 16
    bu = lax.bitcast_convert_type(hr[:, HALF:], jnp.uint32) & jnp.uint32(0xFFFF0000)
    return lax.bitcast_convert_type(au | bu, jnp.int32)


def _pack_rows(h):
    return _pack_rounded(h.astype(jnp.bfloat16).astype(jnp.float32))


def _unpack_rows(w):
    u = lax.bitcast_convert_type(w, jnp.uint32)
    a = lax.bitcast_convert_type(u << 16, jnp.float32)
    b = lax.bitcast_convert_type(u & jnp.uint32(0xFFFF0000), jnp.float32)
    return a, b


def _mixer_kernel(x_ref, xn_ref, win_ref, lb_ref, hg_ref, lng_ref, lnb_ref, wm_ref, bst_ref,
                  wout_ref, l1g_ref, l1b_ref, rwh_ref, rwl_ref, rb_ref, tri_ref, upp_ref,
                  h1f_ref, h1p_ref, idx_ref, gate_ref, rank_ref, cnt_ref,
                  p_scr, lf_scr, kk_scr, g_scr, kc_scr, o_scr, st_scr, carry_scr,
                  amat_scr, qg_scr, kd_scr, upd_scr, mix_scr):
    b = pl.program_id(0)
    t = pl.program_id(1)
    step = b * pl.num_programs(1) + t
    f32 = jnp.float32
    bf16 = jnp.bfloat16

    @pl.when(t == 0)
    def _():
        st_scr[...] = jnp.zeros_like(st_scr)

    @pl.when((b == 0) & (t == 0))
    def _():
        carry_scr[...] = jnp.zeros_like(carry_scr)

    def project(xb, group):
        cols = slice(group * PROJ_GROUP, (group + 1) * PROJ_GROUP)
        p_scr[:, cols] = jnp.dot(xb, win_ref[:, cols], preferred_element_type=f32)

    def hgrn_gates():
        fl = p_scr[:, HGRN_WIDTH:2 * HGRN_WIDTH]
        z = jnp.exp(-jnp.abs(fl))
        r = 1.0 / (1.0 + z)
        k = (1.0 - lb_ref[...]) * jnp.where(fl >= 0, z * r, r)
        kk_scr[...] = k
        lf_scr[...] = jnp.log2(1.0 - k)

    @pl.when(step == 0)
    def _():
        xb0 = x_ref[...].astype(bf16)
        for group in range(IN_COLS // PROJ_GROUP):
            project(xb0, group)
        hgrn_gates()

    tri = tri_ref[...]
    row64 = lax.broadcasted_iota(jnp.int32, (CHUNK, CHUNK), 0)
    col64 = lax.broadcasted_iota(jnp.int32, (CHUNK, CHUNK), 1)
    lane_sub = lax.broadcasted_iota(jnp.int32, (SUB, CHUNK), 1)
    scale = HEAD_DIM ** -0.5

    def pad_rows(piece, lo_r):
        parts = []
        if lo_r > 0:
            parts.append(jnp.zeros((lo_r, HEAD_DIM), f32))
        parts.append(piece)
        rest = CHUNK - lo_r - piece.shape[0]
        if rest > 0:
            parts.append(jnp.zeros((rest, HEAD_DIM), f32))
        return jnp.concatenate(parts, axis=0) if len(parts) > 1 else piece

    def intra_factorised(q, k, gh):
        qts, kts = [], []
        for a in range(N_SUB):
            lo_r, hi_r = a * SUB, (a + 1) * SUB
            if a == 0:
                qa = q[:hi_r] * jnp.exp2(gh[:hi_r])
                ka = k[:hi_r] * jnp.exp2(-gh[:hi_r])
            else:
                ra = gh[lo_r - 1:lo_r]
                qa = q[lo_r:hi_r] * jnp.exp2(gh[lo_r:hi_r] - ra)
                ka = k[:hi_r] * jnp.exp2(ra - gh[:hi_r])
            qts.append(pad_rows(qa, lo_r))
            kts.append(pad_rows(ka, 0))
        a_mat = lax.dot_general(jnp.concatenate(qts, axis=1).astype(bf16),
                                jnp.concatenate(kts, axis=1).astype(bf16), _NT,
                                preferred_element_type=f32)
        return jnp.where(row64 >= col64, a_mat, 0.0)

    def chunks_factorised():
        n_chunks = TB // CHUNK
        heads = [(c, h) for c in range(n_chunks) for h in range(HGRN_HEADS)]

        def cols(h, base=0):
            return slice(base + h * HEAD_DIM, base + (h + 1) * HEAD_DIM)

        for c in range(n_chunks):
            rows = slice(c * CHUNK, (c + 1) * CHUNK)
            lf = lf_scr[rows, :]
            hi = lf.astype(bf16)
            lo = (lf - hi.astype(f32)).astype(bf16)
            gg = jnp.dot(tri, jnp.concatenate([hi, lo], axis=1), preferred_element_type=f32)
            lf_scr[rows, :] = gg[:, :HGRN_WIDTH] + gg[:, HGRN_WIDTH:]
        for c, h in heads:
            rows = slice(c * CHUNK, (c + 1) * CHUNK)
            q = p_scr[rows, cols(h)]
            k = kk_scr[rows, cols(h)]
            gh = lf_scr[rows, cols(h)]
            amat_scr[c * HGRN_HEADS + h] = intra_factorised(q, k, gh).astype(bf16)
            qg_scr[rows, cols(h)] = (q * jnp.exp2(gh)).astype(bf16)
            kd_scr[rows, cols(h)] = (k * jnp.exp2(gh[CHUNK - 1:CHUNK] - gh)).astype(bf16)
        for c, h in heads:
            rows = slice(c * CHUNK, (c + 1) * CHUNK)
            v = p_scr[rows, cols(h, 2 * HGRN_WIDTH)]
            o_scr[rows, cols(h)] = jnp.dot(amat_scr[c * HGRN_HEADS + h], v.astype(bf16),
                                           preferred_element_type=f32)
            upd_scr[c * HGRN_HEADS + h] = jnp.dot(v.T.astype(bf16), kd_scr[rows, cols(h)],
                                                  preferred_element_type=f32)
        states = [st_scr[h] for h in range(HGRN_HEADS)]
        for c, h in heads:
            rows = slice(c * CHUNK, (c + 1) * CHUNK)
            st = states[h]
            o_inter = lax.dot_general(qg_scr[rows, cols(h)], st.astype(bf16), _NT,
                                      preferred_element_type=f32)
            o_scr[rows, cols(h)] = (o_scr[rows, cols(h)] + o_inter) * scale
            gl = lf_scr[(c + 1) * CHUNK - 1:(c + 1) * CHUNK, cols(h)]
            states[h] = st * jnp.exp2(gl) + upd_scr[c * HGRN_HEADS + h]
        for h in range(HGRN_HEADS):
            st_scr[h] = states[h]

    def intra_exact_diagonal(q, k, gh, cs):
        qts, kts = [], []
        for a in range(1, N_SUB):
            lo_r, hi_r = a * SUB, (a + 1) * SUB
            ra = g_scr[lo_r - 1:lo_r, cs]
            qts.append(pad_rows(q[lo_r:hi_r] * jnp.exp2(gh[lo_r:hi_r] - ra), lo_r))
            kts.append(pad_rows(k[:lo_r] * jnp.exp2(ra - gh[:lo_r]), 0))
        a_off = lax.dot_general(jnp.concatenate(qts, axis=1).astype(bf16),
                                jnp.concatenate(kts, axis=1).astype(bf16), _NT,
                                preferred_element_type=f32)
        diag_rows = []
        for a in range(N_SUB):
            lo_r = a * SUB
            gs = gh[lo_r:lo_r + SUB]
            qs = q[lo_r:lo_r + SUB]
            blk = jnp.zeros((SUB, CHUNK), f32)
            for jl in range(SUB):
                j = lo_r + jl
                gj = g_scr[j:j + 1, cs]
                kj = kc_scr[j:j + 1, cs]
                e = jnp.exp2(jnp.minimum(gs - gj, 0.0))
                col = jnp.sum(qs * (kj * e), axis=-1, keepdims=True)
                blk = jnp.where(lane_sub == j, col, blk)
            diag_rows.append(blk)
        a_diag = jnp.concatenate(diag_rows, axis=0)
        return a_off + jnp.where(row64 >= col64, a_diag, 0.0)

    def make_chunk_body(intra):
        def chunk_body(c, carry):
            r0 = pl.multiple_of(c * CHUNK, CHUNK)
            rows = pl.ds(r0, CHUNK)
            lf = lf_scr[rows, :]
            hi = lf.astype(bf16)
            lo = (lf - hi.astype(f32)).astype(bf16)
            gg = jnp.dot(tri, jnp.concatenate([hi, lo], axis=1), preferred_element_type=f32)
            g_all = gg[:, :HGRN_WIDTH] + gg[:, HGRN_WIDTH:]
            g_scr[...] = g_all
            kc_scr[...] = kk_scr[rows, :]
            for h in range(HGRN_HEADS):
                cs = slice(h * HEAD_DIM, (h + 1) * HEAD_DIM)
                q = p_scr[rows, h * HEAD_DIM:(h + 1) * HEAD_DIM]
                v = p_scr[rows, 2 * HGRN_WIDTH + h * HEAD_DIM:2 * HGRN_WIDTH + (h + 1) * HEAD_DIM]
                k = kc_scr[:, cs]
                gh = g_all[:, cs]
                st = st_scr[h]
                o_inter = lax.dot_general((q * jnp.exp2(gh)).astype(bf16), st.astype(bf16), _NT,
                                          preferred_element_type=f32)
                a_mat = intra(q, k, gh, cs)
                o = jnp.dot(a_mat.astype(bf16), v.astype(bf16), preferred_element_type=f32) + o_inter
                o_scr[rows, h * HEAD_DIM:(h + 1) * HEAD_DIM] = o * scale

                gl = g_scr[CHUNK - 1:CHUNK, cs]
                kd = k * jnp.exp2(gl - gh)
                upd = jnp.dot(v.T.astype(bf16), kd.astype(bf16), preferred_element_type=f32)
                st_scr[h] = st * jnp.exp2(gl) + upd
            return carry
        return chunk_body

    lf_all = lf_scr[...]
    sub_decay = -jnp.sum(lf_all.reshape(TB // SUB, SUB, HGRN_WIDTH), axis=1)
    bounded = jnp.max(sub_decay) <= MAX_SUB_DECAY

    f_groups = list(range(HGRN_WIDTH // PROJ_GROUP, 2 * HGRN_WIDTH // PROJ_GROUP))
    qi_groups = [g for g in range(3 * HGRN_WIDTH // PROJ_GROUP) if g not in f_groups]
    rest_groups = list(range(3 * HGRN_WIDTH // PROJ_GROUP, IN_COLS // PROJ_GROUP))

    def gmlp_stage(xnb):
        for g in f_groups:
            project(xnb, g)
        for w in range(TB // GMLP_BLOCK):
            rows = slice(w * GMLP_BLOCK, (w + 1) * GMLP_BLOCK)
            u = _gelu(p_scr[rows, 4 * HGRN_WIDTH:4 * HGRN_WIDTH + GMLP_WIDTH])
            vn = _layer_norm(_gelu(p_scr[rows, 4 * HGRN_WIDTH + GMLP_WIDTH:]),
                             lng_ref[...], lnb_ref[...])
            vnb = vn.astype(bf16)
            cols = []
            for g in range(GMLP_GROUPS):
                s = jnp.dot(wm_ref[g], vnb[:, g * HEAD_DIM:(g + 1) * HEAD_DIM],
                            preferred_element_type=f32) + bst_ref[:, g:g + 1]
                cols.append(s)
            mix_scr[rows, HGRN_WIDTH:] = (u * jnp.concatenate(cols, axis=1)).astype(bf16)

    def output_stage(xnb):
        for g in qi_groups:
            project(xnb, g)
        for w in range(TB // GMLP_BLOCK):
            rows = slice(w * GMLP_BLOCK, (w + 1) * GMLP_BLOCK)
            o = o_scr[rows, :]
            ms = jnp.mean(o * o, axis=-1, keepdims=True)
            gate = p_scr[rows, 3 * HGRN_WIDTH:4 * HGRN_WIDTH]
            half_gate = 0.5 * gate
            silu = half_gate * (jnp.tanh(half_gate) + 1.0)
            y_rec = o * lax.rsqrt(ms + EPS) * hg_ref[...] * silu
            mix_scr[rows, :HGRN_WIDTH] = y_rec.astype(bf16)

        mix = jnp.dot(mix_scr[...], wout_ref[...], preferred_element_type=f32)
        hgrn_gates()
        h1 = _layer_norm(ALPHA * x_ref[...] + mix, l1g_ref[...], l1b_ref[...])
        h1f_ref[...] = h1
        hh = h1.astype(bf16)
        h1_rounded = hh.astype(f32)
        h1p_ref[...] = _pack_rounded(h1_rounded)
        for g in rest_groups[:2]:
            project(xnb, g)

        hl = (h1 - h1_rounded).astype(bf16)
        rwh = rwh_ref[...]
        by_hh = lax.dot_general(jnp.concatenate([rwh, rwl_ref[...]], axis=0), hh, _NT,
                                preferred_element_type=f32)
        logits = (by_hh[:N_EXPERTS] + by_hh[N_EXPERTS:]
                  + lax.dot_general(rwh, hl, _NT, preferred_element_type=f32) + rb_ref[...])
        for g in rest_groups[2:]:
            project(xnb, g)

        e_iota = lax.broadcasted_iota(jnp.int32, (N_EXPERTS, TB), 0)
        work = logits
        vals, idxs, hots = [], [], []
        for _ in range(TOP_K):
            m = jnp.max(work, axis=0, keepdims=True)
            ik = jnp.min(jnp.where(work == m, e_iota, N_EXPERTS), axis=0, keepdims=True)
            hot = e_iota == ik
            work = jnp.where(hot, -jnp.inf, work)
            vals.append(m)
            idxs.append(ik)
            hots.append(hot)
        exps = [jnp.exp(vk - vals[0]) for vk in vals]
        denom = exps[0] + exps[1] + exps[2] + exps[3]
        gate_ref[...] = jnp.concatenate([ek / denom for ek in exps], axis=0)
        idx_ref[...] = jnp.concatenate(idxs, axis=0)

        hot_any = jnp.where(hots[0] | hots[1] | hots[2] | hots[3], 1.0, 0.0)
        prefix = jnp.dot(hot_any.astype(bf16), upp_ref[...], preferred_element_type=f32)
        base = prefix + carry_scr[:, 0:1]
        ranks = [jnp.sum(jnp.where(hk, base, 0.0), axis=0, keepdims=True) for hk in hots]
        rank_ref[...] = jnp.concatenate(ranks, axis=0).astype(jnp.int32)
        new_carry = carry_scr[...] + jnp.sum(hot_any, axis=1, keepdims=True)
        carry_scr[...] = new_carry
        cnt_ref[...] = new_carry

    @pl.when(bounded)
    def _():
        xnb = xn_ref[...].astype(bf16)
        gmlp_stage(xnb)
        chunks_factorised()
        output_stage(xnb)

    @pl.when(jnp.logical_not(bounded))
    def _():
        xnb = xn_ref[...].astype(bf16)
        lax.fori_loop(0, TB // CHUNK, make_chunk_body(intra_exact_diagonal), 0)
        gmlp_stage(xnb)
        output_stage(xnb)


def _mixer(xt, win, lb, hg, lng, lnb, wm, bst, wout, l1g, l1b, rwh, rwl, rb, tri, upp,
           seq, batch, batch0):
    nt = seq // TB
    t_total = batch * seq
    nblk = batch * nt
    const2 = lambda b, t: (0, 0)
    const3 = lambda b, t: (0, 0, 0)
    row_blk = lambda b, t: (b * nt + t, 0)
    meta_blk = lambda b, t: (0, b * nt + t)
    once = dict(pipeline_mode=pl.Buffered(1))
    in_specs = [
        pl.BlockSpec((TB, D_MODEL), lambda b, t: ((b + batch0) * nt + t, 0)),
        pl.BlockSpec((TB, D_MODEL),
                     lambda b, t: (batch0 * nt + jnp.minimum(b * nt + t + 1, nblk - 1), 0)),
        pl.BlockSpec((D_MODEL, IN_COLS), const2, **once),
        pl.BlockSpec((1, HGRN_WIDTH), const2),
        pl.BlockSpec((1, HGRN_WIDTH), const2),
        pl.BlockSpec((1, GMLP_WIDTH), const2),
        pl.BlockSpec((1, GMLP_WIDTH), const2),
        pl.BlockSpec((GMLP_GROUPS, GMLP_BLOCK, GMLP_BLOCK), const3),
        pl.BlockSpec((GMLP_BLOCK, GMLP_GROUPS), const2),
        pl.BlockSpec((D_MODEL, D_MODEL), const2, **once),
        pl.BlockSpec((1, D_MODEL), const2),
        pl.BlockSpec((1, D_MODEL), const2),
        pl.BlockSpec((N_EXPERTS, D_MODEL), const2),
        pl.BlockSpec((N_EXPERTS, D_MODEL), const2),
        pl.BlockSpec((N_EXPERTS, 1), const2),
        pl.BlockSpec((CHUNK, CHUNK), const2),
        pl.BlockSpec((TB, TB), const2, **once),
    ]
    out_shape = [
        jax.ShapeDtypeStruct((t_total, D_MODEL), jnp.float32),
        jax.ShapeDtypeStruct((t_total, HALF), jnp.int32),
        jax.ShapeDtypeStruct((TOP_K, t_total), jnp.int32),
        jax.ShapeDtypeStruct((TOP_K, t_total), jnp.float32),
        jax.ShapeDtypeStruct((TOP_K, t_total), jnp.int32),
        jax.ShapeDtypeStruct((N_EXPERTS, 128), jnp.float32),
    ]
    out_specs = [
        pl.BlockSpec((TB, D_MODEL), row_blk),
        pl.BlockSpec((TB, HALF), row_blk),
        pl.BlockSpec((TOP_K, TB), meta_blk),
        pl.BlockSpec((TOP_K, TB), meta_blk),
        pl.BlockSpec((TOP_K, TB), meta_blk),
        pl.BlockSpec((N_EXPERTS, 128), const2),
    ]
    scratch = [
        pltpu.VMEM((TB, IN_COLS), jnp.float32),
        pltpu.VMEM((TB, HGRN_WIDTH), jnp.float32),
        pltpu.VMEM((TB, HGRN_WIDTH), jnp.float32),
        pltpu.VMEM((CHUNK, HGRN_WIDTH), jnp.float32),
        pltpu.VMEM((CHUNK, HGRN_WIDTH), jnp.float32),
        pltpu.VMEM((TB, HGRN_WIDTH), jnp.float32),
        pltpu.VMEM((HGRN_HEADS, HEAD_DIM, HEAD_DIM), jnp.float32),
        pltpu.VMEM((N_EXPERTS, 128), jnp.float32),
        pltpu.VMEM((TB // CHUNK * HGRN_HEADS, CHUNK, CHUNK), jnp.bfloat16),
        pltpu.VMEM((TB, HGRN_WIDTH), jnp.bfloat16),
        pltpu.VMEM((TB, HGRN_WIDTH), jnp.bfloat16),
        pltpu.VMEM((TB // CHUNK * HGRN_HEADS, HEAD_DIM, HEAD_DIM), jnp.float32),
        pltpu.VMEM((TB, D_MODEL), jnp.bfloat16),
    ]
    return pl.pallas_call(
        _mixer_kernel,
        grid=(batch, nt),
        in_specs=in_specs,
        out_specs=out_specs,
        out_shape=out_shape,
        scratch_shapes=scratch,
        compiler_params=pltpu.CompilerParams(
            dimension_semantics=("arbitrary", "arbitrary"),
            vmem_limit_bytes=VMEM_LIMIT),
        name="mixer",
    )(xt, xt, win, lb, hg, lng, lnb, wm, bst, wout, l1g, l1b, rwh, rwl, rb, tri, upp)


def _plan_kernel(cnt_ref, idx_ref, rank_ref, dest_ref, tiles_ref):
    f32 = jnp.float32
    n_e = N_EXPERTS
    e_sub = lax.broadcasted_iota(jnp.int32, (n_e, n_e), 0)
    e_lane = lax.broadcasted_iota(jnp.int32, (n_e, n_e), 1)
    counts = cnt_ref[:, 0:1]
    padded = jnp.floor((counts + (TM - 1)) * (1.0 / TM)) * TM
    as_row = lambda col: jnp.sum(jnp.where(e_sub == e_lane, col, 0.0), axis=0, keepdims=True)
    padded_row = as_row(padded)
    counts_row = as_row(counts)
    ends = jnp.sum(jnp.where(e_lane <= e_sub, padded_row, 0.0), axis=1, keepdims=True)
    starts = ends - padded
    owns_later = (e_lane > e_sub) & (counts_row > 0.0)
    nxt = jnp.min(jnp.where(owns_later, e_lane, n_e), axis=1, keepdims=True)
    own = lax.broadcasted_iota(jnp.int32, (n_e, 1), 0)
    nxt = jnp.where(nxt == n_e, own, nxt).astype(f32)

    n_lanes = tiles_ref.shape[1]
    tile_row = lax.broadcasted_iota(jnp.int32, (n_e, n_lanes), 1).astype(f32) * TM
    e_of = lax.broadcasted_iota(jnp.int32, (n_e, n_lanes), 0)
    tile_e = jnp.minimum(jnp.sum(jnp.where(tile_row >= ends, 1, 0), axis=0, keepdims=True), n_e - 1)
    mine = e_of == tile_e
    pick = lambda col: jnp.sum(jnp.where(mine, col, 0.0), axis=0, keepdims=True)
    valid = jnp.clip(pick(starts + counts) - tile_row[0:1], 0.0, float(TM))
    rows = [tile_e, valid.astype(jnp.int32), pick(nxt).astype(jnp.int32)]
    rows.append(jnp.zeros((tiles_ref.shape[0] - len(rows), n_lanes), jnp.int32))
    tiles_ref[...] = jnp.concatenate(rows, axis=0)

    chunk = PLAN_CHUNK
    e_chunk = lax.broadcasted_iota(jnp.int32, (n_e, chunk), 0)

    def body(c, carry):
        lanes = pl.ds(pl.multiple_of(c * chunk, chunk), chunk)
        for k in range(TOP_K):
            hit = e_chunk == idx_ref[k:k + 1, lanes]
            start_of = jnp.sum(jnp.where(hit, starts, 0.0), axis=0, keepdims=True)
            dest_ref[k:k + 1, lanes] = rank_ref[k:k + 1, lanes] + start_of.astype(jnp.int32)
        return carry

    lax.fori_loop(0, idx_ref.shape[1] // chunk, body, 0)


def _plan(cnt, idx, rank, n_tiles):
    t_part = idx.shape[1]
    n_lanes = -(-n_tiles // 128) * 128
    full = lambda shape: pl.BlockSpec(shape, lambda i: (0,) * len(shape))
    dest, tiles = pl.pallas_call(
        _plan_kernel,
        grid=(1,),
        in_specs=[full(cnt.shape), full(idx.shape), full(rank.shape)],
        out_specs=[full(idx.shape), full((8, n_lanes))],
        out_shape=[jax.ShapeDtypeStruct((TOP_K, t_part), jnp.int32),
                   jax.ShapeDtypeStruct((8, n_lanes), jnp.int32)],
        compiler_params=pltpu.CompilerParams(
            dimension_semantics=("arbitrary",), vmem_limit_bytes=VMEM_LIMIT),
        name="plan",
    )(cnt, idx, rank)
    return dest, tiles[0, :n_tiles], tiles[1, :n_tiles], tiles[2, :n_tiles]


def _sc_workers():
    info = plsc.get_sparse_core_info()
    return info.num_cores, info.num_cores * info.num_subcores


def _sc_dispatch(rows, dest, n_out):
    t_total, dw = rows.shape
    nc, nw = _sc_workers()
    per_w = t_total // nw
    mesh = plsc.VectorSubcoreMesh(core_axis_name="c", subcore_axis_name="s")

    @functools.partial(
        pl.kernel,
        out_type=jax.ShapeDtypeStruct((n_out, dw), rows.dtype),
        mesh=mesh,
        scratch_types=[pltpu.VMEM((SC_WINDOW,), jnp.int32) for _ in range(TOP_K)]
        + [pltpu.VMEM((SC_WINDOW, dw), rows.dtype), pltpu.SemaphoreType.DMA],
        name="sc_dispatch",
    )
    def k(x_hbm, i_hbm, o_hbm, i0, i1, i2, i3, rows_v, sem):
        wid = lax.axis_index("s") * nc + lax.axis_index("c")
        idx_bufs = (i0, i1, i2, i3)

        @pl.loop(0, per_w // SC_WINDOW)
        def _(j):
            base = wid * per_w + j * SC_WINDOW
            for kk in range(TOP_K):
                pltpu.sync_copy(i_hbm.at[kk, pl.ds(base, SC_WINDOW)], idx_bufs[kk])
            pltpu.sync_copy(x_hbm.at[pl.ds(base, SC_WINDOW)], rows_v)
            copies = [pltpu.async_copy(rows_v, o_hbm.at[idx_bufs[kk]], sem) for kk in range(TOP_K)]
            for cp in copies:
                cp.wait()

    return k(rows, dest)


def _sc_gather(table, idx):
    n_k, t_total = idx.shape
    n = n_k * t_total
    dw = table.shape[1]
    nc, nw = _sc_workers()
    per_w = n // nw
    w_per_k = nw // n_k
    mesh = plsc.VectorSubcoreMesh(core_axis_name="c", subcore_axis_name="s")

    @functools.partial(
        pl.kernel,
        out_type=jax.ShapeDtypeStruct((n, dw), table.dtype),
        mesh=mesh,
        scratch_types=[pltpu.VMEM((SC_WINDOW,), jnp.int32),
                       pltpu.VMEM((SC_WINDOW, dw), table.dtype),
                       pltpu.SemaphoreType.DMA],
        name="sc_gather",
    )
    def k(t_hbm, i_hbm, o_hbm, idx_v, rows_v, sem):
        wid = lax.axis_index("s") * nc + lax.axis_index("c")

        row = wid // w_per_k
        col0 = (wid % w_per_k) * per_w

        @pl.loop(0, per_w // SC_WINDOW)
        def _(j):
            col = col0 + j * SC_WINDOW
            pltpu.sync_copy(i_hbm.at[row, pl.ds(col, SC_WINDOW)], idx_v)
            pltpu.async_copy(t_hbm.at[idx_v], rows_v, sem).wait()
            pltpu.sync_copy(rows_v, o_hbm.at[pl.ds(row * t_total + col, SC_WINDOW)])

    return k(table, idx)


def _expert_kernel(te_ref, tv_ref, tn_ref, xs_ref, w1_hbm, w2_hbm, b1g_ref, b1l_ref, b2_ref,
                   perm_ref, y_ref, w1f_scr, w2f_scr, w1g_scr, w1l_scr, w2_scr, sems):
    i = pl.program_id(0)
    valid = tv_ref[i]
    expert = te_ref[i]
    f32 = jnp.float32
    bf16 = jnp.bfloat16
    expert_changed = (i == 0) | (expert != te_ref[jnp.maximum(i - 1, 0)])

    def weight_copies(e):
        return (pltpu.make_async_copy(w1_hbm.at[e], w1f_scr, sems.at[0]),
                pltpu.make_async_copy(w2_hbm.at[e], w2f_scr, sems.at[1]))

    @pl.when(valid == 0)
    def _():
        y_ref[...] = jnp.zeros_like(y_ref)

    @pl.when(i == 0)
    def _():
        for cp in weight_copies(expert):
            cp.start()

    @pl.when((valid > 0) & expert_changed)
    def _():
        for cp in weight_copies(expert):
            cp.wait()
        w2_scr[...] = w2f_scr[...].astype(bf16)
        perm = perm_ref[...]
        for c in range(2 * D_MODEL // PERM_BLOCK):
            blk = w1f_scr[:, c * PERM_BLOCK:(c + 1) * PERM_BLOCK].astype(bf16)
            r = jnp.dot(blk, perm, preferred_element_type=f32).astype(bf16)
            half = PERM_BLOCK // 2
            w1g_scr[:, c * half:(c + 1) * half] = r[:, :half]
            w1l_scr[:, c * half:(c + 1) * half] = r[:, half:]

        @pl.when(tn_ref[i] != expert)
        def _():
            for cp in weight_copies(tn_ref[i]):
                cp.start()

    def expert_rows(n_rows):
        keep = lax.broadcasted_iota(jnp.int32, (n_rows, 1), 0) < valid
        a, b = _unpack_rows(jnp.where(keep, xs_ref[:n_rows, :], 0))
        x = jnp.concatenate([a, b], axis=1).astype(bf16)
        hg = jnp.dot(x, w1g_scr[...], preferred_element_type=f32) + b1g_ref[0]
        xl1 = jnp.clip(jnp.dot(x, w1l_scr[...], preferred_element_type=f32) + (b1l_ref[0] + 1.0),
                       1.0 - SWIGLU_LIMIT, 1.0 + SWIGLU_LIMIT)
        xg = jnp.minimum(hg, SWIGLU_LIMIT)
        act = (0.5 * xg) * (jnp.tanh((0.5 * SWIGLU_ALPHA) * xg) + 1.0) * xl1
        y = jnp.dot(act.astype(bf16), w2_scr[...], preferred_element_type=f32) + b2_ref[0]
        y_ref[:n_rows, :] = _pack_rows(y)
        if n_rows < TM:
            y_ref[n_rows:, :] = jnp.zeros((TM - n_rows, HALF), jnp.int32)

    for blocks in range(TM // ROW_STEP, 0, -1):
        @pl.when((valid > (blocks - 1) * ROW_STEP) & (valid <= blocks * ROW_STEP))
        def _(blocks=blocks):
            expert_rows(blocks * ROW_STEP)


def _experts(xs, tile_expert, tile_valid, tile_next, w1, w2, b1g, b1l, b2, perm):
    n_slots = xs.shape[0]
    n_tiles = n_slots // TM
    d_ff = w2.shape[1]
    wmap = lambda i, te, tv, tn: (te[i], 0, 0)
    grid_spec = pltpu.PrefetchScalarGridSpec(
        num_scalar_prefetch=3,
        grid=(n_tiles,),
        in_specs=[
            pl.BlockSpec((TM, HALF), lambda i, te, tv, tn: (i, 0)),
            pl.BlockSpec(memory_space=pl.ANY),
            pl.BlockSpec(memory_space=pl.ANY),
            pl.BlockSpec((1, 1, d_ff), wmap),
            pl.BlockSpec((1, 1, d_ff), wmap),
            pl.BlockSpec((1, 1, D_MODEL), wmap),
            pl.BlockSpec((PERM_BLOCK, PERM_BLOCK), lambda i, te, tv, tn: (0, 0)),
        ],
        out_specs=pl.BlockSpec((TM, HALF), lambda i, te, tv, tn: (i, 0)),
        scratch_shapes=[
            pltpu.VMEM((D_MODEL, 2 * d_ff), jnp.float32),
            pltpu.VMEM((d_ff, D_MODEL), jnp.float32),
            pltpu.VMEM((D_MODEL, d_ff), jnp.bfloat16),
            pltpu.VMEM((D_MODEL, d_ff), jnp.bfloat16),
            pltpu.VMEM((d_ff, D_MODEL), jnp.bfloat16),
            pltpu.SemaphoreType.DMA((2,)),
        ],
    )
    return pl.pallas_call(
        _expert_kernel,
        grid_spec=grid_spec,
        out_shape=jax.ShapeDtypeStruct((n_slots, HALF), jnp.int32),
        compiler_params=pltpu.CompilerParams(
            dimension_semantics=("arbitrary",),
            vmem_limit_bytes=VMEM_LIMIT),
        name="experts",
    )(tile_expert, tile_valid, tile_next, xs, w1, w2, b1g, b1l, b2, perm)


def _combine_kernel(h_ref, yk_ref, gate_ref, g_ref, b_ref, *rest):
    o_ref = rest[-1]
    gates = gate_ref[...].T
    acc_a = None
    acc_b = None
    for k in range(TOP_K):
        a, b = _unpack_rows(yk_ref[k])
        gk = gates[:, k:k + 1]
        acc_a = gk * a if acc_a is None else acc_a + gk * a
        acc_b = gk * b if acc_b is None else acc_b + gk * b
    ffn = jnp.concatenate([acc_a, acc_b], axis=1)
    o_ref[...] = _layer_norm(ALPHA * h_ref[...] + ffn, g_ref[...], b_ref[...])


def _combine(h1f, yk, gates, g2, b2, out_prev, row0, t_all):
    t_part = h1f.shape[0]
    blk0 = row0 // TC3
    row = lambda i: (i, 0)
    const = lambda i: (0, 0)
    in_specs = [
        pl.BlockSpec((TC3, D_MODEL), row),
        pl.BlockSpec((TOP_K, TC3, HALF), lambda i: (0, i, 0)),
        pl.BlockSpec((TOP_K, TC3), lambda i: (0, i)),
        pl.BlockSpec((1, D_MODEL), const),
        pl.BlockSpec((1, D_MODEL), const),
    ]
    args = [h1f, yk, gates, g2, b2]
    aliases = {}
    if out_prev is not None:
        in_specs.append(pl.BlockSpec(memory_space=pl.ANY))
        args.append(out_prev)
        aliases = {len(args) - 1: 0}
    return pl.pallas_call(
        _combine_kernel,
        grid=(t_part // TC3,),
        in_specs=in_specs,
        out_specs=pl.BlockSpec((TC3, D_MODEL), lambda i: (i + blk0, 0)),
        out_shape=jax.ShapeDtypeStruct((t_all, D_MODEL), jnp.float32),
        input_output_aliases=aliases,
        compiler_params=pltpu.CompilerParams(
            dimension_semantics=("arbitrary",),
            vmem_limit_bytes=VMEM_LIMIT),
        name="combine",
    )(*args)


def kernel(x, w_in, lb_logits, hgrn_norm_g, gmlp_ln_g, gmlp_ln_b, gmlp_ws, gmlp_bs, w_out, ln1_g, ln1_b, router_w, router_b, exp_w1, exp_b1, exp_w2, exp_b2, ln2_g, ln2_b):
    batch, seq, d = x.shape
    assert d == D_MODEL and seq % TB == 0 and w_in.shape[0] == 1
    t_total = batch * seq
    f32 = jnp.float32
    bf16 = jnp.bfloat16

    lb = jnp.cumsum(jax.nn.softmax(lb_logits.astype(f32), axis=0), axis=0)[0:1]
    chunk_id = jnp.arange(GMLP_BLOCK) // CHUNK
    wm = jnp.where((chunk_id[None, :] <= chunk_id[:, None])[None], gmlp_ws[0], 0.0).astype(bf16)
    rwt = router_w[0].T
    rwh = rwt.astype(bf16)
    rwl = (rwt - rwh.astype(f32)).astype(bf16)
    tri = (jnp.arange(CHUNK)[None, :] <= jnp.arange(CHUNK)[:, None]).astype(bf16)
    upp = (jnp.arange(TB)[:, None] < jnp.arange(TB)[None, :]).astype(bf16)

    lane = jnp.arange(PERM_BLOCK)
    src = jnp.where(lane < PERM_BLOCK // 2, 2 * lane, 2 * (lane - PERM_BLOCK // 2) + 1)
    perm = (jnp.arange(PERM_BLOCK)[:, None] == src[None, :]).astype(bf16)
    xt = x.reshape(t_total, d)
    win = w_in[0].astype(bf16)
    wout = w_out[0].astype(bf16)
    b1g, b1l, b2e = exp_b1[0][:, None, 0::2], exp_b1[0][:, None, 1::2], exp_b2[0][:, None, :]

    last = max(batch * LAST_PART_NUM // LAST_PART_DEN, 1) if batch > 1 else 0
    part_sizes = [pb for pb in (batch - last, last) if pb > 0]
    out = None
    b0 = 0
    for pb in part_sizes:
        t_part = pb * seq
        n_slots = t_part * TOP_K + N_EXPERTS * TM
        n_tiles = n_slots // TM
        h1f, h1p, idx, gates, rank, cnt = _mixer(
            xt, win, lb, hgrn_norm_g[0:1], gmlp_ln_g[0:1], gmlp_ln_b[0:1], wm, gmlp_bs[0].T, wout,
            ln1_g[0:1], ln1_b[0:1], rwh, rwl, router_b[0][:, None], tri, upp, seq, pb, b0)

        dest, tile_expert, tile_valid, tile_next = _plan(cnt, idx, rank, n_tiles)
        xs = _sc_dispatch(h1p, dest, n_slots)
        y = _experts(xs, tile_expert, tile_valid, tile_next,
                     exp_w1[0], exp_w2[0], b1g, b1l, b2e, perm)
        yk = _sc_gather(y, dest).reshape(TOP_K, t_part, HALF)
        out = _combine(h1f, yk, gates, ln2_g[0:1], ln2_b[0:1], out, b0 * seq, t_total)
        b0 += pb
    return out.reshape(batch, seq, d)
```

```python
import functools
import math

import jax
import jax.numpy as jnp
from jax import lax
from jax.experimental import pallas as pl
from jax.experimental.pallas import tpu as pltpu
from jax.experimental.pallas import tpu_sc as plsc

D_MODEL = 1024
CHUNK = 64
SUB = 16
N_SUB = CHUNK // SUB
MAX_SUB_DECAY = 86.0
HGRN_WIDTH = 512
HGRN_HEADS = 4
HEAD_DIM = 128
GMLP_WIDTH = 512
GMLP_BLOCK = 128
GMLP_GROUPS = 4
IN_COLS = 3072
N_EXPERTS = 32
TOP_K = 4
SWIGLU_LIMIT = 7.0
SWIGLU_ALPHA = 1.702
ALPHA = 2.0 ** 0.25
EPS = 1e-5
HALF = D_MODEL // 2

TB = 512
TM = 1024
ROW_STEP = 256
PLAN_CHUNK = 2048
TC3 = 512
LAST_PART_NUM, LAST_PART_DEN = 1, 4
SC_WINDOW = 128
PROJ_GROUP = 256
PERM_BLOCK = 256
LANES = 128
V7X_VMEM_BYTES = 64 * 1024 * 1024
VMEM_LIMIT = V7X_VMEM_BYTES * 7 // 8

_NT = (((1,), (1,)), ((), ()))


def _gelu(x):
    return 0.5 * x * (1.0 + lax.erf(x * (1.0 / math.sqrt(2.0))))


def _layer_norm(x, g, b):
    mu = jnp.mean(x, axis=-1, keepdims=True)
    xc = x - mu
    var = jnp.mean(xc * xc, axis=-1, keepdims=True)
    return xc * lax.rsqrt(var + EPS) * g + b


def _pack_rounded(hr):
    au = lax.bitcast_convert_type(hr[:, :HALF], jnp.uint32) >> 16
    bu = lax.bitcast_convert_type(hr[:, HALF:], jnp.uint32) & jnp.uint32(0xFFFF0000)
    return lax.bitcast_convert_type(au | bu, jnp.int32)


def _pack_rows(h):
    return _pack_rounded(h.astype(jnp.bfloat16).astype(jnp.float32))


def _unpack_rows(w):
    u = lax.bitcast_convert_type(w, jnp.uint32)
    a = lax.bitcast_convert_type(u << 16, jnp.float32)
    b = lax.bitcast_convert_type(u & jnp.uint32(0xFFFF0000), jnp.float32)
    return a, b


def _mixer_kernel(x_ref, xn_ref, win_ref, lb_ref, hg_ref, lng_ref, lnb_ref, wm_ref, bst_ref,
                  wout_ref, l1g_ref, l1b_ref, rwh_ref, rwl_ref, rb_ref, tri_ref, upp_ref,
                  h1f_ref, h1p_ref, idx_ref, gate_ref, rank_ref, cnt_ref,
                  p_scr, lf_scr, kk_scr, g_scr, kc_scr, o_scr, st_scr, carry_scr,
                  amat_scr, qg_scr, kd_scr, upd_scr, mix_scr):
    b = pl.program_id(0)
    t = pl.program_id(1)
    step = b * pl.num_programs(1) + t
    f32 = jnp.float32
    bf16 = jnp.bfloat16

    @pl.when(t == 0)
    def _():
        st_scr[...] = jnp.zeros_like(st_scr)

    @pl.when((b == 0) & (t == 0))
    def _():
        carry_scr[...] = jnp.zeros_like(carry_scr)

    def project(xb, group):
        cols = slice(group * PROJ_GROUP, (group + 1) * PROJ_GROUP)
        p_scr[:, cols] = jnp.dot(xb, win_ref[:, cols], preferred_element_type=f32)

    def hgrn_gates():
        fl = p_scr[:, HGRN_WIDTH:2 * HGRN_WIDTH]
        z = jnp.exp(-jnp.abs(fl))
        r = 1.0 / (1.0 + z)
        k = (1.0 - lb_ref[...]) * jnp.where(fl >= 0, z * r, r)
        kk_scr[...] = k
        lf_scr[...] = jnp.log2(1.0 - k)

    @pl.when(step == 0)
    def _():
        xb0 = x_ref[...].astype(bf16)
        for group in range(IN_COLS // PROJ_GROUP):
            project(xb0, group)
        hgrn_gates()

    tri = tri_ref[...]
    row64 = lax.broadcasted_iota(jnp.int32, (CHUNK, CHUNK), 0)
    col64 = lax.broadcasted_iota(jnp.int32, (CHUNK, CHUNK), 1)
    lane_sub = lax.broadcasted_iota(jnp.int32, (SUB, CHUNK), 1)
    scale = HEAD_DIM ** -0.5

    def pad_rows(piece, lo_r):
        parts = []
        if lo_r > 0:
            parts.append(jnp.zeros((lo_r, HEAD_DIM), f32))
        parts.append(piece)
        rest = CHUNK - lo_r - piece.shape[0]
        if rest > 0:
            parts.append(jnp.zeros((rest, HEAD_DIM), f32))
        return jnp.concatenate(parts, axis=0) if len(parts) > 1 else piece

    def intra_factorised(q, k, gh):
        qts, kts = [], []
        for a in range(N_SUB):
            lo_r, hi_r = a * SUB, (a + 1) * SUB
            if a == 0:
                qa = q[:hi_r] * jnp.exp2(gh[:hi_r])
                ka = k[:hi_r] * jnp.exp2(-gh[:hi_r])
            else:
                ra = gh[lo_r - 1:lo_r]
                qa = q[lo_r:hi_r] * jnp.exp2(gh[lo_r:hi_r] - ra)
                ka = k[:hi_r] * jnp.exp2(ra - gh[:hi_r])
            qts.append(pad_rows(qa, lo_r))
            kts.append(pad_rows(ka, 0))
        a_mat = lax.dot_general(jnp.concatenate(qts, axis=1).astype(bf16),
                                jnp.concatenate(kts, axis=1).astype(bf16), _NT,
                                preferred_element_type=f32)
        return jnp.where(row64 >= col64, a_mat, 0.0)

    def chunks_factorised():
        n_chunks = TB // CHUNK
        heads = [(c, h) for c in range(n_chunks) for h in range(HGRN_HEADS)]

        def cols(h, base=0):
            return slice(base + h * HEAD_DIM, base + (h + 1) * HEAD_DIM)

        for c in range(n_chunks):
            rows = slice(c * CHUNK, (c + 1) * CHUNK)
            lf = lf_scr[rows, :]
            hi = lf.astype(bf16)
            lo = (lf - hi.astype(f32)).astype(bf16)
            gg = jnp.dot(tri, jnp.concatenate([hi, lo], axis=1), preferred_element_type=f32)
            lf_scr[rows, :] = gg[:, :HGRN_WIDTH] + gg[:, HGRN_WIDTH:]
        for c, h in heads:
            rows = slice(c * CHUNK, (c + 1) * CHUNK)
            q = p_scr[rows, cols(h)]
            k = kk_scr[rows, cols(h)]
            gh = lf_scr[rows, cols(h)]
            amat_scr[c * HGRN_HEADS + h] = intra_factorised(q, k, gh).astype(bf16)
            qg_scr[rows, cols(h)] = (q * jnp.exp2(gh)).astype(bf16)
            kd_scr[rows, cols(h)] = (k * jnp.exp2(gh[CHUNK - 1:CHUNK] - gh)).astype(bf16)
        for c, h in heads:
            rows = slice(c * CHUNK, (c + 1) * CHUNK)
            v = p_scr[rows, cols(h, 2 * HGRN_WIDTH)]
            o_scr[rows, cols(h)] = jnp.dot(amat_scr[c * HGRN_HEADS + h], v.astype(bf16),
                                           preferred_element_type=f32)
            upd_scr[c * HGRN_HEADS + h] = jnp.dot(v.T.astype(bf16), kd_scr[rows, cols(h)],
                                                  preferred_element_type=f32)
        states = [st_scr[h] for h in range(HGRN_HEADS)]
        for c, h in heads:
            rows = slice(c * CHUNK, (c + 1) * CHUNK)
            st = states[h]
            o_inter = lax.dot_general(qg_scr[rows, cols(h)], st.astype(bf16), _NT,
                                      preferred_element_type=f32)
            o_scr[rows, cols(h)] = (o_scr[rows, cols(h)] + o_inter) * scale
            gl = lf_scr[(c + 1) * CHUNK - 1:(c + 1) * CHUNK, cols(h)]
            states[h] = st * jnp.exp2(gl) + upd_scr[c * HGRN_HEADS + h]
        for h in range(HGRN_HEADS):
            st_scr[h] = states[h]

    def intra_exact_diagonal(q, k, gh, cs):
        qts, kts = [], []
        for a in range(1, N_SUB):
            lo_r, hi_r = a * SUB, (a + 1) * SUB
            ra = g_scr[lo_r - 1:lo_r, cs]
            qts.append(pad_rows(q[lo_r:hi_r] * jnp.exp2(gh[lo_r:hi_r] - ra), lo_r))
            kts.append(pad_rows(k[:lo_r] * jnp.exp2(ra - gh[:lo_r]), 0))
        a_off = lax.dot_general(jnp.concatenate(qts, axis=1).astype(bf16),
                                jnp.concatenate(kts, axis=1).astype(bf16), _NT,
                                preferred_element_type=f32)
        diag_rows = []
        for a in range(N_SUB):
            lo_r = a * SUB
            gs = gh[lo_r:lo_r + SUB]
            qs = q[lo_r:lo_r + SUB]
            blk = jnp.zeros((SUB, CHUNK), f32)
            for jl in range(SUB):
                j = lo_r + jl
                gj = g_scr[j:j + 1, cs]
                kj = kc_scr[j:j + 1, cs]
                e = jnp.exp2(jnp.minimum(gs - gj, 0.0))
                col = jnp.sum(qs * (kj * e), axis=-1, keepdims=True)
                blk = jnp.where(lane_sub == j, col, blk)
            diag_rows.append(blk)
        a_diag = jnp.concatenate(diag_rows, axis=0)
        return a_off + jnp.where(row64 >= col64, a_diag, 0.0)

    def make_chunk_body(intra):
        def chunk_body(c, carry):
            r0 = pl.multiple_of(c * CHUNK, CHUNK)
            rows = pl.ds(r0, CHUNK)
            lf = lf_scr[rows, :]
            hi = lf.astype(bf16)
            lo = (lf - hi.astype(f32)).astype(bf16)
            gg = jnp.dot(tri, jnp.concatenate([hi, lo], axis=1), preferred_element_type=f32)
            g_all = gg[:, :HGRN_WIDTH] + gg[:, HGRN_WIDTH:]
            g_scr[...] = g_all
            kc_scr[...] = kk_scr[rows, :]
            for h in range(HGRN_HEADS):
                cs = slice(h * HEAD_DIM, (h + 1) * HEAD_DIM)
                q = p_scr[rows, h * HEAD_DIM:(h + 1) * HEAD_DIM]
                v = p_scr[rows, 2 * HGRN_WIDTH + h * HEAD_DIM:2 * HGRN_WIDTH + (h + 1) * HEAD_DIM]
                k = kc_scr[:, cs]
                gh = g_all[:, cs]
                st = st_scr[h]
                o_inter = lax.dot_general((q * jnp.exp2(gh)).astype(bf16), st.astype(bf16), _NT,
                                          preferred_element_type=f32)
                a_mat = intra(q, k, gh, cs)
                o = jnp.dot(a_mat.astype(bf16), v.astype(bf16), preferred_element_type=f32) + o_inter
                o_scr[rows, h * HEAD_DIM:(h + 1) * HEAD_DIM] = o * scale

                gl = g_scr[CHUNK - 1:CHUNK, cs]
                kd = k * jnp.exp2(gl - gh)
                upd = jnp.dot(v.T.astype(bf16), kd.astype(bf16), preferred_element_type=f32)
                st_scr[h] = st * jnp.exp2(gl) + upd
            return carry
        return chunk_body

    lf_all = lf_scr[...]
    sub_decay = -jnp.sum(lf_all.reshape(TB // SUB, SUB, HGRN_WIDTH), axis=1)
    bounded = jnp.max(sub_decay) <= MAX_SUB_DECAY

    f_groups = list(range(HGRN_WIDTH // PROJ_GROUP, 2 * HGRN_WIDTH // PROJ_GROUP))
    qi_groups = [g for g in range(3 * HGRN_WIDTH // PROJ_GROUP) if g not in f_groups]
    rest_groups = list(range(3 * HGRN_WIDTH // PROJ_GROUP, IN_COLS // PROJ_GROUP))

    def gmlp_stage(xnb):
        for g in f_groups:
            project(xnb, g)
        for w in range(TB // GMLP_BLOCK):
            rows = slice(w * GMLP_BLOCK, (w + 1) * GMLP_BLOCK)
            u = _gelu(p_scr[rows, 4 * HGRN_WIDTH:4 * HGRN_WIDTH + GMLP_WIDTH])
            vn = _layer_norm(_gelu(p_scr[rows, 4 * HGRN_WIDTH + GMLP_WIDTH:]),
                             lng_ref[...], lnb_ref[...])
            vnb = vn.astype(bf16)
            cols = []
            for g in range(GMLP_GROUPS):
                s = jnp.dot(wm_ref[g], vnb[:, g * HEAD_DIM:(g + 1) * HEAD_DIM],
                            preferred_element_type=f32) + bst_ref[:, g:g + 1]
                cols.append(s)
            mix_scr[rows, HGRN_WIDTH:] = (u * jnp.concatenate(cols, axis=1)).astype(bf16)

    def output_stage(xnb):
        for g in qi_groups:
            project(xnb, g)
        for w in range(TB // GMLP_BLOCK):
            rows = slice(w * GMLP_BLOCK, (w + 1) * GMLP_BLOCK)
            o = o_scr[rows, :]
            ms = jnp.mean(o * o, axis=-1, keepdims=True)
            gate = p_scr[rows, 3 * HGRN_WIDTH:4 * HGRN_WIDTH]
            half_gate = 0.5 * gate
            silu = half_gate * (jnp.tanh(half_gate) + 1.0)
            y_rec = o * lax.rsqrt(ms + EPS) * hg_ref[...] * silu
            mix_scr[rows, :HGRN_WIDTH] = y_rec.astype(bf16)

        mix = jnp.dot(mix_scr[...], wout_ref[...], preferred_element_type=f32)
        hgrn_gates()
        h1 = _layer_norm(ALPHA * x_ref[...] + mix, l1g_ref[...], l1b_ref[...])
        h1f_ref[...] = h1
        hh = h1.astype(bf16)
        h1_rounded = hh.astype(f32)
        h1p_ref[...] = _pack_rounded(h1_rounded)
        for g in rest_groups[:2]:
            project(xnb, g)

        hl = (h1 - h1_rounded).astype(bf16)
        rwh = rwh_ref[...]
        by_hh = lax.dot_general(jnp.concatenate([rwh, rwl_ref[...]], axis=0), hh, _NT,
                                preferred_element_type=f32)
        logits = (by_hh[:N_EXPERTS] + by_hh[N_EXPERTS:]
                  + lax.dot_general(rwh, hl, _NT, preferred_element_type=f32) + rb_ref[...])
        for g in rest_groups[2:]:
            project(xnb, g)

        e_iota = lax.broadcasted_iota(jnp.int32, (N_EXPERTS, TB), 0)
        work = logits
        vals, idxs, hots = [], [], []
        for _ in range(TOP_K):
            m = jnp.max(work, axis=0, keepdims=True)
            ik = jnp.min(jnp.where(work == m, e_iota, N_EXPERTS), axis=0, keepdims=True)
            hot = e_iota == ik
            work = jnp.where(hot, -jnp.inf, work)
            vals.append(m)
            idxs.append(ik)
            hots.append(hot)
        exps = [jnp.exp(vk - vals[0]) for vk in vals]
        denom = exps[0] + exps[1] + exps[2] + exps[3]
        gate_ref[...] = jnp.concatenate([ek / denom for ek in exps], axis=0)
        idx_ref[...] = jnp.concatenate(idxs, axis=0)

        hot_any = jnp.where(hots[0] | hots[1] | hots[2] | hots[3], 1.0, 0.0)
        prefix = jnp.dot(hot_any.astype(bf16), upp_ref[...], preferred_element_type=f32)
        base = prefix + carry_scr[:, 0:1]
        ranks = [jnp.sum(jnp.where(hk, base, 0.0), axis=0, keepdims=True) for hk in hots]
        rank_ref[...] = jnp.concatenate(ranks, axis=0).astype(jnp.int32)
        new_carry = carry_scr[...] + jnp.sum(hot_any, axis=1, keepdims=True)
        carry_scr[...] = new_carry
        cnt_ref[...] = new_carry

    @pl.when(bounded)
    def _():
        xnb = xn_ref[...].astype(bf16)
        gmlp_stage(xnb)
        chunks_factorised()
        output_stage(xnb)

    @pl.when(jnp.logical_not(bounded))
    def _():
        xnb = xn_ref[...].astype(bf16)
        lax.fori_loop(0, TB // CHUNK, make_chunk_body(intra_exact_diagonal), 0)
        gmlp_stage(xnb)
        output_stage(xnb)


def _mixer(xt, win, lb, hg, lng, lnb, wm, bst, wout, l1g, l1b, rwh, rwl, rb, tri, upp,
           seq, batch, batch0):
    nt = seq // TB
    t_total = batch * seq
    nblk = batch * nt
    const2 = lambda b, t: (0, 0)
    const3 = lambda b, t: (0, 0, 0)
    row_blk = lambda b, t: (b * nt + t, 0)
    meta_blk = lambda b, t: (0, b * nt + t)
    once = dict(pipeline_mode=pl.Buffered(1))
    in_specs = [
        pl.BlockSpec((TB, D_MODEL), lambda b, t: ((b + batch0) * nt + t, 0)),
        pl.BlockSpec((TB, D_MODEL),
                     lambda b, t: (batch0 * nt + jnp.minimum(b * nt + t + 1, nblk - 1), 0)),
        pl.BlockSpec((D_MODEL, IN_COLS), const2, **once),
        pl.BlockSpec((1, HGRN_WIDTH), const2),
        pl.BlockSpec((1, HGRN_WIDTH), const2),
        pl.BlockSpec((1, GMLP_WIDTH), const2),
        pl.BlockSpec((1, GMLP_WIDTH), const2),
        pl.BlockSpec((GMLP_GROUPS, GMLP_BLOCK, GMLP_BLOCK), const3),
        pl.BlockSpec((GMLP_BLOCK, GMLP_GROUPS), const2),
        pl.BlockSpec((D_MODEL, D_MODEL), const2, **once),
        pl.BlockSpec((1, D_MODEL), const2),
        pl.BlockSpec((1, D_MODEL), const2),
        pl.BlockSpec((N_EXPERTS, D_MODEL), const2),
        pl.BlockSpec((N_EXPERTS, D_MODEL), const2),
        pl.BlockSpec((N_EXPERTS, 1), const2),
        pl.BlockSpec((CHUNK, CHUNK), const2),
        pl.BlockSpec((TB, TB), const2, **once),
    ]
    out_shape = [
        jax.ShapeDtypeStruct((t_total, D_MODEL), jnp.float32),
        jax.ShapeDtypeStruct((t_total, HALF), jnp.int32),
        jax.ShapeDtypeStruct((TOP_K, t_total), jnp.int32),
        jax.ShapeDtypeStruct((TOP_K, t_total), jnp.float32),
        jax.ShapeDtypeStruct((TOP_K, t_total), jnp.int32),
        jax.ShapeDtypeStruct((N_EXPERTS, LANES), jnp.float32),
    ]
    out_specs = [
        pl.BlockSpec((TB, D_MODEL), row_blk),
        pl.BlockSpec((TB, HALF), row_blk),
        pl.BlockSpec((TOP_K, TB), meta_blk),
        pl.BlockSpec((TOP_K, TB), meta_blk),
        pl.BlockSpec((TOP_K, TB), meta_blk),
        pl.BlockSpec((N_EXPERTS, LANES), const2),
    ]
    scratch = [
        pltpu.VMEM((TB, IN_COLS), jnp.float32),
        pltpu.VMEM((TB, HGRN_WIDTH), jnp.float32),
        pltpu.VMEM((TB, HGRN_WIDTH), jnp.float32),
        pltpu.VMEM((CHUNK, HGRN_WIDTH), jnp.float32),
        pltpu.VMEM((CHUNK, HGRN_WIDTH), jnp.float32),
        pltpu.VMEM((TB, HGRN_WIDTH), jnp.float32),
        pltpu.VMEM((HGRN_HEADS, HEAD_DIM, HEAD_DIM), jnp.float32),
        pltpu.VMEM((N_EXPERTS, LANES), jnp.float32),
        pltpu.VMEM((TB // CHUNK * HGRN_HEADS, CHUNK, CHUNK), jnp.bfloat16),
        pltpu.VMEM((TB, HGRN_WIDTH), jnp.bfloat16),
        pltpu.VMEM((TB, HGRN_WIDTH), jnp.bfloat16),
        pltpu.VMEM((TB // CHUNK * HGRN_HEADS, HEAD_DIM, HEAD_DIM), jnp.float32),
        pltpu.VMEM((TB, D_MODEL), jnp.bfloat16),
    ]
    return pl.pallas_call(
        _mixer_kernel,
        grid=(batch, nt),
        in_specs=in_specs,
        out_specs=out_specs,
        out_shape=out_shape,
        scratch_shapes=scratch,
        compiler_params=pltpu.CompilerParams(
            dimension_semantics=("arbitrary", "arbitrary"),
            vmem_limit_bytes=VMEM_LIMIT),
        name="mixer",
    )(xt, xt, win, lb, hg, lng, lnb, wm, bst, wout, l1g, l1b, rwh, rwl, rb, tri, upp)


def _plan_kernel(cnt_ref, idx_ref, rank_ref, dest_ref, tiles_ref):
    f32 = jnp.float32
    n_e = N_EXPERTS
    e_sub = lax.broadcasted_iota(jnp.int32, (n_e, n_e), 0)
    e_lane = lax.broadcasted_iota(jnp.int32, (n_e, n_e), 1)
    counts = cnt_ref[:, 0:1]
    padded = jnp.floor((counts + (TM - 1)) * (1.0 / TM)) * TM
    as_row = lambda col: jnp.sum(jnp.where(e_sub == e_lane, col, 0.0), axis=0, keepdims=True)
    padded_row = as_row(padded)
    counts_row = as_row(counts)
    ends = jnp.sum(jnp.where(e_lane <= e_sub, padded_row, 0.0), axis=1, keepdims=True)
    starts = ends - padded
    owns_later = (e_lane > e_sub) & (counts_row > 0.0)
    nxt = jnp.min(jnp.where(owns_later, e_lane, n_e), axis=1, keepdims=True)
    own = lax.broadcasted_iota(jnp.int32, (n_e, 1), 0)
    nxt = jnp.where(nxt == n_e, own, nxt).astype(f32)

    n_lanes = tiles_ref.shape[1]
    tile_row = lax.broadcasted_iota(jnp.int32, (n_e, n_lanes), 1).astype(f32) * TM
    e_of = lax.broadcasted_iota(jnp.int32, (n_e, n_lanes), 0)
    tile_e = jnp.minimum(jnp.sum(jnp.where(tile_row >= ends, 1, 0), axis=0, keepdims=True), n_e - 1)
    mine = e_of == tile_e
    pick = lambda col: jnp.sum(jnp.where(mine, col, 0.0), axis=0, keepdims=True)
    valid = jnp.clip(pick(starts + counts) - tile_row[0:1], 0.0, float(TM))
    rows = [tile_e, valid.astype(jnp.int32), pick(nxt).astype(jnp.int32)]
    rows.append(jnp.zeros((tiles_ref.shape[0] - len(rows), n_lanes), jnp.int32))
    tiles_ref[...] = jnp.concatenate(rows, axis=0)

    chunk = PLAN_CHUNK
    e_chunk = lax.broadcasted_iota(jnp.int32, (n_e, chunk), 0)

    def body(c, carry):
        lanes = pl.ds(pl.multiple_of(c * chunk, chunk), chunk)
        for k in range(TOP_K):
            hit = e_chunk == idx_ref[k:k + 1, lanes]
            start_of = jnp.sum(jnp.where(hit, starts, 0.0), axis=0, keepdims=True)
            dest_ref[k:k + 1, lanes] = rank_ref[k:k + 1, lanes] + start_of.astype(jnp.int32)
        return carry

    lax.fori_loop(0, idx_ref.shape[1] // chunk, body, 0)


def _plan(cnt, idx, rank, n_tiles):
    t_part = idx.shape[1]
    n_lanes = -(-n_tiles // LANES) * LANES
    full = lambda shape: pl.BlockSpec(shape, lambda i: (0,) * len(shape))
    dest, tiles = pl.pallas_call(
        _plan_kernel,
        grid=(1,),
        in_specs=[full(cnt.shape), full(idx.shape), full(rank.shape)],
        out_specs=[full(idx.shape), full((8, n_lanes))],
        out_shape=[jax.ShapeDtypeStruct((TOP_K, t_part), jnp.int32),
                   jax.ShapeDtypeStruct((8, n_lanes), jnp.int32)],
        compiler_params=pltpu.CompilerParams(
            dimension_semantics=("arbitrary",), vmem_limit_bytes=VMEM_LIMIT),
        name="plan",
    )(cnt, idx, rank)
    return dest, tiles[0, :n_tiles], tiles[1, :n_tiles], tiles[2, :n_tiles]


def _sc_workers():
    info = plsc.get_sparse_core_info()
    return info.num_cores, info.num_cores * info.num_subcores


def _sc_dispatch(rows, dest, n_out):
    t_total, dw = rows.shape
    nc, nw = _sc_workers()
    per_w = t_total // nw
    mesh = plsc.VectorSubcoreMesh(core_axis_name="c", subcore_axis_name="s")

    @functools.partial(
        pl.kernel,
        out_type=jax.ShapeDtypeStruct((n_out, dw), rows.dtype),
        mesh=mesh,
        scratch_types=[pltpu.VMEM((SC_WINDOW,), jnp.int32) for _ in range(TOP_K)]
        + [pltpu.VMEM((SC_WINDOW, dw), rows.dtype), pltpu.SemaphoreType.DMA],
        name="sc_dispatch",
    )
    def k(x_hbm, i_hbm, o_hbm, i0, i1, i2, i3, rows_v, sem):
        wid = lax.axis_index("s") * nc + lax.axis_index("c")
        idx_bufs = (i0, i1, i2, i3)

        @pl.loop(0, per_w // SC_WINDOW)
        def _(j):
            base = wid * per_w + j * SC_WINDOW
            for kk in range(TOP_K):
                pltpu.sync_copy(i_hbm.at[kk, pl.ds(base, SC_WINDOW)], idx_bufs[kk])
            pltpu.sync_copy(x_hbm.at[pl.ds(base, SC_WINDOW)], rows_v)
            copies = [pltpu.async_copy(rows_v, o_hbm.at[idx_bufs[kk]], sem) for kk in range(TOP_K)]
            for cp in copies:
                cp.wait()

    return k(rows, dest)


def _sc_gather(table, idx):
    n_k, t_total = idx.shape
    n = n_k * t_total
    dw = table.shape[1]
    nc, nw = _sc_workers()
    per_w = n // nw
    w_per_k = nw // n_k
    mesh = plsc.VectorSubcoreMesh(core_axis_name="c", subcore_axis_name="s")

    @functools.partial(
        pl.kernel,
        out_type=jax.ShapeDtypeStruct((n, dw), table.dtype),
        mesh=mesh,
        scratch_types=[pltpu.VMEM((SC_WINDOW,), jnp.int32),
                       pltpu.VMEM((SC_WINDOW, dw), table.dtype),
                       pltpu.SemaphoreType.DMA],
        name="sc_gather",
    )
    def k(t_hbm, i_hbm, o_hbm, idx_v, rows_v, sem):
        wid = lax.axis_index("s") * nc + lax.axis_index("c")

        row = wid // w_per_k
        col0 = (wid % w_per_k) * per_w

        @pl.loop(0, per_w // SC_WINDOW)
        def _(j):
            col = col0 + j * SC_WINDOW
            pltpu.sync_copy(i_hbm.at[row, pl.ds(col, SC_WINDOW)], idx_v)
            pltpu.async_copy(t_hbm.at[idx_v], rows_v, sem).wait()
            pltpu.sync_copy(rows_v, o_hbm.at[pl.ds(row * t_total + col, SC_WINDOW)])

    return k(table, idx)


def _expert_kernel(te_ref, tv_ref, tn_ref, xs_ref, w1_hbm, w2_hbm, b1g_ref, b1l_ref, b2_ref,
                   perm_ref, y_ref, w1f_scr, w2f_scr, w1g_scr, w1l_scr, w2_scr, sems):
    i = pl.program_id(0)
    valid = tv_ref[i]
    expert = te_ref[i]
    f32 = jnp.float32
    bf16 = jnp.bfloat16
    expert_changed = (i == 0) | (expert != te_ref[jnp.maximum(i - 1, 0)])

    def weight_copies(e):
        return (pltpu.make_async_copy(w1_hbm.at[e], w1f_scr, sems.at[0]),
                pltpu.make_async_copy(w2_hbm.at[e], w2f_scr, sems.at[1]))

    @pl.when(valid == 0)
    def _():
        y_ref[...] = jnp.zeros_like(y_ref)

    @pl.when(i == 0)
    def _():
        for cp in weight_copies(expert):
            cp.start()

    @pl.when((valid > 0) & expert_changed)
    def _():
        for cp in weight_copies(expert):
            cp.wait()
        w2_scr[...] = w2f_scr[...].astype(bf16)
        perm = perm_ref[...]
        for c in range(2 * D_MODEL // PERM_BLOCK):
            blk = w1f_scr[:, c * PERM_BLOCK:(c + 1) * PERM_BLOCK].astype(bf16)
            r = jnp.dot(blk, perm, preferred_element_type=f32).astype(bf16)
            half = PERM_BLOCK // 2
            w1g_scr[:, c * half:(c + 1) * half] = r[:, :half]
            w1l_scr[:, c * half:(c + 1) * half] = r[:, half:]

        @pl.when(tn_ref[i] != expert)
        def _():
            for cp in weight_copies(tn_ref[i]):
                cp.start()

    def expert_rows(n_rows):
        keep = lax.broadcasted_iota(jnp.int32, (n_rows, 1), 0) < valid
        a, b = _unpack_rows(jnp.where(keep, xs_ref[:n_rows, :], 0))
        x = jnp.concatenate([a, b], axis=1).astype(bf16)
        hg = jnp.dot(x, w1g_scr[...], preferred_element_type=f32) + b1g_ref[0]
        xl1 = jnp.clip(jnp.dot(x, w1l_scr[...], preferred_element_type=f32) + (b1l_ref[0] + 1.0),
                       1.0 - SWIGLU_LIMIT, 1.0 + SWIGLU_LIMIT)
        xg = jnp.minimum(hg, SWIGLU_LIMIT)
        act = (0.5 * xg) * (jnp.tanh((0.5 * SWIGLU_ALPHA) * xg) + 1.0) * xl1
        y = jnp.dot(act.astype(bf16), w2_scr[...], preferred_element_type=f32) + b2_ref[0]
        y_ref[:n_rows, :] = _pack_rows(y)
        if n_rows < TM:
            y_ref[n_rows:, :] = jnp.zeros((TM - n_rows, HALF), jnp.int32)

    for blocks in range(TM // ROW_STEP, 0, -1):
        @pl.when((valid > (blocks - 1) * ROW_STEP) & (valid <= blocks * ROW_STEP))
        def _(blocks=blocks):
            expert_rows(blocks * ROW_STEP)


def _experts(xs, tile_expert, tile_valid, tile_next, w1, w2, b1g, b1l, b2, perm):
    n_slots = xs.shape[0]
    n_tiles = n_slots // TM
    d_ff = w2.shape[1]
    wmap = lambda i, te, tv, tn: (te[i], 0, 0)
    grid_spec = pltpu.PrefetchScalarGridSpec(
        num_scalar_prefetch=3,
        grid=(n_tiles,),
        in_specs=[
            pl.BlockSpec((TM, HALF), lambda i, te, tv, tn: (i, 0)),
            pl.BlockSpec(memory_space=pl.ANY),
            pl.BlockSpec(memory_space=pl.ANY),
            pl.BlockSpec((1, 1, d_ff), wmap),
            pl.BlockSpec((1, 1, d_ff), wmap),
            pl.BlockSpec((1, 1, D_MODEL), wmap),
            pl.BlockSpec((PERM_BLOCK, PERM_BLOCK), lambda i, te, tv, tn: (0, 0)),
        ],
        out_specs=pl.BlockSpec((TM, HALF), lambda i, te, tv, tn: (i, 0)),
        scratch_shapes=[
            pltpu.VMEM((D_MODEL, 2 * d_ff), jnp.float32),
            pltpu.VMEM((d_ff, D_MODEL), jnp.float32),
            pltpu.VMEM((D_MODEL, d_ff), jnp.bfloat16),
            pltpu.VMEM((D_MODEL, d_ff), jnp.bfloat16),
            pltpu.VMEM((d_ff, D_MODEL), jnp.bfloat16),
            pltpu.SemaphoreType.DMA((2,)),
        ],
    )
    return pl.pallas_call(
        _expert_kernel,
        grid_spec=grid_spec,
        out_shape=jax.ShapeDtypeStruct((n_slots, HALF), jnp.int32),
        compiler_params=pltpu.CompilerParams(
            dimension_semantics=("arbitrary",),
            vmem_limit_bytes=VMEM_LIMIT),
        name="experts",
    )(tile_expert, tile_valid, tile_next, xs, w1, w2, b1g, b1l, b2, perm)


def _combine_kernel(h_ref, yk_ref, gate_ref, g_ref, b_ref, *rest):
    o_ref = rest[-1]
    gates = gate_ref[...].T
    acc_a = None
    acc_b = None
    for k in range(TOP_K):
        a, b = _unpack_rows(yk_ref[k])
        gk = gates[:, k:k + 1]
        acc_a = gk * a if acc_a is None else acc_a + gk * a
        acc_b = gk * b if acc_b is None else acc_b + gk * b
    ffn = jnp.concatenate([acc_a, acc_b], axis=1)
    o_ref[...] = _layer_norm(ALPHA * h_ref[...] + ffn, g_ref[...], b_ref[...])


def _combine(h1f, yk, gates, g2, b2, out_prev, row0, t_all):
    t_part = h1f.shape[0]
    blk0 = row0 // TC3
    row = lambda i: (i, 0)
    const = lambda i: (0, 0)
    in_specs = [
        pl.BlockSpec((TC3, D_MODEL), row),
        pl.BlockSpec((TOP_K, TC3, HALF), lambda i: (0, i, 0)),
        pl.BlockSpec((TOP_K, TC3), lambda i: (0, i)),
        pl.BlockSpec((1, D_MODEL), const),
        pl.BlockSpec((1, D_MODEL), const),
    ]
    args = [h1f, yk, gates, g2, b2]
    aliases = {}
    if out_prev is not None:
        in_specs.append(pl.BlockSpec(memory_space=pl.ANY))
        args.append(out_prev)
        aliases = {len(args) - 1: 0}
    return pl.pallas_call(
        _combine_kernel,
        grid=(t_part // TC3,),
        in_specs=in_specs,
        out_specs=pl.BlockSpec((TC3, D_MODEL), lambda i: (i + blk0, 0)),
        out_shape=jax.ShapeDtypeStruct((t_all, D_MODEL), jnp.float32),
        input_output_aliases=aliases,
        compiler_params=pltpu.CompilerParams(
            dimension_semantics=("arbitrary",),
            vmem_limit_bytes=VMEM_LIMIT),
        name="combine",
    )(*args)


def kernel(x, w_in, lb_logits, hgrn_norm_g, gmlp_ln_g, gmlp_ln_b, gmlp_ws, gmlp_bs, w_out, ln1_g, ln1_b, router_w, router_b, exp_w1, exp_b1, exp_w2, exp_b2, ln2_g, ln2_b):
    batch, seq, d = x.shape
    assert d == D_MODEL and seq % TB == 0 and w_in.shape[0] == 1
    t_total = batch * seq
    f32 = jnp.float32
    bf16 = jnp.bfloat16

    lb = jnp.cumsum(jax.nn.softmax(lb_logits.astype(f32), axis=0), axis=0)[0:1]
    chunk_id = jnp.arange(GMLP_BLOCK) // CHUNK
    wm = jnp.where((chunk_id[None, :] <= chunk_id[:, None])[None], gmlp_ws[0], 0.0).astype(bf16)
    rwt = router_w[0].T
    rwh = rwt.astype(bf16)
    rwl = (rwt - rwh.astype(f32)).astype(bf16)
    tri = (jnp.arange(CHUNK)[None, :] <= jnp.arange(CHUNK)[:, None]).astype(bf16)
    upp = (jnp.arange(TB)[:, None] < jnp.arange(TB)[None, :]).astype(bf16)

    lane = jnp.arange(PERM_BLOCK)
    src = jnp.where(lane < PERM_BLOCK // 2, 2 * lane, 2 * (lane - PERM_BLOCK // 2) + 1)
    perm = (jnp.arange(PERM_BLOCK)[:, None] == src[None, :]).astype(bf16)
    xt = x.reshape(t_total, d)
    win = w_in[0].astype(bf16)
    wout = w_out[0].astype(bf16)
    b1g, b1l, b2e = exp_b1[0][:, None, 0::2], exp_b1[0][:, None, 1::2], exp_b2[0][:, None, :]

    last = max(batch * LAST_PART_NUM // LAST_PART_DEN, 1) if batch > 1 else 0
    part_sizes = [pb for pb in (batch - last, last) if pb > 0]
    out = None
    b0 = 0
    for pb in part_sizes:
        t_part = pb * seq
        n_slots = t_part * TOP_K + N_EXPERTS * TM
        n_tiles = n_slots // TM
        h1f, h1p, idx, gates, rank, cnt = _mixer(
            xt, win, lb, hgrn_norm_g[0:1], gmlp_ln_g[0:1], gmlp_ln_b[0:1], wm, gmlp_bs[0].T, wout,
            ln1_g[0:1], ln1_b[0:1], rwh, rwl, router_b[0][:, None], tri, upp, seq, pb, b0)

        dest, tile_expert, tile_valid, tile_next = _plan(cnt, idx, rank, n_tiles)
        xs = _sc_dispatch(h1p, dest, n_slots)
        y = _experts(xs, tile_expert, tile_valid, tile_next,
                     exp_w1[0], exp_w2[0], b1g, b1l, b2e, perm)
        yk = _sc_gather(y, dest).reshape(TOP_K, t_part, HALF)
        out = _combine(h1f, yk, gates, ln2_g[0:1], ln2_b[0:1], out, b0 * seq, t_total)
        b0 += pb
    return out.reshape(batch, seq, d)
```

```python
import functools
import math

import jax
import jax.numpy as jnp
from jax import lax
from jax.experimental import pallas as pl
from jax.experimental.pallas import tpu as pltpu
from jax.experimental.pallas import tpu_sc as plsc

D_MODEL = 1024
CHUNK = 64
SUB = 16
N_SUB = CHUNK // SUB
MAX_SUB_DECAY = 86.0
HGRN_WIDTH = 512
HGRN_HEADS = 4
HEAD_DIM = 128
GMLP_WIDTH = 512
GMLP_BLOCK = 128
GMLP_GROUPS = 4
IN_COLS = 3072
N_EXPERTS = 32
TOP_K = 4
SWIGLU_LIMIT = 7.0
SWIGLU_ALPHA = 1.702
ALPHA = 2.0 ** 0.25
EPS = 1e-5
HALF = D_MODEL // 2

TB = 512
TM = 1024
ROW_STEP = 256
PLAN_CHUNK = 2048
TC3 = 512
LAST_PART_NUM, LAST_PART_DEN = 1, 4
SC_WINDOW = 128
PROJ_GROUP = 256
PERM_BLOCK = 256
LANES = 128
V7X_VMEM_BYTES = 64 * 1024 * 1024
VMEM_LIMIT = V7X_VMEM_BYTES * 7 // 8

_NT = (((1,), (1,)), ((), ()))


def _gelu(x):
    return 0.5 * x * (1.0 + lax.erf(x * (1.0 / math.sqrt(2.0))))


def _layer_norm(x, g, b):
    mu = jnp.mean(x, axis=-1, keepdims=True)
    xc = x - mu
    var = jnp.mean(xc * xc, axis=-1, keepdims=True)
    return xc * lax.rsqrt(var + EPS) * g + b


def _pack_rounded(hr):
    au = lax.bitcast_convert_type(hr[:, :HALF], jnp.uint32) >> 16
    bu = lax.bitcast_convert_type(hr[:, HALF:], jnp.uint32) & jnp.uint32(0xFFFF0000)
    return lax.bitcast_convert_type(au | bu, jnp.int32)


def _pack_rows(h):
    return _pack_rounded(h.astype(jnp.bfloat16).astype(jnp.float32))


def _unpack_rows(w):
    u = lax.bitcast_convert_type(w, jnp.uint32)
    a = lax.bitcast_convert_type(u << 16, jnp.float32)
    b = lax.bitcast_convert_type(u & jnp.uint32(0xFFFF0000), jnp.float32)
    return a, b


def _mixer_kernel(x_ref, xn_ref, win_ref, lb_ref, hg_ref, lng_ref, lnb_ref, wm_ref, bst_ref,
                  wout_ref, l1g_ref, l1b_ref, rwh_ref, rb_ref, tri_ref, upp_ref,
                  h1f_ref, h1p_ref, idx_ref, gate_ref, rank_ref, cnt_ref,
                  p_scr, lf_scr, kk_scr, g_scr, kc_scr, o_scr, st_scr, carry_scr,
                  amat_scr, qg_scr, kd_scr, upd_scr, mix_scr):
    b = pl.program_id(0)
    t = pl.program_id(1)
    step = b * pl.num_programs(1) + t
    f32 = jnp.float32
    bf16 = jnp.bfloat16

    @pl.when(t == 0)
    def _():
        st_scr[...] = jnp.zeros_like(st_scr)

    @pl.when((b == 0) & (t == 0))
    def _():
        carry_scr[...] = jnp.zeros_like(carry_scr)

    def project(xb, group):
        cols = slice(group * PROJ_GROUP, (group + 1) * PROJ_GROUP)
        p_scr[:, cols] = jnp.dot(xb, win_ref[:, cols], preferred_element_type=f32)

    def hgrn_gates():
        fl = p_scr[:, HGRN_WIDTH:2 * HGRN_WIDTH]
        z = jnp.exp(-jnp.abs(fl))
        r = 1.0 / (1.0 + z)
        k = (1.0 - lb_ref[...]) * jnp.where(fl >= 0, z * r, r)
        kk_scr[...] = k
        lf_scr[...] = jnp.log2(1.0 - k)

    @pl.when(step == 0)
    def _():
        xb0 = x_ref[...].astype(bf16)
        for group in range(IN_COLS // PROJ_GROUP):
            project(xb0, group)
        hgrn_gates()

    tri = tri_ref[...]
    row64 = lax.broadcasted_iota(jnp.int32, (CHUNK, CHUNK), 0)
    col64 = lax.broadcasted_iota(jnp.int32, (CHUNK, CHUNK), 1)
    lane_sub = lax.broadcasted_iota(jnp.int32, (SUB, CHUNK), 1)
    scale = HEAD_DIM ** -0.5

    def pad_rows(piece, lo_r):
        parts = []
        if lo_r > 0:
            parts.append(jnp.zeros((lo_r, HEAD_DIM), f32))
        parts.append(piece)
        rest = CHUNK - lo_r - piece.shape[0]
        if rest > 0:
            parts.append(jnp.zeros((rest, HEAD_DIM), f32))
        return jnp.concatenate(parts, axis=0) if len(parts) > 1 else piece

    def intra_factorised(q, k, gh):
        qts, kts = [], []
        for a in range(N_SUB):
            lo_r, hi_r = a * SUB, (a + 1) * SUB
            if a == 0:
                qa = q[:hi_r] * jnp.exp2(gh[:hi_r])
                ka = k[:hi_r] * jnp.exp2(-gh[:hi_r])
            else:
                ra = gh[lo_r - 1:lo_r]
                qa = q[lo_r:hi_r] * jnp.exp2(gh[lo_r:hi_r] - ra)
                ka = k[:hi_r] * jnp.exp2(ra - gh[:hi_r])
            qts.append(pad_rows(qa, lo_r))
            kts.append(pad_rows(ka, 0))
        a_mat = lax.dot_general(jnp.concatenate(qts, axis=1).astype(bf16),
                                jnp.concatenate(kts, axis=1).astype(bf16), _NT,
                                preferred_element_type=f32)
        return jnp.where(row64 >= col64, a_mat, 0.0)

    def chunks_factorised():
        n_chunks = TB // CHUNK
        heads = [(c, h) for c in range(n_chunks) for h in range(HGRN_HEADS)]

        def cols(h, base=0):
            return slice(base + h * HEAD_DIM, base + (h + 1) * HEAD_DIM)

        for c in range(n_chunks):
            rows = slice(c * CHUNK, (c + 1) * CHUNK)
            lf = lf_scr[rows, :]
            hi = lf.astype(bf16)
            lo = (lf - hi.astype(f32)).astype(bf16)
            gg = jnp.dot(tri, jnp.concatenate([hi, lo], axis=1), preferred_element_type=f32)
            lf_scr[rows, :] = gg[:, :HGRN_WIDTH] + gg[:, HGRN_WIDTH:]
        for c, h in heads:
            rows = slice(c * CHUNK, (c + 1) * CHUNK)
            q = p_scr[rows, cols(h)]
            k = kk_scr[rows, cols(h)]
            gh = lf_scr[rows, cols(h)]
            amat_scr[c * HGRN_HEADS + h] = intra_factorised(q, k, gh).astype(bf16)
            qg_scr[rows, cols(h)] = (q * jnp.exp2(gh)).astype(bf16)
            kd_scr[rows, cols(h)] = (k * jnp.exp2(gh[CHUNK - 1:CHUNK] - gh)).astype(bf16)
        for c, h in heads:
            rows = slice(c * CHUNK, (c + 1) * CHUNK)
            v = p_scr[rows, cols(h, 2 * HGRN_WIDTH)]
            o_scr[rows, cols(h)] = jnp.dot(amat_scr[c * HGRN_HEADS + h], v.astype(bf16),
                                           preferred_element_type=f32)
            upd_scr[c * HGRN_HEADS + h] = jnp.dot(v.T.astype(bf16), kd_scr[rows, cols(h)],
                                                  preferred_element_type=f32)
        states = [st_scr[h] for h in range(HGRN_HEADS)]
        for c, h in heads:
            rows = slice(c * CHUNK, (c + 1) * CHUNK)
            st = states[h]
            o_inter = lax.dot_general(qg_scr[rows, cols(h)], st.astype(bf16), _NT,
                                      preferred_element_type=f32)
            o_scr[rows, cols(h)] = (o_scr[rows, cols(h)] + o_inter) * scale
            gl = lf_scr[(c + 1) * CHUNK - 1:(c + 1) * CHUNK, cols(h)]
            states[h] = st * jnp.exp2(gl) + upd_scr[c * HGRN_HEADS + h]
        for h in range(HGRN_HEADS):
            st_scr[h] = states[h]

    def intra_exact_diagonal(q, k, gh, cs):
        qts, kts = [], []
        for a in range(1, N_SUB):
            lo_r, hi_r = a * SUB, (a + 1) * SUB
            ra = g_scr[lo_r - 1:lo_r, cs]
            qts.append(pad_rows(q[lo_r:hi_r] * jnp.exp2(gh[lo_r:hi_r] - ra), lo_r))
            kts.append(pad_rows(k[:lo_r] * jnp.exp2(ra - gh[:lo_r]), 0))
        a_off = lax.dot_general(jnp.concatenate(qts, axis=1).astype(bf16),
                                jnp.concatenate(kts, axis=1).astype(bf16), _NT,
                                preferred_element_type=f32)
        diag_rows = []
        for a in range(N_SUB):
            lo_r = a * SUB
            gs = gh[lo_r:lo_r + SUB]
            qs = q[lo_r:lo_r + SUB]
            blk = jnp.zeros((SUB, CHUNK), f32)
            for jl in range(SUB):
                j = lo_r + jl
                gj = g_scr[j:j + 1, cs]
                kj = kc_scr[j:j + 1, cs]
                e = jnp.exp2(jnp.minimum(gs - gj, 0.0))
                col = jnp.sum(qs * (kj * e), axis=-1, keepdims=True)
                blk = jnp.where(lane_sub == j, col, blk)
            diag_rows.append(blk)
        a_diag = jnp.concatenate(diag_rows, axis=0)
        return a_off + jnp.where(row64 >= col64, a_diag, 0.0)

    def make_chunk_body(intra):
        def chunk_body(c, carry):
            r0 = pl.multiple_of(c * CHUNK, CHUNK)
            rows = pl.ds(r0, CHUNK)
            lf = lf_scr[rows, :]
            hi = lf.astype(bf16)
            lo = (lf - hi.astype(f32)).astype(bf16)
            gg = jnp.dot(tri, jnp.concatenate([hi, lo], axis=1), preferred_element_type=f32)
            g_all = gg[:, :HGRN_WIDTH] + gg[:, HGRN_WIDTH:]
            g_scr[...] = g_all
            kc_scr[...] = kk_scr[rows, :]
            for h in range(HGRN_HEADS):
                cs = slice(h * HEAD_DIM, (h + 1) * HEAD_DIM)
                q = p_scr[rows, h * HEAD_DIM:(h + 1) * HEAD_DIM]
                v = p_scr[rows, 2 * HGRN_WIDTH + h * HEAD_DIM:2 * HGRN_WIDTH + (h + 1) * HEAD_DIM]
                k = kc_scr[:, cs]
                gh = g_all[:, cs]
                st = st_scr[h]
                o_inter = lax.dot_general((q * jnp.exp2(gh)).astype(bf16), st.astype(bf16), _NT,
                                          preferred_element_type=f32)
                a_mat = intra(q, k, gh, cs)
                o = jnp.dot(a_mat.astype(bf16), v.astype(bf16), preferred_element_type=f32) + o_inter
                o_scr[rows, h * HEAD_DIM:(h + 1) * HEAD_DIM] = o * scale

                gl = g_scr[CHUNK - 1:CHUNK, cs]
                kd = k * jnp.exp2(gl - gh)
                upd = jnp.dot(v.T.astype(bf16), kd.astype(bf16), preferred_element_type=f32)
                st_scr[h] = st * jnp.exp2(gl) + upd
            return carry
        return chunk_body

    lf_all = lf_scr[...]
    sub_decay = -jnp.sum(lf_all.reshape(TB // SUB, SUB, HGRN_WIDTH), axis=1)
    bounded = jnp.max(sub_decay) <= MAX_SUB_DECAY

    f_groups = list(range(HGRN_WIDTH // PROJ_GROUP, 2 * HGRN_WIDTH // PROJ_GROUP))
    qi_groups = [g for g in range(3 * HGRN_WIDTH // PROJ_GROUP) if g not in f_groups]
    rest_groups = list(range(3 * HGRN_WIDTH // PROJ_GROUP, IN_COLS // PROJ_GROUP))

    def gmlp_stage(xnb):
        for g in f_groups:
            project(xnb, g)
        for w in range(TB // GMLP_BLOCK):
            rows = slice(w * GMLP_BLOCK, (w + 1) * GMLP_BLOCK)
            u = _gelu(p_scr[rows, 4 * HGRN_WIDTH:4 * HGRN_WIDTH + GMLP_WIDTH])
            vn = _layer_norm(_gelu(p_scr[rows, 4 * HGRN_WIDTH + GMLP_WIDTH:]),
                             lng_ref[...], lnb_ref[...])
            vnb = vn.astype(bf16)
            cols = []
            for g in range(GMLP_GROUPS):
                s = jnp.dot(wm_ref[g], vnb[:, g * HEAD_DIM:(g + 1) * HEAD_DIM],
                            preferred_element_type=f32) + bst_ref[:, g:g + 1]
                cols.append(s)
            mix_scr[rows, HGRN_WIDTH:] = (u * jnp.concatenate(cols, axis=1)).astype(bf16)

    def output_stage(xnb):
        for g in qi_groups:
            project(xnb, g)
        for w in range(TB // GMLP_BLOCK):
            rows = slice(w * GMLP_BLOCK, (w + 1) * GMLP_BLOCK)
            o = o_scr[rows, :]
            ms = jnp.mean(o * o, axis=-1, keepdims=True)
            gate = p_scr[rows, 3 * HGRN_WIDTH:4 * HGRN_WIDTH]
            half_gate = 0.5 * gate
            silu = half_gate * (jnp.tanh(half_gate) + 1.0)
            y_rec = o * lax.rsqrt(ms + EPS) * hg_ref[...] * silu
            mix_scr[rows, :HGRN_WIDTH] = y_rec.astype(bf16)

        mix = jnp.dot(mix_scr[...], wout_ref[...], preferred_element_type=f32)
        hgrn_gates()
        h1 = _layer_norm(ALPHA * x_ref[...] + mix, l1g_ref[...], l1b_ref[...])
        h1f_ref[...] = h1
        hh = h1.astype(bf16)
        h1_rounded = hh.astype(f32)
        h1p_ref[...] = _pack_rounded(h1_rounded)
        for g in rest_groups[:2]:
            project(xnb, g)

        logits = lax.dot_general(rwh_ref[...], hh, _NT, preferred_element_type=f32) + rb_ref[...]
        for g in rest_groups[2:]:
            project(xnb, g)

        e_iota = lax.broadcasted_iota(jnp.int32, (N_EXPERTS, TB), 0)
        work = logits
        vals, idxs, hots = [], [], []
        for _ in range(TOP_K):
            m = jnp.max(work, axis=0, keepdims=True)
            ik = jnp.min(jnp.where(work == m, e_iota, N_EXPERTS), axis=0, keepdims=True)
            hot = e_iota == ik
            work = jnp.where(hot, -jnp.inf, work)
            vals.append(m)
            idxs.append(ik)
            hots.append(hot)
        exps = [jnp.exp(vk - vals[0]) for vk in vals]
        denom = exps[0] + exps[1] + exps[2] + exps[3]
        gate_ref[...] = jnp.concatenate([ek / denom for ek in exps], axis=0)
        idx_ref[...] = jnp.concatenate(idxs, axis=0)

        hot_any = jnp.where(hots[0] | hots[1] | hots[2] | hots[3], 1.0, 0.0)
        prefix = jnp.dot(hot_any.astype(bf16), upp_ref[...], preferred_element_type=f32)
        base = prefix + carry_scr[:, 0:1]
        ranks = [jnp.sum(jnp.where(hk, base, 0.0), axis=0, keepdims=True) for hk in hots]
        rank_ref[...] = jnp.concatenate(ranks, axis=0).astype(jnp.int32)
        new_carry = carry_scr[...] + jnp.sum(hot_any, axis=1, keepdims=True)
        carry_scr[...] = new_carry
        cnt_ref[...] = new_carry

    @pl.when(bounded)
    def _():
        xnb = xn_ref[...].astype(bf16)
        gmlp_stage(xnb)
        chunks_factorised()
        output_stage(xnb)

    @pl.when(jnp.logical_not(bounded))
    def _():
        xnb = xn_ref[...].astype(bf16)
        lax.fori_loop(0, TB // CHUNK, make_chunk_body(intra_exact_diagonal), 0)
        gmlp_stage(xnb)
        output_stage(xnb)


def _mixer(xt, win, lb, hg, lng, lnb, wm, bst, wout, l1g, l1b, rwh, rb, tri, upp,
           seq, batch, batch0):
    nt = seq // TB
    t_total = batch * seq
    nblk = batch * nt
    const2 = lambda b, t: (0, 0)
    const3 = lambda b, t: (0, 0, 0)
    row_blk = lambda b, t: (b * nt + t, 0)
    meta_blk = lambda b, t: (0, b * nt + t)
    once = dict(pipeline_mode=pl.Buffered(1))
    in_specs = [
        pl.BlockSpec((TB, D_MODEL), lambda b, t: ((b + batch0) * nt + t, 0)),
        pl.BlockSpec((TB, D_MODEL),
                     lambda b, t: (batch0 * nt + jnp.minimum(b * nt + t + 1, nblk - 1), 0)),
        pl.BlockSpec((D_MODEL, IN_COLS), const2, **once),
        pl.BlockSpec((1, HGRN_WIDTH), const2),
        pl.BlockSpec((1, HGRN_WIDTH), const2),
        pl.BlockSpec((1, GMLP_WIDTH), const2),
        pl.BlockSpec((1, GMLP_WIDTH), const2),
        pl.BlockSpec((GMLP_GROUPS, GMLP_BLOCK, GMLP_BLOCK), const3),
        pl.BlockSpec((GMLP_BLOCK, GMLP_GROUPS), const2),
        pl.BlockSpec((D_MODEL, D_MODEL), const2, **once),
        pl.BlockSpec((1, D_MODEL), const2),
        pl.BlockSpec((1, D_MODEL), const2),
        pl.BlockSpec((N_EXPERTS, D_MODEL), const2),
        pl.BlockSpec((N_EXPERTS, 1), const2),
        pl.BlockSpec((CHUNK, CHUNK), const2),
        pl.BlockSpec((TB, TB), const2, **once),
    ]
    out_shape = [
        jax.ShapeDtypeStruct((t_total, D_MODEL), jnp.float32),
        jax.ShapeDtypeStruct((t_total, HALF), jnp.int32),
        jax.ShapeDtypeStruct((TOP_K, t_total), jnp.int32),
        jax.ShapeDtypeStruct((TOP_K, t_total), jnp.float32),
        jax.ShapeDtypeStruct((TOP_K, t_total), jnp.int32),
        jax.ShapeDtypeStruct((N_EXPERTS, LANES), jnp.float32),
    ]
    out_specs = [
        pl.BlockSpec((TB, D_MODEL), row_blk),
        pl.BlockSpec((TB, HALF), row_blk),
        pl.BlockSpec((TOP_K, TB), meta_blk),
        pl.BlockSpec((TOP_K, TB), meta_blk),
        pl.BlockSpec((TOP_K, TB), meta_blk),
        pl.BlockSpec((N_EXPERTS, LANES), const2),
    ]
    scratch = [
        pltpu.VMEM((TB, IN_COLS), jnp.float32),
        pltpu.VMEM((TB, HGRN_WIDTH), jnp.float32),
        pltpu.VMEM((TB, HGRN_WIDTH), jnp.float32),
        pltpu.VMEM((CHUNK, HGRN_WIDTH), jnp.float32),
        pltpu.VMEM((CHUNK, HGRN_WIDTH), jnp.float32),
        pltpu.VMEM((TB, HGRN_WIDTH), jnp.float32),
        pltpu.VMEM((HGRN_HEADS, HEAD_DIM, HEAD_DIM), jnp.float32),
        pltpu.VMEM((N_EXPERTS, LANES), jnp.float32),
        pltpu.VMEM((TB // CHUNK * HGRN_HEADS, CHUNK, CHUNK), jnp.bfloat16),
        pltpu.VMEM((TB, HGRN_WIDTH), jnp.bfloat16),
        pltpu.VMEM((TB, HGRN_WIDTH), jnp.bfloat16),
        pltpu.VMEM((TB // CHUNK * HGRN_HEADS, HEAD_DIM, HEAD_DIM), jnp.float32),
        pltpu.VMEM((TB, D_MODEL), jnp.bfloat16),
    ]
    return pl.pallas_call(
        _mixer_kernel,
        grid=(batch, nt),
        in_specs=in_specs,
        out_specs=out_specs,
        out_shape=out_shape,
        scratch_shapes=scratch,
        compiler_params=pltpu.CompilerParams(
            dimension_semantics=("arbitrary", "arbitrary"),
            vmem_limit_bytes=VMEM_LIMIT),
        name="mixer",
    )(xt, xt, win, lb, hg, lng, lnb, wm, bst, wout, l1g, l1b, rwh, rb, tri, upp)


def _plan_kernel(cnt_ref, idx_ref, rank_ref, dest_ref, tiles_ref):
    f32 = jnp.float32
    n_e = N_EXPERTS
    e_sub = lax.broadcasted_iota(jnp.int32, (n_e, n_e), 0)
    e_lane = lax.broadcasted_iota(jnp.int32, (n_e, n_e), 1)
    counts = cnt_ref[:, 0:1]
    padded = jnp.floor((counts + (TM - 1)) * (1.0 / TM)) * TM
    as_row = lambda col: jnp.sum(jnp.where(e_sub == e_lane, col, 0.0), axis=0, keepdims=True)
    padded_row = as_row(padded)
    counts_row = as_row(counts)
    ends = jnp.sum(jnp.where(e_lane <= e_sub, padded_row, 0.0), axis=1, keepdims=True)
    starts = ends - padded
    owns_later = (e_lane > e_sub) & (counts_row > 0.0)
    nxt = jnp.min(jnp.where(owns_later, e_lane, n_e), axis=1, keepdims=True)
    own = lax.broadcasted_iota(jnp.int32, (n_e, 1), 0)
    nxt = jnp.where(nxt == n_e, own, nxt).astype(f32)

    n_lanes = tiles_ref.shape[1]
    tile_row = lax.broadcasted_iota(jnp.int32, (n_e, n_lanes), 1).astype(f32) * TM
    e_of = lax.broadcasted_iota(jnp.int32, (n_e, n_lanes), 0)
    tile_e = jnp.minimum(jnp.sum(jnp.where(tile_row >= ends, 1, 0), axis=0, keepdims=True), n_e - 1)
    mine = e_of == tile_e
    pick = lambda col: jnp.sum(jnp.where(mine, col, 0.0), axis=0, keepdims=True)
    valid = jnp.clip(pick(starts + counts) - tile_row[0:1], 0.0, float(TM))
    rows = [tile_e, valid.astype(jnp.int32), pick(nxt).astype(jnp.int32)]
    rows.append(jnp.zeros((tiles_ref.shape[0] - len(rows), n_lanes), jnp.int32))
    tiles_ref[...] = jnp.concatenate(rows, axis=0)

    chunk = PLAN_CHUNK
    e_chunk = lax.broadcasted_iota(jnp.int32, (n_e, chunk), 0)

    def body(c, carry):
        lanes = pl.ds(pl.multiple_of(c * chunk, chunk), chunk)
        for k in range(TOP_K):
            hit = e_chunk == idx_ref[k:k + 1, lanes]
            start_of = jnp.sum(jnp.where(hit, starts, 0.0), axis=0, keepdims=True)
            dest_ref[k:k + 1, lanes] = rank_ref[k:k + 1, lanes] + start_of.astype(jnp.int32)
        return carry

    lax.fori_loop(0, idx_ref.shape[1] // chunk, body, 0)


def _plan(cnt, idx, rank, n_tiles):
    t_part = idx.shape[1]
    n_lanes = -(-n_tiles // LANES) * LANES
    full = lambda shape: pl.BlockSpec(shape, lambda i: (0,) * len(shape))
    dest, tiles = pl.pallas_call(
        _plan_kernel,
        grid=(1,),
        in_specs=[full(cnt.shape), full(idx.shape), full(rank.shape)],
        out_specs=[full(idx.shape), full((8, n_lanes))],
        out_shape=[jax.ShapeDtypeStruct((TOP_K, t_part), jnp.int32),
                   jax.ShapeDtypeStruct((8, n_lanes), jnp.int32)],
        compiler_params=pltpu.CompilerParams(
            dimension_semantics=("arbitrary",), vmem_limit_bytes=VMEM_LIMIT),
        name="plan",
    )(cnt, idx, rank)
    return dest, tiles[0, :n_tiles], tiles[1, :n_tiles], tiles[2, :n_tiles]


def _sc_workers():
    info = plsc.get_sparse_core_info()
    return info.num_cores, info.num_cores * info.num_subcores


def _sc_dispatch(rows, dest, n_out):
    t_total, dw = rows.shape
    nc, nw = _sc_workers()
    per_w = t_total // nw
    mesh = plsc.VectorSubcoreMesh(core_axis_name="c", subcore_axis_name="s")

    @functools.partial(
        pl.kernel,
        out_type=jax.ShapeDtypeStruct((n_out, dw), rows.dtype),
        mesh=mesh,
        scratch_types=[pltpu.VMEM((SC_WINDOW,), jnp.int32) for _ in range(TOP_K)]
        + [pltpu.VMEM((SC_WINDOW, dw), rows.dtype), pltpu.SemaphoreType.DMA],
        name="sc_dispatch",
    )
    def k(x_hbm, i_hbm, o_hbm, i0, i1, i2, i3, rows_v, sem):
        wid = lax.axis_index("s") * nc + lax.axis_index("c")
        idx_bufs = (i0, i1, i2, i3)

        @pl.loop(0, per_w // SC_WINDOW)
        def _(j):
            base = wid * per_w + j * SC_WINDOW
            for kk in range(TOP_K):
                pltpu.sync_copy(i_hbm.at[kk, pl.ds(base, SC_WINDOW)], idx_bufs[kk])
            pltpu.sync_copy(x_hbm.at[pl.ds(base, SC_WINDOW)], rows_v)
            copies = [pltpu.async_copy(rows_v, o_hbm.at[idx_bufs[kk]], sem) for kk in range(TOP_K)]
            for cp in copies:
                cp.wait()

    return k(rows, dest)


def _sc_gather(table, idx):
    n_k, t_total = idx.shape
    n = n_k * t_total
    dw = table.shape[1]
    nc, nw = _sc_workers()
    per_w = n // nw
    w_per_k = nw // n_k
    mesh = plsc.VectorSubcoreMesh(core_axis_name="c", subcore_axis_name="s")

    @functools.partial(
        pl.kernel,
        out_type=jax.ShapeDtypeStruct((n, dw), table.dtype),
        mesh=mesh,
        scratch_types=[pltpu.VMEM((SC_WINDOW,), jnp.int32),
                       pltpu.VMEM((SC_WINDOW, dw), table.dtype),
                       pltpu.SemaphoreType.DMA],
        name="sc_gather",
    )
    def k(t_hbm, i_hbm, o_hbm, idx_v, rows_v, sem):
        wid = lax.axis_index("s") * nc + lax.axis_index("c")

        row = wid // w_per_k
        col0 = (wid % w_per_k) * per_w

        @pl.loop(0, per_w // SC_WINDOW)
        def _(j):
            col = col0 + j * SC_WINDOW
            pltpu.sync_copy(i_hbm.at[row, pl.ds(col, SC_WINDOW)], idx_v)
            pltpu.async_copy(t_hbm.at[idx_v], rows_v, sem).wait()
            pltpu.sync_copy(rows_v, o_hbm.at[pl.ds(row * t_total + col, SC_WINDOW)])

    return k(table, idx)


def _expert_kernel(te_ref, tv_ref, tn_ref, xs_ref, w1_hbm, w2_hbm, b1g_ref, b1l_ref, b2_ref,
                   perm_ref, y_ref, w1f_scr, w2f_scr, w1g_scr, w1l_scr, w2_scr, sems):
    i = pl.program_id(0)
    valid = tv_ref[i]
    expert = te_ref[i]
    f32 = jnp.float32
    bf16 = jnp.bfloat16
    expert_changed = (i == 0) | (expert != te_ref[jnp.maximum(i - 1, 0)])

    def weight_copies(e):
        return (pltpu.make_async_copy(w1_hbm.at[e], w1f_scr, sems.at[0]),
                pltpu.make_async_copy(w2_hbm.at[e], w2f_scr, sems.at[1]))

    @pl.when(valid == 0)
    def _():
        y_ref[...] = jnp.zeros_like(y_ref)

    @pl.when(i == 0)
    def _():
        for cp in weight_copies(expert):
            cp.start()

    @pl.when((valid > 0) & expert_changed)
    def _():
        for cp in weight_copies(expert):
            cp.wait()
        w2_scr[...] = w2f_scr[...].astype(bf16)
        perm = perm_ref[...]
        for c in range(2 * D_MODEL // PERM_BLOCK):
            blk = w1f_scr[:, c * PERM_BLOCK:(c + 1) * PERM_BLOCK].astype(bf16)
            r = jnp.dot(blk, perm, preferred_element_type=f32).astype(bf16)
            half = PERM_BLOCK // 2
            w1g_scr[:, c * half:(c + 1) * half] = r[:, :half]
            w1l_scr[:, c * half:(c + 1) * half] = r[:, half:]

        @pl.when(tn_ref[i] != expert)
        def _():
            for cp in weight_copies(tn_ref[i]):
                cp.start()

    def expert_rows(n_rows):
        keep = lax.broadcasted_iota(jnp.int32, (n_rows, 1), 0) < valid
        a, b = _unpack_rows(jnp.where(keep, xs_ref[:n_rows, :], 0))
        x = jnp.concatenate([a, b], axis=1).astype(bf16)
        hg = jnp.dot(x, w1g_scr[...], preferred_element_type=f32) + b1g_ref[0]
        xl1 = jnp.clip(jnp.dot(x, w1l_scr[...], preferred_element_type=f32) + (b1l_ref[0] + 1.0),
                       1.0 - SWIGLU_LIMIT, 1.0 + SWIGLU_LIMIT)
        xg = jnp.minimum(hg, SWIGLU_LIMIT)
        act = (0.5 * xg) * (jnp.tanh((0.5 * SWIGLU_ALPHA) * xg) + 1.0) * xl1
        y = jnp.dot(act.astype(bf16), w2_scr[...], preferred_element_type=f32) + b2_ref[0]
        y_ref[:n_rows, :] = _pack_rows(y)
        if n_rows < TM:
            y_ref[n_rows:, :] = jnp.zeros((TM - n_rows, HALF), jnp.int32)

    for blocks in range(TM // ROW_STEP, 0, -1):
        @pl.when((valid > (blocks - 1) * ROW_STEP) & (valid <= blocks * ROW_STEP))
        def _(blocks=blocks):
            expert_rows(blocks * ROW_STEP)


def _experts(xs, tile_expert, tile_valid, tile_next, w1, w2, b1g, b1l, b2, perm):
    n_slots = xs.shape[0]
    n_tiles = n_slots // TM
    d_ff = w2.shape[1]
    wmap = lambda i, te, tv, tn: (te[i], 0, 0)
    grid_spec = pltpu.PrefetchScalarGridSpec(
        num_scalar_prefetch=3,
        grid=(n_tiles,),
        in_specs=[
            pl.BlockSpec((TM, HALF), lambda i, te, tv, tn: (i, 0)),
            pl.BlockSpec(memory_space=pl.ANY),
            pl.BlockSpec(memory_space=pl.ANY),
            pl.BlockSpec((1, 1, d_ff), wmap),
            pl.BlockSpec((1, 1, d_ff), wmap),
            pl.BlockSpec((1, 1, D_MODEL), wmap),
            pl.BlockSpec((PERM_BLOCK, PERM_BLOCK), lambda i, te, tv, tn: (0, 0)),
        ],
        out_specs=pl.BlockSpec((TM, HALF), lambda i, te, tv, tn: (i, 0)),
        scratch_shapes=[
            pltpu.VMEM((D_MODEL, 2 * d_ff), jnp.float32),
            pltpu.VMEM((d_ff, D_MODEL), jnp.float32),
            pltpu.VMEM((D_MODEL, d_ff), jnp.bfloat16),
            pltpu.VMEM((D_MODEL, d_ff), jnp.bfloat16),
            pltpu.VMEM((d_ff, D_MODEL), jnp.bfloat16),
            pltpu.SemaphoreType.DMA((2,)),
        ],
    )
    return pl.pallas_call(
        _expert_kernel,
        grid_spec=grid_spec,
        out_shape=jax.ShapeDtypeStruct((n_slots, HALF), jnp.int32),
        compiler_params=pltpu.CompilerParams(
            dimension_semantics=("arbitrary",),
            vmem_limit_bytes=VMEM_LIMIT),
        name="experts",
    )(tile_expert, tile_valid, tile_next, xs, w1, w2, b1g, b1l, b2, perm)


def _combine_kernel(h_ref, yk_ref, gate_ref, g_ref, b_ref, *rest):
    o_ref = rest[-1]
    gates = gate_ref[...].T
    acc_a = None
    acc_b = None
    for k in range(TOP_K):
        a, b = _unpack_rows(yk_ref[k])
        gk = gates[:, k:k + 1]
        acc_a = gk * a if acc_a is None else acc_a + gk * a
        acc_b = gk * b if acc_b is None else acc_b + gk * b
    ffn = jnp.concatenate([acc_a, acc_b], axis=1)
    o_ref[...] = _layer_norm(ALPHA * h_ref[...] + ffn, g_ref[...], b_ref[...])


def _combine(h1f, yk, gates, g2, b2, out_prev, row0, t_all):
    t_part = h1f.shape[0]
    blk0 = row0 // TC3
    row = lambda i: (i, 0)
    const = lambda i: (0, 0)
    in_specs = [
        pl.BlockSpec((TC3, D_MODEL), row),
        pl.BlockSpec((TOP_K, TC3, HALF), lambda i: (0, i, 0)),
        pl.BlockSpec((TOP_K, TC3), lambda i: (0, i)),
        pl.BlockSpec((1, D_MODEL), const),
        pl.BlockSpec((1, D_MODEL), const),
    ]
    args = [h1f, yk, gates, g2, b2]
    aliases = {}
    if out_prev is not None:
        in_specs.append(pl.BlockSpec(memory_space=pl.ANY))
        args.append(out_prev)
        aliases = {len(args) - 1: 0}
    return pl.pallas_call(
        _combine_kernel,
        grid=(t_part // TC3,),
        in_specs=in_specs,
        out_specs=pl.BlockSpec((TC3, D_MODEL), lambda i: (i + blk0, 0)),
        out_shape=jax.ShapeDtypeStruct((t_all, D_MODEL), jnp.float32),
        input_output_aliases=aliases,
        compiler_params=pltpu.CompilerParams(
            dimension_semantics=("arbitrary",),
            vmem_limit_bytes=VMEM_LIMIT),
        name="combine",
    )(*args)


def kernel(x, w_in, lb_logits, hgrn_norm_g, gmlp_ln_g, gmlp_ln_b, gmlp_ws, gmlp_bs, w_out, ln1_g, ln1_b, router_w, router_b, exp_w1, exp_b1, exp_w2, exp_b2, ln2_g, ln2_b):
    batch, seq, d = x.shape
    assert d == D_MODEL and seq % TB == 0 and w_in.shape[0] == 1
    t_total = batch * seq
    f32 = jnp.float32
    bf16 = jnp.bfloat16

    lb = jnp.cumsum(jax.nn.softmax(lb_logits.astype(f32), axis=0), axis=0)[0:1]
    chunk_id = jnp.arange(GMLP_BLOCK) // CHUNK
    wm = jnp.where((chunk_id[None, :] <= chunk_id[:, None])[None], gmlp_ws[0], 0.0).astype(bf16)
    rwh = router_w[0].T.astype(bf16)
    tri = (jnp.arange(CHUNK)[None, :] <= jnp.arange(CHUNK)[:, None]).astype(bf16)
    upp = (jnp.arange(TB)[:, None] < jnp.arange(TB)[None, :]).astype(bf16)

    lane = jnp.arange(PERM_BLOCK)
    src = jnp.where(lane < PERM_BLOCK // 2, 2 * lane, 2 * (lane - PERM_BLOCK // 2) + 1)
    perm = (jnp.arange(PERM_BLOCK)[:, None] == src[None, :]).astype(bf16)
    xt = x.reshape(t_total, d)
    win = w_in[0].astype(bf16)
    wout = w_out[0].astype(bf16)
    b1g, b1l, b2e = exp_b1[0][:, None, 0::2], exp_b1[0][:, None, 1::2], exp_b2[0][:, None, :]

    last = max(batch * LAST_PART_NUM // LAST_PART_DEN, 1) if batch > 1 else 0
    part_sizes = [pb for pb in (batch - last, last) if pb > 0]
    out = None
    b0 = 0
    for pb in part_sizes:
        t_part = pb * seq
        n_slots = t_part * TOP_K + N_EXPERTS * TM
        n_tiles = n_slots // TM
        h1f, h1p, idx, gates, rank, cnt = _mixer(
            xt, win, lb, hgrn_norm_g[0:1], gmlp_ln_g[0:1], gmlp_ln_b[0:1], wm, gmlp_bs[0].T, wout,
            ln1_g[0:1], ln1_b[0:1], rwh, router_b[0][:, None], tri, upp, seq, pb, b0)

        dest, tile_expert, tile_valid, tile_next = _plan(cnt, idx, rank, n_tiles)
        xs = _sc_dispatch(h1p, dest, n_slots)
        y = _experts(xs, tile_expert, tile_valid, tile_next,
                     exp_w1[0], exp_w2[0], b1g, b1l, b2e, perm)
        yk = _sc_gather(y, dest).reshape(TOP_K, t_part, HALF)
        out = _combine(h1f, yk, gates, ln2_g[0:1], ln2_b[0:1], out, b0 * seq, t_total)
        b0 += pb
    return out.reshape(batch, seq, d)
```

```python
import functools
import math

import jax
import jax.numpy as jnp
from jax import lax
from jax.experimental import pallas as pl
from jax.experimental.pallas import tpu as pltpu
from jax.experimental.pallas import tpu_sc as plsc

D_MODEL = 1024
CHUNK = 64
SUB = 16
N_SUB = CHUNK // SUB
MAX_SUB_DECAY = 86.0
HGRN_WIDTH = 512
HGRN_HEADS = 4
HEAD_DIM = 128
GMLP_WIDTH = 512
GMLP_BLOCK = 128
GMLP_GROUPS = 4
IN_COLS = 3072
N_EXPERTS = 32
TOP_K = 4
SWIGLU_LIMIT = 7.0
SWIGLU_ALPHA = 1.702
ALPHA = 2.0 ** 0.25
EPS = 1e-5
HALF = D_MODEL // 2

TB = 512
TM = 1024
ROW_STEP = 256
PLAN_CHUNK = 2048
TC3 = 1024
LAST_PART_NUM, LAST_PART_DEN = 1, 4
SC_WINDOW = 128
PROJ_GROUP = 256
PERM_BLOCK = 256
LANES = 128
V7X_VMEM_BYTES = 64 * 1024 * 1024
VMEM_LIMIT = V7X_VMEM_BYTES * 7 // 8

_NT = (((1,), (1,)), ((), ()))


def _gelu(x):
    return 0.5 * x * (1.0 + lax.erf(x * (1.0 / math.sqrt(2.0))))


def _layer_norm(x, g, b):
    mu = jnp.mean(x, axis=-1, keepdims=True)
    xc = x - mu
    var = jnp.mean(xc * xc, axis=-1, keepdims=True)
    return xc * lax.rsqrt(var + EPS) * g + b


def _pack_rounded(hr):
    au = lax.bitcast_convert_type(hr[:, :HALF], jnp.uint32) >> 16
    bu = lax.bitcast_convert_type(hr[:, HALF:], jnp.uint32) & jnp.uint32(0xFFFF0000)
    return lax.bitcast_convert_type(au | bu, jnp.int32)


def _pack_rows(h):
    return _pack_rounded(h.astype(jnp.bfloat16).astype(jnp.float32))


def _unpack_rows(w):
    u = lax.bitcast_convert_type(w, jnp.uint32)
    a = lax.bitcast_convert_type(u << 16, jnp.float32)
    b = lax.bitcast_convert_type(u & jnp.uint32(0xFFFF0000), jnp.float32)
    return a, b


def _mixer_kernel(x_ref, xn_ref, win_ref, lb_ref, hg_ref, lng_ref, lnb_ref, wm_ref, bst_ref,
                  wout_ref, l1g_ref, l1b_ref, rwh_ref, rb_ref, tri_ref, upp_ref,
                  h1f_ref, h1p_ref, idx_ref, gate_ref, rank_ref, cnt_ref,
                  p_scr, lf_scr, kk_scr, g_scr, kc_scr, o_scr, st_scr, carry_scr,
                  amat_scr, qg_scr, kd_scr, upd_scr, mix_scr):
    b = pl.program_id(0)
    t = pl.program_id(1)
    step = b * pl.num_programs(1) + t
    f32 = jnp.float32
    bf16 = jnp.bfloat16

    @pl.when(t == 0)
    def _():
        st_scr[...] = jnp.zeros_like(st_scr)

    @pl.when((b == 0) & (t == 0))
    def _():
        carry_scr[...] = jnp.zeros_like(carry_scr)

    def project(xb, group):
        cols = slice(group * PROJ_GROUP, (group + 1) * PROJ_GROUP)
        p_scr[:, cols] = jnp.dot(xb, win_ref[:, cols], preferred_element_type=f32)

    def hgrn_gates():
        fl = p_scr[:, HGRN_WIDTH:2 * HGRN_WIDTH]
        z = jnp.exp(-jnp.abs(fl))
        r = 1.0 / (1.0 + z)
        k = (1.0 - lb_ref[...]) * jnp.where(fl >= 0, z * r, r)
        kk_scr[...] = k
        lf_scr[...] = jnp.log2(1.0 - k)

    @pl.when(step == 0)
    def _():
        xb0 = x_ref[...].astype(bf16)
        for group in range(IN_COLS // PROJ_GROUP):
            project(xb0, group)
        hgrn_gates()

    tri = tri_ref[...]
    row64 = lax.broadcasted_iota(jnp.int32, (CHUNK, CHUNK), 0)
    col64 = lax.broadcasted_iota(jnp.int32, (CHUNK, CHUNK), 1)
    lane_sub = lax.broadcasted_iota(jnp.int32, (SUB, CHUNK), 1)
    scale = HEAD_DIM ** -0.5

    def pad_rows(piece, lo_r):
        parts = []
        if lo_r > 0:
            parts.append(jnp.zeros((lo_r, HEAD_DIM), f32))
        parts.append(piece)
        rest = CHUNK - lo_r - piece.shape[0]
        if rest > 0:
            parts.append(jnp.zeros((rest, HEAD_DIM), f32))
        return jnp.concatenate(parts, axis=0) if len(parts) > 1 else piece

    def intra_factorised(q, k, gh):
        qts, kts = [], []
        for a in range(N_SUB):
            lo_r, hi_r = a * SUB, (a + 1) * SUB
            if a == 0:
                qa = q[:hi_r] * jnp.exp2(gh[:hi_r])
                ka = k[:hi_r] * jnp.exp2(-gh[:hi_r])
            else:
                ra = gh[lo_r - 1:lo_r]
                qa = q[lo_r:hi_r] * jnp.exp2(gh[lo_r:hi_r] - ra)
                ka = k[:hi_r] * jnp.exp2(ra - gh[:hi_r])
            qts.append(pad_rows(qa, lo_r))
            kts.append(pad_rows(ka, 0))
        a_mat = lax.dot_general(jnp.concatenate(qts, axis=1).astype(bf16),
                                jnp.concatenate(kts, axis=1).astype(bf16), _NT,
                                preferred_element_type=f32)
        return jnp.where(row64 >= col64, a_mat, 0.0)

    def chunks_factorised():
        n_chunks = TB // CHUNK
        heads = [(c, h) for c in range(n_chunks) for h in range(HGRN_HEADS)]

        def cols(h, base=0):
            return slice(base + h * HEAD_DIM, base + (h + 1) * HEAD_DIM)

        for c in range(n_chunks):
            rows = slice(c * CHUNK, (c + 1) * CHUNK)
            lf = lf_scr[rows, :]
            hi = lf.astype(bf16)
            lo = (lf - hi.astype(f32)).astype(bf16)
            gg = jnp.dot(tri, jnp.concatenate([hi, lo], axis=1), preferred_element_type=f32)
            lf_scr[rows, :] = gg[:, :HGRN_WIDTH] + gg[:, HGRN_WIDTH:]
        for c, h in heads:
            rows = slice(c * CHUNK, (c + 1) * CHUNK)
            q = p_scr[rows, cols(h)]
            k = kk_scr[rows, cols(h)]
            gh = lf_scr[rows, cols(h)]
            amat_scr[c * HGRN_HEADS + h] = intra_factorised(q, k, gh).astype(bf16)
            qg_scr[rows, cols(h)] = (q * jnp.exp2(gh)).astype(bf16)
            kd_scr[rows, cols(h)] = (k * jnp.exp2(gh[CHUNK - 1:CHUNK] - gh)).astype(bf16)
        for c, h in heads:
            rows = slice(c * CHUNK, (c + 1) * CHUNK)
            v = p_scr[rows, cols(h, 2 * HGRN_WIDTH)]
            o_scr[rows, cols(h)] = jnp.dot(amat_scr[c * HGRN_HEADS + h], v.astype(bf16),
                                           preferred_element_type=f32)
            upd_scr[c * HGRN_HEADS + h] = jnp.dot(v.T.astype(bf16), kd_scr[rows, cols(h)],
                                                  preferred_element_type=f32)
        states = [st_scr[h] for h in range(HGRN_HEADS)]
        for c, h in heads:
            rows = slice(c * CHUNK, (c + 1) * CHUNK)
            st = states[h]
            o_inter = lax.dot_general(qg_scr[rows, cols(h)], st.astype(bf16), _NT,
                                      preferred_element_type=f32)
            o_scr[rows, cols(h)] = (o_scr[rows, cols(h)] + o_inter) * scale
            gl = lf_scr[(c + 1) * CHUNK - 1:(c + 1) * CHUNK, cols(h)]
            states[h] = st * jnp.exp2(gl) + upd_scr[c * HGRN_HEADS + h]
        for h in range(HGRN_HEADS):
            st_scr[h] = states[h]

    def intra_exact_diagonal(q, k, gh, cs):
        qts, kts = [], []
        for a in range(1, N_SUB):
            lo_r, hi_r = a * SUB, (a + 1) * SUB
            ra = g_scr[lo_r - 1:lo_r, cs]
            qts.append(pad_rows(q[lo_r:hi_r] * jnp.exp2(gh[lo_r:hi_r] - ra), lo_r))
            kts.append(pad_rows(k[:lo_r] * jnp.exp2(ra - gh[:lo_r]), 0))
        a_off = lax.dot_general(jnp.concatenate(qts, axis=1).astype(bf16),
                                jnp.concatenate(kts, axis=1).astype(bf16), _NT,
                                preferred_element_type=f32)
        diag_rows = []
        for a in range(N_SUB):
            lo_r = a * SUB
            gs = gh[lo_r:lo_r + SUB]
            qs = q[lo_r:lo_r + SUB]
            blk = jnp.zeros((SUB, CHUNK), f32)
            for jl in range(SUB):
                j = lo_r + jl
                gj = g_scr[j:j + 1, cs]
                kj = kc_scr[j:j + 1, cs]
                e = jnp.exp2(jnp.minimum(gs - gj, 0.0))
                col = jnp.sum(qs * (kj * e), axis=-1, keepdims=True)
                blk = jnp.where(lane_sub == j, col, blk)
            diag_rows.append(blk)
        a_diag = jnp.concatenate(diag_rows, axis=0)
        return a_off + jnp.where(row64 >= col64, a_diag, 0.0)

    def make_chunk_body(intra):
        def chunk_body(c, carry):
            r0 = pl.multiple_of(c * CHUNK, CHUNK)
            rows = pl.ds(r0, CHUNK)
            lf = lf_scr[rows, :]
            hi = lf.astype(bf16)
            lo = (lf - hi.astype(f32)).astype(bf16)
            gg = jnp.dot(tri, jnp.concatenate([hi, lo], axis=1), preferred_element_type=f32)
            g_all = gg[:, :HGRN_WIDTH] + gg[:, HGRN_WIDTH:]
            g_scr[...] = g_all
            kc_scr[...] = kk_scr[rows, :]
            for h in range(HGRN_HEADS):
                cs = slice(h * HEAD_DIM, (h + 1) * HEAD_DIM)
                q = p_scr[rows, h * HEAD_DIM:(h + 1) * HEAD_DIM]
                v = p_scr[rows, 2 * HGRN_WIDTH + h * HEAD_DIM:2 * HGRN_WIDTH + (h + 1) * HEAD_DIM]
                k = kc_scr[:, cs]
                gh = g_all[:, cs]
                st = st_scr[h]
                o_inter = lax.dot_general((q * jnp.exp2(gh)).astype(bf16), st.astype(bf16), _NT,
                                          preferred_element_type=f32)
                a_mat = intra(q, k, gh, cs)
                o = jnp.dot(a_mat.astype(bf16), v.astype(bf16), preferred_element_type=f32) + o_inter
                o_scr[rows, h * HEAD_DIM:(h + 1) * HEAD_DIM] = o * scale

                gl = g_scr[CHUNK - 1:CHUNK, cs]
                kd = k * jnp.exp2(gl - gh)
                upd = jnp.dot(v.T.astype(bf16), kd.astype(bf16), preferred_element_type=f32)
                st_scr[h] = st * jnp.exp2(gl) + upd
            return carry
        return chunk_body

    lf_all = lf_scr[...]
    sub_decay = -jnp.sum(lf_all.reshape(TB // SUB, SUB, HGRN_WIDTH), axis=1)
    bounded = jnp.max(sub_decay) <= MAX_SUB_DECAY

    f_groups = list(range(HGRN_WIDTH // PROJ_GROUP, 2 * HGRN_WIDTH // PROJ_GROUP))
    qi_groups = [g for g in range(3 * HGRN_WIDTH // PROJ_GROUP) if g not in f_groups]
    rest_groups = list(range(3 * HGRN_WIDTH // PROJ_GROUP, IN_COLS // PROJ_GROUP))

    def gmlp_stage(xnb):
        for g in f_groups:
            project(xnb, g)
        for w in range(TB // GMLP_BLOCK):
            rows = slice(w * GMLP_BLOCK, (w + 1) * GMLP_BLOCK)
            u = _gelu(p_scr[rows, 4 * HGRN_WIDTH:4 * HGRN_WIDTH + GMLP_WIDTH])
            vn = _layer_norm(_gelu(p_scr[rows, 4 * HGRN_WIDTH + GMLP_WIDTH:]),
                             lng_ref[...], lnb_ref[...])
            vnb = vn.astype(bf16)
            cols = []
            for g in range(GMLP_GROUPS):
                s = jnp.dot(wm_ref[g], vnb[:, g * HEAD_DIM:(g + 1) * HEAD_DIM],
                            preferred_element_type=f32) + bst_ref[:, g:g + 1]
                cols.append(s)
            mix_scr[rows, HGRN_WIDTH:] = (u * jnp.concatenate(cols, axis=1)).astype(bf16)

    def output_stage(xnb):
        for g in qi_groups:
            project(xnb, g)
        for w in range(TB // GMLP_BLOCK):
            rows = slice(w * GMLP_BLOCK, (w + 1) * GMLP_BLOCK)
            o = o_scr[rows, :]
            ms = jnp.mean(o * o, axis=-1, keepdims=True)
            gate = p_scr[rows, 3 * HGRN_WIDTH:4 * HGRN_WIDTH]
            half_gate = 0.5 * gate
            silu = half_gate * (jnp.tanh(half_gate) + 1.0)
            y_rec = o * lax.rsqrt(ms + EPS) * hg_ref[...] * silu
            mix_scr[rows, :HGRN_WIDTH] = y_rec.astype(bf16)

        mix = jnp.dot(mix_scr[...], wout_ref[...], preferred_element_type=f32)
        hgrn_gates()
        h1 = _layer_norm(ALPHA * x_ref[...] + mix, l1g_ref[...], l1b_ref[...])
        h1f_ref[...] = h1
        hh = h1.astype(bf16)
        h1_rounded = hh.astype(f32)
        h1p_ref[...] = _pack_rounded(h1_rounded)
        for g in rest_groups[:2]:
            project(xnb, g)

        logits = lax.dot_general(rwh_ref[...], hh, _NT, preferred_element_type=f32) + rb_ref[...]
        for g in rest_groups[2:]:
            project(xnb, g)

        e_iota = lax.broadcasted_iota(jnp.int32, (N_EXPERTS, TB), 0)
        work = logits
        vals, idxs, hots = [], [], []
        for _ in range(TOP_K):
            m = jnp.max(work, axis=0, keepdims=True)
            ik = jnp.min(jnp.where(work == m, e_iota, N_EXPERTS), axis=0, keepdims=True)
            hot = e_iota == ik
            work = jnp.where(hot, -jnp.inf, work)
            vals.append(m)
            idxs.append(ik)
            hots.append(hot)
        exps = [jnp.exp(vk - vals[0]) for vk in vals]
        denom = exps[0] + exps[1] + exps[2] + exps[3]
        gate_ref[...] = jnp.concatenate([ek / denom for ek in exps], axis=0)
        idx_ref[...] = jnp.concatenate(idxs, axis=0)

        hot_any = jnp.where(hots[0] | hots[1] | hots[2] | hots[3], 1.0, 0.0)
        prefix = jnp.dot(hot_any.astype(bf16), upp_ref[...], preferred_element_type=f32)
        base = prefix + carry_scr[:, 0:1]
        ranks = [jnp.sum(jnp.where(hk, base, 0.0), axis=0, keepdims=True) for hk in hots]
        rank_ref[...] = jnp.concatenate(ranks, axis=0).astype(jnp.int32)
        new_carry = carry_scr[...] + jnp.sum(hot_any, axis=1, keepdims=True)
        carry_scr[...] = new_carry
        cnt_ref[...] = new_carry

    @pl.when(bounded)
    def _():
        xnb = xn_ref[...].astype(bf16)
        gmlp_stage(xnb)
        chunks_factorised()
        output_stage(xnb)

    @pl.when(jnp.logical_not(bounded))
    def _():
        xnb = xn_ref[...].astype(bf16)
        lax.fori_loop(0, TB // CHUNK, make_chunk_body(intra_exact_diagonal), 0)
        gmlp_stage(xnb)
        output_stage(xnb)


def _mixer(xt, win, lb, hg, lng, lnb, wm, bst, wout, l1g, l1b, rwh, rb, tri, upp,
           seq, batch, batch0):
    nt = seq // TB
    t_total = batch * seq
    nblk = batch * nt
    const2 = lambda b, t: (0, 0)
    const3 = lambda b, t: (0, 0, 0)
    row_blk = lambda b, t: (b * nt + t, 0)
    meta_blk = lambda b, t: (0, b * nt + t)
    once = dict(pipeline_mode=pl.Buffered(1))
    in_specs = [
        pl.BlockSpec((TB, D_MODEL), lambda b, t: ((b + batch0) * nt + t, 0)),
        pl.BlockSpec((TB, D_MODEL),
                     lambda b, t: (batch0 * nt + jnp.minimum(b * nt + t + 1, nblk - 1), 0)),
        pl.BlockSpec((D_MODEL, IN_COLS), const2, **once),
        pl.BlockSpec((1, HGRN_WIDTH), const2),
        pl.BlockSpec((1, HGRN_WIDTH), const2),
        pl.BlockSpec((1, GMLP_WIDTH), const2),
        pl.BlockSpec((1, GMLP_WIDTH), const2),
        pl.BlockSpec((GMLP_GROUPS, GMLP_BLOCK, GMLP_BLOCK), const3),
        pl.BlockSpec((GMLP_BLOCK, GMLP_GROUPS), const2),
        pl.BlockSpec((D_MODEL, D_MODEL), const2, **once),
        pl.BlockSpec((1, D_MODEL), const2),
        pl.BlockSpec((1, D_MODEL), const2),
        pl.BlockSpec((N_EXPERTS, D_MODEL), const2),
        pl.BlockSpec((N_EXPERTS, 1), const2),
        pl.BlockSpec((CHUNK, CHUNK), const2),
        pl.BlockSpec((TB, TB), const2, **once),
    ]
    out_shape = [
        jax.ShapeDtypeStruct((t_total, D_MODEL), jnp.float32),
        jax.ShapeDtypeStruct((t_total, HALF), jnp.int32),
        jax.ShapeDtypeStruct((TOP_K, t_total), jnp.int32),
        jax.ShapeDtypeStruct((TOP_K, t_total), jnp.float32),
        jax.ShapeDtypeStruct((TOP_K, t_total), jnp.int32),
        jax.ShapeDtypeStruct((N_EXPERTS, LANES), jnp.float32),
    ]
    out_specs = [
        pl.BlockSpec((TB, D_MODEL), row_blk),
        pl.BlockSpec((TB, HALF), row_blk),
        pl.BlockSpec((TOP_K, TB), meta_blk),
        pl.BlockSpec((TOP_K, TB), meta_blk),
        pl.BlockSpec((TOP_K, TB), meta_blk),
        pl.BlockSpec((N_EXPERTS, LANES), const2),
    ]
    scratch = [
        pltpu.VMEM((TB, IN_COLS), jnp.float32),
        pltpu.VMEM((TB, HGRN_WIDTH), jnp.float32),
        pltpu.VMEM((TB, HGRN_WIDTH), jnp.float32),
        pltpu.VMEM((CHUNK, HGRN_WIDTH), jnp.float32),
        pltpu.VMEM((CHUNK, HGRN_WIDTH), jnp.float32),
        pltpu.VMEM((TB, HGRN_WIDTH), jnp.float32),
        pltpu.VMEM((HGRN_HEADS, HEAD_DIM, HEAD_DIM), jnp.float32),
        pltpu.VMEM((N_EXPERTS, LANES), jnp.float32),
        pltpu.VMEM((TB // CHUNK * HGRN_HEADS, CHUNK, CHUNK), jnp.bfloat16),
        pltpu.VMEM((TB, HGRN_WIDTH), jnp.bfloat16),
        pltpu.VMEM((TB, HGRN_WIDTH), jnp.bfloat16),
        pltpu.VMEM((TB // CHUNK * HGRN_HEADS, HEAD_DIM, HEAD_DIM), jnp.float32),
        pltpu.VMEM((TB, D_MODEL), jnp.bfloat16),
    ]
    return pl.pallas_call(
        _mixer_kernel,
        grid=(batch, nt),
        in_specs=in_specs,
        out_specs=out_specs,
        out_shape=out_shape,
        scratch_shapes=scratch,
        compiler_params=pltpu.CompilerParams(
            dimension_semantics=("arbitrary", "arbitrary"),
            vmem_limit_bytes=VMEM_LIMIT),
        name="mixer",
    )(xt, xt, win, lb, hg, lng, lnb, wm, bst, wout, l1g, l1b, rwh, rb, tri, upp)


def _plan_kernel(cnt_ref, idx_ref, rank_ref, dest_ref, tiles_ref):
    f32 = jnp.float32
    n_e = N_EXPERTS
    e_sub = lax.broadcasted_iota(jnp.int32, (n_e, n_e), 0)
    e_lane = lax.broadcasted_iota(jnp.int32, (n_e, n_e), 1)
    counts = cnt_ref[:, 0:1]
    padded = jnp.floor((counts + (TM - 1)) * (1.0 / TM)) * TM
    as_row = lambda col: jnp.sum(jnp.where(e_sub == e_lane, col, 0.0), axis=0, keepdims=True)
    padded_row = as_row(padded)
    counts_row = as_row(counts)
    ends = jnp.sum(jnp.where(e_lane <= e_sub, padded_row, 0.0), axis=1, keepdims=True)
    starts = ends - padded
    owns_later = (e_lane > e_sub) & (counts_row > 0.0)
    nxt = jnp.min(jnp.where(owns_later, e_lane, n_e), axis=1, keepdims=True)
    own = lax.broadcasted_iota(jnp.int32, (n_e, 1), 0)
    nxt = jnp.where(nxt == n_e, own, nxt).astype(f32)

    n_lanes = tiles_ref.shape[1]
    tile_row = lax.broadcasted_iota(jnp.int32, (n_e, n_lanes), 1).astype(f32) * TM
    e_of = lax.broadcasted_iota(jnp.int32, (n_e, n_lanes), 0)
    tile_e = jnp.minimum(jnp.sum(jnp.where(tile_row >= ends, 1, 0), axis=0, keepdims=True), n_e - 1)
    mine = e_of == tile_e
    pick = lambda col: jnp.sum(jnp.where(mine, col, 0.0), axis=0, keepdims=True)
    valid = jnp.clip(pick(starts + counts) - tile_row[0:1], 0.0, float(TM))
    rows = [tile_e, valid.astype(jnp.int32), pick(nxt).astype(jnp.int32)]
    rows.append(jnp.zeros((tiles_ref.shape[0] - len(rows), n_lanes), jnp.int32))
    tiles_ref[...] = jnp.concatenate(rows, axis=0)

    chunk = PLAN_CHUNK
    e_chunk = lax.broadcasted_iota(jnp.int32, (n_e, chunk), 0)

    def body(c, carry):
        lanes = pl.ds(pl.multiple_of(c * chunk, chunk), chunk)
        for k in range(TOP_K):
            hit = e_chunk == idx_ref[k:k + 1, lanes]
            start_of = jnp.sum(jnp.where(hit, starts, 0.0), axis=0, keepdims=True)
            dest_ref[k:k + 1, lanes] = rank_ref[k:k + 1, lanes] + start_of.astype(jnp.int32)
        return carry

    lax.fori_loop(0, idx_ref.shape[1] // chunk, body, 0)


def _plan(cnt, idx, rank, n_tiles):
    t_part = idx.shape[1]
    n_lanes = -(-n_tiles // LANES) * LANES
    full = lambda shape: pl.BlockSpec(shape, lambda i: (0,) * len(shape))
    dest, tiles = pl.pallas_call(
        _plan_kernel,
        grid=(1,),
        in_specs=[full(cnt.shape), full(idx.shape), full(rank.shape)],
        out_specs=[full(idx.shape), full((8, n_lanes))],
        out_shape=[jax.ShapeDtypeStruct((TOP_K, t_part), jnp.int32),
                   jax.ShapeDtypeStruct((8, n_lanes), jnp.int32)],
        compiler_params=pltpu.CompilerParams(
            dimension_semantics=("arbitrary",), vmem_limit_bytes=VMEM_LIMIT),
        name="plan",
    )(cnt, idx, rank)
    return dest, tiles[0, :n_tiles], tiles[1, :n_tiles], tiles[2, :n_tiles]


def _sc_workers():
    info = plsc.get_sparse_core_info()
    return info.num_cores, info.num_cores * info.num_subcores


def _sc_dispatch(rows, dest, n_out):
    t_total, dw = rows.shape
    nc, nw = _sc_workers()
    per_w = t_total // nw
    mesh = plsc.VectorSubcoreMesh(core_axis_name="c", subcore_axis_name="s")

    @functools.partial(
        pl.kernel,
        out_type=jax.ShapeDtypeStruct((n_out, dw), rows.dtype),
        mesh=mesh,
        scratch_types=[pltpu.VMEM((SC_WINDOW,), jnp.int32) for _ in range(TOP_K)]
        + [pltpu.VMEM((SC_WINDOW, dw), rows.dtype), pltpu.SemaphoreType.DMA],
        name="sc_dispatch",
    )
    def k(x_hbm, i_hbm, o_hbm, i0, i1, i2, i3, rows_v, sem):
        wid = lax.axis_index("s") * nc + lax.axis_index("c")
        idx_bufs = (i0, i1, i2, i3)

        @pl.loop(0, per_w // SC_WINDOW)
        def _(j):
            base = wid * per_w + j * SC_WINDOW
            for kk in range(TOP_K):
                pltpu.sync_copy(i_hbm.at[kk, pl.ds(base, SC_WINDOW)], idx_bufs[kk])
            pltpu.sync_copy(x_hbm.at[pl.ds(base, SC_WINDOW)], rows_v)
            copies = [pltpu.async_copy(rows_v, o_hbm.at[idx_bufs[kk]], sem) for kk in range(TOP_K)]
            for cp in copies:
                cp.wait()

    return k(rows, dest)


def _sc_gather(table, idx):
    n_k, t_total = idx.shape
    n = n_k * t_total
    dw = table.shape[1]
    nc, nw = _sc_workers()
    per_w = n // nw
    w_per_k = nw // n_k
    mesh = plsc.VectorSubcoreMesh(core_axis_name="c", subcore_axis_name="s")

    @functools.partial(
        pl.kernel,
        out_type=jax.ShapeDtypeStruct((n, dw), table.dtype),
        mesh=mesh,
        scratch_types=[pltpu.VMEM((SC_WINDOW,), jnp.int32),
                       pltpu.VMEM((SC_WINDOW, dw), table.dtype),
                       pltpu.SemaphoreType.DMA],
        name="sc_gather",
    )
    def k(t_hbm, i_hbm, o_hbm, idx_v, rows_v, sem):
        wid = lax.axis_index("s") * nc + lax.axis_index("c")

        row = wid // w_per_k
        col0 = (wid % w_per_k) * per_w

        @pl.loop(0, per_w // SC_WINDOW)
        def _(j):
            col = col0 + j * SC_WINDOW
            pltpu.sync_copy(i_hbm.at[row, pl.ds(col, SC_WINDOW)], idx_v)
            pltpu.async_copy(t_hbm.at[idx_v], rows_v, sem).wait()
            pltpu.sync_copy(rows_v, o_hbm.at[pl.ds(row * t_total + col, SC_WINDOW)])

    return k(table, idx)


def _expert_kernel(te_ref, tv_ref, tn_ref, xs_ref, w1_hbm, w2_hbm, b1g_ref, b1l_ref, b2_ref,
                   perm_ref, y_ref, w1f_scr, w2f_scr, w1g_scr, w1l_scr, w2_scr, sems):
    i = pl.program_id(0)
    valid = tv_ref[i]
    expert = te_ref[i]
    f32 = jnp.float32
    bf16 = jnp.bfloat16
    expert_changed = (i == 0) | (expert != te_ref[jnp.maximum(i - 1, 0)])

    def weight_copies(e):
        return (pltpu.make_async_copy(w1_hbm.at[e], w1f_scr, sems.at[0]),
                pltpu.make_async_copy(w2_hbm.at[e], w2f_scr, sems.at[1]))

    @pl.when(valid == 0)
    def _():
        y_ref[...] = jnp.zeros_like(y_ref)

    @pl.when(i == 0)
    def _():
        for cp in weight_copies(expert):
            cp.start()

    @pl.when((valid > 0) & expert_changed)
    def _():
        for cp in weight_copies(expert):
            cp.wait()
        w2_scr[...] = w2f_scr[...].astype(bf16)
        perm = perm_ref[...]
        for c in range(2 * D_MODEL // PERM_BLOCK):
            blk = w1f_scr[:, c * PERM_BLOCK:(c + 1) * PERM_BLOCK].astype(bf16)
            r = jnp.dot(blk, perm, preferred_element_type=f32).astype(bf16)
            half = PERM_BLOCK // 2
            w1g_scr[:, c * half:(c + 1) * half] = r[:, :half]
            w1l_scr[:, c * half:(c + 1) * half] = r[:, half:]

        @pl.when(tn_ref[i] != expert)
        def _():
            for cp in weight_copies(tn_ref[i]):
                cp.start()

    def expert_rows(n_rows):
        keep = lax.broadcasted_iota(jnp.int32, (n_rows, 1), 0) < valid
        a, b = _unpack_rows(jnp.where(keep, xs_ref[:n_rows, :], 0))
        x = jnp.concatenate([a, b], axis=1).astype(bf16)
        hg = jnp.dot(x, w1g_scr[...], preferred_element_type=f32) + b1g_ref[0]
        xl1 = jnp.clip(jnp.dot(x, w1l_scr[...], preferred_element_type=f32) + (b1l_ref[0] + 1.0),
                       1.0 - SWIGLU_LIMIT, 1.0 + SWIGLU_LIMIT)
        xg = jnp.minimum(hg, SWIGLU_LIMIT)
        act = (0.5 * xg) * (jnp.tanh((0.5 * SWIGLU_ALPHA) * xg) + 1.0) * xl1
        y = jnp.dot(act.astype(bf16), w2_scr[...], preferred_element_type=f32) + b2_ref[0]
        y_ref[:n_rows, :] = _pack_rows(y)
        if n_rows < TM:
            y_ref[n_rows:, :] = jnp.zeros((TM - n_rows, HALF), jnp.int32)

    for blocks in range(TM // ROW_STEP, 0, -1):
        @pl.when((valid > (blocks - 1) * ROW_STEP) & (valid <= blocks * ROW_STEP))
        def _(blocks=blocks):
            expert_rows(blocks * ROW_STEP)


def _experts(xs, tile_expert, tile_valid, tile_next, w1, w2, b1g, b1l, b2, perm):
    n_slots = xs.shape[0]
    n_tiles = n_slots // TM
    d_ff = w2.shape[1]
    wmap = lambda i, te, tv, tn: (te[i], 0, 0)
    grid_spec = pltpu.PrefetchScalarGridSpec(
        num_scalar_prefetch=3,
        grid=(n_tiles,),
        in_specs=[
            pl.BlockSpec((TM, HALF), lambda i, te, tv, tn: (i, 0)),
            pl.BlockSpec(memory_space=pl.ANY),
            pl.BlockSpec(memory_space=pl.ANY),
            pl.BlockSpec((1, 1, d_ff), wmap),
            pl.BlockSpec((1, 1, d_ff), wmap),
            pl.BlockSpec((1, 1, D_MODEL), wmap),
            pl.BlockSpec((PERM_BLOCK, PERM_BLOCK), lambda i, te, tv, tn: (0, 0)),
        ],
        out_specs=pl.BlockSpec((TM, HALF), lambda i, te, tv, tn: (i, 0)),
        scratch_shapes=[
            pltpu.VMEM((D_MODEL, 2 * d_ff), jnp.float32),
            pltpu.VMEM((d_ff, D_MODEL), jnp.float32),
            pltpu.VMEM((D_MODEL, d_ff), jnp.bfloat16),
            pltpu.VMEM((D_MODEL, d_ff), jnp.bfloat16),
            pltpu.VMEM((d_ff, D_MODEL), jnp.bfloat16),
            pltpu.SemaphoreType.DMA((2,)),
        ],
    )
    return pl.pallas_call(
        _expert_kernel,
        grid_spec=grid_spec,
        out_shape=jax.ShapeDtypeStruct((n_slots, HALF), jnp.int32),
        compiler_params=pltpu.CompilerParams(
            dimension_semantics=("arbitrary",),
            vmem_limit_bytes=VMEM_LIMIT),
        name="experts",
    )(tile_expert, tile_valid, tile_next, xs, w1, w2, b1g, b1l, b2, perm)


def _combine_kernel(h_ref, yk_ref, gate_ref, g_ref, b_ref, *rest):
    o_ref = rest[-1]
    gates = gate_ref[...].T
    acc_a = None
    acc_b = None
    for k in range(TOP_K):
        a, b = _unpack_rows(yk_ref[k])
        gk = gates[:, k:k + 1]
        acc_a = gk * a if acc_a is None else acc_a + gk * a
        acc_b = gk * b if acc_b is None else acc_b + gk * b
    ffn = jnp.concatenate([acc_a, acc_b], axis=1)
    o_ref[...] = _layer_norm(ALPHA * h_ref[...] + ffn, g_ref[...], b_ref[...])


def _combine(h1f, yk, gates, g2, b2, out_prev, row0, t_all):
    t_part = h1f.shape[0]
    blk0 = row0 // TC3
    row = lambda i: (i, 0)
    const = lambda i: (0, 0)
    in_specs = [
        pl.BlockSpec((TC3, D_MODEL), row),
        pl.BlockSpec((TOP_K, TC3, HALF), lambda i: (0, i, 0)),
        pl.BlockSpec((TOP_K, TC3), lambda i: (0, i)),
        pl.BlockSpec((1, D_MODEL), const),
        pl.BlockSpec((1, D_MODEL), const),
    ]
    args = [h1f, yk, gates, g2, b2]
    aliases = {}
    if out_prev is not None:
        in_specs.append(pl.BlockSpec(memory_space=pl.ANY))
        args.append(out_prev)
        aliases = {len(args) - 1: 0}
    return pl.pallas_call(
        _combine_kernel,
        grid=(t_part // TC3,),
        in_specs=in_specs,
        out_specs=pl.BlockSpec((TC3, D_MODEL), lambda i: (i + blk0, 0)),
        out_shape=jax.ShapeDtypeStruct((t_all, D_MODEL), jnp.float32),
        input_output_aliases=aliases,
        compiler_params=pltpu.CompilerParams(
            dimension_semantics=("arbitrary",),
            vmem_limit_bytes=VMEM_LIMIT),
        name="combine",
    )(*args)


def kernel(x, w_in, lb_logits, hgrn_norm_g, gmlp_ln_g, gmlp_ln_b, gmlp_ws, gmlp_bs, w_out, ln1_g, ln1_b, router_w, router_b, exp_w1, exp_b1, exp_w2, exp_b2, ln2_g, ln2_b):
    batch, seq, d = x.shape
    assert d == D_MODEL and seq % TB == 0 and w_in.shape[0] == 1
    t_total = batch * seq
    f32 = jnp.float32
    bf16 = jnp.bfloat16

    lb = jnp.cumsum(jax.nn.softmax(lb_logits.astype(f32), axis=0), axis=0)[0:1]
    chunk_id = jnp.arange(GMLP_BLOCK) // CHUNK
    wm = jnp.where((chunk_id[None, :] <= chunk_id[:, None])[None], gmlp_ws[0], 0.0).astype(bf16)
    rwh = router_w[0].T.astype(bf16)
    tri = (jnp.arange(CHUNK)[None, :] <= jnp.arange(CHUNK)[:, None]).astype(bf16)
    upp = (jnp.arange(TB)[:, None] < jnp.arange(TB)[None, :]).astype(bf16)

    lane = jnp.arange(PERM_BLOCK)
    src = jnp.where(lane < PERM_BLOCK // 2, 2 * lane, 2 * (lane - PERM_BLOCK // 2) + 1)
    perm = (jnp.arange(PERM_BLOCK)[:, None] == src[None, :]).astype(bf16)
    xt = x.reshape(t_total, d)
    win = w_in[0].astype(bf16)
    wout = w_out[0].astype(bf16)
    b1g, b1l, b2e = exp_b1[0][:, None, 0::2], exp_b1[0][:, None, 1::2], exp_b2[0][:, None, :]

    last = max(batch * LAST_PART_NUM // LAST_PART_DEN, 1) if batch > 1 else 0
    part_sizes = [pb for pb in (batch - last, last) if pb > 0]
    out = None
    b0 = 0
    for pb in part_sizes:
        t_part = pb * seq
        n_slots = t_part * TOP_K + N_EXPERTS * TM
        n_tiles = n_slots // TM
        h1f, h1p, idx, gates, rank, cnt = _mixer(
            xt, win, lb, hgrn_norm_g[0:1], gmlp_ln_g[0:1], gmlp_ln_b[0:1], wm, gmlp_bs[0].T, wout,
            ln1_g[0:1], ln1_b[0:1], rwh, router_b[0][:, None], tri, upp, seq, pb, b0)

        dest, tile_expert, tile_valid, tile_next = _plan(cnt, idx, rank, n_tiles)
        xs = _sc_dispatch(h1p, dest, n_slots)
        y = _experts(xs, tile_expert, tile_valid, tile_next,
                     exp_w1[0], exp_w2[0], b1g, b1l, b2e, perm)
        yk = _sc_gather(y, dest).reshape(TOP_K, t_part, HALF)
        out = _combine(h1f, yk, gates, ln2_g[0:1], ln2_b[0:1], out, b0 * seq, t_total)
        b0 += pb
    return out.reshape(batch, seq, d)
```

```python
import functools
import math

import jax
import jax.numpy as jnp
from jax import lax
from jax.experimental import pallas as pl
from jax.experimental.pallas import tpu as pltpu
from jax.experimental.pallas import tpu_sc as plsc

D_MODEL = 1024
CHUNK = 64
SUB = 16
N_SUB = CHUNK // SUB
MAX_SUB_DECAY = 86.0
HGRN_WIDTH = 512
HGRN_HEADS = 4
HEAD_DIM = 128
GMLP_WIDTH = 512
GMLP_BLOCK = 128
GMLP_GROUPS = 4
IN_COLS = 3072
N_EXPERTS = 32
TOP_K = 4
SWIGLU_LIMIT = 7.0
SWIGLU_ALPHA = 1.702
ALPHA = 2.0 ** 0.25
EPS = 1e-5
HALF = D_MODEL // 2

TB = 512
TM = 1024
ROW_STEP = 256
TILES_PER_STEP = 2
PLAN_CHUNK = 2048
TC3 = 1024
LAST_PART_NUM, LAST_PART_DEN = 1, 4
SC_WINDOW = 128
PROJ_GROUP = 256
PERM_BLOCK = 256
LANES = 128
V7X_VMEM_BYTES = 64 * 1024 * 1024
VMEM_LIMIT = V7X_VMEM_BYTES * 7 // 8

_NT = (((1,), (1,)), ((), ()))


def _gelu(x):
    return 0.5 * x * (1.0 + lax.erf(x * (1.0 / math.sqrt(2.0))))


def _layer_norm(x, g, b):
    mu = jnp.mean(x, axis=-1, keepdims=True)
    xc = x - mu
    var = jnp.mean(xc * xc, axis=-1, keepdims=True)
    return xc * lax.rsqrt(var + EPS) * g + b


def _pack_rounded(hr):
    au = lax.bitcast_convert_type(hr[:, :HALF], jnp.uint32) >> 16
    bu = lax.bitcast_convert_type(hr[:, HALF:], jnp.uint32) & jnp.uint32(0xFFFF0000)
    return lax.bitcast_convert_type(au | bu, jnp.int32)


def _pack_rows(h):
    return _pack_rounded(h.astype(jnp.bfloat16).astype(jnp.float32))


def _unpack_rows(w):
    u = lax.bitcast_convert_type(w, jnp.uint32)
    a = lax.bitcast_convert_type(u << 16, jnp.float32)
    b = lax.bitcast_convert_type(u & jnp.uint32(0xFFFF0000), jnp.float32)
    return a, b


def _mixer_kernel(x_ref, xn_ref, win_ref, lb_ref, hg_ref, lng_ref, lnb_ref, wm_ref, bst_ref,
                  wout_ref, l1g_ref, l1b_ref, rwh_ref, rb_ref, tri_ref, upp_ref,
                  h1f_ref, h1p_ref, idx_ref, gate_ref, rank_ref, cnt_ref,
                  p_scr, lf_scr, kk_scr, g_scr, kc_scr, o_scr, st_scr, carry_scr,
                  amat_scr, qg_scr, kd_scr, upd_scr, mix_scr):
    b = pl.program_id(0)
    t = pl.program_id(1)
    step = b * pl.num_programs(1) + t
    f32 = jnp.float32
    bf16 = jnp.bfloat16

    @pl.when(t == 0)
    def _():
        st_scr[...] = jnp.zeros_like(st_scr)

    @pl.when((b == 0) & (t == 0))
    def _():
        carry_scr[...] = jnp.zeros_like(carry_scr)

    def project(xb, group):
        cols = slice(group * PROJ_GROUP, (group + 1) * PROJ_GROUP)
        p_scr[:, cols] = jnp.dot(xb, win_ref[:, cols], preferred_element_type=f32)

    def hgrn_gates():
        fl = p_scr[:, HGRN_WIDTH:2 * HGRN_WIDTH]
        z = jnp.exp(-jnp.abs(fl))
        r = 1.0 / (1.0 + z)
        k = (1.0 - lb_ref[...]) * jnp.where(fl >= 0, z * r, r)
        kk_scr[...] = k
        lf_scr[...] = jnp.log2(1.0 - k)

    @pl.when(step == 0)
    def _():
        xb0 = x_ref[...].astype(bf16)
        for group in range(IN_COLS // PROJ_GROUP):
            project(xb0, group)
        hgrn_gates()

    tri = tri_ref[...]
    row64 = lax.broadcasted_iota(jnp.int32, (CHUNK, CHUNK), 0)
    col64 = lax.broadcasted_iota(jnp.int32, (CHUNK, CHUNK), 1)
    lane_sub = lax.broadcasted_iota(jnp.int32, (SUB, CHUNK), 1)
    scale = HEAD_DIM ** -0.5

    def pad_rows(piece, lo_r):
        parts = []
        if lo_r > 0:
            parts.append(jnp.zeros((lo_r, HEAD_DIM), f32))
        parts.append(piece)
        rest = CHUNK - lo_r - piece.shape[0]
        if rest > 0:
            parts.append(jnp.zeros((rest, HEAD_DIM), f32))
        return jnp.concatenate(parts, axis=0) if len(parts) > 1 else piece

    def intra_factorised(q, k, gh):
        qts, kts = [], []
        for a in range(N_SUB):
            lo_r, hi_r = a * SUB, (a + 1) * SUB
            if a == 0:
                qa = q[:hi_r] * jnp.exp2(gh[:hi_r])
                ka = k[:hi_r] * jnp.exp2(-gh[:hi_r])
            else:
                ra = gh[lo_r - 1:lo_r]
                qa = q[lo_r:hi_r] * jnp.exp2(gh[lo_r:hi_r] - ra)
                ka = k[:hi_r] * jnp.exp2(ra - gh[:hi_r])
            qts.append(pad_rows(qa, lo_r))
            kts.append(pad_rows(ka, 0))
        a_mat = lax.dot_general(jnp.concatenate(qts, axis=1).astype(bf16),
                                jnp.concatenate(kts, axis=1).astype(bf16), _NT,
                                preferred_element_type=f32)
        return jnp.where(row64 >= col64, a_mat, 0.0)

    def chunks_factorised():
        n_chunks = TB // CHUNK
        heads = [(c, h) for c in range(n_chunks) for h in range(HGRN_HEADS)]

        def cols(h, base=0):
            return slice(base + h * HEAD_DIM, base + (h + 1) * HEAD_DIM)

        for c in range(n_chunks):
            rows = slice(c * CHUNK, (c + 1) * CHUNK)
            lf = lf_scr[rows, :]
            hi = lf.astype(bf16)
            lo = (lf - hi.astype(f32)).astype(bf16)
            gg = jnp.dot(tri, jnp.concatenate([hi, lo], axis=1), preferred_element_type=f32)
            lf_scr[rows, :] = gg[:, :HGRN_WIDTH] + gg[:, HGRN_WIDTH:]
        for c, h in heads:
            rows = slice(c * CHUNK, (c + 1) * CHUNK)
            q = p_scr[rows, cols(h)]
            k = kk_scr[rows, cols(h)]
            gh = lf_scr[rows, cols(h)]
            amat_scr[c * HGRN_HEADS + h] = intra_factorised(q, k, gh).astype(bf16)
            qg_scr[rows, cols(h)] = (q * jnp.exp2(gh)).astype(bf16)
            kd_scr[rows, cols(h)] = (k * jnp.exp2(gh[CHUNK - 1:CHUNK] - gh)).astype(bf16)
        for c, h in heads:
            rows = slice(c * CHUNK, (c + 1) * CHUNK)
            v = p_scr[rows, cols(h, 2 * HGRN_WIDTH)]
            o_scr[rows, cols(h)] = jnp.dot(amat_scr[c * HGRN_HEADS + h], v.astype(bf16),
                                           preferred_element_type=f32)
            upd_scr[c * HGRN_HEADS + h] = jnp.dot(v.T.astype(bf16), kd_scr[rows, cols(h)],
                                                  preferred_element_type=f32)
        states = [st_scr[h] for h in range(HGRN_HEADS)]
        for c, h in heads:
            rows = slice(c * CHUNK, (c + 1) * CHUNK)
            st = states[h]
            o_inter = lax.dot_general(qg_scr[rows, cols(h)], st.astype(bf16), _NT,
                                      preferred_element_type=f32)
            o_scr[rows, cols(h)] = (o_scr[rows, cols(h)] + o_inter) * scale
            gl = lf_scr[(c + 1) * CHUNK - 1:(c + 1) * CHUNK, cols(h)]
            states[h] = st * jnp.exp2(gl) + upd_scr[c * HGRN_HEADS + h]
        for h in range(HGRN_HEADS):
            st_scr[h] = states[h]

    def intra_exact_diagonal(q, k, gh, cs):
        qts, kts = [], []
        for a in range(1, N_SUB):
            lo_r, hi_r = a * SUB, (a + 1) * SUB
            ra = g_scr[lo_r - 1:lo_r, cs]
            qts.append(pad_rows(q[lo_r:hi_r] * jnp.exp2(gh[lo_r:hi_r] - ra), lo_r))
            kts.append(pad_rows(k[:lo_r] * jnp.exp2(ra - gh[:lo_r]), 0))
        a_off = lax.dot_general(jnp.concatenate(qts, axis=1).astype(bf16),
                                jnp.concatenate(kts, axis=1).astype(bf16), _NT,
                                preferred_element_type=f32)
        diag_rows = []
        for a in range(N_SUB):
            lo_r = a * SUB
            gs = gh[lo_r:lo_r + SUB]
            qs = q[lo_r:lo_r + SUB]
            blk = jnp.zeros((SUB, CHUNK), f32)
            for jl in range(SUB):
                j = lo_r + jl
                gj = g_scr[j:j + 1, cs]
                kj = kc_scr[j:j + 1, cs]
                e = jnp.exp2(jnp.minimum(gs - gj, 0.0))
                col = jnp.sum(qs * (kj * e), axis=-1, keepdims=True)
                blk = jnp.where(lane_sub == j, col, blk)
            diag_rows.append(blk)
        a_diag = jnp.concatenate(diag_rows, axis=0)
        return a_off + jnp.where(row64 >= col64, a_diag, 0.0)

    def make_chunk_body(intra):
        def chunk_body(c, carry):
            r0 = pl.multiple_of(c * CHUNK, CHUNK)
            rows = pl.ds(r0, CHUNK)
            lf = lf_scr[rows, :]
            hi = lf.astype(bf16)
            lo = (lf - hi.astype(f32)).astype(bf16)
            gg = jnp.dot(tri, jnp.concatenate([hi, lo], axis=1), preferred_element_type=f32)
            g_all = gg[:, :HGRN_WIDTH] + gg[:, HGRN_WIDTH:]
            g_scr[...] = g_all
            kc_scr[...] = kk_scr[rows, :]
            for h in range(HGRN_HEADS):
                cs = slice(h * HEAD_DIM, (h + 1) * HEAD_DIM)
                q = p_scr[rows, h * HEAD_DIM:(h + 1) * HEAD_DIM]
                v = p_scr[rows, 2 * HGRN_WIDTH + h * HEAD_DIM:2 * HGRN_WIDTH + (h + 1) * HEAD_DIM]
                k = kc_scr[:, cs]
                gh = g_all[:, cs]
                st = st_scr[h]
                o_inter = lax.dot_general((q * jnp.exp2(gh)).astype(bf16), st.astype(bf16), _NT,
                                          preferred_element_type=f32)
                a_mat = intra(q, k, gh, cs)
                o = jnp.dot(a_mat.astype(bf16), v.astype(bf16), preferred_element_type=f32) + o_inter
                o_scr[rows, h * HEAD_DIM:(h + 1) * HEAD_DIM] = o * scale

                gl = g_scr[CHUNK - 1:CHUNK, cs]
                kd = k * jnp.exp2(gl - gh)
                upd = jnp.dot(v.T.astype(bf16), kd.astype(bf16), preferred_element_type=f32)
                st_scr[h] = st * jnp.exp2(gl) + upd
            return carry
        return chunk_body

    lf_all = lf_scr[...]
    sub_decay = -jnp.sum(lf_all.reshape(TB // SUB, SUB, HGRN_WIDTH), axis=1)
    bounded = jnp.max(sub_decay) <= MAX_SUB_DECAY

    f_groups = list(range(HGRN_WIDTH // PROJ_GROUP, 2 * HGRN_WIDTH // PROJ_GROUP))
    qi_groups = [g for g in range(3 * HGRN_WIDTH // PROJ_GROUP) if g not in f_groups]
    rest_groups = list(range(3 * HGRN_WIDTH // PROJ_GROUP, IN_COLS // PROJ_GROUP))

    def gmlp_stage(xnb):
        for g in f_groups:
            project(xnb, g)
        for w in range(TB // GMLP_BLOCK):
            rows = slice(w * GMLP_BLOCK, (w + 1) * GMLP_BLOCK)
            u = _gelu(p_scr[rows, 4 * HGRN_WIDTH:4 * HGRN_WIDTH + GMLP_WIDTH])
            vn = _layer_norm(_gelu(p_scr[rows, 4 * HGRN_WIDTH + GMLP_WIDTH:]),
                             lng_ref[...], lnb_ref[...])
            vnb = vn.astype(bf16)
            cols = []
            for g in range(GMLP_GROUPS):
                s = jnp.dot(wm_ref[g], vnb[:, g * HEAD_DIM:(g + 1) * HEAD_DIM],
                            preferred_element_type=f32) + bst_ref[:, g:g + 1]
                cols.append(s)
            mix_scr[rows, HGRN_WIDTH:] = (u * jnp.concatenate(cols, axis=1)).astype(bf16)

    def output_stage(xnb):
        for g in qi_groups:
            project(xnb, g)
        for w in range(TB // GMLP_BLOCK):
            rows = slice(w * GMLP_BLOCK, (w + 1) * GMLP_BLOCK)
            o = o_scr[rows, :]
            ms = jnp.mean(o * o, axis=-1, keepdims=True)
            gate = p_scr[rows, 3 * HGRN_WIDTH:4 * HGRN_WIDTH]
            half_gate = 0.5 * gate
            silu = half_gate * (jnp.tanh(half_gate) + 1.0)
            y_rec = o * lax.rsqrt(ms + EPS) * hg_ref[...] * silu
            mix_scr[rows, :HGRN_WIDTH] = y_rec.astype(bf16)

        mix = jnp.dot(mix_scr[...], wout_ref[...], preferred_element_type=f32)
        hgrn_gates()
        h1 = _layer_norm(ALPHA * x_ref[...] + mix, l1g_ref[...], l1b_ref[...])
        h1f_ref[...] = h1
        hh = h1.astype(bf16)
        h1_rounded = hh.astype(f32)
        h1p_ref[...] = _pack_rounded(h1_rounded)
        for g in rest_groups[:2]:
            project(xnb, g)

        logits = lax.dot_general(rwh_ref[...], hh, _NT, preferred_element_type=f32) + rb_ref[...]
        for g in rest_groups[2:]:
            project(xnb, g)

        e_iota = lax.broadcasted_iota(jnp.int32, (N_EXPERTS, TB), 0)
        work = logits
        vals, idxs, hots = [], [], []
        for _ in range(TOP_K):
            m = jnp.max(work, axis=0, keepdims=True)
            ik = jnp.min(jnp.where(work == m, e_iota, N_EXPERTS), axis=0, keepdims=True)
            hot = e_iota == ik
            work = jnp.where(hot, -jnp.inf, work)
            vals.append(m)
            idxs.append(ik)
            hots.append(hot)
        exps = [jnp.exp(vk - vals[0]) for vk in vals]
        denom = exps[0] + exps[1] + exps[2] + exps[3]
        gate_ref[...] = jnp.concatenate([ek / denom for ek in exps], axis=0)
        idx_ref[...] = jnp.concatenate(idxs, axis=0)

        hot_any = jnp.where(hots[0] | hots[1] | hots[2] | hots[3], 1.0, 0.0)
        prefix = jnp.dot(hot_any.astype(bf16), upp_ref[...], preferred_element_type=f32)
        base = prefix + carry_scr[:, 0:1]
        ranks = [jnp.sum(jnp.where(hk, base, 0.0), axis=0, keepdims=True) for hk in hots]
        rank_ref[...] = jnp.concatenate(ranks, axis=0).astype(jnp.int32)
        new_carry = carry_scr[...] + jnp.sum(hot_any, axis=1, keepdims=True)
        carry_scr[...] = new_carry
        cnt_ref[...] = new_carry

    @pl.when(bounded)
    def _():
        xnb = xn_ref[...].astype(bf16)
        gmlp_stage(xnb)
        chunks_factorised()
        output_stage(xnb)

    @pl.when(jnp.logical_not(bounded))
    def _():
        xnb = xn_ref[...].astype(bf16)
        lax.fori_loop(0, TB // CHUNK, make_chunk_body(intra_exact_diagonal), 0)
        gmlp_stage(xnb)
        output_stage(xnb)


def _mixer(xt, win, lb, hg, lng, lnb, wm, bst, wout, l1g, l1b, rwh, rb, tri, upp,
           seq, batch, batch0):
    nt = seq // TB
    t_total = batch * seq
    nblk = batch * nt
    const2 = lambda b, t: (0, 0)
    const3 = lambda b, t: (0, 0, 0)
    row_blk = lambda b, t: (b * nt + t, 0)
    meta_blk = lambda b, t: (0, b * nt + t)
    once = dict(pipeline_mode=pl.Buffered(1))
    in_specs = [
        pl.BlockSpec((TB, D_MODEL), lambda b, t: ((b + batch0) * nt + t, 0)),
        pl.BlockSpec((TB, D_MODEL),
                     lambda b, t: (batch0 * nt + jnp.minimum(b * nt + t + 1, nblk - 1), 0)),
        pl.BlockSpec((D_MODEL, IN_COLS), const2, **once),
        pl.BlockSpec((1, HGRN_WIDTH), const2),
        pl.BlockSpec((1, HGRN_WIDTH), const2),
        pl.BlockSpec((1, GMLP_WIDTH), const2),
        pl.BlockSpec((1, GMLP_WIDTH), const2),
        pl.BlockSpec((GMLP_GROUPS, GMLP_BLOCK, GMLP_BLOCK), const3),
        pl.BlockSpec((GMLP_BLOCK, GMLP_GROUPS), const2),
        pl.BlockSpec((D_MODEL, D_MODEL), const2, **once),
        pl.BlockSpec((1, D_MODEL), const2),
        pl.BlockSpec((1, D_MODEL), const2),
        pl.BlockSpec((N_EXPERTS, D_MODEL), const2),
        pl.BlockSpec((N_EXPERTS, 1), const2),
        pl.BlockSpec((CHUNK, CHUNK), const2),
        pl.BlockSpec((TB, TB), const2, **once),
    ]
    out_shape = [
        jax.ShapeDtypeStruct((t_total, D_MODEL), jnp.float32),
        jax.ShapeDtypeStruct((t_total, HALF), jnp.int32),
        jax.ShapeDtypeStruct((TOP_K, t_total), jnp.int32),
        jax.ShapeDtypeStruct((TOP_K, t_total), jnp.float32),
        jax.ShapeDtypeStruct((TOP_K, t_total), jnp.int32),
        jax.ShapeDtypeStruct((N_EXPERTS, LANES), jnp.float32),
    ]
    out_specs = [
        pl.BlockSpec((TB, D_MODEL), row_blk),
        pl.BlockSpec((TB, HALF), row_blk),
        pl.BlockSpec((TOP_K, TB), meta_blk),
        pl.BlockSpec((TOP_K, TB), meta_blk),
        pl.BlockSpec((TOP_K, TB), meta_blk),
        pl.BlockSpec((N_EXPERTS, LANES), const2),
    ]
    scratch = [
        pltpu.VMEM((TB, IN_COLS), jnp.float32),
        pltpu.VMEM((TB, HGRN_WIDTH), jnp.float32),
        pltpu.VMEM((TB, HGRN_WIDTH), jnp.float32),
        pltpu.VMEM((CHUNK, HGRN_WIDTH), jnp.float32),
        pltpu.VMEM((CHUNK, HGRN_WIDTH), jnp.float32),
        pltpu.VMEM((TB, HGRN_WIDTH), jnp.float32),
        pltpu.VMEM((HGRN_HEADS, HEAD_DIM, HEAD_DIM), jnp.float32),
        pltpu.VMEM((N_EXPERTS, LANES), jnp.float32),
        pltpu.VMEM((TB // CHUNK * HGRN_HEADS, CHUNK, CHUNK), jnp.bfloat16),
        pltpu.VMEM((TB, HGRN_WIDTH), jnp.bfloat16),
        pltpu.VMEM((TB, HGRN_WIDTH), jnp.bfloat16),
        pltpu.VMEM((TB // CHUNK * HGRN_HEADS, HEAD_DIM, HEAD_DIM), jnp.float32),
        pltpu.VMEM((TB, D_MODEL), jnp.bfloat16),
    ]
    return pl.pallas_call(
        _mixer_kernel,
        grid=(batch, nt),
        in_specs=in_specs,
        out_specs=out_specs,
        out_shape=out_shape,
        scratch_shapes=scratch,
        compiler_params=pltpu.CompilerParams(
            dimension_semantics=("arbitrary", "arbitrary"),
            vmem_limit_bytes=VMEM_LIMIT),
        name="mixer",
    )(xt, xt, win, lb, hg, lng, lnb, wm, bst, wout, l1g, l1b, rwh, rb, tri, upp)


def _plan_kernel(cnt_ref, idx_ref, rank_ref, dest_ref, tiles_ref):
    f32 = jnp.float32
    n_e = N_EXPERTS
    e_sub = lax.broadcasted_iota(jnp.int32, (n_e, n_e), 0)
    e_lane = lax.broadcasted_iota(jnp.int32, (n_e, n_e), 1)
    counts = cnt_ref[:, 0:1]
    padded = jnp.floor((counts + (TM - 1)) * (1.0 / TM)) * TM
    as_row = lambda col: jnp.sum(jnp.where(e_sub == e_lane, col, 0.0), axis=0, keepdims=True)
    padded_row = as_row(padded)
    counts_row = as_row(counts)
    ends = jnp.sum(jnp.where(e_lane <= e_sub, padded_row, 0.0), axis=1, keepdims=True)
    starts = ends - padded
    owns_later = (e_lane > e_sub) & (counts_row > 0.0)
    nxt = jnp.min(jnp.where(owns_later, e_lane, n_e), axis=1, keepdims=True)
    own = lax.broadcasted_iota(jnp.int32, (n_e, 1), 0)
    nxt = jnp.where(nxt == n_e, own, nxt).astype(f32)

    n_lanes = tiles_ref.shape[1]
    tile_row = lax.broadcasted_iota(jnp.int32, (n_e, n_lanes), 1).astype(f32) * TM
    e_of = lax.broadcasted_iota(jnp.int32, (n_e, n_lanes), 0)
    tile_e = jnp.minimum(jnp.sum(jnp.where(tile_row >= ends, 1, 0), axis=0, keepdims=True), n_e - 1)
    mine = e_of == tile_e
    pick = lambda col: jnp.sum(jnp.where(mine, col, 0.0), axis=0, keepdims=True)
    valid = jnp.clip(pick(starts + counts) - tile_row[0:1], 0.0, float(TM))
    rows = [tile_e, valid.astype(jnp.int32), pick(nxt).astype(jnp.int32)]
    rows.append(jnp.zeros((tiles_ref.shape[0] - len(rows), n_lanes), jnp.int32))
    tiles_ref[...] = jnp.concatenate(rows, axis=0)

    chunk = PLAN_CHUNK
    e_chunk = lax.broadcasted_iota(jnp.int32, (n_e, chunk), 0)

    def body(c, carry):
        lanes = pl.ds(pl.multiple_of(c * chunk, chunk), chunk)
        for k in range(TOP_K):
            hit = e_chunk == idx_ref[k:k + 1, lanes]
            start_of = jnp.sum(jnp.where(hit, starts, 0.0), axis=0, keepdims=True)
            dest_ref[k:k + 1, lanes] = rank_ref[k:k + 1, lanes] + start_of.astype(jnp.int32)
        return carry

    lax.fori_loop(0, idx_ref.shape[1] // chunk, body, 0)


def _plan(cnt, idx, rank, n_tiles):
    t_part = idx.shape[1]
    n_lanes = -(-n_tiles // LANES) * LANES
    full = lambda shape: pl.BlockSpec(shape, lambda i: (0,) * len(shape))
    dest, tiles = pl.pallas_call(
        _plan_kernel,
        grid=(1,),
        in_specs=[full(cnt.shape), full(idx.shape), full(rank.shape)],
        out_specs=[full(idx.shape), full((8, n_lanes))],
        out_shape=[jax.ShapeDtypeStruct((TOP_K, t_part), jnp.int32),
                   jax.ShapeDtypeStruct((8, n_lanes), jnp.int32)],
        compiler_params=pltpu.CompilerParams(
            dimension_semantics=("arbitrary",), vmem_limit_bytes=VMEM_LIMIT),
        name="plan",
    )(cnt, idx, rank)
    return dest, tiles[0, :n_tiles], tiles[1, :n_tiles], tiles[2, :n_tiles]


def _sc_workers():
    info = plsc.get_sparse_core_info()
    return info.num_cores, info.num_cores * info.num_subcores


def _sc_dispatch(rows, dest, n_out):
    t_total, dw = rows.shape
    nc, nw = _sc_workers()
    per_w = t_total // nw
    mesh = plsc.VectorSubcoreMesh(core_axis_name="c", subcore_axis_name="s")

    @functools.partial(
        pl.kernel,
        out_type=jax.ShapeDtypeStruct((n_out, dw), rows.dtype),
        mesh=mesh,
        scratch_types=[pltpu.VMEM((SC_WINDOW,), jnp.int32) for _ in range(TOP_K)]
        + [pltpu.VMEM((SC_WINDOW, dw), rows.dtype), pltpu.SemaphoreType.DMA],
        name="sc_dispatch",
    )
    def k(x_hbm, i_hbm, o_hbm, i0, i1, i2, i3, rows_v, sem):
        wid = lax.axis_index("s") * nc + lax.axis_index("c")
        idx_bufs = (i0, i1, i2, i3)

        @pl.loop(0, per_w // SC_WINDOW)
        def _(j):
            base = wid * per_w + j * SC_WINDOW
            for kk in range(TOP_K):
                pltpu.sync_copy(i_hbm.at[kk, pl.ds(base, SC_WINDOW)], idx_bufs[kk])
            pltpu.sync_copy(x_hbm.at[pl.ds(base, SC_WINDOW)], rows_v)
            copies = [pltpu.async_copy(rows_v, o_hbm.at[idx_bufs[kk]], sem) for kk in range(TOP_K)]
            for cp in copies:
                cp.wait()

    return k(rows, dest)


def _sc_gather(table, idx):
    n_k, t_total = idx.shape
    n = n_k * t_total
    dw = table.shape[1]
    nc, nw = _sc_workers()
    per_w = n // nw
    w_per_k = nw // n_k
    mesh = plsc.VectorSubcoreMesh(core_axis_name="c", subcore_axis_name="s")

    @functools.partial(
        pl.kernel,
        out_type=jax.ShapeDtypeStruct((n, dw), table.dtype),
        mesh=mesh,
        scratch_types=[pltpu.VMEM((SC_WINDOW,), jnp.int32),
                       pltpu.VMEM((SC_WINDOW, dw), table.dtype),
                       pltpu.SemaphoreType.DMA],
        name="sc_gather",
    )
    def k(t_hbm, i_hbm, o_hbm, idx_v, rows_v, sem):
        wid = lax.axis_index("s") * nc + lax.axis_index("c")

        row = wid // w_per_k
        col0 = (wid % w_per_k) * per_w

        @pl.loop(0, per_w // SC_WINDOW)
        def _(j):
            col = col0 + j * SC_WINDOW
            pltpu.sync_copy(i_hbm.at[row, pl.ds(col, SC_WINDOW)], idx_v)
            pltpu.async_copy(t_hbm.at[idx_v], rows_v, sem).wait()
            pltpu.sync_copy(rows_v, o_hbm.at[pl.ds(row * t_total + col, SC_WINDOW)])

    return k(table, idx)


def _expert_kernel(te_ref, tv_ref, tn_ref, xs_ref, w1_hbm, w2_hbm, b1g_ref, b1l_ref, b2_ref,
                   perm_ref, y_ref, w1f_scr, w2f_scr, w1g_scr, w1l_scr, w2_scr, sems):
    def tile_body(h, carry):
        _expert_tile(pl.program_id(0) * TILES_PER_STEP + h, pl.multiple_of(h * TM, TM),
                     te_ref, tv_ref, tn_ref, xs_ref, w1_hbm, w2_hbm, b1g_ref, b1l_ref, b2_ref,
                     perm_ref, y_ref, w1f_scr, w2f_scr, w1g_scr, w1l_scr, w2_scr, sems)
        return carry

    lax.fori_loop(0, TILES_PER_STEP, tile_body, 0)


def _expert_tile(i, row0, te_ref, tv_ref, tn_ref, xs_ref, w1_hbm, w2_hbm, b1g_ref, b1l_ref, b2_ref,
                 perm_ref, y_ref, w1f_scr, w2f_scr, w1g_scr, w1l_scr, w2_scr, sems):
    valid = tv_ref[i]
    expert = te_ref[i]
    f32 = jnp.float32
    bf16 = jnp.bfloat16
    expert_changed = (i == 0) | (expert != te_ref[jnp.maximum(i - 1, 0)])

    def weight_copies(e):
        return (pltpu.make_async_copy(w1_hbm.at[e], w1f_scr, sems.at[0]),
                pltpu.make_async_copy(w2_hbm.at[e], w2f_scr, sems.at[1]))

    @pl.when(valid == 0)
    def _():
        y_ref[pl.ds(row0, TM), :] = jnp.zeros((TM, HALF), jnp.int32)

    @pl.when(i == 0)
    def _():
        for cp in weight_copies(expert):
            cp.start()

    @pl.when((valid > 0) & expert_changed)
    def _():
        for cp in weight_copies(expert):
            cp.wait()
        w2_scr[...] = w2f_scr[...].astype(bf16)
        perm = perm_ref[...]
        for c in range(2 * D_MODEL // PERM_BLOCK):
            blk = w1f_scr[:, c * PERM_BLOCK:(c + 1) * PERM_BLOCK].astype(bf16)
            r = jnp.dot(blk, perm, preferred_element_type=f32).astype(bf16)
            half = PERM_BLOCK // 2
            w1g_scr[:, c * half:(c + 1) * half] = r[:, :half]
            w1l_scr[:, c * half:(c + 1) * half] = r[:, half:]

        @pl.when(tn_ref[i] != expert)
        def _():
            for cp in weight_copies(tn_ref[i]):
                cp.start()

    def expert_rows(n_rows):
        keep = lax.broadcasted_iota(jnp.int32, (n_rows, 1), 0) < valid
        a, b = _unpack_rows(jnp.where(keep, xs_ref[pl.ds(row0, n_rows), :], 0))
        x = jnp.concatenate([a, b], axis=1).astype(bf16)
        bias_row = pl.ds(expert, 1)
        hg = jnp.dot(x, w1g_scr[...], preferred_element_type=f32) + b1g_ref[bias_row, :]
        xl1 = jnp.clip(jnp.dot(x, w1l_scr[...], preferred_element_type=f32)
                       + (b1l_ref[bias_row, :] + 1.0), 1.0 - SWIGLU_LIMIT, 1.0 + SWIGLU_LIMIT)
        xg = jnp.minimum(hg, SWIGLU_LIMIT)
        act = (0.5 * xg) * (jnp.tanh((0.5 * SWIGLU_ALPHA) * xg) + 1.0) * xl1
        y = jnp.dot(act.astype(bf16), w2_scr[...], preferred_element_type=f32) + b2_ref[bias_row, :]
        y_ref[pl.ds(row0, n_rows), :] = _pack_rows(y)
        if n_rows < TM:
            y_ref[pl.ds(row0 + n_rows, TM - n_rows), :] = jnp.zeros((TM - n_rows, HALF), jnp.int32)

    for blocks in range(TM // ROW_STEP, 0, -1):
        @pl.when((valid > (blocks - 1) * ROW_STEP) & (valid <= blocks * ROW_STEP))
        def _(blocks=blocks):
            expert_rows(blocks * ROW_STEP)


def _experts(xs, tile_expert, tile_valid, tile_next, w1, w2, b1g, b1l, b2, perm):
    n_slots = xs.shape[0]
    step_rows = TILES_PER_STEP * TM
    assert n_slots % step_rows == 0
    d_ff = w2.shape[1]
    whole = lambda i, te, tv, tn: (0, 0)
    grid_spec = pltpu.PrefetchScalarGridSpec(
        num_scalar_prefetch=3,
        grid=(n_slots // step_rows,),
        in_specs=[
            pl.BlockSpec((step_rows, HALF), lambda i, te, tv, tn: (i, 0)),
            pl.BlockSpec(memory_space=pl.ANY),
            pl.BlockSpec(memory_space=pl.ANY),
            pl.BlockSpec((N_EXPERTS, d_ff), whole),
            pl.BlockSpec((N_EXPERTS, d_ff), whole),
            pl.BlockSpec((N_EXPERTS, D_MODEL), whole),
            pl.BlockSpec((PERM_BLOCK, PERM_BLOCK), whole),
        ],
        out_specs=pl.BlockSpec((step_rows, HALF), lambda i, te, tv, tn: (i, 0)),
        scratch_shapes=[
            pltpu.VMEM((D_MODEL, 2 * d_ff), jnp.float32),
            pltpu.VMEM((d_ff, D_MODEL), jnp.float32),
            pltpu.VMEM((D_MODEL, d_ff), jnp.bfloat16),
            pltpu.VMEM((D_MODEL, d_ff), jnp.bfloat16),
            pltpu.VMEM((d_ff, D_MODEL), jnp.bfloat16),
            pltpu.SemaphoreType.DMA((2,)),
        ],
    )
    return pl.pallas_call(
        _expert_kernel,
        grid_spec=grid_spec,
        out_shape=jax.ShapeDtypeStruct((n_slots, HALF), jnp.int32),
        compiler_params=pltpu.CompilerParams(
            dimension_semantics=("arbitrary",),
            vmem_limit_bytes=VMEM_LIMIT),
        name="experts",
    )(tile_expert, tile_valid, tile_next, xs, w1, w2, b1g, b1l, b2, perm)


def _combine_kernel(h_ref, yk_ref, gate_ref, g_ref, b_ref, *rest):
    o_ref = rest[-1]
    gates = gate_ref[...].T
    acc_a = None
    acc_b = None
    for k in range(TOP_K):
        a, b = _unpack_rows(yk_ref[k])
        gk = gates[:, k:k + 1]
        acc_a = gk * a if acc_a is None else acc_a + gk * a
        acc_b = gk * b if acc_b is None else acc_b + gk * b
    ffn = jnp.concatenate([acc_a, acc_b], axis=1)
    o_ref[...] = _layer_norm(ALPHA * h_ref[...] + ffn, g_ref[...], b_ref[...])


def _combine(h1f, yk, gates, g2, b2, out_prev, row0, t_all):
    t_part = h1f.shape[0]
    blk0 = row0 // TC3
    row = lambda i: (i, 0)
    const = lambda i: (0, 0)
    in_specs = [
        pl.BlockSpec((TC3, D_MODEL), row),
        pl.BlockSpec((TOP_K, TC3, HALF), lambda i: (0, i, 0)),
        pl.BlockSpec((TOP_K, TC3), lambda i: (0, i)),
        pl.BlockSpec((1, D_MODEL), const),
        pl.BlockSpec((1, D_MODEL), const),
    ]
    args = [h1f, yk, gates, g2, b2]
    aliases = {}
    if out_prev is not None:
        in_specs.append(pl.BlockSpec(memory_space=pl.ANY))
        args.append(out_prev)
        aliases = {len(args) - 1: 0}
    return pl.pallas_call(
        _combine_kernel,
        grid=(t_part // TC3,),
        in_specs=in_specs,
        out_specs=pl.BlockSpec((TC3, D_MODEL), lambda i: (i + blk0, 0)),
        out_shape=jax.ShapeDtypeStruct((t_all, D_MODEL), jnp.float32),
        input_output_aliases=aliases,
        compiler_params=pltpu.CompilerParams(
            dimension_semantics=("arbitrary",),
            vmem_limit_bytes=VMEM_LIMIT),
        name="combine",
    )(*args)


def kernel(x, w_in, lb_logits, hgrn_norm_g, gmlp_ln_g, gmlp_ln_b, gmlp_ws, gmlp_bs, w_out, ln1_g, ln1_b, router_w, router_b, exp_w1, exp_b1, exp_w2, exp_b2, ln2_g, ln2_b):
    batch, seq, d = x.shape
    assert d == D_MODEL and seq % TB == 0 and w_in.shape[0] == 1
    t_total = batch * seq
    f32 = jnp.float32
    bf16 = jnp.bfloat16

    lb = jnp.cumsum(jax.nn.softmax(lb_logits.astype(f32), axis=0), axis=0)[0:1]
    chunk_id = jnp.arange(GMLP_BLOCK) // CHUNK
    wm = jnp.where((chunk_id[None, :] <= chunk_id[:, None])[None], gmlp_ws[0], 0.0).astype(bf16)
    rwh = router_w[0].T.astype(bf16)
    tri = (jnp.arange(CHUNK)[None, :] <= jnp.arange(CHUNK)[:, None]).astype(bf16)
    upp = (jnp.arange(TB)[:, None] < jnp.arange(TB)[None, :]).astype(bf16)

    lane = jnp.arange(PERM_BLOCK)
    src = jnp.where(lane < PERM_BLOCK // 2, 2 * lane, 2 * (lane - PERM_BLOCK // 2) + 1)
    perm = (jnp.arange(PERM_BLOCK)[:, None] == src[None, :]).astype(bf16)
    xt = x.reshape(t_total, d)
    win = w_in[0].astype(bf16)
    wout = w_out[0].astype(bf16)
    b1g, b1l, b2e = exp_b1[0][:, 0::2], exp_b1[0][:, 1::2], exp_b2[0]

    last = max(batch * LAST_PART_NUM // LAST_PART_DEN, 1) if batch > 1 else 0
    part_sizes = [pb for pb in (batch - last, last) if pb > 0]
    out = None
    b0 = 0
    for pb in part_sizes:
        t_part = pb * seq
        step_rows = TILES_PER_STEP * TM
        n_slots = -(-(t_part * TOP_K + N_EXPERTS * TM) // step_rows) * step_rows
        n_tiles = n_slots // TM
        h1f, h1p, idx, gates, rank, cnt = _mixer(
            xt, win, lb, hgrn_norm_g[0:1], gmlp_ln_g[0:1], gmlp_ln_b[0:1], wm, gmlp_bs[0].T, wout,
            ln1_g[0:1], ln1_b[0:1], rwh, router_b[0][:, None], tri, upp, seq, pb, b0)

        dest, tile_expert, tile_valid, tile_next = _plan(cnt, idx, rank, n_tiles)
        xs = _sc_dispatch(h1p, dest, n_slots)
        y = _experts(xs, tile_expert, tile_valid, tile_next,
                     exp_w1[0], exp_w2[0], b1g, b1l, b2e, perm)
        yk = _sc_gather(y, dest).reshape(TOP_K, t_part, HALF)
        out = _combine(h1f, yk, gates, ln2_g[0:1], ln2_b[0:1], out, b0 * seq, t_total)
        b0 += pb
    return out.reshape(batch, seq, d)
```

```python
import functools
import math

import jax
import jax.numpy as jnp
from jax import lax
from jax.experimental import pallas as pl
from jax.experimental.pallas import tpu as pltpu
from jax.experimental.pallas import tpu_sc as plsc

D_MODEL = 1024
CHUNK = 64
SUB = 16
N_SUB = CHUNK // SUB
MAX_SUB_DECAY = 86.0
HGRN_WIDTH = 512
HGRN_HEADS = 4
HEAD_DIM = 128
GMLP_WIDTH = 512
GMLP_BLOCK = 128
GMLP_GROUPS = 4
IN_COLS = 3072
N_EXPERTS = 32
TOP_K = 4
SWIGLU_LIMIT = 7.0
SWIGLU_ALPHA = 1.702
ALPHA = 2.0 ** 0.25
EPS = 1e-5
HALF = D_MODEL // 2

TB = 512
TM = 1024
ROW_STEP = 256
TILES_PER_STEP = 3
PLAN_CHUNK = 2048
TC3 = 1024
LAST_PART_NUM, LAST_PART_DEN = 1, 4
SC_WINDOW = 128
PROJ_GROUP = 256
PERM_BLOCK = 256
LANES = 128
V7X_VMEM_BYTES = 64 * 1024 * 1024
VMEM_LIMIT = V7X_VMEM_BYTES * 7 // 8

_NT = (((1,), (1,)), ((), ()))


def _gelu(x):
    return 0.5 * x * (1.0 + lax.erf(x * (1.0 / math.sqrt(2.0))))


def _layer_norm(x, g, b):
    mu = jnp.mean(x, axis=-1, keepdims=True)
    xc = x - mu
    var = jnp.mean(xc * xc, axis=-1, keepdims=True)
    return xc * lax.rsqrt(var + EPS) * g + b


def _pack_rounded(hr):
    au = lax.bitcast_convert_type(hr[:, :HALF], jnp.uint32) >> 16
    bu = lax.bitcast_convert_type(hr[:, HALF:], jnp.uint32) & jnp.uint32(0xFFFF0000)
    return lax.bitcast_convert_type(au | bu, jnp.int32)


def _pack_rows(h):
    return _pack_rounded(h.astype(jnp.bfloat16).astype(jnp.float32))


def _unpack_rows(w):
    u = lax.bitcast_convert_type(w, jnp.uint32)
    a = lax.bitcast_convert_type(u << 16, jnp.float32)
    b = lax.bitcast_convert_type(u & jnp.uint32(0xFFFF0000), jnp.float32)
    return a, b


def _mixer_kernel(x_ref, xn_ref, win_ref, lb_ref, hg_ref, lng_ref, lnb_ref, wm_ref, bst_ref,
                  wout_ref, l1g_ref, l1b_ref, rwh_ref, rb_ref, tri_ref, upp_ref,
                  h1f_ref, h1p_ref, idx_ref, gate_ref, rank_ref, cnt_ref,
                  p_scr, lf_scr, kk_scr, g_scr, kc_scr, o_scr, st_scr, carry_scr,
                  amat_scr, qg_scr, kd_scr, upd_scr, mix_scr):
    b = pl.program_id(0)
    t = pl.program_id(1)
    step = b * pl.num_programs(1) + t
    f32 = jnp.float32
    bf16 = jnp.bfloat16

    @pl.when(t == 0)
    def _():
        st_scr[...] = jnp.zeros_like(st_scr)

    @pl.when((b == 0) & (t == 0))
    def _():
        carry_scr[...] = jnp.zeros_like(carry_scr)

    def project(xb, group):
        cols = slice(group * PROJ_GROUP, (group + 1) * PROJ_GROUP)
        p_scr[:, cols] = jnp.dot(xb, win_ref[:, cols], preferred_element_type=f32)

    def hgrn_gates():
        fl = p_scr[:, HGRN_WIDTH:2 * HGRN_WIDTH]
        z = jnp.exp(-jnp.abs(fl))
        r = 1.0 / (1.0 + z)
        k = (1.0 - lb_ref[...]) * jnp.where(fl >= 0, z * r, r)
        kk_scr[...] = k
        lf_scr[...] = jnp.log2(1.0 - k)

    @pl.when(step == 0)
    def _():
        xb0 = x_ref[...].astype(bf16)
        for group in range(IN_COLS // PROJ_GROUP):
            project(xb0, group)
        hgrn_gates()

    tri = tri_ref[...]
    row64 = lax.broadcasted_iota(jnp.int32, (CHUNK, CHUNK), 0)
    col64 = lax.broadcasted_iota(jnp.int32, (CHUNK, CHUNK), 1)
    lane_sub = lax.broadcasted_iota(jnp.int32, (SUB, CHUNK), 1)
    scale = HEAD_DIM ** -0.5

    def pad_rows(piece, lo_r):
        parts = []
        if lo_r > 0:
            parts.append(jnp.zeros((lo_r, HEAD_DIM), f32))
        parts.append(piece)
        rest = CHUNK - lo_r - piece.shape[0]
        if rest > 0:
            parts.append(jnp.zeros((rest, HEAD_DIM), f32))
        return jnp.concatenate(parts, axis=0) if len(parts) > 1 else piece

    def intra_factorised(q, k, gh):
        qts, kts = [], []
        for a in range(N_SUB):
            lo_r, hi_r = a * SUB, (a + 1) * SUB
            if a == 0:
                qa = q[:hi_r] * jnp.exp2(gh[:hi_r])
                ka = k[:hi_r] * jnp.exp2(-gh[:hi_r])
            else:
                ra = gh[lo_r - 1:lo_r]
                qa = q[lo_r:hi_r] * jnp.exp2(gh[lo_r:hi_r] - ra)
                ka = k[:hi_r] * jnp.exp2(ra - gh[:hi_r])
            qts.append(pad_rows(qa, lo_r))
            kts.append(pad_rows(ka, 0))
        a_mat = lax.dot_general(jnp.concatenate(qts, axis=1).astype(bf16),
                                jnp.concatenate(kts, axis=1).astype(bf16), _NT,
                                preferred_element_type=f32)
        return jnp.where(row64 >= col64, a_mat, 0.0)

    def chunks_factorised():
        n_chunks = TB // CHUNK
        heads = [(c, h) for c in range(n_chunks) for h in range(HGRN_HEADS)]

        def cols(h, base=0):
            return slice(base + h * HEAD_DIM, base + (h + 1) * HEAD_DIM)

        for c in range(n_chunks):
            rows = slice(c * CHUNK, (c + 1) * CHUNK)
            lf = lf_scr[rows, :]
            hi = lf.astype(bf16)
            lo = (lf - hi.astype(f32)).astype(bf16)
            gg = jnp.dot(tri, jnp.concatenate([hi, lo], axis=1), preferred_element_type=f32)
            lf_scr[rows, :] = gg[:, :HGRN_WIDTH] + gg[:, HGRN_WIDTH:]
        for c, h in heads:
            rows = slice(c * CHUNK, (c + 1) * CHUNK)
            q = p_scr[rows, cols(h)]
            k = kk_scr[rows, cols(h)]
            gh = lf_scr[rows, cols(h)]
            amat_scr[c * HGRN_HEADS + h] = intra_factorised(q, k, gh).astype(bf16)
            qg_scr[rows, cols(h)] = (q * jnp.exp2(gh)).astype(bf16)
            kd_scr[rows, cols(h)] = (k * jnp.exp2(gh[CHUNK - 1:CHUNK] - gh)).astype(bf16)
        for c, h in heads:
            rows = slice(c * CHUNK, (c + 1) * CHUNK)
            v = p_scr[rows, cols(h, 2 * HGRN_WIDTH)]
            o_scr[rows, cols(h)] = jnp.dot(amat_scr[c * HGRN_HEADS + h], v.astype(bf16),
                                           preferred_element_type=f32)
            upd_scr[c * HGRN_HEADS + h] = jnp.dot(v.T.astype(bf16), kd_scr[rows, cols(h)],
                                                  preferred_element_type=f32)
        states = [st_scr[h] for h in range(HGRN_HEADS)]
        for c, h in heads:
            rows = slice(c * CHUNK, (c + 1) * CHUNK)
            st = states[h]
            o_inter = lax.dot_general(qg_scr[rows, cols(h)], st.astype(bf16), _NT,
                                      preferred_element_type=f32)
            o_scr[rows, cols(h)] = (o_scr[rows, cols(h)] + o_inter) * scale
            gl = lf_scr[(c + 1) * CHUNK - 1:(c + 1) * CHUNK, cols(h)]
            states[h] = st * jnp.exp2(gl) + upd_scr[c * HGRN_HEADS + h]
        for h in range(HGRN_HEADS):
            st_scr[h] = states[h]

    def intra_exact_diagonal(q, k, gh, cs):
        qts, kts = [], []
        for a in range(1, N_SUB):
            lo_r, hi_r = a * SUB, (a + 1) * SUB
            ra = g_scr[lo_r - 1:lo_r, cs]
            qts.append(pad_rows(q[lo_r:hi_r] * jnp.exp2(gh[lo_r:hi_r] - ra), lo_r))
            kts.append(pad_rows(k[:lo_r] * jnp.exp2(ra - gh[:lo_r]), 0))
        a_off = lax.dot_general(jnp.concatenate(qts, axis=1).astype(bf16),
                                jnp.concatenate(kts, axis=1).astype(bf16), _NT,
                                preferred_element_type=f32)
        diag_rows = []
        for a in range(N_SUB):
            lo_r = a * SUB
            gs = gh[lo_r:lo_r + SUB]
            qs = q[lo_r:lo_r + SUB]
            blk = jnp.zeros((SUB, CHUNK), f32)
            for jl in range(SUB):
                j = lo_r + jl
                gj = g_scr[j:j + 1, cs]
                kj = kc_scr[j:j + 1, cs]
                e = jnp.exp2(jnp.minimum(gs - gj, 0.0))
                col = jnp.sum(qs * (kj * e), axis=-1, keepdims=True)
                blk = jnp.where(lane_sub == j, col, blk)
            diag_rows.append(blk)
        a_diag = jnp.concatenate(diag_rows, axis=0)
        return a_off + jnp.where(row64 >= col64, a_diag, 0.0)

    def make_chunk_body(intra):
        def chunk_body(c, carry):
            r0 = pl.multiple_of(c * CHUNK, CHUNK)
            rows = pl.ds(r0, CHUNK)
            lf = lf_scr[rows, :]
            hi = lf.astype(bf16)
            lo = (lf - hi.astype(f32)).astype(bf16)
            gg = jnp.dot(tri, jnp.concatenate([hi, lo], axis=1), preferred_element_type=f32)
            g_all = gg[:, :HGRN_WIDTH] + gg[:, HGRN_WIDTH:]
            g_scr[...] = g_all
            kc_scr[...] = kk_scr[rows, :]
            for h in range(HGRN_HEADS):
                cs = slice(h * HEAD_DIM, (h + 1) * HEAD_DIM)
                q = p_scr[rows, h * HEAD_DIM:(h + 1) * HEAD_DIM]
                v = p_scr[rows, 2 * HGRN_WIDTH + h * HEAD_DIM:2 * HGRN_WIDTH + (h + 1) * HEAD_DIM]
                k = kc_scr[:, cs]
                gh = g_all[:, cs]
                st = st_scr[h]
                o_inter = lax.dot_general((q * jnp.exp2(gh)).astype(bf16), st.astype(bf16), _NT,
                                          preferred_element_type=f32)
                a_mat = intra(q, k, gh, cs)
                o = jnp.dot(a_mat.astype(bf16), v.astype(bf16), preferred_element_type=f32) + o_inter
                o_scr[rows, h * HEAD_DIM:(h + 1) * HEAD_DIM] = o * scale

                gl = g_scr[CHUNK - 1:CHUNK, cs]
                kd = k * jnp.exp2(gl - gh)
                upd = jnp.dot(v.T.astype(bf16), kd.astype(bf16), preferred_element_type=f32)
                st_scr[h] = st * jnp.exp2(gl) + upd
            return carry
        return chunk_body

    lf_all = lf_scr[...]
    sub_decay = -jnp.sum(lf_all.reshape(TB // SUB, SUB, HGRN_WIDTH), axis=1)
    bounded = jnp.max(sub_decay) <= MAX_SUB_DECAY

    f_groups = list(range(HGRN_WIDTH // PROJ_GROUP, 2 * HGRN_WIDTH // PROJ_GROUP))
    qi_groups = [g for g in range(3 * HGRN_WIDTH // PROJ_GROUP) if g not in f_groups]
    rest_groups = list(range(3 * HGRN_WIDTH // PROJ_GROUP, IN_COLS // PROJ_GROUP))

    def gmlp_stage(xnb):
        for g in f_groups:
            project(xnb, g)
        for w in range(TB // GMLP_BLOCK):
            rows = slice(w * GMLP_BLOCK, (w + 1) * GMLP_BLOCK)
            u = _gelu(p_scr[rows, 4 * HGRN_WIDTH:4 * HGRN_WIDTH + GMLP_WIDTH])
            vn = _layer_norm(_gelu(p_scr[rows, 4 * HGRN_WIDTH + GMLP_WIDTH:]),
                             lng_ref[...], lnb_ref[...])
            vnb = vn.astype(bf16)
            cols = []
            for g in range(GMLP_GROUPS):
                s = jnp.dot(wm_ref[g], vnb[:, g * HEAD_DIM:(g + 1) * HEAD_DIM],
                            preferred_element_type=f32) + bst_ref[:, g:g + 1]
                cols.append(s)
            mix_scr[rows, HGRN_WIDTH:] = (u * jnp.concatenate(cols, axis=1)).astype(bf16)

    def output_stage(xnb):
        for g in qi_groups:
            project(xnb, g)
        for w in range(TB // GMLP_BLOCK):
            rows = slice(w * GMLP_BLOCK, (w + 1) * GMLP_BLOCK)
            o = o_scr[rows, :]
            ms = jnp.mean(o * o, axis=-1, keepdims=True)
            gate = p_scr[rows, 3 * HGRN_WIDTH:4 * HGRN_WIDTH]
            half_gate = 0.5 * gate
            silu = half_gate * (jnp.tanh(half_gate) + 1.0)
            y_rec = o * lax.rsqrt(ms + EPS) * hg_ref[...] * silu
            mix_scr[rows, :HGRN_WIDTH] = y_rec.astype(bf16)

        mix = jnp.dot(mix_scr[...], wout_ref[...], preferred_element_type=f32)
        hgrn_gates()
        h1 = _layer_norm(ALPHA * x_ref[...] + mix, l1g_ref[...], l1b_ref[...])
        h1f_ref[...] = h1
        hh = h1.astype(bf16)
        h1_rounded = hh.astype(f32)
        h1p_ref[...] = _pack_rounded(h1_rounded)
        for g in rest_groups[:2]:
            project(xnb, g)

        logits = lax.dot_general(rwh_ref[...], hh, _NT, preferred_element_type=f32) + rb_ref[...]
        for g in rest_groups[2:]:
            project(xnb, g)

        e_iota = lax.broadcasted_iota(jnp.int32, (N_EXPERTS, TB), 0)
        work = logits
        vals, idxs, hots = [], [], []
        for _ in range(TOP_K):
            m = jnp.max(work, axis=0, keepdims=True)
            ik = jnp.min(jnp.where(work == m, e_iota, N_EXPERTS), axis=0, keepdims=True)
            hot = e_iota == ik
            work = jnp.where(hot, -jnp.inf, work)
            vals.append(m)
            idxs.append(ik)
            hots.append(hot)
        exps = [jnp.exp(vk - vals[0]) for vk in vals]
        denom = exps[0] + exps[1] + exps[2] + exps[3]
        gate_ref[...] = jnp.concatenate([ek / denom for ek in exps], axis=0)
        idx_ref[...] = jnp.concatenate(idxs, axis=0)

        hot_any = jnp.where(hots[0] | hots[1] | hots[2] | hots[3], 1.0, 0.0)
        prefix = jnp.dot(hot_any.astype(bf16), upp_ref[...], preferred_element_type=f32)
        base = prefix + carry_scr[:, 0:1]
        ranks = [jnp.sum(jnp.where(hk, base, 0.0), axis=0, keepdims=True) for hk in hots]
        rank_ref[...] = jnp.concatenate(ranks, axis=0).astype(jnp.int32)
        new_carry = carry_scr[...] + jnp.sum(hot_any, axis=1, keepdims=True)
        carry_scr[...] = new_carry
        cnt_ref[...] = new_carry

    @pl.when(bounded)
    def _():
        xnb = xn_ref[...].astype(bf16)
        gmlp_stage(xnb)
        chunks_factorised()
        output_stage(xnb)

    @pl.when(jnp.logical_not(bounded))
    def _():
        xnb = xn_ref[...].astype(bf16)
        lax.fori_loop(0, TB // CHUNK, make_chunk_body(intra_exact_diagonal), 0)
        gmlp_stage(xnb)
        output_stage(xnb)


def _mixer(xt, win, lb, hg, lng, lnb, wm, bst, wout, l1g, l1b, rwh, rb, tri, upp,
           seq, batch, batch0):
    nt = seq // TB
    t_total = batch * seq
    nblk = batch * nt
    const2 = lambda b, t: (0, 0)
    const3 = lambda b, t: (0, 0, 0)
    row_blk = lambda b, t: (b * nt + t, 0)
    meta_blk = lambda b, t: (0, b * nt + t)
    once = dict(pipeline_mode=pl.Buffered(1))
    in_specs = [
        pl.BlockSpec((TB, D_MODEL), lambda b, t: ((b + batch0) * nt + t, 0)),
        pl.BlockSpec((TB, D_MODEL),
                     lambda b, t: (batch0 * nt + jnp.minimum(b * nt + t + 1, nblk - 1), 0)),
        pl.BlockSpec((D_MODEL, IN_COLS), const2, **once),
        pl.BlockSpec((1, HGRN_WIDTH), const2),
        pl.BlockSpec((1, HGRN_WIDTH), const2),
        pl.BlockSpec((1, GMLP_WIDTH), const2),
        pl.BlockSpec((1, GMLP_WIDTH), const2),
        pl.BlockSpec((GMLP_GROUPS, GMLP_BLOCK, GMLP_BLOCK), const3),
        pl.BlockSpec((GMLP_BLOCK, GMLP_GROUPS), const2),
        pl.BlockSpec((D_MODEL, D_MODEL), const2, **once),
        pl.BlockSpec((1, D_MODEL), const2),
        pl.BlockSpec((1, D_MODEL), const2),
        pl.BlockSpec((N_EXPERTS, D_MODEL), const2),
        pl.BlockSpec((N_EXPERTS, 1), const2),
        pl.BlockSpec((CHUNK, CHUNK), const2),
        pl.BlockSpec((TB, TB), const2, **once),
    ]
    out_shape = [
        jax.ShapeDtypeStruct((t_total, D_MODEL), jnp.float32),
        jax.ShapeDtypeStruct((t_total, HALF), jnp.int32),
        jax.ShapeDtypeStruct((TOP_K, t_total), jnp.int32),
        jax.ShapeDtypeStruct((TOP_K, t_total), jnp.float32),
        jax.ShapeDtypeStruct((TOP_K, t_total), jnp.int32),
        jax.ShapeDtypeStruct((N_EXPERTS, LANES), jnp.float32),
    ]
    out_specs = [
        pl.BlockSpec((TB, D_MODEL), row_blk),
        pl.BlockSpec((TB, HALF), row_blk),
        pl.BlockSpec((TOP_K, TB), meta_blk),
        pl.BlockSpec((TOP_K, TB), meta_blk),
        pl.BlockSpec((TOP_K, TB), meta_blk),
        pl.BlockSpec((N_EXPERTS, LANES), const2),
    ]
    scratch = [
        pltpu.VMEM((TB, IN_COLS), jnp.float32),
        pltpu.VMEM((TB, HGRN_WIDTH), jnp.float32),
        pltpu.VMEM((TB, HGRN_WIDTH), jnp.float32),
        pltpu.VMEM((CHUNK, HGRN_WIDTH), jnp.float32),
        pltpu.VMEM((CHUNK, HGRN_WIDTH), jnp.float32),
        pltpu.VMEM((TB, HGRN_WIDTH), jnp.float32),
        pltpu.VMEM((HGRN_HEADS, HEAD_DIM, HEAD_DIM), jnp.float32),
        pltpu.VMEM((N_EXPERTS, LANES), jnp.float32),
        pltpu.VMEM((TB // CHUNK * HGRN_HEADS, CHUNK, CHUNK), jnp.bfloat16),
        pltpu.VMEM((TB, HGRN_WIDTH), jnp.bfloat16),
        pltpu.VMEM((TB, HGRN_WIDTH), jnp.bfloat16),
        pltpu.VMEM((TB // CHUNK * HGRN_HEADS, HEAD_DIM, HEAD_DIM), jnp.float32),
        pltpu.VMEM((TB, D_MODEL), jnp.bfloat16),
    ]
    return pl.pallas_call(
        _mixer_kernel,
        grid=(batch, nt),
        in_specs=in_specs,
        out_specs=out_specs,
        out_shape=out_shape,
        scratch_shapes=scratch,
        compiler_params=pltpu.CompilerParams(
            dimension_semantics=("arbitrary", "arbitrary"),
            vmem_limit_bytes=VMEM_LIMIT),
        name="mixer",
    )(xt, xt, win, lb, hg, lng, lnb, wm, bst, wout, l1g, l1b, rwh, rb, tri, upp)


def _plan_kernel(cnt_ref, idx_ref, rank_ref, dest_ref, tiles_ref):
    f32 = jnp.float32
    n_e = N_EXPERTS
    e_sub = lax.broadcasted_iota(jnp.int32, (n_e, n_e), 0)
    e_lane = lax.broadcasted_iota(jnp.int32, (n_e, n_e), 1)
    counts = cnt_ref[:, 0:1]
    padded = jnp.floor((counts + (TM - 1)) * (1.0 / TM)) * TM
    as_row = lambda col: jnp.sum(jnp.where(e_sub == e_lane, col, 0.0), axis=0, keepdims=True)
    padded_row = as_row(padded)
    counts_row = as_row(counts)
    ends = jnp.sum(jnp.where(e_lane <= e_sub, padded_row, 0.0), axis=1, keepdims=True)
    starts = ends - padded
    owns_later = (e_lane > e_sub) & (counts_row > 0.0)
    nxt = jnp.min(jnp.where(owns_later, e_lane, n_e), axis=1, keepdims=True)
    own = lax.broadcasted_iota(jnp.int32, (n_e, 1), 0)
    nxt = jnp.where(nxt == n_e, own, nxt).astype(f32)

    n_lanes = tiles_ref.shape[1]
    tile_row = lax.broadcasted_iota(jnp.int32, (n_e, n_lanes), 1).astype(f32) * TM
    e_of = lax.broadcasted_iota(jnp.int32, (n_e, n_lanes), 0)
    tile_e = jnp.minimum(jnp.sum(jnp.where(tile_row >= ends, 1, 0), axis=0, keepdims=True), n_e - 1)
    mine = e_of == tile_e
    pick = lambda col: jnp.sum(jnp.where(mine, col, 0.0), axis=0, keepdims=True)
    valid = jnp.clip(pick(starts + counts) - tile_row[0:1], 0.0, float(TM))
    rows = [tile_e, valid.astype(jnp.int32), pick(nxt).astype(jnp.int32)]
    rows.append(jnp.zeros((tiles_ref.shape[0] - len(rows), n_lanes), jnp.int32))
    tiles_ref[...] = jnp.concatenate(rows, axis=0)

    chunk = PLAN_CHUNK
    e_chunk = lax.broadcasted_iota(jnp.int32, (n_e, chunk), 0)

    def body(c, carry):
        lanes = pl.ds(pl.multiple_of(c * chunk, chunk), chunk)
        for k in range(TOP_K):
            hit = e_chunk == idx_ref[k:k + 1, lanes]
            start_of = jnp.sum(jnp.where(hit, starts, 0.0), axis=0, keepdims=True)
            dest_ref[k:k + 1, lanes] = rank_ref[k:k + 1, lanes] + start_of.astype(jnp.int32)
        return carry

    lax.fori_loop(0, idx_ref.shape[1] // chunk, body, 0)


def _plan(cnt, idx, rank, n_tiles):
    t_part = idx.shape[1]
    n_lanes = -(-n_tiles // LANES) * LANES
    full = lambda shape: pl.BlockSpec(shape, lambda i: (0,) * len(shape))
    dest, tiles = pl.pallas_call(
        _plan_kernel,
        grid=(1,),
        in_specs=[full(cnt.shape), full(idx.shape), full(rank.shape)],
        out_specs=[full(idx.shape), full((8, n_lanes))],
        out_shape=[jax.ShapeDtypeStruct((TOP_K, t_part), jnp.int32),
                   jax.ShapeDtypeStruct((8, n_lanes), jnp.int32)],
        compiler_params=pltpu.CompilerParams(
            dimension_semantics=("arbitrary",), vmem_limit_bytes=VMEM_LIMIT),
        name="plan",
    )(cnt, idx, rank)
    return dest, tiles[0, :n_tiles], tiles[1, :n_tiles], tiles[2, :n_tiles]


def _sc_workers():
    info = plsc.get_sparse_core_info()
    return info.num_cores, info.num_cores * info.num_subcores


def _sc_dispatch(rows, dest, n_out):
    t_total, dw = rows.shape
    nc, nw = _sc_workers()
    per_w = t_total // nw
    mesh = plsc.VectorSubcoreMesh(core_axis_name="c", subcore_axis_name="s")

    @functools.partial(
        pl.kernel,
        out_type=jax.ShapeDtypeStruct((n_out, dw), rows.dtype),
        mesh=mesh,
        scratch_types=[pltpu.VMEM((SC_WINDOW,), jnp.int32) for _ in range(TOP_K)]
        + [pltpu.VMEM((SC_WINDOW, dw), rows.dtype), pltpu.SemaphoreType.DMA],
        name="sc_dispatch",
    )
    def k(x_hbm, i_hbm, o_hbm, i0, i1, i2, i3, rows_v, sem):
        wid = lax.axis_index("s") * nc + lax.axis_index("c")
        idx_bufs = (i0, i1, i2, i3)

        @pl.loop(0, per_w // SC_WINDOW)
        def _(j):
            base = wid * per_w + j * SC_WINDOW
            for kk in range(TOP_K):
                pltpu.sync_copy(i_hbm.at[kk, pl.ds(base, SC_WINDOW)], idx_bufs[kk])
            pltpu.sync_copy(x_hbm.at[pl.ds(base, SC_WINDOW)], rows_v)
            copies = [pltpu.async_copy(rows_v, o_hbm.at[idx_bufs[kk]], sem) for kk in range(TOP_K)]
            for cp in copies:
                cp.wait()

    return k(rows, dest)


def _sc_gather(table, idx):
    n_k, t_total = idx.shape
    n = n_k * t_total
    dw = table.shape[1]
    nc, nw = _sc_workers()
    per_w = n // nw
    w_per_k = nw // n_k
    mesh = plsc.VectorSubcoreMesh(core_axis_name="c", subcore_axis_name="s")

    @functools.partial(
        pl.kernel,
        out_type=jax.ShapeDtypeStruct((n, dw), table.dtype),
        mesh=mesh,
        scratch_types=[pltpu.VMEM((SC_WINDOW,), jnp.int32),
                       pltpu.VMEM((SC_WINDOW, dw), table.dtype),
                       pltpu.SemaphoreType.DMA],
        name="sc_gather",
    )
    def k(t_hbm, i_hbm, o_hbm, idx_v, rows_v, sem):
        wid = lax.axis_index("s") * nc + lax.axis_index("c")

        row = wid // w_per_k
        col0 = (wid % w_per_k) * per_w

        @pl.loop(0, per_w // SC_WINDOW)
        def _(j):
            col = col0 + j * SC_WINDOW
            pltpu.sync_copy(i_hbm.at[row, pl.ds(col, SC_WINDOW)], idx_v)
            pltpu.async_copy(t_hbm.at[idx_v], rows_v, sem).wait()
            pltpu.sync_copy(rows_v, o_hbm.at[pl.ds(row * t_total + col, SC_WINDOW)])

    return k(table, idx)


def _expert_kernel(te_ref, tv_ref, tn_ref, xs_ref, w1_hbm, w2_hbm, b1g_ref, b1l_ref, b2_ref,
                   perm_ref, y_ref, w1f_scr, w2f_scr, w1g_scr, w1l_scr, w2_scr, sems):
    def tile_body(h, carry):
        _expert_tile(pl.program_id(0) * TILES_PER_STEP + h, pl.multiple_of(h * TM, TM),
                     te_ref, tv_ref, tn_ref, xs_ref, w1_hbm, w2_hbm, b1g_ref, b1l_ref, b2_ref,
                     perm_ref, y_ref, w1f_scr, w2f_scr, w1g_scr, w1l_scr, w2_scr, sems)
        return carry

    lax.fori_loop(0, TILES_PER_STEP, tile_body, 0)


def _expert_tile(i, row0, te_ref, tv_ref, tn_ref, xs_ref, w1_hbm, w2_hbm, b1g_ref, b1l_ref, b2_ref,
                 perm_ref, y_ref, w1f_scr, w2f_scr, w1g_scr, w1l_scr, w2_scr, sems):
    valid = tv_ref[i]
    expert = te_ref[i]
    f32 = jnp.float32
    bf16 = jnp.bfloat16
    expert_changed = (i == 0) | (expert != te_ref[jnp.maximum(i - 1, 0)])

    def weight_copies(e):
        return (pltpu.make_async_copy(w1_hbm.at[e], w1f_scr, sems.at[0]),
                pltpu.make_async_copy(w2_hbm.at[e], w2f_scr, sems.at[1]))

    @pl.when(valid == 0)
    def _():
        y_ref[pl.ds(row0, TM), :] = jnp.zeros((TM, HALF), jnp.int32)

    @pl.when(i == 0)
    def _():
        for cp in weight_copies(expert):
            cp.start()

    @pl.when((valid > 0) & expert_changed)
    def _():
        for cp in weight_copies(expert):
            cp.wait()
        w2_scr[...] = w2f_scr[...].astype(bf16)
        perm = perm_ref[...]
        for c in range(2 * D_MODEL // PERM_BLOCK):
            blk = w1f_scr[:, c * PERM_BLOCK:(c + 1) * PERM_BLOCK].astype(bf16)
            r = jnp.dot(blk, perm, preferred_element_type=f32).astype(bf16)
            half = PERM_BLOCK // 2
            w1g_scr[:, c * half:(c + 1) * half] = r[:, :half]
            w1l_scr[:, c * half:(c + 1) * half] = r[:, half:]

        @pl.when(tn_ref[i] != expert)
        def _():
            for cp in weight_copies(tn_ref[i]):
                cp.start()

    def expert_rows(n_rows):
        keep = lax.broadcasted_iota(jnp.int32, (n_rows, 1), 0) < valid
        a, b = _unpack_rows(jnp.where(keep, xs_ref[pl.ds(row0, n_rows), :], 0))
        x = jnp.concatenate([a, b], axis=1).astype(bf16)
        bias_row = pl.ds(expert, 1)
        hg = jnp.dot(x, w1g_scr[...], preferred_element_type=f32) + b1g_ref[bias_row, :]
        xl1 = jnp.clip(jnp.dot(x, w1l_scr[...], preferred_element_type=f32)
                       + (b1l_ref[bias_row, :] + 1.0), 1.0 - SWIGLU_LIMIT, 1.0 + SWIGLU_LIMIT)
        xg = jnp.minimum(hg, SWIGLU_LIMIT)
        act = (0.5 * xg) * (jnp.tanh((0.5 * SWIGLU_ALPHA) * xg) + 1.0) * xl1
        y = jnp.dot(act.astype(bf16), w2_scr[...], preferred_element_type=f32) + b2_ref[bias_row, :]
        y_ref[pl.ds(row0, n_rows), :] = _pack_rows(y)
        if n_rows < TM:
            y_ref[pl.ds(row0 + n_rows, TM - n_rows), :] = jnp.zeros((TM - n_rows, HALF), jnp.int32)

    for blocks in range(TM // ROW_STEP, 0, -1):
        @pl.when((valid > (blocks - 1) * ROW_STEP) & (valid <= blocks * ROW_STEP))
        def _(blocks=blocks):
            expert_rows(blocks * ROW_STEP)


def _experts(xs, tile_expert, tile_valid, tile_next, w1, w2, b1g, b1l, b2, perm):
    n_slots = xs.shape[0]
    step_rows = TILES_PER_STEP * TM
    assert n_slots % step_rows == 0
    d_ff = w2.shape[1]
    whole = lambda i, te, tv, tn: (0, 0)
    grid_spec = pltpu.PrefetchScalarGridSpec(
        num_scalar_prefetch=3,
        grid=(n_slots // step_rows,),
        in_specs=[
            pl.BlockSpec((step_rows, HALF), lambda i, te, tv, tn: (i, 0)),
            pl.BlockSpec(memory_space=pl.ANY),
            pl.BlockSpec(memory_space=pl.ANY),
            pl.BlockSpec((N_EXPERTS, d_ff), whole),
            pl.BlockSpec((N_EXPERTS, d_ff), whole),
            pl.BlockSpec((N_EXPERTS, D_MODEL), whole),
            pl.BlockSpec((PERM_BLOCK, PERM_BLOCK), whole),
        ],
        out_specs=pl.BlockSpec((step_rows, HALF), lambda i, te, tv, tn: (i, 0)),
        scratch_shapes=[
            pltpu.VMEM((D_MODEL, 2 * d_ff), jnp.float32),
            pltpu.VMEM((d_ff, D_MODEL), jnp.float32),
            pltpu.VMEM((D_MODEL, d_ff), jnp.bfloat16),
            pltpu.VMEM((D_MODEL, d_ff), jnp.bfloat16),
            pltpu.VMEM((d_ff, D_MODEL), jnp.bfloat16),
            pltpu.SemaphoreType.DMA((2,)),
        ],
    )
    return pl.pallas_call(
        _expert_kernel,
        grid_spec=grid_spec,
        out_shape=jax.ShapeDtypeStruct((n_slots, HALF), jnp.int32),
        compiler_params=pltpu.CompilerParams(
            dimension_semantics=("arbitrary",),
            vmem_limit_bytes=VMEM_LIMIT),
        name="experts",
    )(tile_expert, tile_valid, tile_next, xs, w1, w2, b1g, b1l, b2, perm)


def _combine_kernel(h_ref, yk_ref, gate_ref, g_ref, b_ref, *rest):
    o_ref = rest[-1]
    gates = gate_ref[...].T
    acc_a = None
    acc_b = None
    for k in range(TOP_K):
        a, b = _unpack_rows(yk_ref[k])
        gk = gates[:, k:k + 1]
        acc_a = gk * a if acc_a is None else acc_a + gk * a
        acc_b = gk * b if acc_b is None else acc_b + gk * b
    ffn = jnp.concatenate([acc_a, acc_b], axis=1)
    o_ref[...] = _layer_norm(ALPHA * h_ref[...] + ffn, g_ref[...], b_ref[...])


def _combine(h1f, yk, gates, g2, b2, out_prev, row0, t_all):
    t_part = h1f.shape[0]
    blk0 = row0 // TC3
    row = lambda i: (i, 0)
    const = lambda i: (0, 0)
    in_specs = [
        pl.BlockSpec((TC3, D_MODEL), row),
        pl.BlockSpec((TOP_K, TC3, HALF), lambda i: (0, i, 0)),
        pl.BlockSpec((TOP_K, TC3), lambda i: (0, i)),
        pl.BlockSpec((1, D_MODEL), const),
        pl.BlockSpec((1, D_MODEL), const),
    ]
    args = [h1f, yk, gates, g2, b2]
    aliases = {}
    if out_prev is not None:
        in_specs.append(pl.BlockSpec(memory_space=pl.ANY))
        args.append(out_prev)
        aliases = {len(args) - 1: 0}
    return pl.pallas_call(
        _combine_kernel,
        grid=(t_part // TC3,),
        in_specs=in_specs,
        out_specs=pl.BlockSpec((TC3, D_MODEL), lambda i: (i + blk0, 0)),
        out_shape=jax.ShapeDtypeStruct((t_all, D_MODEL), jnp.float32),
        input_output_aliases=aliases,
        compiler_params=pltpu.CompilerParams(
            dimension_semantics=("arbitrary",),
            vmem_limit_bytes=VMEM_LIMIT),
        name="combine",
    )(*args)


def kernel(x, w_in, lb_logits, hgrn_norm_g, gmlp_ln_g, gmlp_ln_b, gmlp_ws, gmlp_bs, w_out, ln1_g, ln1_b, router_w, router_b, exp_w1, exp_b1, exp_w2, exp_b2, ln2_g, ln2_b):
    batch, seq, d = x.shape
    assert d == D_MODEL and seq % TB == 0 and w_in.shape[0] == 1
    t_total = batch * seq
    f32 = jnp.float32
    bf16 = jnp.bfloat16

    lb = jnp.cumsum(jax.nn.softmax(lb_logits.astype(f32), axis=0), axis=0)[0:1]
    chunk_id = jnp.arange(GMLP_BLOCK) // CHUNK
    wm = jnp.where((chunk_id[None, :] <= chunk_id[:, None])[None], gmlp_ws[0], 0.0).astype(bf16)
    rwh = router_w[0].T.astype(bf16)
    tri = (jnp.arange(CHUNK)[None, :] <= jnp.arange(CHUNK)[:, None]).astype(bf16)
    upp = (jnp.arange(TB)[:, None] < jnp.arange(TB)[None, :]).astype(bf16)

    lane = jnp.arange(PERM_BLOCK)
    src = jnp.where(lane < PERM_BLOCK // 2, 2 * lane, 2 * (lane - PERM_BLOCK // 2) + 1)
    perm = (jnp.arange(PERM_BLOCK)[:, None] == src[None, :]).astype(bf16)
    xt = x.reshape(t_total, d)
    win = w_in[0].astype(bf16)
    wout = w_out[0].astype(bf16)
    b1g, b1l, b2e = exp_b1[0][:, 0::2], exp_b1[0][:, 1::2], exp_b2[0]

    last = max(batch * LAST_PART_NUM // LAST_PART_DEN, 1) if batch > 1 else 0
    part_sizes = [pb for pb in (batch - last, last) if pb > 0]
    out = None
    b0 = 0
    for pb in part_sizes:
        t_part = pb * seq
        step_rows = TILES_PER_STEP * TM
        n_slots = -(-(t_part * TOP_K + N_EXPERTS * TM) // step_rows) * step_rows
        n_tiles = n_slots // TM
        h1f, h1p, idx, gates, rank, cnt = _mixer(
            xt, win, lb, hgrn_norm_g[0:1], gmlp_ln_g[0:1], gmlp_ln_b[0:1], wm, gmlp_bs[0].T, wout,
            ln1_g[0:1], ln1_b[0:1], rwh, router_b[0][:, None], tri, upp, seq, pb, b0)

        dest, tile_expert, tile_valid, tile_next = _plan(cnt, idx, rank, n_tiles)
        xs = _sc_dispatch(h1p, dest, n_slots)
        y = _experts(xs, tile_expert, tile_valid, tile_next,
                     exp_w1[0], exp_w2[0], b1g, b1l, b2e, perm)
        yk = _sc_gather(y, dest).reshape(TOP_K, t_part, HALF)
        out = _combine(h1f, yk, gates, ln2_g[0:1], ln2_b[0:1], out, b0 * seq, t_total)
        b0 += pb
    return out.reshape(batch, seq, d)
```

```python
import functools
import math

import jax
import jax.numpy as jnp
from jax import lax
from jax.experimental import pallas as pl
from jax.experimental.pallas import tpu as pltpu
from jax.experimental.pallas import tpu_sc as plsc

D_MODEL = 1024
CHUNK = 64
SUB = 32
N_SUB = CHUNK // SUB
MAX_SUB_DECAY = 86.0
HGRN_WIDTH = 512
HGRN_HEADS = 4
HEAD_DIM = 128
GMLP_WIDTH = 512
GMLP_BLOCK = 128
GMLP_GROUPS = 4
IN_COLS = 3072
N_EXPERTS = 32
TOP_K = 4
SWIGLU_LIMIT = 7.0
SWIGLU_ALPHA = 1.702
ALPHA = 2.0 ** 0.25
EPS = 1e-5
HALF = D_MODEL // 2

TB = 512
TM = 1024
ROW_STEP = 256
TILES_PER_STEP = 3
PLAN_CHUNK = 2048
TC3 = 1024
LAST_PART_NUM, LAST_PART_DEN = 1, 4
SC_WINDOW = 128
PROJ_GROUP = 256
PERM_BLOCK = 256
LANES = 128
V7X_VMEM_BYTES = 64 * 1024 * 1024
VMEM_LIMIT = V7X_VMEM_BYTES * 7 // 8

_NT = (((1,), (1,)), ((), ()))


def _gelu(x):
    return 0.5 * x * (1.0 + lax.erf(x * (1.0 / math.sqrt(2.0))))


def _layer_norm(x, g, b):
    mu = jnp.mean(x, axis=-1, keepdims=True)
    xc = x - mu
    var = jnp.mean(xc * xc, axis=-1, keepdims=True)
    return xc * lax.rsqrt(var + EPS) * g + b


def _pack_rounded(hr):
    au = lax.bitcast_convert_type(hr[:, :HALF], jnp.uint32) >> 16
    bu = lax.bitcast_convert_type(hr[:, HALF:], jnp.uint32) & jnp.uint32(0xFFFF0000)
    return lax.bitcast_convert_type(au | bu, jnp.int32)


def _pack_rows(h):
    return _pack_rounded(h.astype(jnp.bfloat16).astype(jnp.float32))


def _unpack_rows(w):
    u = lax.bitcast_convert_type(w, jnp.uint32)
    a = lax.bitcast_convert_type(u << 16, jnp.float32)
    b = lax.bitcast_convert_type(u & jnp.uint32(0xFFFF0000), jnp.float32)
    return a, b


def _mixer_kernel(x_ref, xn_ref, win_ref, lb_ref, hg_ref, lng_ref, lnb_ref, wm_ref, bst_ref,
                  wout_ref, l1g_ref, l1b_ref, rwh_ref, rb_ref, tri_ref, upp_ref,
                  h1f_ref, h1p_ref, idx_ref, gate_ref, rank_ref, cnt_ref,
                  p_scr, lf_scr, kk_scr, g_scr, kc_scr, o_scr, st_scr, carry_scr,
                  amat_scr, qg_scr, kd_scr, upd_scr, mix_scr):
    b = pl.program_id(0)
    t = pl.program_id(1)
    step = b * pl.num_programs(1) + t
    f32 = jnp.float32
    bf16 = jnp.bfloat16

    @pl.when(t == 0)
    def _():
        st_scr[...] = jnp.zeros_like(st_scr)

    @pl.when((b == 0) & (t == 0))
    def _():
        carry_scr[...] = jnp.zeros_like(carry_scr)

    def project(xb, group):
        cols = slice(group * PROJ_GROUP, (group + 1) * PROJ_GROUP)
        p_scr[:, cols] = jnp.dot(xb, win_ref[:, cols], preferred_element_type=f32)

    def hgrn_gates():
        fl = p_scr[:, HGRN_WIDTH:2 * HGRN_WIDTH]
        z = jnp.exp(-jnp.abs(fl))
        r = 1.0 / (1.0 + z)
        k = (1.0 - lb_ref[...]) * jnp.where(fl >= 0, z * r, r)
        kk_scr[...] = k
        lf_scr[...] = jnp.log2(1.0 - k)

    @pl.when(step == 0)
    def _():
        xb0 = x_ref[...].astype(bf16)
        for group in range(IN_COLS // PROJ_GROUP):
            project(xb0, group)
        hgrn_gates()

    tri = tri_ref[...]
    row64 = lax.broadcasted_iota(jnp.int32, (CHUNK, CHUNK), 0)
    col64 = lax.broadcasted_iota(jnp.int32, (CHUNK, CHUNK), 1)
    lane_sub = lax.broadcasted_iota(jnp.int32, (SUB, CHUNK), 1)
    scale = HEAD_DIM ** -0.5

    def pad_rows(piece, lo_r):
        parts = []
        if lo_r > 0:
            parts.append(jnp.zeros((lo_r, HEAD_DIM), f32))
        parts.append(piece)
        rest = CHUNK - lo_r - piece.shape[0]
        if rest > 0:
            parts.append(jnp.zeros((rest, HEAD_DIM), f32))
        return jnp.concatenate(parts, axis=0) if len(parts) > 1 else piece

    def intra_factorised(q, k, gh):
        qts, kts = [], []
        for a in range(N_SUB):
            lo_r, hi_r = a * SUB, (a + 1) * SUB
            if a == 0:
                qa = q[:hi_r] * jnp.exp2(gh[:hi_r])
                ka = k[:hi_r] * jnp.exp2(-gh[:hi_r])
            else:
                ra = gh[lo_r - 1:lo_r]
                qa = q[lo_r:hi_r] * jnp.exp2(gh[lo_r:hi_r] - ra)
                ka = k[:hi_r] * jnp.exp2(ra - gh[:hi_r])
            qts.append(pad_rows(qa, lo_r))
            kts.append(pad_rows(ka, 0))
        a_mat = lax.dot_general(jnp.concatenate(qts, axis=1).astype(bf16),
                                jnp.concatenate(kts, axis=1).astype(bf16), _NT,
                                preferred_element_type=f32)
        return jnp.where(row64 >= col64, a_mat, 0.0)

    def chunks_factorised():
        n_chunks = TB // CHUNK
        heads = [(c, h) for c in range(n_chunks) for h in range(HGRN_HEADS)]

        def cols(h, base=0):
            return slice(base + h * HEAD_DIM, base + (h + 1) * HEAD_DIM)

        for c in range(n_chunks):
            rows = slice(c * CHUNK, (c + 1) * CHUNK)
            lf = lf_scr[rows, :]
            hi = lf.astype(bf16)
            lo = (lf - hi.astype(f32)).astype(bf16)
            gg = jnp.dot(tri, jnp.concatenate([hi, lo], axis=1), preferred_element_type=f32)
            lf_scr[rows, :] = gg[:, :HGRN_WIDTH] + gg[:, HGRN_WIDTH:]
        for c, h in heads:
            rows = slice(c * CHUNK, (c + 1) * CHUNK)
            q = p_scr[rows, cols(h)]
            k = kk_scr[rows, cols(h)]
            gh = lf_scr[rows, cols(h)]
            amat_scr[c * HGRN_HEADS + h] = intra_factorised(q, k, gh).astype(bf16)
            qg_scr[rows, cols(h)] = (q * jnp.exp2(gh)).astype(bf16)
            kd_scr[rows, cols(h)] = (k * jnp.exp2(gh[CHUNK - 1:CHUNK] - gh)).astype(bf16)
        for c, h in heads:
            rows = slice(c * CHUNK, (c + 1) * CHUNK)
            v = p_scr[rows, cols(h, 2 * HGRN_WIDTH)]
            o_scr[rows, cols(h)] = jnp.dot(amat_scr[c * HGRN_HEADS + h], v.astype(bf16),
                                           preferred_element_type=f32)
            upd_scr[c * HGRN_HEADS + h] = jnp.dot(v.T.astype(bf16), kd_scr[rows, cols(h)],
                                                  preferred_element_type=f32)
        states = [st_scr[h] for h in range(HGRN_HEADS)]
        for c, h in heads:
            rows = slice(c * CHUNK, (c + 1) * CHUNK)
            st = states[h]
            o_inter = lax.dot_general(qg_scr[rows, cols(h)], st.astype(bf16), _NT,
                                      preferred_element_type=f32)
            o_scr[rows, cols(h)] = (o_scr[rows, cols(h)] + o_inter) * scale
            gl = lf_scr[(c + 1) * CHUNK - 1:(c + 1) * CHUNK, cols(h)]
            states[h] = st * jnp.exp2(gl) + upd_scr[c * HGRN_HEADS + h]
        for h in range(HGRN_HEADS):
            st_scr[h] = states[h]

    def intra_exact_diagonal(q, k, gh, cs):
        qts, kts = [], []
        for a in range(1, N_SUB):
            lo_r, hi_r = a * SUB, (a + 1) * SUB
            ra = g_scr[lo_r - 1:lo_r, cs]
            qts.append(pad_rows(q[lo_r:hi_r] * jnp.exp2(gh[lo_r:hi_r] - ra), lo_r))
            kts.append(pad_rows(k[:lo_r] * jnp.exp2(ra - gh[:lo_r]), 0))
        a_off = lax.dot_general(jnp.concatenate(qts, axis=1).astype(bf16),
                                jnp.concatenate(kts, axis=1).astype(bf16), _NT,
                                preferred_element_type=f32)
        diag_rows = []
        for a in range(N_SUB):
            lo_r = a * SUB
            gs = gh[lo_r:lo_r + SUB]
            qs = q[lo_r:lo_r + SUB]
            blk = jnp.zeros((SUB, CHUNK), f32)
            for jl in range(SUB):
                j = lo_r + jl
                gj = g_scr[j:j + 1, cs]
                kj = kc_scr[j:j + 1, cs]
                e = jnp.exp2(jnp.minimum(gs - gj, 0.0))
                col = jnp.sum(qs * (kj * e), axis=-1, keepdims=True)
                blk = jnp.where(lane_sub == j, col, blk)
            diag_rows.append(blk)
        a_diag = jnp.concatenate(diag_rows, axis=0)
        return a_off + jnp.where(row64 >= col64, a_diag, 0.0)

    def make_chunk_body(intra):
        def chunk_body(c, carry):
            r0 = pl.multiple_of(c * CHUNK, CHUNK)
            rows = pl.ds(r0, CHUNK)
            lf = lf_scr[rows, :]
            hi = lf.astype(bf16)
            lo = (lf - hi.astype(f32)).astype(bf16)
            gg = jnp.dot(tri, jnp.concatenate([hi, lo], axis=1), preferred_element_type=f32)
            g_all = gg[:, :HGRN_WIDTH] + gg[:, HGRN_WIDTH:]
            g_scr[...] = g_all
            kc_scr[...] = kk_scr[rows, :]
            for h in range(HGRN_HEADS):
                cs = slice(h * HEAD_DIM, (h + 1) * HEAD_DIM)
                q = p_scr[rows, h * HEAD_DIM:(h + 1) * HEAD_DIM]
                v = p_scr[rows, 2 * HGRN_WIDTH + h * HEAD_DIM:2 * HGRN_WIDTH + (h + 1) * HEAD_DIM]
                k = kc_scr[:, cs]
                gh = g_all[:, cs]
                st = st_scr[h]
                o_inter = lax.dot_general((q * jnp.exp2(gh)).astype(bf16), st.astype(bf16), _NT,
                                          preferred_element_type=f32)
                a_mat = intra(q, k, gh, cs)
                o = jnp.dot(a_mat.astype(bf16), v.astype(bf16), preferred_element_type=f32) + o_inter
                o_scr[rows, h * HEAD_DIM:(h + 1) * HEAD_DIM] = o * scale

                gl = g_scr[CHUNK - 1:CHUNK, cs]
                kd = k * jnp.exp2(gl - gh)
                upd = jnp.dot(v.T.astype(bf16), kd.astype(bf16), preferred_element_type=f32)
                st_scr[h] = st * jnp.exp2(gl) + upd
            return carry
        return chunk_body

    lf_all = lf_scr[...]
    sub_decay = -jnp.sum(lf_all.reshape(TB // SUB, SUB, HGRN_WIDTH), axis=1)
    bounded = jnp.max(sub_decay) <= MAX_SUB_DECAY

    f_groups = list(range(HGRN_WIDTH // PROJ_GROUP, 2 * HGRN_WIDTH // PROJ_GROUP))
    qi_groups = [g for g in range(3 * HGRN_WIDTH // PROJ_GROUP) if g not in f_groups]
    rest_groups = list(range(3 * HGRN_WIDTH // PROJ_GROUP, IN_COLS // PROJ_GROUP))

    def gmlp_stage(xnb):
        for g in f_groups:
            project(xnb, g)
        for w in range(TB // GMLP_BLOCK):
            rows = slice(w * GMLP_BLOCK, (w + 1) * GMLP_BLOCK)
            u = _gelu(p_scr[rows, 4 * HGRN_WIDTH:4 * HGRN_WIDTH + GMLP_WIDTH])
            vn = _layer_norm(_gelu(p_scr[rows, 4 * HGRN_WIDTH + GMLP_WIDTH:]),
                             lng_ref[...], lnb_ref[...])
            vnb = vn.astype(bf16)
            cols = []
            for g in range(GMLP_GROUPS):
                s = jnp.dot(wm_ref[g], vnb[:, g * HEAD_DIM:(g + 1) * HEAD_DIM],
                            preferred_element_type=f32) + bst_ref[:, g:g + 1]
                cols.append(s)
            mix_scr[rows, HGRN_WIDTH:] = (u * jnp.concatenate(cols, axis=1)).astype(bf16)

    def output_stage(xnb):
        for g in qi_groups:
            project(xnb, g)
        for w in range(TB // GMLP_BLOCK):
            rows = slice(w * GMLP_BLOCK, (w + 1) * GMLP_BLOCK)
            o = o_scr[rows, :]
            ms = jnp.mean(o * o, axis=-1, keepdims=True)
            gate = p_scr[rows, 3 * HGRN_WIDTH:4 * HGRN_WIDTH]
            half_gate = 0.5 * gate
            silu = half_gate * (jnp.tanh(half_gate) + 1.0)
            y_rec = o * lax.rsqrt(ms + EPS) * hg_ref[...] * silu
            mix_scr[rows, :HGRN_WIDTH] = y_rec.astype(bf16)

        mix = jnp.dot(mix_scr[...], wout_ref[...], preferred_element_type=f32)
        hgrn_gates()
        h1 = _layer_norm(ALPHA * x_ref[...] + mix, l1g_ref[...], l1b_ref[...])
        h1f_ref[...] = h1
        hh = h1.astype(bf16)
        h1_rounded = hh.astype(f32)
        h1p_ref[...] = _pack_rounded(h1_rounded)
        for g in rest_groups[:2]:
            project(xnb, g)

        logits = lax.dot_general(rwh_ref[...], hh, _NT, preferred_element_type=f32) + rb_ref[...]
        for g in rest_groups[2:]:
            project(xnb, g)

        e_iota = lax.broadcasted_iota(jnp.int32, (N_EXPERTS, TB), 0)
        work = logits
        vals, idxs, hots = [], [], []
        for _ in range(TOP_K):
            m = jnp.max(work, axis=0, keepdims=True)
            ik = jnp.min(jnp.where(work == m, e_iota, N_EXPERTS), axis=0, keepdims=True)
            hot = e_iota == ik
            work = jnp.where(hot, -jnp.inf, work)
            vals.append(m)
            idxs.append(ik)
            hots.append(hot)
        exps = [jnp.exp(vk - vals[0]) for vk in vals]
        denom = exps[0] + exps[1] + exps[2] + exps[3]
        gate_ref[...] = jnp.concatenate([ek / denom for ek in exps], axis=0)
        idx_ref[...] = jnp.concatenate(idxs, axis=0)

        hot_any = jnp.where(hots[0] | hots[1] | hots[2] | hots[3], 1.0, 0.0)
        prefix = jnp.dot(hot_any.astype(bf16), upp_ref[...], preferred_element_type=f32)
        base = prefix + carry_scr[:, 0:1]
        ranks = [jnp.sum(jnp.where(hk, base, 0.0), axis=0, keepdims=True) for hk in hots]
        rank_ref[...] = jnp.concatenate(ranks, axis=0).astype(jnp.int32)
        new_carry = carry_scr[...] + jnp.sum(hot_any, axis=1, keepdims=True)
        carry_scr[...] = new_carry
        cnt_ref[...] = new_carry

    @pl.when(bounded)
    def _():
        xnb = xn_ref[...].astype(bf16)
        gmlp_stage(xnb)
        chunks_factorised()
        output_stage(xnb)

    @pl.when(jnp.logical_not(bounded))
    def _():
        xnb = xn_ref[...].astype(bf16)
        lax.fori_loop(0, TB // CHUNK, make_chunk_body(intra_exact_diagonal), 0)
        gmlp_stage(xnb)
        output_stage(xnb)


def _mixer(xt, win, lb, hg, lng, lnb, wm, bst, wout, l1g, l1b, rwh, rb, tri, upp,
           seq, batch, batch0):
    nt = seq // TB
    t_total = batch * seq
    nblk = batch * nt
    const2 = lambda b, t: (0, 0)
    const3 = lambda b, t: (0, 0, 0)
    row_blk = lambda b, t: (b * nt + t, 0)
    meta_blk = lambda b, t: (0, b * nt + t)
    once = dict(pipeline_mode=pl.Buffered(1))
    in_specs = [
        pl.BlockSpec((TB, D_MODEL), lambda b, t: ((b + batch0) * nt + t, 0)),
        pl.BlockSpec((TB, D_MODEL),
                     lambda b, t: (batch0 * nt + jnp.minimum(b * nt + t + 1, nblk - 1), 0)),
        pl.BlockSpec((D_MODEL, IN_COLS), const2, **once),
        pl.BlockSpec((1, HGRN_WIDTH), const2),
        pl.BlockSpec((1, HGRN_WIDTH), const2),
        pl.BlockSpec((1, GMLP_WIDTH), const2),
        pl.BlockSpec((1, GMLP_WIDTH), const2),
        pl.BlockSpec((GMLP_GROUPS, GMLP_BLOCK, GMLP_BLOCK), const3),
        pl.BlockSpec((GMLP_BLOCK, GMLP_GROUPS), const2),
        pl.BlockSpec((D_MODEL, D_MODEL), const2, **once),
        pl.BlockSpec((1, D_MODEL), const2),
        pl.BlockSpec((1, D_MODEL), const2),
        pl.BlockSpec((N_EXPERTS, D_MODEL), const2),
        pl.BlockSpec((N_EXPERTS, 1), const2),
        pl.BlockSpec((CHUNK, CHUNK), const2),
        pl.BlockSpec((TB, TB), const2, **once),
    ]
    out_shape = [
        jax.ShapeDtypeStruct((t_total, D_MODEL), jnp.float32),
        jax.ShapeDtypeStruct((t_total, HALF), jnp.int32),
        jax.ShapeDtypeStruct((TOP_K, t_total), jnp.int32),
        jax.ShapeDtypeStruct((TOP_K, t_total), jnp.float32),
        jax.ShapeDtypeStruct((TOP_K, t_total), jnp.int32),
        jax.ShapeDtypeStruct((N_EXPERTS, LANES), jnp.float32),
    ]
    out_specs = [
        pl.BlockSpec((TB, D_MODEL), row_blk),
        pl.BlockSpec((TB, HALF), row_blk),
        pl.BlockSpec((TOP_K, TB), meta_blk),
        pl.BlockSpec((TOP_K, TB), meta_blk),
        pl.BlockSpec((TOP_K, TB), meta_blk),
        pl.BlockSpec((N_EXPERTS, LANES), const2),
    ]
    scratch = [
        pltpu.VMEM((TB, IN_COLS), jnp.float32),
        pltpu.VMEM((TB, HGRN_WIDTH), jnp.float32),
        pltpu.VMEM((TB, HGRN_WIDTH), jnp.float32),
        pltpu.VMEM((CHUNK, HGRN_WIDTH), jnp.float32),
        pltpu.VMEM((CHUNK, HGRN_WIDTH), jnp.float32),
        pltpu.VMEM((TB, HGRN_WIDTH), jnp.float32),
        pltpu.VMEM((HGRN_HEADS, HEAD_DIM, HEAD_DIM), jnp.float32),
        pltpu.VMEM((N_EXPERTS, LANES), jnp.float32),
        pltpu.VMEM((TB // CHUNK * HGRN_HEADS, CHUNK, CHUNK), jnp.bfloat16),
        pltpu.VMEM((TB, HGRN_WIDTH), jnp.bfloat16),
        pltpu.VMEM((TB, HGRN_WIDTH), jnp.bfloat16),
        pltpu.VMEM((TB // CHUNK * HGRN_HEADS, HEAD_DIM, HEAD_DIM), jnp.float32),
        pltpu.VMEM((TB, D_MODEL), jnp.bfloat16),
    ]
    return pl.pallas_call(
        _mixer_kernel,
        grid=(batch, nt),
        in_specs=in_specs,
        out_specs=out_specs,
        out_shape=out_shape,
        scratch_shapes=scratch,
        compiler_params=pltpu.CompilerParams(
            dimension_semantics=("arbitrary", "arbitrary"),
            vmem_limit_bytes=VMEM_LIMIT),
        name="mixer",
    )(xt, xt, win, lb, hg, lng, lnb, wm, bst, wout, l1g, l1b, rwh, rb, tri, upp)


def _plan_kernel(cnt_ref, idx_ref, rank_ref, dest_ref, tiles_ref):
    f32 = jnp.float32
    n_e = N_EXPERTS
    e_sub = lax.broadcasted_iota(jnp.int32, (n_e, n_e), 0)
    e_lane = lax.broadcasted_iota(jnp.int32, (n_e, n_e), 1)
    counts = cnt_ref[:, 0:1]
    padded = jnp.floor((counts + (TM - 1)) * (1.0 / TM)) * TM
    as_row = lambda col: jnp.sum(jnp.where(e_sub == e_lane, col, 0.0), axis=0, keepdims=True)
    padded_row = as_row(padded)
    counts_row = as_row(counts)
    ends = jnp.sum(jnp.where(e_lane <= e_sub, padded_row, 0.0), axis=1, keepdims=True)
    starts = ends - padded
    owns_later = (e_lane > e_sub) & (counts_row > 0.0)
    nxt = jnp.min(jnp.where(owns_later, e_lane, n_e), axis=1, keepdims=True)
    own = lax.broadcasted_iota(jnp.int32, (n_e, 1), 0)
    nxt = jnp.where(nxt == n_e, own, nxt).astype(f32)

    n_lanes = tiles_ref.shape[1]
    tile_row = lax.broadcasted_iota(jnp.int32, (n_e, n_lanes), 1).astype(f32) * TM
    e_of = lax.broadcasted_iota(jnp.int32, (n_e, n_lanes), 0)
    tile_e = jnp.minimum(jnp.sum(jnp.where(tile_row >= ends, 1, 0), axis=0, keepdims=True), n_e - 1)
    mine = e_of == tile_e
    pick = lambda col: jnp.sum(jnp.where(mine, col, 0.0), axis=0, keepdims=True)
    valid = jnp.clip(pick(starts + counts) - tile_row[0:1], 0.0, float(TM))
    rows = [tile_e, valid.astype(jnp.int32), pick(nxt).astype(jnp.int32)]
    rows.append(jnp.zeros((tiles_ref.shape[0] - len(rows), n_lanes), jnp.int32))
    tiles_ref[...] = jnp.concatenate(rows, axis=0)

    chunk = PLAN_CHUNK
    e_chunk = lax.broadcasted_iota(jnp.int32, (n_e, chunk), 0)

    def body(c, carry):
        lanes = pl.ds(pl.multiple_of(c * chunk, chunk), chunk)
        for k in range(TOP_K):
            hit = e_chunk == idx_ref[k:k + 1, lanes]
            start_of = jnp.sum(jnp.where(hit, starts, 0.0), axis=0, keepdims=True)
            dest_ref[k:k + 1, lanes] = rank_ref[k:k + 1, lanes] + start_of.astype(jnp.int32)
        return carry

    lax.fori_loop(0, idx_ref.shape[1] // chunk, body, 0)


def _plan(cnt, idx, rank, n_tiles):
    t_part = idx.shape[1]
    n_lanes = -(-n_tiles // LANES) * LANES
    full = lambda shape: pl.BlockSpec(shape, lambda i: (0,) * len(shape))
    dest, tiles = pl.pallas_call(
        _plan_kernel,
        grid=(1,),
        in_specs=[full(cnt.shape), full(idx.shape), full(rank.shape)],
        out_specs=[full(idx.shape), full((8, n_lanes))],
        out_shape=[jax.ShapeDtypeStruct((TOP_K, t_part), jnp.int32),
                   jax.ShapeDtypeStruct((8, n_lanes), jnp.int32)],
        compiler_params=pltpu.CompilerParams(
            dimension_semantics=("arbitrary",), vmem_limit_bytes=VMEM_LIMIT),
        name="plan",
    )(cnt, idx, rank)
    return dest, tiles[0, :n_tiles], tiles[1, :n_tiles], tiles[2, :n_tiles]


def _sc_workers():
    info = plsc.get_sparse_core_info()
    return info.num_cores, info.num_cores * info.num_subcores


def _sc_dispatch(rows, dest, n_out):
    t_total, dw = rows.shape
    nc, nw = _sc_workers()
    per_w = t_total // nw
    mesh = plsc.VectorSubcoreMesh(core_axis_name="c", subcore_axis_name="s")

    @functools.partial(
        pl.kernel,
        out_type=jax.ShapeDtypeStruct((n_out, dw), rows.dtype),
        mesh=mesh,
        scratch_types=[pltpu.VMEM((SC_WINDOW,), jnp.int32) for _ in range(TOP_K)]
        + [pltpu.VMEM((SC_WINDOW, dw), rows.dtype), pltpu.SemaphoreType.DMA],
        name="sc_dispatch",
    )
    def k(x_hbm, i_hbm, o_hbm, i0, i1, i2, i3, rows_v, sem):
        wid = lax.axis_index("s") * nc + lax.axis_index("c")
        idx_bufs = (i0, i1, i2, i3)

        @pl.loop(0, per_w // SC_WINDOW)
        def _(j):
            base = wid * per_w + j * SC_WINDOW
            for kk in range(TOP_K):
                pltpu.sync_copy(i_hbm.at[kk, pl.ds(base, SC_WINDOW)], idx_bufs[kk])
            pltpu.sync_copy(x_hbm.at[pl.ds(base, SC_WINDOW)], rows_v)
            copies = [pltpu.async_copy(rows_v, o_hbm.at[idx_bufs[kk]], sem) for kk in range(TOP_K)]
            for cp in copies:
                cp.wait()

    return k(rows, dest)


def _sc_gather(table, idx):
    n_k, t_total = idx.shape
    n = n_k * t_total
    dw = table.shape[1]
    nc, nw = _sc_workers()
    per_w = n // nw
    w_per_k = nw // n_k
    mesh = plsc.VectorSubcoreMesh(core_axis_name="c", subcore_axis_name="s")

    @functools.partial(
        pl.kernel,
        out_type=jax.ShapeDtypeStruct((n, dw), table.dtype),
        mesh=mesh,
        scratch_types=[pltpu.VMEM((SC_WINDOW,), jnp.int32),
                       pltpu.VMEM((SC_WINDOW, dw), table.dtype),
                       pltpu.SemaphoreType.DMA],
        name="sc_gather",
    )
    def k(t_hbm, i_hbm, o_hbm, idx_v, rows_v, sem):
        wid = lax.axis_index("s") * nc + lax.axis_index("c")

        row = wid // w_per_k
        col0 = (wid % w_per_k) * per_w

        @pl.loop(0, per_w // SC_WINDOW)
        def _(j):
            col = col0 + j * SC_WINDOW
            pltpu.sync_copy(i_hbm.at[row, pl.ds(col, SC_WINDOW)], idx_v)
            pltpu.async_copy(t_hbm.at[idx_v], rows_v, sem).wait()
            pltpu.sync_copy(rows_v, o_hbm.at[pl.ds(row * t_total + col, SC_WINDOW)])

    return k(table, idx)


def _expert_kernel(te_ref, tv_ref, tn_ref, xs_ref, w1_hbm, w2_hbm, b1g_ref, b1l_ref, b2_ref,
                   perm_ref, y_ref, w1f_scr, w2f_scr, w1g_scr, w1l_scr, w2_scr, sems):
    def tile_body(h, carry):
        _expert_tile(pl.program_id(0) * TILES_PER_STEP + h, pl.multiple_of(h * TM, TM),
                     te_ref, tv_ref, tn_ref, xs_ref, w1_hbm, w2_hbm, b1g_ref, b1l_ref, b2_ref,
                     perm_ref, y_ref, w1f_scr, w2f_scr, w1g_scr, w1l_scr, w2_scr, sems)
        return carry

    lax.fori_loop(0, TILES_PER_STEP, tile_body, 0)


def _expert_tile(i, row0, te_ref, tv_ref, tn_ref, xs_ref, w1_hbm, w2_hbm, b1g_ref, b1l_ref, b2_ref,
                 perm_ref, y_ref, w1f_scr, w2f_scr, w1g_scr, w1l_scr, w2_scr, sems):
    valid = tv_ref[i]
    expert = te_ref[i]
    f32 = jnp.float32
    bf16 = jnp.bfloat16
    expert_changed = (i == 0) | (expert != te_ref[jnp.maximum(i - 1, 0)])

    def weight_copies(e):
        return (pltpu.make_async_copy(w1_hbm.at[e], w1f_scr, sems.at[0]),
                pltpu.make_async_copy(w2_hbm.at[e], w2f_scr, sems.at[1]))

    @pl.when(valid == 0)
    def _():
        y_ref[pl.ds(row0, TM), :] = jnp.zeros((TM, HALF), jnp.int32)

    @pl.when(i == 0)
    def _():
        for cp in weight_copies(expert):
            cp.start()

    @pl.when((valid > 0) & expert_changed)
    def _():
        for cp in weight_copies(expert):
            cp.wait()
        w2_scr[...] = w2f_scr[...].astype(bf16)
        perm = perm_ref[...]
        for c in range(2 * D_MODEL // PERM_BLOCK):
            blk = w1f_scr[:, c * PERM_BLOCK:(c + 1) * PERM_BLOCK].astype(bf16)
            r = jnp.dot(blk, perm, preferred_element_type=f32).astype(bf16)
            half = PERM_BLOCK // 2
            w1g_scr[:, c * half:(c + 1) * half] = r[:, :half]
            w1l_scr[:, c * half:(c + 1) * half] = r[:, half:]

        @pl.when(tn_ref[i] != expert)
        def _():
            for cp in weight_copies(tn_ref[i]):
                cp.start()

    def expert_rows(n_rows):
        keep = lax.broadcasted_iota(jnp.int32, (n_rows, 1), 0) < valid
        a, b = _unpack_rows(jnp.where(keep, xs_ref[pl.ds(row0, n_rows), :], 0))
        x = jnp.concatenate([a, b], axis=1).astype(bf16)
        bias_row = pl.ds(expert, 1)
        hg = jnp.dot(x, w1g_scr[...], preferred_element_type=f32) + b1g_ref[bias_row, :]
        xl1 = jnp.clip(jnp.dot(x, w1l_scr[...], preferred_element_type=f32)
                       + (b1l_ref[bias_row, :] + 1.0), 1.0 - SWIGLU_LIMIT, 1.0 + SWIGLU_LIMIT)
        xg = jnp.minimum(hg, SWIGLU_LIMIT)
        act = (0.5 * xg) * (jnp.tanh((0.5 * SWIGLU_ALPHA) * xg) + 1.0) * xl1
        y = jnp.dot(act.astype(bf16), w2_scr[...], preferred_element_type=f32) + b2_ref[bias_row, :]
        y_ref[pl.ds(row0, n_rows), :] = _pack_rows(y)
        if n_rows < TM:
            y_ref[pl.ds(row0 + n_rows, TM - n_rows), :] = jnp.zeros((TM - n_rows, HALF), jnp.int32)

    for blocks in range(TM // ROW_STEP, 0, -1):
        @pl.when((valid > (blocks - 1) * ROW_STEP) & (valid <= blocks * ROW_STEP))
        def _(blocks=blocks):
            expert_rows(blocks * ROW_STEP)


def _experts(xs, tile_expert, tile_valid, tile_next, w1, w2, b1g, b1l, b2, perm):
    n_slots = xs.shape[0]
    step_rows = TILES_PER_STEP * TM
    assert n_slots % step_rows == 0
    d_ff = w2.shape[1]
    whole = lambda i, te, tv, tn: (0, 0)
    grid_spec = pltpu.PrefetchScalarGridSpec(
        num_scalar_prefetch=3,
        grid=(n_slots // step_rows,),
        in_specs=[
            pl.BlockSpec((step_rows, HALF), lambda i, te, tv, tn: (i, 0)),
            pl.BlockSpec(memory_space=pl.ANY),
            pl.BlockSpec(memory_space=pl.ANY),
            pl.BlockSpec((N_EXPERTS, d_ff), whole),
            pl.BlockSpec((N_EXPERTS, d_ff), whole),
            pl.BlockSpec((N_EXPERTS, D_MODEL), whole),
            pl.BlockSpec((PERM_BLOCK, PERM_BLOCK), whole),
        ],
        out_specs=pl.BlockSpec((step_rows, HALF), lambda i, te, tv, tn: (i, 0)),
        scratch_shapes=[
            pltpu.VMEM((D_MODEL, 2 * d_ff), jnp.float32),
            pltpu.VMEM((d_ff, D_MODEL), jnp.float32),
            pltpu.VMEM((D_MODEL, d_ff), jnp.bfloat16),
            pltpu.VMEM((D_MODEL, d_ff), jnp.bfloat16),
            pltpu.VMEM((d_ff, D_MODEL), jnp.bfloat16),
            pltpu.SemaphoreType.DMA((2,)),
        ],
    )
    return pl.pallas_call(
        _expert_kernel,
        grid_spec=grid_spec,
        out_shape=jax.ShapeDtypeStruct((n_slots, HALF), jnp.int32),
        compiler_params=pltpu.CompilerParams(
            dimension_semantics=("arbitrary",),
            vmem_limit_bytes=VMEM_LIMIT),
        name="experts",
    )(tile_expert, tile_valid, tile_next, xs, w1, w2, b1g, b1l, b2, perm)


def _combine_kernel(h_ref, yk_ref, gate_ref, g_ref, b_ref, *rest):
    o_ref = rest[-1]
    gates = gate_ref[...].T
    acc_a = None
    acc_b = None
    for k in range(TOP_K):
        a, b = _unpack_rows(yk_ref[k])
        gk = gates[:, k:k + 1]
        acc_a = gk * a if acc_a is None else acc_a + gk * a
        acc_b = gk * b if acc_b is None else acc_b + gk * b
    ffn = jnp.concatenate([acc_a, acc_b], axis=1)
    o_ref[...] = _layer_norm(ALPHA * h_ref[...] + ffn, g_ref[...], b_ref[...])


def _combine(h1f, yk, gates, g2, b2, out_prev, row0, t_all):
    t_part = h1f.shape[0]
    blk0 = row0 // TC3
    row = lambda i: (i, 0)
    const = lambda i: (0, 0)
    in_specs = [
        pl.BlockSpec((TC3, D_MODEL), row),
        pl.BlockSpec((TOP_K, TC3, HALF), lambda i: (0, i, 0)),
        pl.BlockSpec((TOP_K, TC3), lambda i: (0, i)),
        pl.BlockSpec((1, D_MODEL), const),
        pl.BlockSpec((1, D_MODEL), const),
    ]
    args = [h1f, yk, gates, g2, b2]
    aliases = {}
    if out_prev is not None:
        in_specs.append(pl.BlockSpec(memory_space=pl.ANY))
        args.append(out_prev)
        aliases = {len(args) - 1: 0}
    return pl.pallas_call(
        _combine_kernel,
        grid=(t_part // TC3,),
        in_specs=in_specs,
        out_specs=pl.BlockSpec((TC3, D_MODEL), lambda i: (i + blk0, 0)),
        out_shape=jax.ShapeDtypeStruct((t_all, D_MODEL), jnp.float32),
        input_output_aliases=aliases,
        compiler_params=pltpu.CompilerParams(
            dimension_semantics=("arbitrary",),
            vmem_limit_bytes=VMEM_LIMIT),
        name="combine",
    )(*args)


def kernel(x, w_in, lb_logits, hgrn_norm_g, gmlp_ln_g, gmlp_ln_b, gmlp_ws, gmlp_bs, w_out, ln1_g, ln1_b, router_w, router_b, exp_w1, exp_b1, exp_w2, exp_b2, ln2_g, ln2_b):
    batch, seq, d = x.shape
    assert d == D_MODEL and seq % TB == 0 and w_in.shape[0] == 1
    t_total = batch * seq
    f32 = jnp.float32
    bf16 = jnp.bfloat16

    lb = jnp.cumsum(jax.nn.softmax(lb_logits.astype(f32), axis=0), axis=0)[0:1]
    chunk_id = jnp.arange(GMLP_BLOCK) // CHUNK
    wm = jnp.where((chunk_id[None, :] <= chunk_id[:, None])[None], gmlp_ws[0], 0.0).astype(bf16)
    rwh = router_w[0].T.astype(bf16)
    tri = (jnp.arange(CHUNK)[None, :] <= jnp.arange(CHUNK)[:, None]).astype(bf16)
    upp = (jnp.arange(TB)[:, None] < jnp.arange(TB)[None, :]).astype(bf16)

    lane = jnp.arange(PERM_BLOCK)
    src = jnp.where(lane < PERM_BLOCK // 2, 2 * lane, 2 * (lane - PERM_BLOCK // 2) + 1)
    perm = (jnp.arange(PERM_BLOCK)[:, None] == src[None, :]).astype(bf16)
    xt = x.reshape(t_total, d)
    win = w_in[0].astype(bf16)
    wout = w_out[0].astype(bf16)
    b1g, b1l, b2e = exp_b1[0][:, 0::2], exp_b1[0][:, 1::2], exp_b2[0]

    last = max(batch * LAST_PART_NUM // LAST_PART_DEN, 1) if batch > 1 else 0
    part_sizes = [pb for pb in (batch - last, last) if pb > 0]
    out = None
    b0 = 0
    for pb in part_sizes:
        t_part = pb * seq
        step_rows = TILES_PER_STEP * TM
        n_slots = -(-(t_part * TOP_K + N_EXPERTS * TM) // step_rows) * step_rows
        n_tiles = n_slots // TM
        h1f, h1p, idx, gates, rank, cnt = _mixer(
            xt, win, lb, hgrn_norm_g[0:1], gmlp_ln_g[0:1], gmlp_ln_b[0:1], wm, gmlp_bs[0].T, wout,
            ln1_g[0:1], ln1_b[0:1], rwh, router_b[0][:, None], tri, upp, seq, pb, b0)

        dest, tile_expert, tile_valid, tile_next = _plan(cnt, idx, rank, n_tiles)
        xs = _sc_dispatch(h1p, dest, n_slots)
        y = _experts(xs, tile_expert, tile_valid, tile_next,
                     exp_w1[0], exp_w2[0], b1g, b1l, b2e, perm)
        yk = _sc_gather(y, dest).reshape(TOP_K, t_part, HALF)
        out = _combine(h1f, yk, gates, ln2_g[0:1], ln2_b[0:1], out, b0 * seq, t_total)
        b0 += pb
    return out.reshape(batch, seq, d)
```

```python
import functools
import math

import jax
import jax.numpy as jnp
from jax import lax
from jax.experimental import pallas as pl
from jax.experimental.pallas import tpu as pltpu
from jax.experimental.pallas import tpu_sc as plsc

D_MODEL = 1024
CHUNK = 64
SUB = 32
N_SUB = CHUNK // SUB
MAX_SUB_DECAY = 86.0
HGRN_WIDTH = 512
HGRN_HEADS = 4
HEAD_DIM = 128
GMLP_WIDTH = 512
GMLP_BLOCK = 128
GMLP_GROUPS = 4
IN_COLS = 3072
N_EXPERTS = 32
TOP_K = 4
SWIGLU_LIMIT = 7.0
SWIGLU_ALPHA = 1.702
ALPHA = 2.0 ** 0.25
EPS = 1e-5
HALF = D_MODEL // 2

TB = 512
TM = 1024
ROW_STEP = 256
TILES_PER_STEP = 3
PLAN_CHUNK = 2048
TC3 = 1024
LAST_PART_NUM, LAST_PART_DEN = 1, 4
SC_WINDOW = 128
PROJ_GROUP = 256
PERM_BLOCK = 256
LANES = 128
V7X_VMEM_BYTES = 64 * 1024 * 1024
VMEM_LIMIT = V7X_VMEM_BYTES * 7 // 8

_NT = (((1,), (1,)), ((), ()))


def _gelu(x):
    return 0.5 * x * (1.0 + lax.erf(x * (1.0 / math.sqrt(2.0))))


def _layer_norm(x, g, b):
    mu = jnp.mean(x, axis=-1, keepdims=True)
    xc = x - mu
    var = jnp.mean(xc * xc, axis=-1, keepdims=True)
    return xc * lax.rsqrt(var + EPS) * g + b


def _pack_rounded(hr):
    au = lax.bitcast_convert_type(hr[:, :HALF], jnp.uint32) >> 16
    bu = lax.bitcast_convert_type(hr[:, HALF:], jnp.uint32) & jnp.uint32(0xFFFF0000)
    return lax.bitcast_convert_type(au | bu, jnp.int32)


def _pack_rows(h):
    return _pack_rounded(h.astype(jnp.bfloat16).astype(jnp.float32))


def _unpack_rows(w):
    u = lax.bitcast_convert_type(w, jnp.uint32)
    a = lax.bitcast_convert_type(u << 16, jnp.float32)
    b = lax.bitcast_convert_type(u & jnp.uint32(0xFFFF0000), jnp.float32)
    return a, b


def _mixer_kernel(x_ref, xn_ref, win_ref, lb_ref, hg_ref, lng_ref, lnb_ref, wm_ref, bst_ref,
                  wout_ref, l1g_ref, l1b_ref, rwh_ref, rb_ref, tri_ref, upp_ref,
                  h1f_ref, h1p_ref, idx_ref, gate_ref, rank_ref, cnt_ref,
                  p_scr, lf_scr, kk_scr, g_scr, kc_scr, o_scr, st_scr, carry_scr,
                  amat_scr, qg_scr, kd_scr, upd_scr, mix_scr, decay_scr):
    b = pl.program_id(0)
    t = pl.program_id(1)
    step = b * pl.num_programs(1) + t
    f32 = jnp.float32
    bf16 = jnp.bfloat16

    @pl.when(t == 0)
    def _():
        st_scr[...] = jnp.zeros_like(st_scr)

    @pl.when((b == 0) & (t == 0))
    def _():
        carry_scr[...] = jnp.zeros_like(carry_scr)

    def project(xb, group):
        cols = slice(group * PROJ_GROUP, (group + 1) * PROJ_GROUP)
        p_scr[:, cols] = jnp.dot(xb, win_ref[:, cols], preferred_element_type=f32)

    def hgrn_gates():
        fl = p_scr[:, HGRN_WIDTH:2 * HGRN_WIDTH]
        z = jnp.exp(-jnp.abs(fl))
        r = 1.0 / (1.0 + z)
        k = (1.0 - lb_ref[...]) * jnp.where(fl >= 0, z * r, r)
        kk_scr[...] = k
        lf = jnp.log2(1.0 - k)
        lf_scr[...] = lf
        decay_scr[0] = jnp.max(-jnp.sum(lf.reshape(TB // SUB, SUB, HGRN_WIDTH), axis=1))

    @pl.when(step == 0)
    def _():
        xb0 = x_ref[...].astype(bf16)
        for group in range(IN_COLS // PROJ_GROUP):
            project(xb0, group)
        hgrn_gates()

    tri = tri_ref[...]
    row64 = lax.broadcasted_iota(jnp.int32, (CHUNK, CHUNK), 0)
    col64 = lax.broadcasted_iota(jnp.int32, (CHUNK, CHUNK), 1)
    lane_sub = lax.broadcasted_iota(jnp.int32, (SUB, CHUNK), 1)
    scale = HEAD_DIM ** -0.5

    def pad_rows(piece, lo_r):
        parts = []
        if lo_r > 0:
            parts.append(jnp.zeros((lo_r, HEAD_DIM), f32))
        parts.append(piece)
        rest = CHUNK - lo_r - piece.shape[0]
        if rest > 0:
            parts.append(jnp.zeros((rest, HEAD_DIM), f32))
        return jnp.concatenate(parts, axis=0) if len(parts) > 1 else piece

    def intra_factorised(q, k, gh):
        qts, kts = [], []
        for a in range(N_SUB):
            lo_r, hi_r = a * SUB, (a + 1) * SUB
            if a == 0:
                qa = q[:hi_r] * jnp.exp2(gh[:hi_r])
                ka = k[:hi_r] * jnp.exp2(-gh[:hi_r])
            else:
                ra = gh[lo_r - 1:lo_r]
                qa = q[lo_r:hi_r] * jnp.exp2(gh[lo_r:hi_r] - ra)
                ka = k[:hi_r] * jnp.exp2(ra - gh[:hi_r])
            qts.append(pad_rows(qa, lo_r))
            kts.append(pad_rows(ka, 0))
        a_mat = lax.dot_general(jnp.concatenate(qts, axis=1).astype(bf16),
                                jnp.concatenate(kts, axis=1).astype(bf16), _NT,
                                preferred_element_type=f32)
        return jnp.where(row64 >= col64, a_mat, 0.0)

    def chunks_factorised():
        n_chunks = TB // CHUNK
        heads = [(c, h) for c in range(n_chunks) for h in range(HGRN_HEADS)]

        def cols(h, base=0):
            return slice(base + h * HEAD_DIM, base + (h + 1) * HEAD_DIM)

        for c in range(n_chunks):
            rows = slice(c * CHUNK, (c + 1) * CHUNK)
            lf = lf_scr[rows, :]
            hi = lf.astype(bf16)
            lo = (lf - hi.astype(f32)).astype(bf16)
            gg = jnp.dot(tri, jnp.concatenate([hi, lo], axis=1), preferred_element_type=f32)
            lf_scr[rows, :] = gg[:, :HGRN_WIDTH] + gg[:, HGRN_WIDTH:]
        for c, h in heads:
            rows = slice(c * CHUNK, (c + 1) * CHUNK)
            q = p_scr[rows, cols(h)]
            k = kk_scr[rows, cols(h)]
            gh = lf_scr[rows, cols(h)]
            amat_scr[c * HGRN_HEADS + h] = intra_factorised(q, k, gh).astype(bf16)
            qg_scr[rows, cols(h)] = (q * jnp.exp2(gh)).astype(bf16)
            kd_scr[rows, cols(h)] = (k * jnp.exp2(gh[CHUNK - 1:CHUNK] - gh)).astype(bf16)
        for c, h in heads:
            rows = slice(c * CHUNK, (c + 1) * CHUNK)
            v = p_scr[rows, cols(h, 2 * HGRN_WIDTH)]
            o_scr[rows, cols(h)] = jnp.dot(amat_scr[c * HGRN_HEADS + h], v.astype(bf16),
                                           preferred_element_type=f32)
            upd_scr[c * HGRN_HEADS + h] = jnp.dot(v.T.astype(bf16), kd_scr[rows, cols(h)],
                                                  preferred_element_type=f32)
        states = [st_scr[h] for h in range(HGRN_HEADS)]
        for c, h in heads:
            rows = slice(c * CHUNK, (c + 1) * CHUNK)
            st = states[h]
            o_inter = lax.dot_general(qg_scr[rows, cols(h)], st.astype(bf16), _NT,
                                      preferred_element_type=f32)
            o_scr[rows, cols(h)] = (o_scr[rows, cols(h)] + o_inter) * scale
            gl = lf_scr[(c + 1) * CHUNK - 1:(c + 1) * CHUNK, cols(h)]
            states[h] = st * jnp.exp2(gl) + upd_scr[c * HGRN_HEADS + h]
        for h in range(HGRN_HEADS):
            st_scr[h] = states[h]

    def intra_exact_diagonal(q, k, gh, cs):
        qts, kts = [], []
        for a in range(1, N_SUB):
            lo_r, hi_r = a * SUB, (a + 1) * SUB
            ra = g_scr[lo_r - 1:lo_r, cs]
            qts.append(pad_rows(q[lo_r:hi_r] * jnp.exp2(gh[lo_r:hi_r] - ra), lo_r))
            kts.append(pad_rows(k[:lo_r] * jnp.exp2(ra - gh[:lo_r]), 0))
        a_off = lax.dot_general(jnp.concatenate(qts, axis=1).astype(bf16),
                                jnp.concatenate(kts, axis=1).astype(bf16), _NT,
                                preferred_element_type=f32)
        diag_rows = []
        for a in range(N_SUB):
            lo_r = a * SUB
            gs = gh[lo_r:lo_r + SUB]
            qs = q[lo_r:lo_r + SUB]
            blk = jnp.zeros((SUB, CHUNK), f32)
            for jl in range(SUB):
                j = lo_r + jl
                gj = g_scr[j:j + 1, cs]
                kj = kc_scr[j:j + 1, cs]
                e = jnp.exp2(jnp.minimum(gs - gj, 0.0))
                col = jnp.sum(qs * (kj * e), axis=-1, keepdims=True)
                blk = jnp.where(lane_sub == j, col, blk)
            diag_rows.append(blk)
        a_diag = jnp.concatenate(diag_rows, axis=0)
        return a_off + jnp.where(row64 >= col64, a_diag, 0.0)

    def make_chunk_body(intra):
        def chunk_body(c, carry):
            r0 = pl.multiple_of(c * CHUNK, CHUNK)
            rows = pl.ds(r0, CHUNK)
            lf = lf_scr[rows, :]
            hi = lf.astype(bf16)
            lo = (lf - hi.astype(f32)).astype(bf16)
            gg = jnp.dot(tri, jnp.concatenate([hi, lo], axis=1), preferred_element_type=f32)
            g_all = gg[:, :HGRN_WIDTH] + gg[:, HGRN_WIDTH:]
            g_scr[...] = g_all
            kc_scr[...] = kk_scr[rows, :]
            for h in range(HGRN_HEADS):
                cs = slice(h * HEAD_DIM, (h + 1) * HEAD_DIM)
                q = p_scr[rows, h * HEAD_DIM:(h + 1) * HEAD_DIM]
                v = p_scr[rows, 2 * HGRN_WIDTH + h * HEAD_DIM:2 * HGRN_WIDTH + (h + 1) * HEAD_DIM]
                k = kc_scr[:, cs]
                gh = g_all[:, cs]
                st = st_scr[h]
                o_inter = lax.dot_general((q * jnp.exp2(gh)).astype(bf16), st.astype(bf16), _NT,
                                          preferred_element_type=f32)
                a_mat = intra(q, k, gh, cs)
                o = jnp.dot(a_mat.astype(bf16), v.astype(bf16), preferred_element_type=f32) + o_inter
                o_scr[rows, h * HEAD_DIM:(h + 1) * HEAD_DIM] = o * scale

                gl = g_scr[CHUNK - 1:CHUNK, cs]
                kd = k * jnp.exp2(gl - gh)
                upd = jnp.dot(v.T.astype(bf16), kd.astype(bf16), preferred_element_type=f32)
                st_scr[h] = st * jnp.exp2(gl) + upd
            return carry
        return chunk_body

    bounded = decay_scr[0] <= MAX_SUB_DECAY

    f_groups = list(range(HGRN_WIDTH // PROJ_GROUP, 2 * HGRN_WIDTH // PROJ_GROUP))
    qi_groups = [g for g in range(3 * HGRN_WIDTH // PROJ_GROUP) if g not in f_groups]
    rest_groups = list(range(3 * HGRN_WIDTH // PROJ_GROUP, IN_COLS // PROJ_GROUP))

    def gmlp_stage(xnb):
        for g in f_groups:
            project(xnb, g)
        for w in range(TB // GMLP_BLOCK):
            rows = slice(w * GMLP_BLOCK, (w + 1) * GMLP_BLOCK)
            u = _gelu(p_scr[rows, 4 * HGRN_WIDTH:4 * HGRN_WIDTH + GMLP_WIDTH])
            vn = _layer_norm(_gelu(p_scr[rows, 4 * HGRN_WIDTH + GMLP_WIDTH:]),
                             lng_ref[...], lnb_ref[...])
            vnb = vn.astype(bf16)
            cols = []
            for g in range(GMLP_GROUPS):
                s = jnp.dot(wm_ref[g], vnb[:, g * HEAD_DIM:(g + 1) * HEAD_DIM],
                            preferred_element_type=f32) + bst_ref[:, g:g + 1]
                cols.append(s)
            mix_scr[rows, HGRN_WIDTH:] = (u * jnp.concatenate(cols, axis=1)).astype(bf16)

    def output_stage(xnb):
        for g in qi_groups:
            project(xnb, g)
        for w in range(TB // GMLP_BLOCK):
            rows = slice(w * GMLP_BLOCK, (w + 1) * GMLP_BLOCK)
            o = o_scr[rows, :]
            ms = jnp.mean(o * o, axis=-1, keepdims=True)
            gate = p_scr[rows, 3 * HGRN_WIDTH:4 * HGRN_WIDTH]
            half_gate = 0.5 * gate
            silu = half_gate * (jnp.tanh(half_gate) + 1.0)
            y_rec = o * lax.rsqrt(ms + EPS) * hg_ref[...] * silu
            mix_scr[rows, :HGRN_WIDTH] = y_rec.astype(bf16)

        mix = jnp.dot(mix_scr[...], wout_ref[...], preferred_element_type=f32)
        hgrn_gates()
        h1 = _layer_norm(ALPHA * x_ref[...] + mix, l1g_ref[...], l1b_ref[...])
        h1f_ref[...] = h1
        hh = h1.astype(bf16)
        h1_rounded = hh.astype(f32)
        h1p_ref[...] = _pack_rounded(h1_rounded)
        for g in rest_groups[:2]:
            project(xnb, g)

        logits = lax.dot_general(rwh_ref[...], hh, _NT, preferred_element_type=f32) + rb_ref[...]
        for g in rest_groups[2:]:
            project(xnb, g)

        e_iota = lax.broadcasted_iota(jnp.int32, (N_EXPERTS, TB), 0)
        work = logits
        vals, idxs, hots = [], [], []
        for _ in range(TOP_K):
            m = jnp.max(work, axis=0, keepdims=True)
            ik = jnp.min(jnp.where(work == m, e_iota, N_EXPERTS), axis=0, keepdims=True)
            hot = e_iota == ik
            work = jnp.where(hot, -jnp.inf, work)
            vals.append(m)
            idxs.append(ik)
            hots.append(hot)
        exps = [jnp.exp(vk - vals[0]) for vk in vals]
        denom = exps[0] + exps[1] + exps[2] + exps[3]
        gate_ref[...] = jnp.concatenate([ek / denom for ek in exps], axis=0)
        idx_ref[...] = jnp.concatenate(idxs, axis=0)

        hot_any = jnp.where(hots[0] | hots[1] | hots[2] | hots[3], 1.0, 0.0)
        prefix = jnp.dot(hot_any.astype(bf16), upp_ref[...], preferred_element_type=f32)
        base = prefix + carry_scr[:, 0:1]
        ranks = [jnp.sum(jnp.where(hk, base, 0.0), axis=0, keepdims=True) for hk in hots]
        rank_ref[...] = jnp.concatenate(ranks, axis=0).astype(jnp.int32)
        new_carry = carry_scr[...] + jnp.sum(hot_any, axis=1, keepdims=True)
        carry_scr[...] = new_carry
        cnt_ref[...] = new_carry

    @pl.when(bounded)
    def _():
        xnb = xn_ref[...].astype(bf16)
        gmlp_stage(xnb)
        chunks_factorised()
        output_stage(xnb)

    @pl.when(jnp.logical_not(bounded))
    def _():
        xnb = xn_ref[...].astype(bf16)
        lax.fori_loop(0, TB // CHUNK, make_chunk_body(intra_exact_diagonal), 0)
        gmlp_stage(xnb)
        output_stage(xnb)


def _mixer(xt, win, lb, hg, lng, lnb, wm, bst, wout, l1g, l1b, rwh, rb, tri, upp,
           seq, batch, batch0):
    nt = seq // TB
    t_total = batch * seq
    nblk = batch * nt
    const2 = lambda b, t: (0, 0)
    const3 = lambda b, t: (0, 0, 0)
    row_blk = lambda b, t: (b * nt + t, 0)
    meta_blk = lambda b, t: (0, b * nt + t)
    once = dict(pipeline_mode=pl.Buffered(1))
    in_specs = [
        pl.BlockSpec((TB, D_MODEL), lambda b, t: ((b + batch0) * nt + t, 0)),
        pl.BlockSpec((TB, D_MODEL),
                     lambda b, t: (batch0 * nt + jnp.minimum(b * nt + t + 1, nblk - 1), 0)),
        pl.BlockSpec((D_MODEL, IN_COLS), const2, **once),
        pl.BlockSpec((1, HGRN_WIDTH), const2),
        pl.BlockSpec((1, HGRN_WIDTH), const2),
        pl.BlockSpec((1, GMLP_WIDTH), const2),
        pl.BlockSpec((1, GMLP_WIDTH), const2),
        pl.BlockSpec((GMLP_GROUPS, GMLP_BLOCK, GMLP_BLOCK), const3),
        pl.BlockSpec((GMLP_BLOCK, GMLP_GROUPS), const2),
        pl.BlockSpec((D_MODEL, D_MODEL), const2, **once),
        pl.BlockSpec((1, D_MODEL), const2),
        pl.BlockSpec((1, D_MODEL), const2),
        pl.BlockSpec((N_EXPERTS, D_MODEL), const2),
        pl.BlockSpec((N_EXPERTS, 1), const2),
        pl.BlockSpec((CHUNK, CHUNK), const2),
        pl.BlockSpec((TB, TB), const2, **once),
    ]
    out_shape = [
        jax.ShapeDtypeStruct((t_total, D_MODEL), jnp.float32),
        jax.ShapeDtypeStruct((t_total, HALF), jnp.int32),
        jax.ShapeDtypeStruct((TOP_K, t_total), jnp.int32),
        jax.ShapeDtypeStruct((TOP_K, t_total), jnp.float32),
        jax.ShapeDtypeStruct((TOP_K, t_total), jnp.int32),
        jax.ShapeDtypeStruct((N_EXPERTS, LANES), jnp.float32),
    ]
    out_specs = [
        pl.BlockSpec((TB, D_MODEL), row_blk),
        pl.BlockSpec((TB, HALF), row_blk),
        pl.BlockSpec((TOP_K, TB), meta_blk),
        pl.BlockSpec((TOP_K, TB), meta_blk),
        pl.BlockSpec((TOP_K, TB), meta_blk),
        pl.BlockSpec((N_EXPERTS, LANES), const2),
    ]
    scratch = [
        pltpu.VMEM((TB, IN_COLS), jnp.float32),
        pltpu.VMEM((TB, HGRN_WIDTH), jnp.float32),
        pltpu.VMEM((TB, HGRN_WIDTH), jnp.float32),
        pltpu.VMEM((CHUNK, HGRN_WIDTH), jnp.float32),
        pltpu.VMEM((CHUNK, HGRN_WIDTH), jnp.float32),
        pltpu.VMEM((TB, HGRN_WIDTH), jnp.float32),
        pltpu.VMEM((HGRN_HEADS, HEAD_DIM, HEAD_DIM), jnp.float32),
        pltpu.VMEM((N_EXPERTS, LANES), jnp.float32),
        pltpu.VMEM((TB // CHUNK * HGRN_HEADS, CHUNK, CHUNK), jnp.bfloat16),
        pltpu.VMEM((TB, HGRN_WIDTH), jnp.bfloat16),
        pltpu.VMEM((TB, HGRN_WIDTH), jnp.bfloat16),
        pltpu.VMEM((TB // CHUNK * HGRN_HEADS, HEAD_DIM, HEAD_DIM), jnp.float32),
        pltpu.VMEM((TB, D_MODEL), jnp.bfloat16),
        pltpu.SMEM((1,), jnp.float32),
    ]
    return pl.pallas_call(
        _mixer_kernel,
        grid=(batch, nt),
        in_specs=in_specs,
        out_specs=out_specs,
        out_shape=out_shape,
        scratch_shapes=scratch,
        compiler_params=pltpu.CompilerParams(
            dimension_semantics=("arbitrary", "arbitrary"),
            vmem_limit_bytes=VMEM_LIMIT),
        name="mixer",
    )(xt, xt, win, lb, hg, lng, lnb, wm, bst, wout, l1g, l1b, rwh, rb, tri, upp)


def _plan_kernel(cnt_ref, idx_ref, rank_ref, dest_ref, tiles_ref):
    f32 = jnp.float32
    n_e = N_EXPERTS
    e_sub = lax.broadcasted_iota(jnp.int32, (n_e, n_e), 0)
    e_lane = lax.broadcasted_iota(jnp.int32, (n_e, n_e), 1)
    counts = cnt_ref[:, 0:1]
    padded = jnp.floor((counts + (TM - 1)) * (1.0 / TM)) * TM
    as_row = lambda col: jnp.sum(jnp.where(e_sub == e_lane, col, 0.0), axis=0, keepdims=True)
    padded_row = as_row(padded)
    counts_row = as_row(counts)
    ends = jnp.sum(jnp.where(e_lane <= e_sub, padded_row, 0.0), axis=1, keepdims=True)
    starts = ends - padded
    owns_later = (e_lane > e_sub) & (counts_row > 0.0)
    nxt = jnp.min(jnp.where(owns_later, e_lane, n_e), axis=1, keepdims=True)
    own = lax.broadcasted_iota(jnp.int32, (n_e, 1), 0)
    nxt = jnp.where(nxt == n_e, own, nxt).astype(f32)

    n_lanes = tiles_ref.shape[1]
    tile_row = lax.broadcasted_iota(jnp.int32, (n_e, n_lanes), 1).astype(f32) * TM
    e_of = lax.broadcasted_iota(jnp.int32, (n_e, n_lanes), 0)
    tile_e = jnp.minimum(jnp.sum(jnp.where(tile_row >= ends, 1, 0), axis=0, keepdims=True), n_e - 1)
    mine = e_of == tile_e
    pick = lambda col: jnp.sum(jnp.where(mine, col, 0.0), axis=0, keepdims=True)
    valid = jnp.clip(pick(starts + counts) - tile_row[0:1], 0.0, float(TM))
    rows = [tile_e, valid.astype(jnp.int32), pick(nxt).astype(jnp.int32)]
    rows.append(jnp.zeros((tiles_ref.shape[0] - len(rows), n_lanes), jnp.int32))
    tiles_ref[...] = jnp.concatenate(rows, axis=0)

    chunk = PLAN_CHUNK
    e_chunk = lax.broadcasted_iota(jnp.int32, (n_e, chunk), 0)

    def body(c, carry):
        lanes = pl.ds(pl.multiple_of(c * chunk, chunk), chunk)
        for k in range(TOP_K):
            hit = e_chunk == idx_ref[k:k + 1, lanes]
            start_of = jnp.sum(jnp.where(hit, starts, 0.0), axis=0, keepdims=True)
            dest_ref[k:k + 1, lanes] = rank_ref[k:k + 1, lanes] + start_of.astype(jnp.int32)
        return carry

    lax.fori_loop(0, idx_ref.shape[1] // chunk, body, 0)


def _plan(cnt, idx, rank, n_tiles):
    t_part = idx.shape[1]
    n_lanes = -(-n_tiles // LANES) * LANES
    full = lambda shape: pl.BlockSpec(shape, lambda i: (0,) * len(shape))
    dest, tiles = pl.pallas_call(
        _plan_kernel,
        grid=(1,),
        in_specs=[full(cnt.shape), full(idx.shape), full(rank.shape)],
        out_specs=[full(idx.shape), full((8, n_lanes))],
        out_shape=[jax.ShapeDtypeStruct((TOP_K, t_part), jnp.int32),
                   jax.ShapeDtypeStruct((8, n_lanes), jnp.int32)],
        compiler_params=pltpu.CompilerParams(
            dimension_semantics=("arbitrary",), vmem_limit_bytes=VMEM_LIMIT),
        name="plan",
    )(cnt, idx, rank)
    return dest, tiles[0, :n_tiles], tiles[1, :n_tiles], tiles[2, :n_tiles]


def _sc_workers():
    info = plsc.get_sparse_core_info()
    return info.num_cores, info.num_cores * info.num_subcores


def _sc_dispatch(rows, dest, n_out):
    t_total, dw = rows.shape
    nc, nw = _sc_workers()
    per_w = t_total // nw
    mesh = plsc.VectorSubcoreMesh(core_axis_name="c", subcore_axis_name="s")

    @functools.partial(
        pl.kernel,
        out_type=jax.ShapeDtypeStruct((n_out, dw), rows.dtype),
        mesh=mesh,
        scratch_types=[pltpu.VMEM((SC_WINDOW,), jnp.int32) for _ in range(TOP_K)]
        + [pltpu.VMEM((SC_WINDOW, dw), rows.dtype), pltpu.SemaphoreType.DMA],
        name="sc_dispatch",
    )
    def k(x_hbm, i_hbm, o_hbm, i0, i1, i2, i3, rows_v, sem):
        wid = lax.axis_index("s") * nc + lax.axis_index("c")
        idx_bufs = (i0, i1, i2, i3)

        @pl.loop(0, per_w // SC_WINDOW)
        def _(j):
            base = wid * per_w + j * SC_WINDOW
            for kk in range(TOP_K):
                pltpu.sync_copy(i_hbm.at[kk, pl.ds(base, SC_WINDOW)], idx_bufs[kk])
            pltpu.sync_copy(x_hbm.at[pl.ds(base, SC_WINDOW)], rows_v)
            copies = [pltpu.async_copy(rows_v, o_hbm.at[idx_bufs[kk]], sem) for kk in range(TOP_K)]
            for cp in copies:
                cp.wait()

    return k(rows, dest)


def _sc_gather(table, idx):
    n_k, t_total = idx.shape
    n = n_k * t_total
    dw = table.shape[1]
    nc, nw = _sc_workers()
    per_w = n // nw
    w_per_k = nw // n_k
    mesh = plsc.VectorSubcoreMesh(core_axis_name="c", subcore_axis_name="s")

    @functools.partial(
        pl.kernel,
        out_type=jax.ShapeDtypeStruct((n, dw), table.dtype),
        mesh=mesh,
        scratch_types=[pltpu.VMEM((SC_WINDOW,), jnp.int32),
                       pltpu.VMEM((SC_WINDOW, dw), table.dtype),
                       pltpu.SemaphoreType.DMA],
        name="sc_gather",
    )
    def k(t_hbm, i_hbm, o_hbm, idx_v, rows_v, sem):
        wid = lax.axis_index("s") * nc + lax.axis_index("c")

        row = wid // w_per_k
        col0 = (wid % w_per_k) * per_w

        @pl.loop(0, per_w // SC_WINDOW)
        def _(j):
            col = col0 + j * SC_WINDOW
            pltpu.sync_copy(i_hbm.at[row, pl.ds(col, SC_WINDOW)], idx_v)
            pltpu.async_copy(t_hbm.at[idx_v], rows_v, sem).wait()
            pltpu.sync_copy(rows_v, o_hbm.at[pl.ds(row * t_total + col, SC_WINDOW)])

    return k(table, idx)


def _expert_kernel(te_ref, tv_ref, tn_ref, xs_ref, w1_hbm, w2_hbm, b1g_ref, b1l_ref, b2_ref,
                   perm_ref, y_ref, w1f_scr, w2f_scr, w1g_scr, w1l_scr, w2_scr, sems):
    def tile_body(h, carry):
        _expert_tile(pl.program_id(0) * TILES_PER_STEP + h, pl.multiple_of(h * TM, TM),
                     te_ref, tv_ref, tn_ref, xs_ref, w1_hbm, w2_hbm, b1g_ref, b1l_ref, b2_ref,
                     perm_ref, y_ref, w1f_scr, w2f_scr, w1g_scr, w1l_scr, w2_scr, sems)
        return carry

    lax.fori_loop(0, TILES_PER_STEP, tile_body, 0)


def _expert_tile(i, row0, te_ref, tv_ref, tn_ref, xs_ref, w1_hbm, w2_hbm, b1g_ref, b1l_ref, b2_ref,
                 perm_ref, y_ref, w1f_scr, w2f_scr, w1g_scr, w1l_scr, w2_scr, sems):
    valid = tv_ref[i]
    expert = te_ref[i]
    f32 = jnp.float32
    bf16 = jnp.bfloat16
    expert_changed = (i == 0) | (expert != te_ref[jnp.maximum(i - 1, 0)])

    def weight_copies(e):
        return (pltpu.make_async_copy(w1_hbm.at[e], w1f_scr, sems.at[0]),
                pltpu.make_async_copy(w2_hbm.at[e], w2f_scr, sems.at[1]))

    @pl.when(valid == 0)
    def _():
        y_ref[pl.ds(row0, TM), :] = jnp.zeros((TM, HALF), jnp.int32)

    @pl.when(i == 0)
    def _():
        for cp in weight_copies(expert):
            cp.start()

    @pl.when((valid > 0) & expert_changed)
    def _():
        for cp in weight_copies(expert):
            cp.wait()
        w2_scr[...] = w2f_scr[...].astype(bf16)
        perm = perm_ref[...]
        for c in range(2 * D_MODEL // PERM_BLOCK):
            blk = w1f_scr[:, c * PERM_BLOCK:(c + 1) * PERM_BLOCK].astype(bf16)
            r = jnp.dot(blk, perm, preferred_element_type=f32).astype(bf16)
            half = PERM_BLOCK // 2
            w1g_scr[:, c * half:(c + 1) * half] = r[:, :half]
            w1l_scr[:, c * half:(c + 1) * half] = r[:, half:]

        @pl.when(tn_ref[i] != expert)
        def _():
            for cp in weight_copies(tn_ref[i]):
                cp.start()

    def expert_rows(n_rows):
        keep = lax.broadcasted_iota(jnp.int32, (n_rows, 1), 0) < valid
        a, b = _unpack_rows(jnp.where(keep, xs_ref[pl.ds(row0, n_rows), :], 0))
        x = jnp.concatenate([a, b], axis=1).astype(bf16)
        bias_row = pl.ds(expert, 1)
        hg = jnp.dot(x, w1g_scr[...], preferred_element_type=f32) + b1g_ref[bias_row, :]
        xl1 = jnp.clip(jnp.dot(x, w1l_scr[...], preferred_element_type=f32)
                       + (b1l_ref[bias_row, :] + 1.0), 1.0 - SWIGLU_LIMIT, 1.0 + SWIGLU_LIMIT)
        xg = jnp.minimum(hg, SWIGLU_LIMIT)
        act = (0.5 * xg) * (jnp.tanh((0.5 * SWIGLU_ALPHA) * xg) + 1.0) * xl1
        y = jnp.dot(act.astype(bf16), w2_scr[...], preferred_element_type=f32) + b2_ref[bias_row, :]
        y_ref[pl.ds(row0, n_rows), :] = _pack_rows(y)
        if n_rows < TM:
            y_ref[pl.ds(row0 + n_rows, TM - n_rows), :] = jnp.zeros((TM - n_rows, HALF), jnp.int32)

    for blocks in range(TM // ROW_STEP, 0, -1):
        @pl.when((valid > (blocks - 1) * ROW_STEP) & (valid <= blocks * ROW_STEP))
        def _(blocks=blocks):
            expert_rows(blocks * ROW_STEP)


def _experts(xs, tile_expert, tile_valid, tile_next, w1, w2, b1g, b1l, b2, perm):
    n_slots = xs.shape[0]
    step_rows = TILES_PER_STEP * TM
    assert n_slots % step_rows == 0
    d_ff = w2.shape[1]
    whole = lambda i, te, tv, tn: (0, 0)
    grid_spec = pltpu.PrefetchScalarGridSpec(
        num_scalar_prefetch=3,
        grid=(n_slots // step_rows,),
        in_specs=[
            pl.BlockSpec((step_rows, HALF), lambda i, te, tv, tn: (i, 0)),
            pl.BlockSpec(memory_space=pl.ANY),
            pl.BlockSpec(memory_space=pl.ANY),
            pl.BlockSpec((N_EXPERTS, d_ff), whole),
            pl.BlockSpec((N_EXPERTS, d_ff), whole),
            pl.BlockSpec((N_EXPERTS, D_MODEL), whole),
            pl.BlockSpec((PERM_BLOCK, PERM_BLOCK), whole),
        ],
        out_specs=pl.BlockSpec((step_rows, HALF), lambda i, te, tv, tn: (i, 0)),
        scratch_shapes=[
            pltpu.VMEM((D_MODEL, 2 * d_ff), jnp.float32),
            pltpu.VMEM((d_ff, D_MODEL), jnp.float32),
            pltpu.VMEM((D_MODEL, d_ff), jnp.bfloat16),
            pltpu.VMEM((D_MODEL, d_ff), jnp.bfloat16),
            pltpu.VMEM((d_ff, D_MODEL), jnp.bfloat16),
            pltpu.SemaphoreType.DMA((2,)),
        ],
    )
    return pl.pallas_call(
        _expert_kernel,
        grid_spec=grid_spec,
        out_shape=jax.ShapeDtypeStruct((n_slots, HALF), jnp.int32),
        compiler_params=pltpu.CompilerParams(
            dimension_semantics=("arbitrary",),
            vmem_limit_bytes=VMEM_LIMIT),
        name="experts",
    )(tile_expert, tile_valid, tile_next, xs, w1, w2, b1g, b1l, b2, perm)


def _combine_kernel(h_ref, yk_ref, gate_ref, g_ref, b_ref, *rest):
    o_ref = rest[-1]
    gates = gate_ref[...].T
    acc_a = None
    acc_b = None
    for k in range(TOP_K):
        a, b = _unpack_rows(yk_ref[k])
        gk = gates[:, k:k + 1]
        acc_a = gk * a if acc_a is None else acc_a + gk * a
        acc_b = gk * b if acc_b is None else acc_b + gk * b
    ffn = jnp.concatenate([acc_a, acc_b], axis=1)
    o_ref[...] = _layer_norm(ALPHA * h_ref[...] + ffn, g_ref[...], b_ref[...])


def _combine(h1f, yk, gates, g2, b2, out_prev, row0, t_all):
    t_part = h1f.shape[0]
    blk0 = row0 // TC3
    row = lambda i: (i, 0)
    const = lambda i: (0, 0)
    in_specs = [
        pl.BlockSpec((TC3, D_MODEL), row),
        pl.BlockSpec((TOP_K, TC3, HALF), lambda i: (0, i, 0)),
        pl.BlockSpec((TOP_K, TC3), lambda i: (0, i)),
        pl.BlockSpec((1, D_MODEL), const),
        pl.BlockSpec((1, D_MODEL), const),
    ]
    args = [h1f, yk, gates, g2, b2]
    aliases = {}
    if out_prev is not None:
        in_specs.append(pl.BlockSpec(memory_space=pl.ANY))
        args.append(out_prev)
        aliases = {len(args) - 1: 0}
    return pl.pallas_call(
        _combine_kernel,
        grid=(t_part // TC3,),
        in_specs=in_specs,
        out_specs=pl.BlockSpec((TC3, D_MODEL), lambda i: (i + blk0, 0)),
        out_shape=jax.ShapeDtypeStruct((t_all, D_MODEL), jnp.float32),
        input_output_aliases=aliases,
        compiler_params=pltpu.CompilerParams(
            dimension_semantics=("arbitrary",),
            vmem_limit_bytes=VMEM_LIMIT),
        name="combine",
    )(*args)


def kernel(x, w_in, lb_logits, hgrn_norm_g, gmlp_ln_g, gmlp_ln_b, gmlp_ws, gmlp_bs, w_out, ln1_g, ln1_b, router_w, router_b, exp_w1, exp_b1, exp_w2, exp_b2, ln2_g, ln2_b):
    batch, seq, d = x.shape
    assert d == D_MODEL and seq % TB == 0 and w_in.shape[0] == 1
    t_total = batch * seq
    f32 = jnp.float32
    bf16 = jnp.bfloat16

    lb = jnp.cumsum(jax.nn.softmax(lb_logits.astype(f32), axis=0), axis=0)[0:1]
    chunk_id = jnp.arange(GMLP_BLOCK) // CHUNK
    wm = jnp.where((chunk_id[None, :] <= chunk_id[:, None])[None], gmlp_ws[0], 0.0).astype(bf16)
    rwh = router_w[0].T.astype(bf16)
    tri = (jnp.arange(CHUNK)[None, :] <= jnp.arange(CHUNK)[:, None]).astype(bf16)
    upp = (jnp.arange(TB)[:, None] < jnp.arange(TB)[None, :]).astype(bf16)

    lane = jnp.arange(PERM_BLOCK)
    src = jnp.where(lane < PERM_BLOCK // 2, 2 * lane, 2 * (lane - PERM_BLOCK // 2) + 1)
    perm = (jnp.arange(PERM_BLOCK)[:, None] == src[None, :]).astype(bf16)
    xt = x.reshape(t_total, d)
    win = w_in[0].astype(bf16)
    wout = w_out[0].astype(bf16)
    b1g, b1l, b2e = exp_b1[0][:, 0::2], exp_b1[0][:, 1::2], exp_b2[0]

    last = max(batch * LAST_PART_NUM // LAST_PART_DEN, 1) if batch > 1 else 0
    part_sizes = [pb for pb in (batch - last, last) if pb > 0]
    out = None
    b0 = 0
    for pb in part_sizes:
        t_part = pb * seq
        step_rows = TILES_PER_STEP * TM
        n_slots = -(-(t_part * TOP_K + N_EXPERTS * TM) // step_rows) * step_rows
        n_tiles = n_slots // TM
        h1f, h1p, idx, gates, rank, cnt = _mixer(
            xt, win, lb, hgrn_norm_g[0:1], gmlp_ln_g[0:1], gmlp_ln_b[0:1], wm, gmlp_bs[0].T, wout,
            ln1_g[0:1], ln1_b[0:1], rwh, router_b[0][:, None], tri, upp, seq, pb, b0)

        dest, tile_expert, tile_valid, tile_next = _plan(cnt, idx, rank, n_tiles)
        xs = _sc_dispatch(h1p, dest, n_slots)
        y = _experts(xs, tile_expert, tile_valid, tile_next,
                     exp_w1[0], exp_w2[0], b1g, b1l, b2e, perm)
        yk = _sc_gather(y, dest).reshape(TOP_K, t_part, HALF)
        out = _combine(h1f, yk, gates, ln2_g[0:1], ln2_b[0:1], out, b0 * seq, t_total)
        b0 += pb
    return out.reshape(batch, seq, d)
```

```python
import functools
import math

import jax
import jax.numpy as jnp
from jax import lax
from jax.experimental import pallas as pl
from jax.experimental.pallas import tpu as pltpu
from jax.experimental.pallas import tpu_sc as plsc

D_MODEL = 1024
CHUNK = 64
SUB = 32
N_SUB = CHUNK // SUB
MAX_SUB_DECAY = 86.0
HGRN_WIDTH = 512
HGRN_HEADS = 4
HEAD_DIM = 128
GMLP_WIDTH = 512
GMLP_BLOCK = 128
GMLP_GROUPS = 4
IN_COLS = 3072
N_EXPERTS = 32
TOP_K = 4
SWIGLU_LIMIT = 7.0
SWIGLU_ALPHA = 1.702
ALPHA = 2.0 ** 0.25
EPS = 1e-5
HALF = D_MODEL // 2

TB = 512
TM = 1024
ROW_STEP = 128
TILES_PER_STEP = 3
PLAN_CHUNK = 2048
TC3 = 1024
LAST_PART_NUM, LAST_PART_DEN = 1, 4
SC_WINDOW = 128
PROJ_GROUP = 256
PERM_BLOCK = 256
LANES = 128
V7X_VMEM_BYTES = 64 * 1024 * 1024
VMEM_LIMIT = V7X_VMEM_BYTES * 7 // 8

_NT = (((1,), (1,)), ((), ()))


def _gelu(x):
    return 0.5 * x * (1.0 + lax.erf(x * (1.0 / math.sqrt(2.0))))


def _layer_norm(x, g, b):
    mu = jnp.mean(x, axis=-1, keepdims=True)
    xc = x - mu
    var = jnp.mean(xc * xc, axis=-1, keepdims=True)
    return xc * lax.rsqrt(var + EPS) * g + b


def _pack_rounded(hr):
    au = lax.bitcast_convert_type(hr[:, :HALF], jnp.uint32) >> 16
    bu = lax.bitcast_convert_type(hr[:, HALF:], jnp.uint32) & jnp.uint32(0xFFFF0000)
    return lax.bitcast_convert_type(au | bu, jnp.int32)


def _pack_rows(h):
    return _pack_rounded(h.astype(jnp.bfloat16).astype(jnp.float32))


def _unpack_rows(w):
    u = lax.bitcast_convert_type(w, jnp.uint32)
    a = lax.bitcast_convert_type(u << 16, jnp.float32)
    b = lax.bitcast_convert_type(u & jnp.uint32(0xFFFF0000), jnp.float32)
    return a, b


def _mixer_kernel(x_ref, xn_ref, win_ref, lb_ref, hg_ref, lng_ref, lnb_ref, wm_ref, bst_ref,
                  wout_ref, l1g_ref, l1b_ref, rwh_ref, rb_ref, tri_ref, upp_ref,
                  h1f_ref, h1p_ref, idx_ref, gate_ref, rank_ref, cnt_ref,
                  p_scr, lf_scr, kk_scr, g_scr, kc_scr, o_scr, st_scr, carry_scr,
                  amat_scr, qg_scr, kd_scr, upd_scr, mix_scr, decay_scr):
    b = pl.program_id(0)
    t = pl.program_id(1)
    step = b * pl.num_programs(1) + t
    f32 = jnp.float32
    bf16 = jnp.bfloat16

    @pl.when(t == 0)
    def _():
        st_scr[...] = jnp.zeros_like(st_scr)

    @pl.when((b == 0) & (t == 0))
    def _():
        carry_scr[...] = jnp.zeros_like(carry_scr)

    def project(xb, group):
        cols = slice(group * PROJ_GROUP, (group + 1) * PROJ_GROUP)
        p_scr[:, cols] = jnp.dot(xb, win_ref[:, cols], preferred_element_type=f32)

    def hgrn_gates():
        fl = p_scr[:, HGRN_WIDTH:2 * HGRN_WIDTH]
        z = jnp.exp(-jnp.abs(fl))
        r = 1.0 / (1.0 + z)
        k = (1.0 - lb_ref[...]) * jnp.where(fl >= 0, z * r, r)
        kk_scr[...] = k
        lf = jnp.log2(1.0 - k)
        lf_scr[...] = lf
        decay_scr[0] = jnp.max(-jnp.sum(lf.reshape(TB // SUB, SUB, HGRN_WIDTH), axis=1))

    @pl.when(step == 0)
    def _():
        xb0 = x_ref[...].astype(bf16)
        for group in range(IN_COLS // PROJ_GROUP):
            project(xb0, group)
        hgrn_gates()

    tri = tri_ref[...]
    row64 = lax.broadcasted_iota(jnp.int32, (CHUNK, CHUNK), 0)
    col64 = lax.broadcasted_iota(jnp.int32, (CHUNK, CHUNK), 1)
    lane_sub = lax.broadcasted_iota(jnp.int32, (SUB, CHUNK), 1)
    scale = HEAD_DIM ** -0.5

    def pad_rows(piece, lo_r):
        parts = []
        if lo_r > 0:
            parts.append(jnp.zeros((lo_r, HEAD_DIM), f32))
        parts.append(piece)
        rest = CHUNK - lo_r - piece.shape[0]
        if rest > 0:
            parts.append(jnp.zeros((rest, HEAD_DIM), f32))
        return jnp.concatenate(parts, axis=0) if len(parts) > 1 else piece

    def intra_factorised(q, k, gh):
        qts, kts = [], []
        for a in range(N_SUB):
            lo_r, hi_r = a * SUB, (a + 1) * SUB
            if a == 0:
                qa = q[:hi_r] * jnp.exp2(gh[:hi_r])
                ka = k[:hi_r] * jnp.exp2(-gh[:hi_r])
            else:
                ra = gh[lo_r - 1:lo_r]
                qa = q[lo_r:hi_r] * jnp.exp2(gh[lo_r:hi_r] - ra)
                ka = k[:hi_r] * jnp.exp2(ra - gh[:hi_r])
            qts.append(pad_rows(qa, lo_r))
            kts.append(pad_rows(ka, 0))
        a_mat = lax.dot_general(jnp.concatenate(qts, axis=1).astype(bf16),
                                jnp.concatenate(kts, axis=1).astype(bf16), _NT,
                                preferred_element_type=f32)
        return jnp.where(row64 >= col64, a_mat, 0.0)

    def chunks_factorised():
        n_chunks = TB // CHUNK
        heads = [(c, h) for c in range(n_chunks) for h in range(HGRN_HEADS)]

        def cols(h, base=0):
            return slice(base + h * HEAD_DIM, base + (h + 1) * HEAD_DIM)

        for c in range(n_chunks):
            rows = slice(c * CHUNK, (c + 1) * CHUNK)
            lf = lf_scr[rows, :]
            hi = lf.astype(bf16)
            lo = (lf - hi.astype(f32)).astype(bf16)
            gg = jnp.dot(tri, jnp.concatenate([hi, lo], axis=1), preferred_element_type=f32)
            lf_scr[rows, :] = gg[:, :HGRN_WIDTH] + gg[:, HGRN_WIDTH:]
        for c, h in heads:
            rows = slice(c * CHUNK, (c + 1) * CHUNK)
            q = p_scr[rows, cols(h)]
            k = kk_scr[rows, cols(h)]
            gh = lf_scr[rows, cols(h)]
            amat_scr[c * HGRN_HEADS + h] = intra_factorised(q, k, gh).astype(bf16)
            qg_scr[rows, cols(h)] = (q * jnp.exp2(gh)).astype(bf16)
            kd_scr[rows, cols(h)] = (k * jnp.exp2(gh[CHUNK - 1:CHUNK] - gh)).astype(bf16)
        for c, h in heads:
            rows = slice(c * CHUNK, (c + 1) * CHUNK)
            v = p_scr[rows, cols(h, 2 * HGRN_WIDTH)]
            o_scr[rows, cols(h)] = jnp.dot(amat_scr[c * HGRN_HEADS + h], v.astype(bf16),
                                           preferred_element_type=f32)
            upd_scr[c * HGRN_HEADS + h] = jnp.dot(v.T.astype(bf16), kd_scr[rows, cols(h)],
                                                  preferred_element_type=f32)
        states = [st_scr[h] for h in range(HGRN_HEADS)]
        for c, h in heads:
            rows = slice(c * CHUNK, (c + 1) * CHUNK)
            st = states[h]
            o_inter = lax.dot_general(qg_scr[rows, cols(h)], st.astype(bf16), _NT,
                                      preferred_element_type=f32)
            o_scr[rows, cols(h)] = (o_scr[rows, cols(h)] + o_inter) * scale
            gl = lf_scr[(c + 1) * CHUNK - 1:(c + 1) * CHUNK, cols(h)]
            states[h] = st * jnp.exp2(gl) + upd_scr[c * HGRN_HEADS + h]
        for h in range(HGRN_HEADS):
            st_scr[h] = states[h]

    def intra_exact_diagonal(q, k, gh, cs):
        qts, kts = [], []
        for a in range(1, N_SUB):
            lo_r, hi_r = a * SUB, (a + 1) * SUB
            ra = g_scr[lo_r - 1:lo_r, cs]
            qts.append(pad_rows(q[lo_r:hi_r] * jnp.exp2(gh[lo_r:hi_r] - ra), lo_r))
            kts.append(pad_rows(k[:lo_r] * jnp.exp2(ra - gh[:lo_r]), 0))
        a_off = lax.dot_general(jnp.concatenate(qts, axis=1).astype(bf16),
                                jnp.concatenate(kts, axis=1).astype(bf16), _NT,
                                preferred_element_type=f32)
        diag_rows = []
        for a in range(N_SUB):
            lo_r = a * SUB
            gs = gh[lo_r:lo_r + SUB]
            qs = q[lo_r:lo_r + SUB]
            blk = jnp.zeros((SUB, CHUNK), f32)
            for jl in range(SUB):
                j = lo_r + jl
                gj = g_scr[j:j + 1, cs]
                kj = kc_scr[j:j + 1, cs]
                e = jnp.exp2(jnp.minimum(gs - gj, 0.0))
                col = jnp.sum(qs * (kj * e), axis=-1, keepdims=True)
                blk = jnp.where(lane_sub == j, col, blk)
            diag_rows.append(blk)
        a_diag = jnp.concatenate(diag_rows, axis=0)
        return a_off + jnp.where(row64 >= col64, a_diag, 0.0)

    def make_chunk_body(intra):
        def chunk_body(c, carry):
            r0 = pl.multiple_of(c * CHUNK, CHUNK)
            rows = pl.ds(r0, CHUNK)
            lf = lf_scr[rows, :]
            hi = lf.astype(bf16)
            lo = (lf - hi.astype(f32)).astype(bf16)
            gg = jnp.dot(tri, jnp.concatenate([hi, lo], axis=1), preferred_element_type=f32)
            g_all = gg[:, :HGRN_WIDTH] + gg[:, HGRN_WIDTH:]
            g_scr[...] = g_all
            kc_scr[...] = kk_scr[rows, :]
            for h in range(HGRN_HEADS):
                cs = slice(h * HEAD_DIM, (h + 1) * HEAD_DIM)
                q = p_scr[rows, h * HEAD_DIM:(h + 1) * HEAD_DIM]
                v = p_scr[rows, 2 * HGRN_WIDTH + h * HEAD_DIM:2 * HGRN_WIDTH + (h + 1) * HEAD_DIM]
                k = kc_scr[:, cs]
                gh = g_all[:, cs]
                st = st_scr[h]
                o_inter = lax.dot_general((q * jnp.exp2(gh)).astype(bf16), st.astype(bf16), _NT,
                                          preferred_element_type=f32)
                a_mat = intra(q, k, gh, cs)
                o = jnp.dot(a_mat.astype(bf16), v.astype(bf16), preferred_element_type=f32) + o_inter
                o_scr[rows, h * HEAD_DIM:(h + 1) * HEAD_DIM] = o * scale

                gl = g_scr[CHUNK - 1:CHUNK, cs]
                kd = k * jnp.exp2(gl - gh)
                upd = jnp.dot(v.T.astype(bf16), kd.astype(bf16), preferred_element_type=f32)
                st_scr[h] = st * jnp.exp2(gl) + upd
            return carry
        return chunk_body

    bounded = decay_scr[0] <= MAX_SUB_DECAY

    f_groups = list(range(HGRN_WIDTH // PROJ_GROUP, 2 * HGRN_WIDTH // PROJ_GROUP))
    qi_groups = [g for g in range(3 * HGRN_WIDTH // PROJ_GROUP) if g not in f_groups]
    rest_groups = list(range(3 * HGRN_WIDTH // PROJ_GROUP, IN_COLS // PROJ_GROUP))

    def gmlp_stage(xnb):
        for g in f_groups:
            project(xnb, g)
        for w in range(TB // GMLP_BLOCK):
            rows = slice(w * GMLP_BLOCK, (w + 1) * GMLP_BLOCK)
            u = _gelu(p_scr[rows, 4 * HGRN_WIDTH:4 * HGRN_WIDTH + GMLP_WIDTH])
            vn = _layer_norm(_gelu(p_scr[rows, 4 * HGRN_WIDTH + GMLP_WIDTH:]),
                             lng_ref[...], lnb_ref[...])
            vnb = vn.astype(bf16)
            cols = []
            for g in range(GMLP_GROUPS):
                s = jnp.dot(wm_ref[g], vnb[:, g * HEAD_DIM:(g + 1) * HEAD_DIM],
                            preferred_element_type=f32) + bst_ref[:, g:g + 1]
                cols.append(s)
            mix_scr[rows, HGRN_WIDTH:] = (u * jnp.concatenate(cols, axis=1)).astype(bf16)

    def output_stage(xnb):
        for g in qi_groups:
            project(xnb, g)
        for w in range(TB // GMLP_BLOCK):
            rows = slice(w * GMLP_BLOCK, (w + 1) * GMLP_BLOCK)
            o = o_scr[rows, :]
            ms = jnp.mean(o * o, axis=-1, keepdims=True)
            gate = p_scr[rows, 3 * HGRN_WIDTH:4 * HGRN_WIDTH]
            half_gate = 0.5 * gate
            silu = half_gate * (jnp.tanh(half_gate) + 1.0)
            y_rec = o * lax.rsqrt(ms + EPS) * hg_ref[...] * silu
            mix_scr[rows, :HGRN_WIDTH] = y_rec.astype(bf16)

        mix = jnp.dot(mix_scr[...], wout_ref[...], preferred_element_type=f32)
        hgrn_gates()
        h1 = _layer_norm(ALPHA * x_ref[...] + mix, l1g_ref[...], l1b_ref[...])
        h1f_ref[...] = h1
        hh = h1.astype(bf16)
        h1_rounded = hh.astype(f32)
        h1p_ref[...] = _pack_rounded(h1_rounded)
        for g in rest_groups[:2]:
            project(xnb, g)

        logits = lax.dot_general(rwh_ref[...], hh, _NT, preferred_element_type=f32) + rb_ref[...]
        for g in rest_groups[2:]:
            project(xnb, g)

        e_iota = lax.broadcasted_iota(jnp.int32, (N_EXPERTS, TB), 0)
        work = logits
        vals, idxs, hots = [], [], []
        for _ in range(TOP_K):
            m = jnp.max(work, axis=0, keepdims=True)
            ik = jnp.min(jnp.where(work == m, e_iota, N_EXPERTS), axis=0, keepdims=True)
            hot = e_iota == ik
            work = jnp.where(hot, -jnp.inf, work)
            vals.append(m)
            idxs.append(ik)
            hots.append(hot)
        exps = [jnp.exp(vk - vals[0]) for vk in vals]
        denom = exps[0] + exps[1] + exps[2] + exps[3]
        gate_ref[...] = jnp.concatenate([ek / denom for ek in exps], axis=0)
        idx_ref[...] = jnp.concatenate(idxs, axis=0)

        hot_any = jnp.where(hots[0] | hots[1] | hots[2] | hots[3], 1.0, 0.0)
        prefix = jnp.dot(hot_any.astype(bf16), upp_ref[...], preferred_element_type=f32)
        base = prefix + carry_scr[:, 0:1]
        ranks = [jnp.sum(jnp.where(hk, base, 0.0), axis=0, keepdims=True) for hk in hots]
        rank_ref[...] = jnp.concatenate(ranks, axis=0).astype(jnp.int32)
        new_carry = carry_scr[...] + jnp.sum(hot_any, axis=1, keepdims=True)
        carry_scr[...] = new_carry
        cnt_ref[...] = new_carry

    @pl.when(bounded)
    def _():
        xnb = xn_ref[...].astype(bf16)
        gmlp_stage(xnb)
        chunks_factorised()
        output_stage(xnb)

    @pl.when(jnp.logical_not(bounded))
    def _():
        xnb = xn_ref[...].astype(bf16)
        lax.fori_loop(0, TB // CHUNK, make_chunk_body(intra_exact_diagonal), 0)
        gmlp_stage(xnb)
        output_stage(xnb)


def _mixer(xt, win, lb, hg, lng, lnb, wm, bst, wout, l1g, l1b, rwh, rb, tri, upp,
           seq, batch, batch0):
    nt = seq // TB
    t_total = batch * seq
    nblk = batch * nt
    const2 = lambda b, t: (0, 0)
    const3 = lambda b, t: (0, 0, 0)
    row_blk = lambda b, t: (b * nt + t, 0)
    meta_blk = lambda b, t: (0, b * nt + t)
    once = dict(pipeline_mode=pl.Buffered(1))
    in_specs = [
        pl.BlockSpec((TB, D_MODEL), lambda b, t: ((b + batch0) * nt + t, 0)),
        pl.BlockSpec((TB, D_MODEL),
                     lambda b, t: (batch0 * nt + jnp.minimum(b * nt + t + 1, nblk - 1), 0)),
        pl.BlockSpec((D_MODEL, IN_COLS), const2, **once),
        pl.BlockSpec((1, HGRN_WIDTH), const2),
        pl.BlockSpec((1, HGRN_WIDTH), const2),
        pl.BlockSpec((1, GMLP_WIDTH), const2),
        pl.BlockSpec((1, GMLP_WIDTH), const2),
        pl.BlockSpec((GMLP_GROUPS, GMLP_BLOCK, GMLP_BLOCK), const3),
        pl.BlockSpec((GMLP_BLOCK, GMLP_GROUPS), const2),
        pl.BlockSpec((D_MODEL, D_MODEL), const2, **once),
        pl.BlockSpec((1, D_MODEL), const2),
        pl.BlockSpec((1, D_MODEL), const2),
        pl.BlockSpec((N_EXPERTS, D_MODEL), const2),
        pl.BlockSpec((N_EXPERTS, 1), const2),
        pl.BlockSpec((CHUNK, CHUNK), const2),
        pl.BlockSpec((TB, TB), const2, **once),
    ]
    out_shape = [
        jax.ShapeDtypeStruct((t_total, D_MODEL), jnp.float32),
        jax.ShapeDtypeStruct((t_total, HALF), jnp.int32),
        jax.ShapeDtypeStruct((TOP_K, t_total), jnp.int32),
        jax.ShapeDtypeStruct((TOP_K, t_total), jnp.float32),
        jax.ShapeDtypeStruct((TOP_K, t_total), jnp.int32),
        jax.ShapeDtypeStruct((N_EXPERTS, LANES), jnp.float32),
    ]
    out_specs = [
        pl.BlockSpec((TB, D_MODEL), row_blk),
        pl.BlockSpec((TB, HALF), row_blk),
        pl.BlockSpec((TOP_K, TB), meta_blk),
        pl.BlockSpec((TOP_K, TB), meta_blk),
        pl.BlockSpec((TOP_K, TB), meta_blk),
        pl.BlockSpec((N_EXPERTS, LANES), const2),
    ]
    scratch = [
        pltpu.VMEM((TB, IN_COLS), jnp.float32),
        pltpu.VMEM((TB, HGRN_WIDTH), jnp.float32),
        pltpu.VMEM((TB, HGRN_WIDTH), jnp.float32),
        pltpu.VMEM((CHUNK, HGRN_WIDTH), jnp.float32),
        pltpu.VMEM((CHUNK, HGRN_WIDTH), jnp.float32),
        pltpu.VMEM((TB, HGRN_WIDTH), jnp.float32),
        pltpu.VMEM((HGRN_HEADS, HEAD_DIM, HEAD_DIM), jnp.float32),
        pltpu.VMEM((N_EXPERTS, LANES), jnp.float32),
        pltpu.VMEM((TB // CHUNK * HGRN_HEADS, CHUNK, CHUNK), jnp.bfloat16),
        pltpu.VMEM((TB, HGRN_WIDTH), jnp.bfloat16),
        pltpu.VMEM((TB, HGRN_WIDTH), jnp.bfloat16),
        pltpu.VMEM((TB // CHUNK * HGRN_HEADS, HEAD_DIM, HEAD_DIM), jnp.float32),
        pltpu.VMEM((TB, D_MODEL), jnp.bfloat16),
        pltpu.SMEM((1,), jnp.float32),
    ]
    return pl.pallas_call(
        _mixer_kernel,
        grid=(batch, nt),
        in_specs=in_specs,
        out_specs=out_specs,
        out_shape=out_shape,
        scratch_shapes=scratch,
        compiler_params=pltpu.CompilerParams(
            dimension_semantics=("arbitrary", "arbitrary"),
            vmem_limit_bytes=VMEM_LIMIT),
        name="mixer",
    )(xt, xt, win, lb, hg, lng, lnb, wm, bst, wout, l1g, l1b, rwh, rb, tri, upp)


def _plan_kernel(cnt_ref, idx_ref, rank_ref, dest_ref, tiles_ref):
    f32 = jnp.float32
    n_e = N_EXPERTS
    e_sub = lax.broadcasted_iota(jnp.int32, (n_e, n_e), 0)
    e_lane = lax.broadcasted_iota(jnp.int32, (n_e, n_e), 1)
    counts = cnt_ref[:, 0:1]
    padded = jnp.floor((counts + (TM - 1)) * (1.0 / TM)) * TM
    as_row = lambda col: jnp.sum(jnp.where(e_sub == e_lane, col, 0.0), axis=0, keepdims=True)
    padded_row = as_row(padded)
    counts_row = as_row(counts)
    ends = jnp.sum(jnp.where(e_lane <= e_sub, padded_row, 0.0), axis=1, keepdims=True)
    starts = ends - padded
    owns_later = (e_lane > e_sub) & (counts_row > 0.0)
    nxt = jnp.min(jnp.where(owns_later, e_lane, n_e), axis=1, keepdims=True)
    own = lax.broadcasted_iota(jnp.int32, (n_e, 1), 0)
    nxt = jnp.where(nxt == n_e, own, nxt).astype(f32)

    n_lanes = tiles_ref.shape[1]
    tile_row = lax.broadcasted_iota(jnp.int32, (n_e, n_lanes), 1).astype(f32) * TM
    e_of = lax.broadcasted_iota(jnp.int32, (n_e, n_lanes), 0)
    tile_e = jnp.minimum(jnp.sum(jnp.where(tile_row >= ends, 1, 0), axis=0, keepdims=True), n_e - 1)
    mine = e_of == tile_e
    pick = lambda col: jnp.sum(jnp.where(mine, col, 0.0), axis=0, keepdims=True)
    valid = jnp.clip(pick(starts + counts) - tile_row[0:1], 0.0, float(TM))
    rows = [tile_e, valid.astype(jnp.int32), pick(nxt).astype(jnp.int32)]
    rows.append(jnp.zeros((tiles_ref.shape[0] - len(rows), n_lanes), jnp.int32))
    tiles_ref[...] = jnp.concatenate(rows, axis=0)

    chunk = PLAN_CHUNK
    e_chunk = lax.broadcasted_iota(jnp.int32, (n_e, chunk), 0)

    def body(c, carry):
        lanes = pl.ds(pl.multiple_of(c * chunk, chunk), chunk)
        for k in range(TOP_K):
            hit = e_chunk == idx_ref[k:k + 1, lanes]
            start_of = jnp.sum(jnp.where(hit, starts, 0.0), axis=0, keepdims=True)
            dest_ref[k:k + 1, lanes] = rank_ref[k:k + 1, lanes] + start_of.astype(jnp.int32)
        return carry

    lax.fori_loop(0, idx_ref.shape[1] // chunk, body, 0)


def _plan(cnt, idx, rank, n_tiles):
    t_part = idx.shape[1]
    n_lanes = -(-n_tiles // LANES) * LANES
    full = lambda shape: pl.BlockSpec(shape, lambda i: (0,) * len(shape))
    dest, tiles = pl.pallas_call(
        _plan_kernel,
        grid=(1,),
        in_specs=[full(cnt.shape), full(idx.shape), full(rank.shape)],
        out_specs=[full(idx.shape), full((8, n_lanes))],
        out_shape=[jax.ShapeDtypeStruct((TOP_K, t_part), jnp.int32),
                   jax.ShapeDtypeStruct((8, n_lanes), jnp.int32)],
        compiler_params=pltpu.CompilerParams(
            dimension_semantics=("arbitrary",), vmem_limit_bytes=VMEM_LIMIT),
        name="plan",
    )(cnt, idx, rank)
    return dest, tiles[0, :n_tiles], tiles[1, :n_tiles], tiles[2, :n_tiles]


def _sc_workers():
    info = plsc.get_sparse_core_info()
    return info.num_cores, info.num_cores * info.num_subcores


def _sc_dispatch(rows, dest, n_out):
    t_total, dw = rows.shape
    nc, nw = _sc_workers()
    per_w = t_total // nw
    mesh = plsc.VectorSubcoreMesh(core_axis_name="c", subcore_axis_name="s")

    @functools.partial(
        pl.kernel,
        out_type=jax.ShapeDtypeStruct((n_out, dw), rows.dtype),
        mesh=mesh,
        scratch_types=[pltpu.VMEM((SC_WINDOW,), jnp.int32) for _ in range(TOP_K)]
        + [pltpu.VMEM((SC_WINDOW, dw), rows.dtype), pltpu.SemaphoreType.DMA],
        name="sc_dispatch",
    )
    def k(x_hbm, i_hbm, o_hbm, i0, i1, i2, i3, rows_v, sem):
        wid = lax.axis_index("s") * nc + lax.axis_index("c")
        idx_bufs = (i0, i1, i2, i3)

        @pl.loop(0, per_w // SC_WINDOW)
        def _(j):
            base = wid * per_w + j * SC_WINDOW
            for kk in range(TOP_K):
                pltpu.sync_copy(i_hbm.at[kk, pl.ds(base, SC_WINDOW)], idx_bufs[kk])
            pltpu.sync_copy(x_hbm.at[pl.ds(base, SC_WINDOW)], rows_v)
            copies = [pltpu.async_copy(rows_v, o_hbm.at[idx_bufs[kk]], sem) for kk in range(TOP_K)]
            for cp in copies:
                cp.wait()

    return k(rows, dest)


def _sc_gather(table, idx):
    n_k, t_total = idx.shape
    n = n_k * t_total
    dw = table.shape[1]
    nc, nw = _sc_workers()
    per_w = n // nw
    w_per_k = nw // n_k
    mesh = plsc.VectorSubcoreMesh(core_axis_name="c", subcore_axis_name="s")

    @functools.partial(
        pl.kernel,
        out_type=jax.ShapeDtypeStruct((n, dw), table.dtype),
        mesh=mesh,
        scratch_types=[pltpu.VMEM((SC_WINDOW,), jnp.int32),
                       pltpu.VMEM((SC_WINDOW, dw), table.dtype),
                       pltpu.SemaphoreType.DMA],
        name="sc_gather",
    )
    def k(t_hbm, i_hbm, o_hbm, idx_v, rows_v, sem):
        wid = lax.axis_index("s") * nc + lax.axis_index("c")

        row = wid // w_per_k
        col0 = (wid % w_per_k) * per_w

        @pl.loop(0, per_w // SC_WINDOW)
        def _(j):
            col = col0 + j * SC_WINDOW
            pltpu.sync_copy(i_hbm.at[row, pl.ds(col, SC_WINDOW)], idx_v)
            pltpu.async_copy(t_hbm.at[idx_v], rows_v, sem).wait()
            pltpu.sync_copy(rows_v, o_hbm.at[pl.ds(row * t_total + col, SC_WINDOW)])

    return k(table, idx)


def _expert_kernel(te_ref, tv_ref, tn_ref, xs_ref, w1_hbm, w2_hbm, b1g_ref, b1l_ref, b2_ref,
                   perm_ref, y_ref, w1f_scr, w2f_scr, w1g_scr, w1l_scr, w2_scr, sems):
    def tile_body(h, carry):
        _expert_tile(pl.program_id(0) * TILES_PER_STEP + h, pl.multiple_of(h * TM, TM),
                     te_ref, tv_ref, tn_ref, xs_ref, w1_hbm, w2_hbm, b1g_ref, b1l_ref, b2_ref,
                     perm_ref, y_ref, w1f_scr, w2f_scr, w1g_scr, w1l_scr, w2_scr, sems)
        return carry

    lax.fori_loop(0, TILES_PER_STEP, tile_body, 0)


def _expert_tile(i, row0, te_ref, tv_ref, tn_ref, xs_ref, w1_hbm, w2_hbm, b1g_ref, b1l_ref, b2_ref,
                 perm_ref, y_ref, w1f_scr, w2f_scr, w1g_scr, w1l_scr, w2_scr, sems):
    valid = tv_ref[i]
    expert = te_ref[i]
    f32 = jnp.float32
    bf16 = jnp.bfloat16
    expert_changed = (i == 0) | (expert != te_ref[jnp.maximum(i - 1, 0)])

    def weight_copies(e):
        return (pltpu.make_async_copy(w1_hbm.at[e], w1f_scr, sems.at[0]),
                pltpu.make_async_copy(w2_hbm.at[e], w2f_scr, sems.at[1]))

    @pl.when(valid == 0)
    def _():
        y_ref[pl.ds(row0, TM), :] = jnp.zeros((TM, HALF), jnp.int32)

    @pl.when(i == 0)
    def _():
        for cp in weight_copies(expert):
            cp.start()

    @pl.when((valid > 0) & expert_changed)
    def _():
        for cp in weight_copies(expert):
            cp.wait()
        w2_scr[...] = w2f_scr[...].astype(bf16)
        perm = perm_ref[...]
        for c in range(2 * D_MODEL // PERM_BLOCK):
            blk = w1f_scr[:, c * PERM_BLOCK:(c + 1) * PERM_BLOCK].astype(bf16)
            r = jnp.dot(blk, perm, preferred_element_type=f32).astype(bf16)
            half = PERM_BLOCK // 2
            w1g_scr[:, c * half:(c + 1) * half] = r[:, :half]
            w1l_scr[:, c * half:(c + 1) * half] = r[:, half:]

        @pl.when(tn_ref[i] != expert)
        def _():
            for cp in weight_copies(tn_ref[i]):
                cp.start()

    def expert_rows(n_rows):
        keep = lax.broadcasted_iota(jnp.int32, (n_rows, 1), 0) < valid
        a, b = _unpack_rows(jnp.where(keep, xs_ref[pl.ds(row0, n_rows), :], 0))
        x = jnp.concatenate([a, b], axis=1).astype(bf16)
        bias_row = pl.ds(expert, 1)
        hg = jnp.dot(x, w1g_scr[...], preferred_element_type=f32) + b1g_ref[bias_row, :]
        xl1 = jnp.clip(jnp.dot(x, w1l_scr[...], preferred_element_type=f32)
                       + (b1l_ref[bias_row, :] + 1.0), 1.0 - SWIGLU_LIMIT, 1.0 + SWIGLU_LIMIT)
        xg = jnp.minimum(hg, SWIGLU_LIMIT)
        act = (0.5 * xg) * (jnp.tanh((0.5 * SWIGLU_ALPHA) * xg) + 1.0) * xl1
        y = jnp.dot(act.astype(bf16), w2_scr[...], preferred_element_type=f32) + b2_ref[bias_row, :]
        y_ref[pl.ds(row0, n_rows), :] = _pack_rows(y)
        if n_rows < TM:
            y_ref[pl.ds(row0 + n_rows, TM - n_rows), :] = jnp.zeros((TM - n_rows, HALF), jnp.int32)

    for blocks in range(TM // ROW_STEP, 0, -1):
        @pl.when((valid > (blocks - 1) * ROW_STEP) & (valid <= blocks * ROW_STEP))
        def _(blocks=blocks):
            expert_rows(blocks * ROW_STEP)


def _experts(xs, tile_expert, tile_valid, tile_next, w1, w2, b1g, b1l, b2, perm):
    n_slots = xs.shape[0]
    step_rows = TILES_PER_STEP * TM
    assert n_slots % step_rows == 0
    d_ff = w2.shape[1]
    whole = lambda i, te, tv, tn: (0, 0)
    grid_spec = pltpu.PrefetchScalarGridSpec(
        num_scalar_prefetch=3,
        grid=(n_slots // step_rows,),
        in_specs=[
            pl.BlockSpec((step_rows, HALF), lambda i, te, tv, tn: (i, 0)),
            pl.BlockSpec(memory_space=pl.ANY),
            pl.BlockSpec(memory_space=pl.ANY),
            pl.BlockSpec((N_EXPERTS, d_ff), whole),
            pl.BlockSpec((N_EXPERTS, d_ff), whole),
            pl.BlockSpec((N_EXPERTS, D_MODEL), whole),
            pl.BlockSpec((PERM_BLOCK, PERM_BLOCK), whole),
        ],
        out_specs=pl.BlockSpec((step_rows, HALF), lambda i, te, tv, tn: (i, 0)),
        scratch_shapes=[
            pltpu.VMEM((D_MODEL, 2 * d_ff), jnp.float32),
            pltpu.VMEM((d_ff, D_MODEL), jnp.float32),
            pltpu.VMEM((D_MODEL, d_ff), jnp.bfloat16),
            pltpu.VMEM((D_MODEL, d_ff), jnp.bfloat16),
            pltpu.VMEM((d_ff, D_MODEL), jnp.bfloat16),
            pltpu.SemaphoreType.DMA((2,)),
        ],
    )
    return pl.pallas_call(
        _expert_kernel,
        grid_spec=grid_spec,
        out_shape=jax.ShapeDtypeStruct((n_slots, HALF), jnp.int32),
        compiler_params=pltpu.CompilerParams(
            dimension_semantics=("arbitrary",),
            vmem_limit_bytes=VMEM_LIMIT),
        name="experts",
    )(tile_expert, tile_valid, tile_next, xs, w1, w2, b1g, b1l, b2, perm)


def _combine_kernel(h_ref, yk_ref, gate_ref, g_ref, b_ref, *rest):
    o_ref = rest[-1]
    gates = gate_ref[...].T
    acc_a = None
    acc_b = None
    for k in range(TOP_K):
        a, b = _unpack_rows(yk_ref[k])
        gk = gates[:, k:k + 1]
        acc_a = gk * a if acc_a is None else acc_a + gk * a
        acc_b = gk * b if acc_b is None else acc_b + gk * b
    ffn = jnp.concatenate([acc_a, acc_b], axis=1)
    o_ref[...] = _layer_norm(ALPHA * h_ref[...] + ffn, g_ref[...], b_ref[...])


def _combine(h1f, yk, gates, g2, b2, out_prev, row0, t_all):
    t_part = h1f.shape[0]
    blk0 = row0 // TC3
    row = lambda i: (i, 0)
    const = lambda i: (0, 0)
    in_specs = [
        pl.BlockSpec((TC3, D_MODEL), row),
        pl.BlockSpec((TOP_K, TC3, HALF), lambda i: (0, i, 0)),
        pl.BlockSpec((TOP_K, TC3), lambda i: (0, i)),
        pl.BlockSpec((1, D_MODEL), const),
        pl.BlockSpec((1, D_MODEL), const),
    ]
    args = [h1f, yk, gates, g2, b2]
    aliases = {}
    if out_prev is not None:
        in_specs.append(pl.BlockSpec(memory_space=pl.ANY))
        args.append(out_prev)
        aliases = {len(args) - 1: 0}
    return pl.pallas_call(
        _combine_kernel,
        grid=(t_part // TC3,),
        in_specs=in_specs,
        out_specs=pl.BlockSpec((TC3, D_MODEL), lambda i: (i + blk0, 0)),
        out_shape=jax.ShapeDtypeStruct((t_all, D_MODEL), jnp.float32),
        input_output_aliases=aliases,
        compiler_params=pltpu.CompilerParams(
            dimension_semantics=("arbitrary",),
            vmem_limit_bytes=VMEM_LIMIT),
        name="combine",
    )(*args)


def kernel(x, w_in, lb_logits, hgrn_norm_g, gmlp_ln_g, gmlp_ln_b, gmlp_ws, gmlp_bs, w_out, ln1_g, ln1_b, router_w, router_b, exp_w1, exp_b1, exp_w2, exp_b2, ln2_g, ln2_b):
    batch, seq, d = x.shape
    assert d == D_MODEL and seq % TB == 0 and w_in.shape[0] == 1
    t_total = batch * seq
    f32 = jnp.float32
    bf16 = jnp.bfloat16

    lb = jnp.cumsum(jax.nn.softmax(lb_logits.astype(f32), axis=0), axis=0)[0:1]
    chunk_id = jnp.arange(GMLP_BLOCK) // CHUNK
    wm = jnp.where((chunk_id[None, :] <= chunk_id[:, None])[None], gmlp_ws[0], 0.0).astype(bf16)
    rwh = router_w[0].T.astype(bf16)
    tri = (jnp.arange(CHUNK)[None, :] <= jnp.arange(CHUNK)[:, None]).astype(bf16)
    upp = (jnp.arange(TB)[:, None] < jnp.arange(TB)[None, :]).astype(bf16)

    lane = jnp.arange(PERM_BLOCK)
    src = jnp.where(lane < PERM_BLOCK // 2, 2 * lane, 2 * (lane - PERM_BLOCK // 2) + 1)
    perm = (jnp.arange(PERM_BLOCK)[:, None] == src[None, :]).astype(bf16)
    xt = x.reshape(t_total, d)
    win = w_in[0].astype(bf16)
    wout = w_out[0].astype(bf16)
    b1g, b1l, b2e = exp_b1[0][:, 0::2], exp_b1[0][:, 1::2], exp_b2[0]

    last = max(batch * LAST_PART_NUM // LAST_PART_DEN, 1) if batch > 1 else 0
    part_sizes = [pb for pb in (batch - last, last) if pb > 0]
    out = None
    b0 = 0
    for pb in part_sizes:
        t_part = pb * seq
        step_rows = TILES_PER_STEP * TM
        n_slots = -(-(t_part * TOP_K + N_EXPERTS * TM) // step_rows) * step_rows
        n_tiles = n_slots // TM
        h1f, h1p, idx, gates, rank, cnt = _mixer(
            xt, win, lb, hgrn_norm_g[0:1], gmlp_ln_g[0:1], gmlp_ln_b[0:1], wm, gmlp_bs[0].T, wout,
            ln1_g[0:1], ln1_b[0:1], rwh, router_b[0][:, None], tri, upp, seq, pb, b0)

        dest, tile_expert, tile_valid, tile_next = _plan(cnt, idx, rank, n_tiles)
        xs = _sc_dispatch(h1p, dest, n_slots)
        y = _experts(xs, tile_expert, tile_valid, tile_next,
                     exp_w1[0], exp_w2[0], b1g, b1l, b2e, perm)
        yk = _sc_gather(y, dest).reshape(TOP_K, t_part, HALF)
        out = _combine(h1f, yk, gates, ln2_g[0:1], ln2_b[0:1], out, b0 * seq, t_total)
        b0 += pb
    return out.reshape(batch, seq, d)
```

```python
import functools
import math

import jax
import jax.numpy as jnp
from jax import lax
from jax.experimental import pallas as pl
from jax.experimental.pallas import tpu as pltpu
from jax.experimental.pallas import tpu_sc as plsc

D_MODEL = 1024
CHUNK = 64
SUB = 32
N_SUB = CHUNK // SUB
MAX_SUB_DECAY = 86.0
HGRN_WIDTH = 512
HGRN_HEADS = 4
HEAD_DIM = 128
GMLP_WIDTH = 512
GMLP_BLOCK = 128
GMLP_GROUPS = 4
IN_COLS = 3072
N_EXPERTS = 32
TOP_K = 4
SWIGLU_LIMIT = 7.0
SWIGLU_ALPHA = 1.702
ALPHA = 2.0 ** 0.25
EPS = 1e-5
HALF = D_MODEL // 2

TB = 512
TM = 1024
ROW_STEP = 256
FF_CHUNK = 512
TILES_PER_STEP = 3
PLAN_CHUNK = 2048
TC3 = 1024
LAST_PART_NUM, LAST_PART_DEN = 1, 4
SC_WINDOW = 128
PROJ_GROUP = 256
PERM_BLOCK = 256
LANES = 128
V7X_VMEM_BYTES = 64 * 1024 * 1024
VMEM_LIMIT = V7X_VMEM_BYTES * 7 // 8

_NT = (((1,), (1,)), ((), ()))


def _gelu(x):
    return 0.5 * x * (1.0 + lax.erf(x * (1.0 / math.sqrt(2.0))))


def _layer_norm(x, g, b):
    mu = jnp.mean(x, axis=-1, keepdims=True)
    xc = x - mu
    var = jnp.mean(xc * xc, axis=-1, keepdims=True)
    return xc * lax.rsqrt(var + EPS) * g + b


def _pack_rounded(hr):
    au = lax.bitcast_convert_type(hr[:, :HALF], jnp.uint32) >> 16
    bu = lax.bitcast_convert_type(hr[:, HALF:], jnp.uint32) & jnp.uint32(0xFFFF0000)
    return lax.bitcast_convert_type(au | bu, jnp.int32)


def _pack_rows(h):
    return _pack_rounded(h.astype(jnp.bfloat16).astype(jnp.float32))


def _unpack_rows(w):
    u = lax.bitcast_convert_type(w, jnp.uint32)
    a = lax.bitcast_convert_type(u << 16, jnp.float32)
    b = lax.bitcast_convert_type(u & jnp.uint32(0xFFFF0000), jnp.float32)
    return a, b


def _mixer_kernel(x_ref, xn_ref, win_ref, lb_ref, hg_ref, lng_ref, lnb_ref, wm_ref, bst_ref,
                  wout_ref, l1g_ref, l1b_ref, rwh_ref, rb_ref, tri_ref, upp_ref,
                  h1f_ref, h1p_ref, idx_ref, gate_ref, rank_ref, cnt_ref,
                  p_scr, lf_scr, kk_scr, g_scr, kc_scr, o_scr, st_scr, carry_scr,
                  amat_scr, qg_scr, kd_scr, upd_scr, mix_scr, decay_scr):
    b = pl.program_id(0)
    t = pl.program_id(1)
    step = b * pl.num_programs(1) + t
    f32 = jnp.float32
    bf16 = jnp.bfloat16

    @pl.when(t == 0)
    def _():
        st_scr[...] = jnp.zeros_like(st_scr)

    @pl.when((b == 0) & (t == 0))
    def _():
        carry_scr[...] = jnp.zeros_like(carry_scr)

    def project(xb, group):
        cols = slice(group * PROJ_GROUP, (group + 1) * PROJ_GROUP)
        p_scr[:, cols] = jnp.dot(xb, win_ref[:, cols], preferred_element_type=f32)

    def hgrn_gates():
        fl = p_scr[:, HGRN_WIDTH:2 * HGRN_WIDTH]
        z = jnp.exp(-jnp.abs(fl))
        r = 1.0 / (1.0 + z)
        k = (1.0 - lb_ref[...]) * jnp.where(fl >= 0, z * r, r)
        kk_scr[...] = k
        lf = jnp.log2(1.0 - k)
        lf_scr[...] = lf
        decay_scr[0] = jnp.max(-jnp.sum(lf.reshape(TB // SUB, SUB, HGRN_WIDTH), axis=1))

    @pl.when(step == 0)
    def _():
        xb0 = x_ref[...].astype(bf16)
        for group in range(IN_COLS // PROJ_GROUP):
            project(xb0, group)
        hgrn_gates()

    tri = tri_ref[...]
    row64 = lax.broadcasted_iota(jnp.int32, (CHUNK, CHUNK), 0)
    col64 = lax.broadcasted_iota(jnp.int32, (CHUNK, CHUNK), 1)
    lane_sub = lax.broadcasted_iota(jnp.int32, (SUB, CHUNK), 1)
    scale = HEAD_DIM ** -0.5

    def pad_rows(piece, lo_r):
        parts = []
        if lo_r > 0:
            parts.append(jnp.zeros((lo_r, HEAD_DIM), f32))
        parts.append(piece)
        rest = CHUNK - lo_r - piece.shape[0]
        if rest > 0:
            parts.append(jnp.zeros((rest, HEAD_DIM), f32))
        return jnp.concatenate(parts, axis=0) if len(parts) > 1 else piece

    def intra_factorised(q, k, gh):
        qts, kts = [], []
        for a in range(N_SUB):
            lo_r, hi_r = a * SUB, (a + 1) * SUB
            if a == 0:
                qa = q[:hi_r] * jnp.exp2(gh[:hi_r])
                ka = k[:hi_r] * jnp.exp2(-gh[:hi_r])
            else:
                ra = gh[lo_r - 1:lo_r]
                qa = q[lo_r:hi_r] * jnp.exp2(gh[lo_r:hi_r] - ra)
                ka = k[:hi_r] * jnp.exp2(ra - gh[:hi_r])
            qts.append(pad_rows(qa, lo_r))
            kts.append(pad_rows(ka, 0))
        a_mat = lax.dot_general(jnp.concatenate(qts, axis=1).astype(bf16),
                                jnp.concatenate(kts, axis=1).astype(bf16), _NT,
                                preferred_element_type=f32)
        return jnp.where(row64 >= col64, a_mat, 0.0)

    def chunks_factorised():
        n_chunks = TB // CHUNK
        heads = [(c, h) for c in range(n_chunks) for h in range(HGRN_HEADS)]

        def cols(h, base=0):
            return slice(base + h * HEAD_DIM, base + (h + 1) * HEAD_DIM)

        for c in range(n_chunks):
            rows = slice(c * CHUNK, (c + 1) * CHUNK)
            lf = lf_scr[rows, :]
            hi = lf.astype(bf16)
            lo = (lf - hi.astype(f32)).astype(bf16)
            gg = jnp.dot(tri, jnp.concatenate([hi, lo], axis=1), preferred_element_type=f32)
            lf_scr[rows, :] = gg[:, :HGRN_WIDTH] + gg[:, HGRN_WIDTH:]
        for c, h in heads:
            rows = slice(c * CHUNK, (c + 1) * CHUNK)
            q = p_scr[rows, cols(h)]
            k = kk_scr[rows, cols(h)]
            gh = lf_scr[rows, cols(h)]
            amat_scr[c * HGRN_HEADS + h] = intra_factorised(q, k, gh).astype(bf16)
            qg_scr[rows, cols(h)] = (q * jnp.exp2(gh)).astype(bf16)
            kd_scr[rows, cols(h)] = (k * jnp.exp2(gh[CHUNK - 1:CHUNK] - gh)).astype(bf16)
        for c, h in heads:
            rows = slice(c * CHUNK, (c + 1) * CHUNK)
            v = p_scr[rows, cols(h, 2 * HGRN_WIDTH)]
            o_scr[rows, cols(h)] = jnp.dot(amat_scr[c * HGRN_HEADS + h], v.astype(bf16),
                                           preferred_element_type=f32)
            upd_scr[c * HGRN_HEADS + h] = jnp.dot(v.T.astype(bf16), kd_scr[rows, cols(h)],
                                                  preferred_element_type=f32)
        states = [st_scr[h] for h in range(HGRN_HEADS)]
        for c, h in heads:
            rows = slice(c * CHUNK, (c + 1) * CHUNK)
            st = states[h]
            o_inter = lax.dot_general(qg_scr[rows, cols(h)], st.astype(bf16), _NT,
                                      preferred_element_type=f32)
            o_scr[rows, cols(h)] = (o_scr[rows, cols(h)] + o_inter) * scale
            gl = lf_scr[(c + 1) * CHUNK - 1:(c + 1) * CHUNK, cols(h)]
            states[h] = st * jnp.exp2(gl) + upd_scr[c * HGRN_HEADS + h]
        for h in range(HGRN_HEADS):
            st_scr[h] = states[h]

    def intra_exact_diagonal(q, k, gh, cs):
        qts, kts = [], []
        for a in range(1, N_SUB):
            lo_r, hi_r = a * SUB, (a + 1) * SUB
            ra = g_scr[lo_r - 1:lo_r, cs]
            qts.append(pad_rows(q[lo_r:hi_r] * jnp.exp2(gh[lo_r:hi_r] - ra), lo_r))
            kts.append(pad_rows(k[:lo_r] * jnp.exp2(ra - gh[:lo_r]), 0))
        a_off = lax.dot_general(jnp.concatenate(qts, axis=1).astype(bf16),
                                jnp.concatenate(kts, axis=1).astype(bf16), _NT,
                                preferred_element_type=f32)
        diag_rows = []
        for a in range(N_SUB):
            lo_r = a * SUB
            gs = gh[lo_r:lo_r + SUB]
            qs = q[lo_r:lo_r + SUB]
            blk = jnp.zeros((SUB, CHUNK), f32)
            for jl in range(SUB):
                j = lo_r + jl
                gj = g_scr[j:j + 1, cs]
                kj = kc_scr[j:j + 1, cs]
                e = jnp.exp2(jnp.minimum(gs - gj, 0.0))
                col = jnp.sum(qs * (kj * e), axis=-1, keepdims=True)
                blk = jnp.where(lane_sub == j, col, blk)
            diag_rows.append(blk)
        a_diag = jnp.concatenate(diag_rows, axis=0)
        return a_off + jnp.where(row64 >= col64, a_diag, 0.0)

    def make_chunk_body(intra):
        def chunk_body(c, carry):
            r0 = pl.multiple_of(c * CHUNK, CHUNK)
            rows = pl.ds(r0, CHUNK)
            lf = lf_scr[rows, :]
            hi = lf.astype(bf16)
            lo = (lf - hi.astype(f32)).astype(bf16)
            gg = jnp.dot(tri, jnp.concatenate([hi, lo], axis=1), preferred_element_type=f32)
            g_all = gg[:, :HGRN_WIDTH] + gg[:, HGRN_WIDTH:]
            g_scr[...] = g_all
            kc_scr[...] = kk_scr[rows, :]
            for h in range(HGRN_HEADS):
                cs = slice(h * HEAD_DIM, (h + 1) * HEAD_DIM)
                q = p_scr[rows, h * HEAD_DIM:(h + 1) * HEAD_DIM]
                v = p_scr[rows, 2 * HGRN_WIDTH + h * HEAD_DIM:2 * HGRN_WIDTH + (h + 1) * HEAD_DIM]
                k = kc_scr[:, cs]
                gh = g_all[:, cs]
                st = st_scr[h]
                o_inter = lax.dot_general((q * jnp.exp2(gh)).astype(bf16), st.astype(bf16), _NT,
                                          preferred_element_type=f32)
                a_mat = intra(q, k, gh, cs)
                o = jnp.dot(a_mat.astype(bf16), v.astype(bf16), preferred_element_type=f32) + o_inter
                o_scr[rows, h * HEAD_DIM:(h + 1) * HEAD_DIM] = o * scale

                gl = g_scr[CHUNK - 1:CHUNK, cs]
                kd = k * jnp.exp2(gl - gh)
                upd = jnp.dot(v.T.astype(bf16), kd.astype(bf16), preferred_element_type=f32)
                st_scr[h] = st * jnp.exp2(gl) + upd
            return carry
        return chunk_body

    bounded = decay_scr[0] <= MAX_SUB_DECAY

    f_groups = list(range(HGRN_WIDTH // PROJ_GROUP, 2 * HGRN_WIDTH // PROJ_GROUP))
    qi_groups = [g for g in range(3 * HGRN_WIDTH // PROJ_GROUP) if g not in f_groups]
    rest_groups = list(range(3 * HGRN_WIDTH // PROJ_GROUP, IN_COLS // PROJ_GROUP))

    def gmlp_stage(xnb):
        for g in f_groups:
            project(xnb, g)
        for w in range(TB // GMLP_BLOCK):
            rows = slice(w * GMLP_BLOCK, (w + 1) * GMLP_BLOCK)
            u = _gelu(p_scr[rows, 4 * HGRN_WIDTH:4 * HGRN_WIDTH + GMLP_WIDTH])
            vn = _layer_norm(_gelu(p_scr[rows, 4 * HGRN_WIDTH + GMLP_WIDTH:]),
                             lng_ref[...], lnb_ref[...])
            vnb = vn.astype(bf16)
            cols = []
            for g in range(GMLP_GROUPS):
                s = jnp.dot(wm_ref[g], vnb[:, g * HEAD_DIM:(g + 1) * HEAD_DIM],
                            preferred_element_type=f32) + bst_ref[:, g:g + 1]
                cols.append(s)
            mix_scr[rows, HGRN_WIDTH:] = (u * jnp.concatenate(cols, axis=1)).astype(bf16)

    def output_stage(xnb):
        for g in qi_groups:
            project(xnb, g)
        for w in range(TB // GMLP_BLOCK):
            rows = slice(w * GMLP_BLOCK, (w + 1) * GMLP_BLOCK)
            o = o_scr[rows, :]
            ms = jnp.mean(o * o, axis=-1, keepdims=True)
            gate = p_scr[rows, 3 * HGRN_WIDTH:4 * HGRN_WIDTH]
            half_gate = 0.5 * gate
            silu = half_gate * (jnp.tanh(half_gate) + 1.0)
            y_rec = o * lax.rsqrt(ms + EPS) * hg_ref[...] * silu
            mix_scr[rows, :HGRN_WIDTH] = y_rec.astype(bf16)

        mix = jnp.dot(mix_scr[...], wout_ref[...], preferred_element_type=f32)
        hgrn_gates()
        h1 = _layer_norm(ALPHA * x_ref[...] + mix, l1g_ref[...], l1b_ref[...])
        h1f_ref[...] = h1
        hh = h1.astype(bf16)
        h1_rounded = hh.astype(f32)
        h1p_ref[...] = _pack_rounded(h1_rounded)
        for g in rest_groups[:2]:
            project(xnb, g)

        logits = lax.dot_general(rwh_ref[...], hh, _NT, preferred_element_type=f32) + rb_ref[...]
        for g in rest_groups[2:]:
            project(xnb, g)

        e_iota = lax.broadcasted_iota(jnp.int32, (N_EXPERTS, TB), 0)
        work = logits
        vals, idxs, hots = [], [], []
        for _ in range(TOP_K):
            m = jnp.max(work, axis=0, keepdims=True)
            ik = jnp.min(jnp.where(work == m, e_iota, N_EXPERTS), axis=0, keepdims=True)
            hot = e_iota == ik
            work = jnp.where(hot, -jnp.inf, work)
            vals.append(m)
            idxs.append(ik)
            hots.append(hot)
        exps = [jnp.exp(vk - vals[0]) for vk in vals]
        denom = exps[0] + exps[1] + exps[2] + exps[3]
        gate_ref[...] = jnp.concatenate([ek / denom for ek in exps], axis=0)
        idx_ref[...] = jnp.concatenate(idxs, axis=0)

        hot_any = jnp.where(hots[0] | hots[1] | hots[2] | hots[3], 1.0, 0.0)
        prefix = jnp.dot(hot_any.astype(bf16), upp_ref[...], preferred_element_type=f32)
        base = prefix + carry_scr[:, 0:1]
        ranks = [jnp.sum(jnp.where(hk, base, 0.0), axis=0, keepdims=True) for hk in hots]
        rank_ref[...] = jnp.concatenate(ranks, axis=0).astype(jnp.int32)
        new_carry = carry_scr[...] + jnp.sum(hot_any, axis=1, keepdims=True)
        carry_scr[...] = new_carry
        cnt_ref[...] = new_carry

    @pl.when(bounded)
    def _():
        xnb = xn_ref[...].astype(bf16)
        gmlp_stage(xnb)
        chunks_factorised()
        output_stage(xnb)

    @pl.when(jnp.logical_not(bounded))
    def _():
        xnb = xn_ref[...].astype(bf16)
        lax.fori_loop(0, TB // CHUNK, make_chunk_body(intra_exact_diagonal), 0)
        gmlp_stage(xnb)
        output_stage(xnb)


def _mixer(xt, win, lb, hg, lng, lnb, wm, bst, wout, l1g, l1b, rwh, rb, tri, upp,
           seq, batch, batch0):
    nt = seq // TB
    t_total = batch * seq
    nblk = batch * nt
    const2 = lambda b, t: (0, 0)
    const3 = lambda b, t: (0, 0, 0)
    row_blk = lambda b, t: (b * nt + t, 0)
    meta_blk = lambda b, t: (0, b * nt + t)
    once = dict(pipeline_mode=pl.Buffered(1))
    in_specs = [
        pl.BlockSpec((TB, D_MODEL), lambda b, t: ((b + batch0) * nt + t, 0)),
        pl.BlockSpec((TB, D_MODEL),
                     lambda b, t: (batch0 * nt + jnp.minimum(b * nt + t + 1, nblk - 1), 0)),
        pl.BlockSpec((D_MODEL, IN_COLS), const2, **once),
        pl.BlockSpec((1, HGRN_WIDTH), const2),
        pl.BlockSpec((1, HGRN_WIDTH), const2),
        pl.BlockSpec((1, GMLP_WIDTH), const2),
        pl.BlockSpec((1, GMLP_WIDTH), const2),
        pl.BlockSpec((GMLP_GROUPS, GMLP_BLOCK, GMLP_BLOCK), const3),
        pl.BlockSpec((GMLP_BLOCK, GMLP_GROUPS), const2),
        pl.BlockSpec((D_MODEL, D_MODEL), const2, **once),
        pl.BlockSpec((1, D_MODEL), const2),
        pl.BlockSpec((1, D_MODEL), const2),
        pl.BlockSpec((N_EXPERTS, D_MODEL), const2),
        pl.BlockSpec((N_EXPERTS, 1), const2),
        pl.BlockSpec((CHUNK, CHUNK), const2),
        pl.BlockSpec((TB, TB), const2, **once),
    ]
    out_shape = [
        jax.ShapeDtypeStruct((t_total, D_MODEL), jnp.float32),
        jax.ShapeDtypeStruct((t_total, HALF), jnp.int32),
        jax.ShapeDtypeStruct((TOP_K, t_total), jnp.int32),
        jax.ShapeDtypeStruct((TOP_K, t_total), jnp.float32),
        jax.ShapeDtypeStruct((TOP_K, t_total), jnp.int32),
        jax.ShapeDtypeStruct((N_EXPERTS, LANES), jnp.float32),
    ]
    out_specs = [
        pl.BlockSpec((TB, D_MODEL), row_blk),
        pl.BlockSpec((TB, HALF), row_blk),
        pl.BlockSpec((TOP_K, TB), meta_blk),
        pl.BlockSpec((TOP_K, TB), meta_blk),
        pl.BlockSpec((TOP_K, TB), meta_blk),
        pl.BlockSpec((N_EXPERTS, LANES), const2),
    ]
    scratch = [
        pltpu.VMEM((TB, IN_COLS), jnp.float32),
        pltpu.VMEM((TB, HGRN_WIDTH), jnp.float32),
        pltpu.VMEM((TB, HGRN_WIDTH), jnp.float32),
        pltpu.VMEM((CHUNK, HGRN_WIDTH), jnp.float32),
        pltpu.VMEM((CHUNK, HGRN_WIDTH), jnp.float32),
        pltpu.VMEM((TB, HGRN_WIDTH), jnp.float32),
        pltpu.VMEM((HGRN_HEADS, HEAD_DIM, HEAD_DIM), jnp.float32),
        pltpu.VMEM((N_EXPERTS, LANES), jnp.float32),
        pltpu.VMEM((TB // CHUNK * HGRN_HEADS, CHUNK, CHUNK), jnp.bfloat16),
        pltpu.VMEM((TB, HGRN_WIDTH), jnp.bfloat16),
        pltpu.VMEM((TB, HGRN_WIDTH), jnp.bfloat16),
        pltpu.VMEM((TB // CHUNK * HGRN_HEADS, HEAD_DIM, HEAD_DIM), jnp.float32),
        pltpu.VMEM((TB, D_MODEL), jnp.bfloat16),
        pltpu.SMEM((1,), jnp.float32),
    ]
    return pl.pallas_call(
        _mixer_kernel,
        grid=(batch, nt),
        in_specs=in_specs,
        out_specs=out_specs,
        out_shape=out_shape,
        scratch_shapes=scratch,
        compiler_params=pltpu.CompilerParams(
            dimension_semantics=("arbitrary", "arbitrary"),
            vmem_limit_bytes=VMEM_LIMIT),
        name="mixer",
    )(xt, xt, win, lb, hg, lng, lnb, wm, bst, wout, l1g, l1b, rwh, rb, tri, upp)


def _plan_kernel(cnt_ref, idx_ref, rank_ref, dest_ref, tiles_ref):
    f32 = jnp.float32
    n_e = N_EXPERTS
    e_sub = lax.broadcasted_iota(jnp.int32, (n_e, n_e), 0)
    e_lane = lax.broadcasted_iota(jnp.int32, (n_e, n_e), 1)
    counts = cnt_ref[:, 0:1]
    padded = jnp.floor((counts + (TM - 1)) * (1.0 / TM)) * TM
    as_row = lambda col: jnp.sum(jnp.where(e_sub == e_lane, col, 0.0), axis=0, keepdims=True)
    padded_row = as_row(padded)
    counts_row = as_row(counts)
    ends = jnp.sum(jnp.where(e_lane <= e_sub, padded_row, 0.0), axis=1, keepdims=True)
    starts = ends - padded
    owns_later = (e_lane > e_sub) & (counts_row > 0.0)
    nxt = jnp.min(jnp.where(owns_later, e_lane, n_e), axis=1, keepdims=True)
    own = lax.broadcasted_iota(jnp.int32, (n_e, 1), 0)
    nxt = jnp.where(nxt == n_e, own, nxt).astype(f32)

    n_lanes = tiles_ref.shape[1]
    tile_row = lax.broadcasted_iota(jnp.int32, (n_e, n_lanes), 1).astype(f32) * TM
    e_of = lax.broadcasted_iota(jnp.int32, (n_e, n_lanes), 0)
    tile_e = jnp.minimum(jnp.sum(jnp.where(tile_row >= ends, 1, 0), axis=0, keepdims=True), n_e - 1)
    mine = e_of == tile_e
    pick = lambda col: jnp.sum(jnp.where(mine, col, 0.0), axis=0, keepdims=True)
    valid = jnp.clip(pick(starts + counts) - tile_row[0:1], 0.0, float(TM))
    rows = [tile_e, valid.astype(jnp.int32), pick(nxt).astype(jnp.int32)]
    rows.append(jnp.zeros((tiles_ref.shape[0] - len(rows), n_lanes), jnp.int32))
    tiles_ref[...] = jnp.concatenate(rows, axis=0)

    chunk = PLAN_CHUNK
    e_chunk = lax.broadcasted_iota(jnp.int32, (n_e, chunk), 0)

    def body(c, carry):
        lanes = pl.ds(pl.multiple_of(c * chunk, chunk), chunk)
        for k in range(TOP_K):
            hit = e_chunk == idx_ref[k:k + 1, lanes]
            start_of = jnp.sum(jnp.where(hit, starts, 0.0), axis=0, keepdims=True)
            dest_ref[k:k + 1, lanes] = rank_ref[k:k + 1, lanes] + start_of.astype(jnp.int32)
        return carry

    lax.fori_loop(0, idx_ref.shape[1] // chunk, body, 0)


def _plan(cnt, idx, rank, n_tiles):
    t_part = idx.shape[1]
    n_lanes = -(-n_tiles // LANES) * LANES
    full = lambda shape: pl.BlockSpec(shape, lambda i: (0,) * len(shape))
    dest, tiles = pl.pallas_call(
        _plan_kernel,
        grid=(1,),
        in_specs=[full(cnt.shape), full(idx.shape), full(rank.shape)],
        out_specs=[full(idx.shape), full((8, n_lanes))],
        out_shape=[jax.ShapeDtypeStruct((TOP_K, t_part), jnp.int32),
                   jax.ShapeDtypeStruct((8, n_lanes), jnp.int32)],
        compiler_params=pltpu.CompilerParams(
            dimension_semantics=("arbitrary",), vmem_limit_bytes=VMEM_LIMIT),
        name="plan",
    )(cnt, idx, rank)
    return dest, tiles[0, :n_tiles], tiles[1, :n_tiles], tiles[2, :n_tiles]


def _sc_workers():
    info = plsc.get_sparse_core_info()
    return info.num_cores, info.num_cores * info.num_subcores


def _sc_dispatch(rows, dest, n_out):
    t_total, dw = rows.shape
    nc, nw = _sc_workers()
    per_w = t_total // nw
    mesh = plsc.VectorSubcoreMesh(core_axis_name="c", subcore_axis_name="s")

    @functools.partial(
        pl.kernel,
        out_type=jax.ShapeDtypeStruct((n_out, dw), rows.dtype),
        mesh=mesh,
        scratch_types=[pltpu.VMEM((SC_WINDOW,), jnp.int32) for _ in range(TOP_K)]
        + [pltpu.VMEM((SC_WINDOW, dw), rows.dtype), pltpu.SemaphoreType.DMA],
        name="sc_dispatch",
    )
    def k(x_hbm, i_hbm, o_hbm, i0, i1, i2, i3, rows_v, sem):
        wid = lax.axis_index("s") * nc + lax.axis_index("c")
        idx_bufs = (i0, i1, i2, i3)

        @pl.loop(0, per_w // SC_WINDOW)
        def _(j):
            base = wid * per_w + j * SC_WINDOW
            for kk in range(TOP_K):
                pltpu.sync_copy(i_hbm.at[kk, pl.ds(base, SC_WINDOW)], idx_bufs[kk])
            pltpu.sync_copy(x_hbm.at[pl.ds(base, SC_WINDOW)], rows_v)
            copies = [pltpu.async_copy(rows_v, o_hbm.at[idx_bufs[kk]], sem) for kk in range(TOP_K)]
            for cp in copies:
                cp.wait()

    return k(rows, dest)


def _sc_gather(table, idx):
    n_k, t_total = idx.shape
    n = n_k * t_total
    dw = table.shape[1]
    nc, nw = _sc_workers()
    per_w = n // nw
    w_per_k = nw // n_k
    mesh = plsc.VectorSubcoreMesh(core_axis_name="c", subcore_axis_name="s")

    @functools.partial(
        pl.kernel,
        out_type=jax.ShapeDtypeStruct((n, dw), table.dtype),
        mesh=mesh,
        scratch_types=[pltpu.VMEM((SC_WINDOW,), jnp.int32),
                       pltpu.VMEM((SC_WINDOW, dw), table.dtype),
                       pltpu.SemaphoreType.DMA],
        name="sc_gather",
    )
    def k(t_hbm, i_hbm, o_hbm, idx_v, rows_v, sem):
        wid = lax.axis_index("s") * nc + lax.axis_index("c")

        row = wid // w_per_k
        col0 = (wid % w_per_k) * per_w

        @pl.loop(0, per_w // SC_WINDOW)
        def _(j):
            col = col0 + j * SC_WINDOW
            pltpu.sync_copy(i_hbm.at[row, pl.ds(col, SC_WINDOW)], idx_v)
            pltpu.async_copy(t_hbm.at[idx_v], rows_v, sem).wait()
            pltpu.sync_copy(rows_v, o_hbm.at[pl.ds(row * t_total + col, SC_WINDOW)])

    return k(table, idx)


def _expert_kernel(te_ref, tv_ref, tn_ref, xs_ref, w1_hbm, w2_hbm, b1g_ref, b1l_ref, b2_ref,
                   perm_ref, y_ref, w1f_scr, w2f_scr, w1g_scr, w1l_scr, w2_scr, sems):
    def tile_body(h, carry):
        _expert_tile(pl.program_id(0) * TILES_PER_STEP + h, pl.multiple_of(h * TM, TM),
                     te_ref, tv_ref, tn_ref, xs_ref, w1_hbm, w2_hbm, b1g_ref, b1l_ref, b2_ref,
                     perm_ref, y_ref, w1f_scr, w2f_scr, w1g_scr, w1l_scr, w2_scr, sems)
        return carry

    lax.fori_loop(0, TILES_PER_STEP, tile_body, 0)


def _expert_tile(i, row0, te_ref, tv_ref, tn_ref, xs_ref, w1_hbm, w2_hbm, b1g_ref, b1l_ref, b2_ref,
                 perm_ref, y_ref, w1f_scr, w2f_scr, w1g_scr, w1l_scr, w2_scr, sems):
    valid = tv_ref[i]
    expert = te_ref[i]
    f32 = jnp.float32
    bf16 = jnp.bfloat16
    expert_changed = (i == 0) | (expert != te_ref[jnp.maximum(i - 1, 0)])

    def weight_copies(e):
        return (pltpu.make_async_copy(w1_hbm.at[e], w1f_scr, sems.at[0]),
                pltpu.make_async_copy(w2_hbm.at[e], w2f_scr, sems.at[1]))

    @pl.when(valid == 0)
    def _():
        y_ref[pl.ds(row0, TM), :] = jnp.zeros((TM, HALF), jnp.int32)

    @pl.when(i == 0)
    def _():
        for cp in weight_copies(expert):
            cp.start()

    @pl.when((valid > 0) & expert_changed)
    def _():
        for cp in weight_copies(expert):
            cp.wait()
        w2_scr[...] = w2f_scr[...].astype(bf16)
        perm = perm_ref[...]
        for c in range(2 * D_MODEL // PERM_BLOCK):
            blk = w1f_scr[:, c * PERM_BLOCK:(c + 1) * PERM_BLOCK].astype(bf16)
            r = jnp.dot(blk, perm, preferred_element_type=f32).astype(bf16)
            half = PERM_BLOCK // 2
            w1g_scr[:, c * half:(c + 1) * half] = r[:, :half]
            w1l_scr[:, c * half:(c + 1) * half] = r[:, half:]

        @pl.when(tn_ref[i] != expert)
        def _():
            for cp in weight_copies(tn_ref[i]):
                cp.start()

    def expert_rows(n_rows):
        keep = lax.broadcasted_iota(jnp.int32, (n_rows, 1), 0) < valid
        a, b = _unpack_rows(jnp.where(keep, xs_ref[pl.ds(row0, n_rows), :], 0))
        x = jnp.concatenate([a, b], axis=1).astype(bf16)
        bias_row = pl.ds(expert, 1)
        y = b2_ref[bias_row, :]
        for c0 in range(0, D_MODEL, FF_CHUNK):
            ff = slice(c0, c0 + FF_CHUNK)
            hg = jnp.dot(x, w1g_scr[:, ff], preferred_element_type=f32) + b1g_ref[bias_row, ff]
            xl1 = jnp.clip(jnp.dot(x, w1l_scr[:, ff], preferred_element_type=f32)
                           + (b1l_ref[bias_row, ff] + 1.0), 1.0 - SWIGLU_LIMIT, 1.0 + SWIGLU_LIMIT)
            xg = jnp.minimum(hg, SWIGLU_LIMIT)
            act = (0.5 * xg) * (jnp.tanh((0.5 * SWIGLU_ALPHA) * xg) + 1.0) * xl1
            y = y + jnp.dot(act.astype(bf16), w2_scr[ff, :], preferred_element_type=f32)
        y_ref[pl.ds(row0, n_rows), :] = _pack_rows(y)
        if n_rows < TM:
            y_ref[pl.ds(row0 + n_rows, TM - n_rows), :] = jnp.zeros((TM - n_rows, HALF), jnp.int32)

    for blocks in range(TM // ROW_STEP, 0, -1):
        @pl.when((valid > (blocks - 1) * ROW_STEP) & (valid <= blocks * ROW_STEP))
        def _(blocks=blocks):
            expert_rows(blocks * ROW_STEP)


def _experts(xs, tile_expert, tile_valid, tile_next, w1, w2, b1g, b1l, b2, perm):
    n_slots = xs.shape[0]
    step_rows = TILES_PER_STEP * TM
    assert n_slots % step_rows == 0
    d_ff = w2.shape[1]
    whole = lambda i, te, tv, tn: (0, 0)
    grid_spec = pltpu.PrefetchScalarGridSpec(
        num_scalar_prefetch=3,
        grid=(n_slots // step_rows,),
        in_specs=[
            pl.BlockSpec((step_rows, HALF), lambda i, te, tv, tn: (i, 0)),
            pl.BlockSpec(memory_space=pl.ANY),
            pl.BlockSpec(memory_space=pl.ANY),
            pl.BlockSpec((N_EXPERTS, d_ff), whole),
            pl.BlockSpec((N_EXPERTS, d_ff), whole),
            pl.BlockSpec((N_EXPERTS, D_MODEL), whole),
            pl.BlockSpec((PERM_BLOCK, PERM_BLOCK), whole),
        ],
        out_specs=pl.BlockSpec((step_rows, HALF), lambda i, te, tv, tn: (i, 0)),
        scratch_shapes=[
            pltpu.VMEM((D_MODEL, 2 * d_ff), jnp.float32),
            pltpu.VMEM((d_ff, D_MODEL), jnp.float32),
            pltpu.VMEM((D_MODEL, d_ff), jnp.bfloat16),
            pltpu.VMEM((D_MODEL, d_ff), jnp.bfloat16),
            pltpu.VMEM((d_ff, D_MODEL), jnp.bfloat16),
            pltpu.SemaphoreType.DMA((2,)),
        ],
    )
    return pl.pallas_call(
        _expert_kernel,
        grid_spec=grid_spec,
        out_shape=jax.ShapeDtypeStruct((n_slots, HALF), jnp.int32),
        compiler_params=pltpu.CompilerParams(
            dimension_semantics=("arbitrary",),
            vmem_limit_bytes=VMEM_LIMIT),
        name="experts",
    )(tile_expert, tile_valid, tile_next, xs, w1, w2, b1g, b1l, b2, perm)


def _combine_kernel(h_ref, yk_ref, gate_ref, g_ref, b_ref, *rest):
    o_ref = rest[-1]
    gates = gate_ref[...].T
    acc_a = None
    acc_b = None
    for k in range(TOP_K):
        a, b = _unpack_rows(yk_ref[k])
        gk = gates[:, k:k + 1]
        acc_a = gk * a if acc_a is None else acc_a + gk * a
        acc_b = gk * b if acc_b is None else acc_b + gk * b
    ffn = jnp.concatenate([acc_a, acc_b], axis=1)
    o_ref[...] = _layer_norm(ALPHA * h_ref[...] + ffn, g_ref[...], b_ref[...])


def _combine(h1f, yk, gates, g2, b2, out_prev, row0, t_all):
    t_part = h1f.shape[0]
    blk0 = row0 // TC3
    row = lambda i: (i, 0)
    const = lambda i: (0, 0)
    in_specs = [
        pl.BlockSpec((TC3, D_MODEL), row),
        pl.BlockSpec((TOP_K, TC3, HALF), lambda i: (0, i, 0)),
        pl.BlockSpec((TOP_K, TC3), lambda i: (0, i)),
        pl.BlockSpec((1, D_MODEL), const),
        pl.BlockSpec((1, D_MODEL), const),
    ]
    args = [h1f, yk, gates, g2, b2]
    aliases = {}
    if out_prev is not None:
        in_specs.append(pl.BlockSpec(memory_space=pl.ANY))
        args.append(out_prev)
        aliases = {len(args) - 1: 0}
    return pl.pallas_call(
        _combine_kernel,
        grid=(t_part // TC3,),
        in_specs=in_specs,
        out_specs=pl.BlockSpec((TC3, D_MODEL), lambda i: (i + blk0, 0)),
        out_shape=jax.ShapeDtypeStruct((t_all, D_MODEL), jnp.float32),
        input_output_aliases=aliases,
        compiler_params=pltpu.CompilerParams(
            dimension_semantics=("arbitrary",),
            vmem_limit_bytes=VMEM_LIMIT),
        name="combine",
    )(*args)


def kernel(x, w_in, lb_logits, hgrn_norm_g, gmlp_ln_g, gmlp_ln_b, gmlp_ws, gmlp_bs, w_out, ln1_g, ln1_b, router_w, router_b, exp_w1, exp_b1, exp_w2, exp_b2, ln2_g, ln2_b):
    batch, seq, d = x.shape
    assert d == D_MODEL and seq % TB == 0 and w_in.shape[0] == 1
    t_total = batch * seq
    f32 = jnp.float32
    bf16 = jnp.bfloat16

    lb = jnp.cumsum(jax.nn.softmax(lb_logits.astype(f32), axis=0), axis=0)[0:1]
    chunk_id = jnp.arange(GMLP_BLOCK) // CHUNK
    wm = jnp.where((chunk_id[None, :] <= chunk_id[:, None])[None], gmlp_ws[0], 0.0).astype(bf16)
    rwh = router_w[0].T.astype(bf16)
    tri = (jnp.arange(CHUNK)[None, :] <= jnp.arange(CHUNK)[:, None]).astype(bf16)
    upp = (jnp.arange(TB)[:, None] < jnp.arange(TB)[None, :]).astype(bf16)

    lane = jnp.arange(PERM_BLOCK)
    src = jnp.where(lane < PERM_BLOCK // 2, 2 * lane, 2 * (lane - PERM_BLOCK // 2) + 1)
    perm = (jnp.arange(PERM_BLOCK)[:, None] == src[None, :]).astype(bf16)
    xt = x.reshape(t_total, d)
    win = w_in[0].astype(bf16)
    wout = w_out[0].astype(bf16)
    b1g, b1l, b2e = exp_b1[0][:, 0::2], exp_b1[0][:, 1::2], exp_b2[0]

    last = max(batch * LAST_PART_NUM // LAST_PART_DEN, 1) if batch > 1 else 0
    part_sizes = [pb for pb in (batch - last, last) if pb > 0]
    out = None
    b0 = 0
    for pb in part_sizes:
        t_part = pb * seq
        step_rows = TILES_PER_STEP * TM
        n_slots = -(-(t_part * TOP_K + N_EXPERTS * TM) // step_rows) * step_rows
        n_tiles = n_slots // TM
        h1f, h1p, idx, gates, rank, cnt = _mixer(
            xt, win, lb, hgrn_norm_g[0:1], gmlp_ln_g[0:1], gmlp_ln_b[0:1], wm, gmlp_bs[0].T, wout,
            ln1_g[0:1], ln1_b[0:1], rwh, router_b[0][:, None], tri, upp, seq, pb, b0)

        dest, tile_expert, tile_valid, tile_next = _plan(cnt, idx, rank, n_tiles)
        xs = _sc_dispatch(h1p, dest, n_slots)
        y = _experts(xs, tile_expert, tile_valid, tile_next,
                     exp_w1[0], exp_w2[0], b1g, b1l, b2e, perm)
        yk = _sc_gather(y, dest).reshape(TOP_K, t_part, HALF)
        out = _combine(h1f, yk, gates, ln2_g[0:1], ln2_b[0:1], out, b0 * seq, t_total)
        b0 += pb
    return out.reshape(batch, seq, d)
```
